```python
import math
import jax, jax.numpy as jnp
from jax import lax
import numpy as np

D_MODEL = 1024
BATCH = 8
SEQ = 4096
DEPTH = 2

HEAD_DIM = 64
RWKV_HEADS = 8
ATTN_HEADS = 8
RWKV_WIDTH = RWKV_HEADS * HEAD_DIM
ATTN_WIDTH = ATTN_HEADS * HEAD_DIM
MIX_WIDTH = RWKV_WIDTH + ATTN_WIDTH
IDX_HEADS = 8
IDX_DIM = 64
TOPK_MAX = 256
QUERY_BLOCK = 128
DECAY_LORA = 64
ICLR_LORA = 64
VRES_LORA = 32
GATE_LORA = 128
D_FF = 4 * D_MODEL
N_BUCKETS = 32
MAX_DISTANCE = 128
RMS_EPS = 1e-6
LNX_EPS = 64e-5
IN_COLS = 3 * RWKV_WIDTH + 3 * ATTN_WIDTH + IDX_HEADS * IDX_DIM + IDX_DIM + IDX_HEADS

kernel_name = "hymba_rwkv7_dsa_adaln_trunk"


def rmsnorm(x, g):
    xf = x.astype(jnp.float32)
    y = xf * lax.rsqrt(jnp.mean(xf * xf, axis=-1, keepdims=True) + RMS_EPS)
    return (y * g.astype(jnp.float32)).astype(x.dtype)


def token_shift(z):
    return jnp.pad(z[:, :-1], ((0, 0), (1, 0), (0, 0)))


def split_heads(z, n):
    return z.reshape(*z.shape[:-1], n, z.shape[-1] // n)


def t5_bucket(dist):
    max_exact = N_BUCKETS // 2
    n = jnp.maximum(dist, 0)
    nf = jnp.maximum(n, 1).astype(jnp.float32)
    large = max_exact + (jnp.log(nf / max_exact) / math.log(MAX_DISTANCE / max_exact)
                         * (N_BUCKETS - max_exact)).astype(jnp.int32)
    large = jnp.minimum(large, N_BUCKETS - 1)
    return jnp.where(n < max_exact, n, large)


def rwkv7_recurrence(r, decay, k, v, kk, a):
    B, T, H, N = r.shape

    def step(S, inp):
        r_t, w_t, k_t, v_t, kk_t, a_t = inp
        s_kk = jnp.einsum('bhvk,bhk->bhv', S, kk_t)
        S = (S * w_t[:, :, None, :]
             - s_kk[..., None] * (kk_t * a_t)[:, :, None, :]
             + v_t[..., None] * k_t[:, :, None, :])
        return S, jnp.einsum('bhvk,bhk->bhv', S, r_t)

    xs = tuple(jnp.moveaxis(z.astype(jnp.float32), 1, 0) for z in (r, decay, k, v, kk, a))
    s0 = jnp.zeros((B, H, N, N), jnp.float32)
    _, ys = lax.scan(step, s0, xs)
    return jnp.moveaxis(ys, 0, 1)


def rwkv7_group(h, rkv, v_first, mu_rkv, mu_lora, decay_w0, decay_a, decay_b, iclr_a0, iclr_a,
                iclr_b, gate_a, gate_b, k_k, k_a, r_k, lnx_g, lnx_b, vres):
    B, T, _ = h.shape
    f32 = jnp.float32
    rkv = rkv + (token_shift(rkv) - rkv) * mu_rkv.reshape(-1)
    r, k, v = jnp.split(rkv, 3, axis=-1)
    dx = token_shift(h) - h
    xw = h + dx * mu_lora[0]
    xa = h + dx * mu_lora[1]
    xg = h + dx * mu_lora[2]
    wlog = (decay_w0 + jnp.tanh(xw @ decay_a) @ decay_b).astype(f32)
    wlog = -jax.nn.softplus(-wlog) - 0.5
    decay = jnp.exp(-jnp.exp(wlog))
    a = jax.nn.sigmoid(iclr_a0 + (xa @ iclr_a) @ iclr_b)
    g = jax.nn.sigmoid(xg @ gate_a) @ gate_b
    kk = split_heads(k * k_k, RWKV_HEADS).astype(f32)
    kk = kk / jnp.maximum(jnp.sqrt(jnp.sum(kk * kk, axis=-1, keepdims=True)), 1e-12)
    k = k * (1.0 + (a - 1.0) * k_a)
    if vres is None:
        v_first = v
    else:
        mu_v, v0, v_a, v_b = vres
        xv = h + dx * mu_v
        v = v + (v_first - v) * jax.nn.sigmoid(v0 + (xv @ v_a) @ v_b)
    rh = split_heads(r, RWKV_HEADS).astype(f32)
    kh = split_heads(k, RWKV_HEADS).astype(f32)
    vh = split_heads(v, RWKV_HEADS).astype(f32)
    ah = split_heads(a, RWKV_HEADS).astype(f32)
    y = rwkv7_recurrence(rh, split_heads(decay, RWKV_HEADS), kh, vh, kk, ah)
    mean = jnp.mean(y, axis=-1, keepdims=True)
    var = jnp.mean((y - mean) ** 2, axis=-1, keepdims=True)
    yn = ((y - mean) * lax.rsqrt(var + LNX_EPS) * lnx_g.reshape(RWKV_HEADS, HEAD_DIM).astype(f32)
          + lnx_b.reshape(RWKV_HEADS, HEAD_DIM).astype(f32))
    bonus = jnp.sum(rh * kh * r_k.astype(f32), axis=-1, keepdims=True) * vh
    out = (yn + bonus).reshape(B, T, RWKV_WIDTH) * g.astype(f32)
    return out.astype(h.dtype), v_first


def dsa_attention(q, k, v, qi, ki, wi, rel_bias):
    B, T, H, Dh = q.shape
    L = k.shape[1]
    n_sel = min(TOPK_MAX, L // 4)
    nb = T // QUERY_BLOCK
    key_pos = jnp.arange(L, dtype=jnp.int32)
    kif = ki.astype(jnp.float32)

    def blocks(z):
        return jnp.moveaxis(z.reshape(B, nb, QUERY_BLOCK, *z.shape[2:]), 1, 0)

    def one_block(args):
        start, q_b, qi_b, wi_b = args
        q_pos = start + jnp.arange(QUERY_BLOCK, dtype=jnp.int32)
        dots = jnp.einsum('bqhd,bsd->bqhs', qi_b.astype(jnp.float32), kif) * (IDX_DIM ** -0.5)
        w_b = wi_b.astype(jnp.float32) * (IDX_HEADS ** -0.5)
        score = jnp.einsum('bqh,bqhs->bqs', w_b, jax.nn.relu(dots))
        admissible = key_pos[None, :] <= q_pos[:, None]
        score = jnp.where(admissible[None], score, -jnp.inf)
        _, idx = lax.top_k(score, n_sel)
        valid = idx <= q_pos[None, :, None]
        k_sel = jax.vmap(lambda kb, ib: kb[ib])(k, idx)
        v_sel = jax.vmap(lambda vb, ib: vb[ib])(v, idx)
        logits = jnp.einsum('bqhd,bqkhd->bqhk', q_b, k_sel).astype(jnp.float32) * (Dh ** -0.5)
        bias = rel_bias[t5_bucket(q_pos[None, :, None] - idx)]
        logits = logits + jnp.moveaxis(bias, -1, 2).astype(jnp.float32)
        logits = jnp.where(valid[:, :, None, :], logits, -jnp.inf)
        p = jax.nn.softmax(logits, axis=-1)
        return jnp.einsum('bqhk,bqkhd->bqhd', p.astype(v.dtype), v_sel)

    starts = jnp.arange(nb, dtype=jnp.int32) * QUERY_BLOCK
    out = lax.map(one_block, (starts, blocks(q), blocks(qi), blocks(wi)))
    return jnp.moveaxis(out, 0, 1).reshape(B, T, H, Dh)


def setup_inputs(seed: int = 0) -> dict:
    key = jax.random.key(seed)
    ks = iter(jax.random.split(key, 40))
    nrm = lambda shape, scale: jax.random.normal(next(ks), shape, jnp.float32) * scale
    uni = lambda shape, lo, hi: jax.random.uniform(next(ks), shape, jnp.float32, lo, hi)
    L, D, R, A = DEPTH, D_MODEL, RWKV_WIDTH, ATTN_WIDTH
    return {
        "x": nrm((BATCH, SEQ, D), 1.0),
        "c": nrm((BATCH, D), 1.0),
        "w_ada": nrm((L, D, 6 * D), 0.5 * D ** -0.5),
        "b_ada": nrm((L, 6 * D), 0.01),
        "norm1_g": 1.0 + nrm((L, D), 0.02),
        "norm2_g": 1.0 + nrm((L, D), 0.02),
        "w_in": nrm((L, D, IN_COLS), D ** -0.5),
        "mu_rkv": uni((L, 3, R), 0.0, 1.0),
        "mu_lora": uni((L, 3, D), 0.0, 1.0),
        "decay_w0": uni((L, R), -4.0, -0.5),
        "decay_a": nrm((L, D, DECAY_LORA), D ** -0.5),
        "decay_b": nrm((L, DECAY_LORA, R), 0.1 * DECAY_LORA ** -0.5),
        "iclr_a0": nrm((L, R), 0.1),
        "iclr_a": nrm((L, D, ICLR_LORA), D ** -0.5),
        "iclr_b": nrm((L, ICLR_LORA, R), 0.3 * ICLR_LORA ** -0.5),
        "gate_a": nrm((L, D, GATE_LORA), D ** -0.5),
        "gate_b": nrm((L, GATE_LORA, R), GATE_LORA ** -0.5),
        "k_k": 0.85 + nrm((L, R), 0.05),
        "k_a": 1.0 + nrm((L, R), 0.05),
        "r_k": nrm((L, RWKV_HEADS, HEAD_DIM), 0.1),
        "lnx_g": 1.0 + nrm((L, R), 0.02),
        "lnx_b": nrm((L, R), 0.01),
        "vres_mu": uni((L - 1, D), 0.0, 1.0),
        "vres_v0": nrm((L - 1, R), 0.1),
        "vres_a": nrm((L - 1, D, VRES_LORA), D ** -0.5),
        "vres_b": nrm((L - 1, VRES_LORA, R), 0.3 * VRES_LORA ** -0.5),
        "attn_out_g": 1.0 + nrm((L, A), 0.02),
        "rel_bias": nrm((N_BUCKETS, ATTN_HEADS), 0.5),
        "w_out": nrm((L, MIX_WIDTH, D), MIX_WIDTH ** -0.5),
        "w_mlp1": nrm((L, D, D_FF), D ** -0.5),
        "w_mlp2": nrm((L, D_FF, D), D_FF ** -0.5),
        "final_g": 1.0 + nrm((D,), 0.02),
    }


def reference(x, c, w_ada, b_ada, norm1_g, norm2_g, w_in, mu_rkv, mu_lora, decay_w0, decay_a,
              decay_b, iclr_a0, iclr_a, iclr_b, gate_a, gate_b, k_k, k_a, r_k, lnx_g, lnx_b,
              vres_mu, vres_v0, vres_a, vres_b, attn_out_g, rel_bias, w_out, w_mlp1, w_mlp2,
              final_g):
    B, T, D = x.shape
    R, A = RWKV_WIDTH, ATTN_WIDTH
    cuts = np.cumsum([R, R, R, A, A, A, IDX_HEADS * IDX_DIM, IDX_DIM]).tolist()
    c_act = jax.nn.silu(c)
    v_first = None
    for l in range(DEPTH):
        mod = c_act @ w_ada[l] + b_ada[l]
        sh1, sc1, gt1, sh2, sc2, gt2 = jnp.split(mod, 6, axis=-1)
        h = rmsnorm(x, norm1_g[l]) * (1.0 + sc1[:, None]) + sh1[:, None]
        proj = h @ w_in[l]
        r_p, k_p, v_p, q_a, k_att, v_att, qi, ki, wi = jnp.split(proj, cuts, axis=-1)
        vres = None if l == 0 else (vres_mu[l - 1], vres_v0[l - 1], vres_a[l - 1], vres_b[l - 1])
        rwkv_out, v_first = rwkv7_group(
            h, jnp.concatenate([r_p, k_p, v_p], axis=-1), v_first, mu_rkv[l], mu_lora[l],
            decay_w0[l], decay_a[l], decay_b[l], iclr_a0[l], iclr_a[l], iclr_b[l], gate_a[l],
            gate_b[l], k_k[l], k_a[l], r_k[l], lnx_g[l], lnx_b[l], vres)
        att = dsa_attention(split_heads(q_a, ATTN_HEADS), split_heads(k_att, ATTN_HEADS),
                            split_heads(v_att, ATTN_HEADS), split_heads(qi, IDX_HEADS), ki, wi,
                            rel_bias)
        att = rmsnorm(att, attn_out_g[l].reshape(ATTN_HEADS, HEAD_DIM)).reshape(B, T, A)
        mixed = jnp.concatenate([rwkv_out, att], axis=-1) @ w_out[l]
        x = x + gt1[:, None] * mixed
        h2 = rmsnorm(x, norm2_g[l]) * (1.0 + sc2[:, None]) + sh2[:, None]
        u = jnp.square(jax.nn.relu(h2 @ w_mlp1[l]))
        x = x + gt2[:, None] * (u @ w_mlp2[l])
    return rmsnorm(x, final_g)
```

```python
import functools
import math

import numpy as np
import jax
import jax.numpy as jnp
from jax import lax
from jax.experimental import pallas as pl
from jax.experimental.pallas import tpu as pltpu

F32 = jnp.float32
BF16 = jnp.bfloat16
I32 = jnp.int32

HEAD_DIM = 64
RWKV_HEADS = 8
ATTN_HEADS = 8
IDX_HEADS = 8
IDX_DIM = 64
TOPK_MAX = 256
QUERY_BLOCK = 128
N_BUCKETS = 32
MAX_DISTANCE = 128
RMS_EPS = 1e-6
LNX_EPS = 64e-5

LANES = 128
VMEM_LIMIT = 56 * 1024 * 1024
RWKV_CHUNK = 64
LORA_PAD = 384
IDX_COLS = 640
NEG_BIG = -1e30
HI = lax.Precision.HIGHEST
NT = (((1,), (1,)), ((), ()))
TN = (((0,), (0,)), ((), ()))


def _bdot(a, b):
    return jnp.dot(a.astype(BF16), b.astype(BF16), preferred_element_type=F32)


def _hdot(a, b):
    return jnp.dot(a, b, precision=HI, preferred_element_type=F32)


def _params(*sem):
    return pltpu.CompilerParams(dimension_semantics=sem, vmem_limit_bytes=VMEM_LIMIT)


def _adaln_kernel(c_ref, w_ref, b_ref, o_ref):
    c = c_ref[...]
    c_act = c * jax.nn.sigmoid(c)
    o_ref[...] = _hdot(c_act, w_ref[...]) + b_ref[...]


def _adaln(c, w_ada, b_ada):
    L, D, D6 = w_ada.shape
    B = c.shape[0]
    cb = 1024
    return pl.pallas_call(
        _adaln_kernel,
        grid=(L, D6 // cb),
        in_specs=[pl.BlockSpec((B, D), lambda l, j: (0, 0)),
                  pl.BlockSpec((None, D, cb), lambda l, j: (l, 0, j)),
                  pl.BlockSpec((None, 1, cb), lambda l, j: (l, 0, j))],
        out_specs=pl.BlockSpec((None, B, cb), lambda l, j: (l, 0, j)),
        out_shape=jax.ShapeDtypeStruct((L, B, D6), F32),
        compiler_params=_params("arbitrary", "arbitrary"),
        name="adaln",
    )(c, w_ada, b_ada.reshape(L, 1, D6))


def _norm_mod(x, g, sc, sh):
    ms = jnp.mean(x * x, axis=-1, keepdims=True)
    return (x * lax.rsqrt(ms + RMS_EPS) * g) * (1.0 + sc) + sh


def _inproj_kernel(x_ref, g_ref, sc_ref, sh_ref, wa_ref, wbh_ref, wbl_ref,
                   rkvl_ref, qkv_ref, qiw_ref, kix_ref):
    h = _norm_mod(x_ref[...], g_ref[...], sc_ref[...], sh_ref[...])
    hb = h.astype(BF16)
    hl = (h - hb.astype(F32)).astype(BF16)
    pa = jnp.dot(hb, wa_ref[...], preferred_element_type=F32)
    nr = rkvl_ref.shape[1]
    rkvl_ref[...] = pa[:, :nr]
    qkv_ref[...] = pa[:, nr:].astype(BF16)
    pb = (jnp.dot(hb, wbh_ref[...], preferred_element_type=F32)
          + jnp.dot(hb, wbl_ref[...], preferred_element_type=F32)
          + jnp.dot(hl, wbh_ref[...], preferred_element_type=F32))
    qiw_ref[...] = pb
    ki = pb[:, 512:576]
    kh = ki.astype(BF16)
    kl = (ki - kh.astype(F32)).astype(BF16)
    kix_ref[...] = jnp.concatenate([kh, kl, kh, jnp.zeros_like(kh)], axis=-1)


def _inproj(x2d, g, sc, sh, wa, wbh, wbl, T, tm=256):
    N, D = x2d.shape
    na = wa.shape[1]
    nr = 3 * 512 + 2 * LORA_PAD
    nb = T // tm
    row = lambda i: (i, 0)
    per_b = lambda i: (i // nb, 0, 0)
    const = lambda i: (0, 0)
    return pl.pallas_call(
        _inproj_kernel,
        grid=(N // tm,),
        in_specs=[pl.BlockSpec((tm, D), row),
                  pl.BlockSpec((1, D), const),
                  pl.BlockSpec((None, 1, D), per_b),
                  pl.BlockSpec((None, 1, D), per_b),
                  pl.BlockSpec((D, na), const),
                  pl.BlockSpec((D, IDX_COLS), const),
                  pl.BlockSpec((D, IDX_COLS), const)],
        out_specs=[pl.BlockSpec((tm, nr), row),
                   pl.BlockSpec((tm, na - nr), row),
                   pl.BlockSpec((tm, IDX_COLS), row),
                   pl.BlockSpec((tm, 256), row)],
        out_shape=[jax.ShapeDtypeStruct((N, nr), F32),
                   jax.ShapeDtypeStruct((N, na - nr), BF16),
                   jax.ShapeDtypeStruct((N, IDX_COLS), F32),
                   jax.ShapeDtypeStruct((N, 256), BF16)],
        compiler_params=_params("arbitrary"),
        name="inproj",
    )(x2d, g.reshape(1, D), sc, sh, wa, wbh, wbl)


def _head_ones(n):
    r = lax.broadcasted_iota(I32, (n, n), 0) // HEAD_DIM
    c = lax.broadcasted_iota(I32, (n, n), 1) // HEAD_DIM
    return jnp.where(r == c, 1.0, 0.0).astype(F32)


def _rwkv_kernel(has_vres, TT, *refs):
    C = RWKV_CHUNK
    R = RWKV_HEADS * HEAD_DIM
    if has_vres:
        (rkvl_ref, vf_ref, mu_ref, w0_ref, db_ref, a0_ref, ib_ref, gb_ref, kk_ref, ka_ref,
         rk_ref, lg_ref, lb_ref, v0_ref, vb_ref, out_ref,
         prev_ref, S_ref, r_s, k_s, v_s, kk_s, b_s, lw_s, y_s) = refs
    else:
        (rkvl_ref, mu_ref, w0_ref, db_ref, a0_ref, ib_ref, gb_ref, kk_ref, ka_ref,
         rk_ref, lg_ref, lb_ref, out_ref, vfo_ref,
         prev_ref, S_ref, r_s, k_s, v_s, kk_s, b_s, lw_s, y_s) = refs

    @pl.when(pl.program_id(1) == 0)
    def _():
        prev_ref[...] = jnp.zeros_like(prev_ref)
        S_ref[...] = jnp.zeros_like(S_ref)

    row0 = lax.broadcasted_iota(I32, (TT, 1), 0) == 0
    prev = prev_ref[...]

    def shift(z, p):
        return jnp.where(row0, p, pltpu.roll(z, 1, 0))

    rkv = rkvl_ref[:, :3 * R]
    p1 = rkvl_ref[:, 3 * R:3 * R + LORA_PAD]
    p2 = rkvl_ref[:, 3 * R + LORA_PAD:]
    rkv_new = rkv + (shift(rkv, prev[:, :3 * R]) - rkv) * mu_ref[...]
    lora = p1 + shift(p2, prev[:, 3 * R:])
    prev_ref[...] = jnp.concatenate([rkv[TT - 1:TT, :], p2[TT - 1:TT, :]], axis=-1)

    r = rkv_new[:, :R]
    k = rkv_new[:, R:2 * R]
    v = rkv_new[:, 2 * R:]
    ones_h = _head_ones(R)

    wlog = w0_ref[...] + _bdot(jnp.tanh(lora[:, 0:64]), db_ref[...])
    z = -wlog
    wlog = -(jnp.maximum(z, 0.0) + jnp.log1p(jnp.exp(-jnp.abs(z)))) - 0.5
    lw_s[...] = -jnp.exp(wlog)
    a = jax.nn.sigmoid(a0_ref[...] + _bdot(lora[:, 64:128], ib_ref[...]))
    g = _bdot(jax.nn.sigmoid(lora[:, 128:256]), gb_ref[...])
    kkr = k * kk_ref[...]
    n2 = _hdot(kkr * kkr, ones_h)
    kk = kkr / jnp.maximum(jnp.sqrt(n2), 1e-12)
    k = k * (1.0 + (a - 1.0) * ka_ref[...])
    if has_vres:
        v = v + (vf_ref[...] - v) * jax.nn.sigmoid(v0_ref[...] + _bdot(lora[:, 256:288], vb_ref[...]))
    else:
        vfo_ref[...] = v
    r_s[...] = r
    k_s[...] = k
    v_s[...] = v
    kk_s[...] = kk
    b_s[...] = kk * a

    ri = lax.broadcasted_iota(I32, (C, C), 0)
    ci = lax.broadcasted_iota(I32, (C, C), 1)
    incl = ci <= ri
    strict = ci < ri
    tri = jnp.where(incl, 1.0, 0.0).astype(F32)
    eye = jnp.where(ci == ri, 1.0, 0.0).astype(F32)
    dmask = ci == ri

    def chunk(c, carry):
        sl = pl.ds(pl.multiple_of(c * C, C), C)
        lw = lw_s[sl, :]
        cum = _hdot(tri, lw)
        cl = cum[C - 1:C, :]
        rd = r_s[sl, :] * jnp.exp(cum)
        kkd = kk_s[sl, :] * jnp.exp(cum - lw)
        pinv = jnp.exp(-cum)
        kc_all = k_s[sl, :]
        b_all = b_s[sl, :]
        kt = kc_all * pinv
        bt = b_all * pinv
        pc = jnp.exp(cl - cum)
        kc = kc_all * pc
        bc = b_all * pc
        pl_last = jnp.exp(cl)
        vv = v_s[sl, :]
        ys = []
        for h in range(RWKV_HEADS):
            hs = slice(h * HEAD_DIM, (h + 1) * HEAD_DIM)
            rd_h, kkd_h, kt_h, bt_h, kc_h, bc_h, v_h = (
                rd[:, hs], kkd[:, hs], kt[:, hs], bt[:, hs], kc[:, hs], bc[:, hs], vv[:, hs])
            akk = jnp.where(strict, lax.dot_general(kkd_h, kt_h, NT, precision=HI, preferred_element_type=F32), 0.0)
            akb = jnp.where(strict, lax.dot_general(kkd_h, bt_h, NT, precision=HI, preferred_element_type=F32), 0.0)
            ark = jnp.where(incl, lax.dot_general(rd_h, kt_h, NT, precision=HI, preferred_element_type=F32), 0.0)
            arb = jnp.where(incl, lax.dot_general(rd_h, bt_h, NT, precision=HI, preferred_element_type=F32), 0.0)
            npow = -akb
            x = eye + npow
            for _ in range(int(math.log2(C)) - 1):
                npow = _hdot(npow, npow)
                x = x + _hdot(x, npow)
            w1 = _hdot(x, kkd_h)
            u0 = _hdot(x, _hdot(akk, v_h))
            m_mat = (jnp.where(dmask, jnp.broadcast_to(pl_last[:, hs], (HEAD_DIM, HEAD_DIM)), 0.0)
                     - lax.dot_general(bc_h, w1, TN, precision=HI, preferred_element_type=F32))
            g_mat = (lax.dot_general(kc_h, v_h, TN, precision=HI, preferred_element_type=F32)
                     - lax.dot_general(bc_h, u0, TN, precision=HI, preferred_element_type=F32))
            q_mat = rd_h - _hdot(arb, w1)
            y0 = _hdot(ark, v_h) - _hdot(arb, u0)
            s_old = S_ref[h]
            ys.append(_hdot(q_mat, s_old) + y0)
            S_ref[h] = _hdot(m_mat, s_old) + g_mat
        y_s[sl, :] = jnp.concatenate(ys, axis=-1)
        return carry

    lax.fori_loop(0, TT // C, chunk, 0)

    y = y_s[...]
    inv_n = 1.0 / HEAD_DIM
    mean = _hdot(y, ones_h) * inv_n
    yc = y - mean
    var = _hdot(yc * yc, ones_h) * inv_n
    yn = yc * lax.rsqrt(var + LNX_EPS) * lg_ref[...] + lb_ref[...]
    r = r_s[...]
    bonus = _hdot(r * k_s[...] * rk_ref[...], ones_h) * v_s[...]
    out_ref[...] = ((yn + bonus) * g).astype(out_ref.dtype)


def _rwkv(rkvl, v_first, p, B, T, TT=256):
    N, nc = rkvl.shape
    R = RWKV_HEADS * HEAD_DIM
    nt = T // TT
    has_vres = v_first is not None
    row = lambda b, i: (b * nt + i, 0)
    const = lambda b, i: (0, 0)
    vec = lambda a: a.reshape(1, -1)
    ins = [rkvl]
    specs = [pl.BlockSpec((TT, nc), row)]
    if has_vres:
        ins.append(v_first)
        specs.append(pl.BlockSpec((TT, R), row))
    small = [vec(p["mu_rkv"]), vec(p["decay_w0"]), p["decay_b"].astype(BF16), vec(p["iclr_a0"]),
             p["iclr_b"].astype(BF16), p["gate_b"].astype(BF16), vec(p["k_k"]), vec(p["k_a"]),
             vec(p["r_k"]), vec(p["lnx_g"]), vec(p["lnx_b"])]
    if has_vres:
        small += [vec(p["vres_v0"]), p["vres_b"].astype(BF16)]
    ins += small
    specs += [pl.BlockSpec(a.shape, const) for a in small]
    out_shape = [jax.ShapeDtypeStruct((N, R), BF16)]
    out_specs = [pl.BlockSpec((TT, R), row)]
    if not has_vres:
        out_shape.append(jax.ShapeDtypeStruct((N, R), F32))
        out_specs.append(pl.BlockSpec((TT, R), row))
    scratch = [pltpu.VMEM((1, 3 * R + LORA_PAD), F32),
               pltpu.VMEM((RWKV_HEADS, HEAD_DIM, HEAD_DIM), F32)]
    scratch += [pltpu.VMEM((TT, R), F32) for _ in range(7)]
    res = pl.pallas_call(
        functools.partial(_rwkv_kernel, has_vres, TT),
        grid=(B, nt),
        in_specs=specs,
        out_specs=out_specs,
        out_shape=out_shape,
        scratch_shapes=scratch,
        compiler_params=_params("arbitrary", "arbitrary"),
        name="rwkv",
    )(*ins)
    if has_vres:
        return res[0], v_first
    return res[0], res[1]


def _bucket_boundaries():
    max_exact = N_BUCKETS // 2
    d = np.arange(0, 4 * MAX_DISTANCE, dtype=np.int64)
    nf = np.maximum(d, 1).astype(np.float32)
    large = max_exact + (np.log(nf / np.float32(max_exact)) / np.float32(math.log(MAX_DISTANCE / max_exact))
                         * np.float32(N_BUCKETS - max_exact)).astype(np.int32)
    large = np.minimum(large, N_BUCKETS - 1)
    bucket = np.where(d < max_exact, d, large)
    return [int(np.argmax(bucket >= j)) for j in range(max_exact + 1, N_BUCKETS)]


_BUCKET_STARTS = _bucket_boundaries()
KEY_NEG_INF = int(np.int32(np.array(-np.inf, np.float32).view(np.int32)) ^ np.int32(0x7FFFFFFF))
INT_MIN = -2 ** 31


def _sort_key(s):
    bits = pltpu.bitcast(s, I32)
    return bits ^ ((bits >> 31) & jnp.int32(0x7FFFFFFF))


def _dsa_kernel(T, ksel, q_ref, k_ref, v_ref, qiw_ref, kix_ref, relb_ref, g_ref, out_ref,
                lhs_s, wb_s, key_s, madd_s, bias_s, m_s, l_s, acc_s):
    TQ = QUERY_BLOCK
    H = ATTN_HEADS
    qb = pl.program_id(1)
    nch = qb + 1
    rows = lax.broadcasted_iota(I32, (TQ, TQ), 0)
    cols = lax.broadcasted_iota(I32, (TQ, TQ), 1)

    @pl.when((pl.program_id(0) == 0) & (qb == 0))
    def _():
        max_exact = N_BUCKETS // 2
        for off in range(2):
            d = rows - cols + off * TQ
            bucket = jnp.where(d < max_exact, jnp.maximum(d, 0), max_exact)
            for start in _BUCKET_STARTS:
                bucket = bucket + jnp.where(d >= start, 1, 0)
            for h in range(H):
                tile = jnp.zeros((TQ, TQ), F32)
                for bk in range(N_BUCKETS):
                    tile = jnp.where(bucket == bk, relb_ref[bk, h], tile)
                bias_s[h, off] = tile

    qi = qiw_ref[:, :IDX_HEADS * IDX_DIM] * (IDX_DIM ** -0.5)
    wi = qiw_ref[:, 576:576 + IDX_HEADS] * (IDX_HEADS ** -0.5)
    for h in range(IDX_HEADS):
        qh = qi[:, h * IDX_DIM:(h + 1) * IDX_DIM]
        hi = qh.astype(BF16)
        lo = (qh - hi.astype(F32)).astype(BF16)
        lhs_s[h] = jnp.concatenate([hi, hi, lo, jnp.zeros_like(hi)], axis=-1)
        wb_s[h] = jnp.broadcast_to(wi[:, h:h + 1], (TQ, LANES))

    def scores(j):
        kx = kix_ref[pl.ds(pl.multiple_of(j * TQ, TQ), TQ), :]
        acc = jnp.zeros((TQ, TQ), F32)
        for h in range(IDX_HEADS):
            d = lax.dot_general(lhs_s[h], kx, NT, preferred_element_type=F32)
            acc = acc + wb_s[h] * jnp.maximum(d, 0.0)
        return acc

    def score_body(j, c):
        key_s[j] = _sort_key(scores(j))
        return c

    lax.fori_loop(0, qb, score_body, 0)
    key_s[qb] = _sort_key(jnp.where(cols <= rows, scores(qb), -jnp.inf))

    def count(pred):
        def body(j, acc):
            return acc + jnp.where(pred(key_s[j]), 1, 0)
        acc = lax.fori_loop(0, nch, body, jnp.zeros((TQ, LANES), I32))
        return jnp.broadcast_to(jnp.sum(acc, axis=-1, keepdims=True), (TQ, LANES))

    zero = jnp.zeros((TQ, LANES), I32)
    base = jnp.where(count(lambda kk: kk >= zero) >= ksel, zero, jnp.full((TQ, LANES), INT_MIN, I32))

    def bit_body(i, base):
        cand = base | jnp.left_shift(jnp.int32(1), 30 - i)
        return jnp.where(count(lambda kk: kk >= cand) >= ksel, cand, base)

    thr = lax.fori_loop(0, 31, bit_body, base)
    n_gt = count(lambda kk: kk > thr)
    n_ge = count(lambda kk: kk >= thr)
    need = ksel - n_gt
    tie_rows = jnp.where((n_ge > ksel) & (thr > KEY_NEG_INF), 1, 0)
    any_tie = jnp.max(tie_rows) > 0

    @pl.when(jnp.logical_not(any_tie))
    def _():
        def body(j, c):
            kk = key_s[j]
            madd_s[j] = jnp.where((kk >= thr) & (kk > KEY_NEG_INF), 0.0, NEG_BIG)
            return c
        lax.fori_loop(0, nch, body, 0)

    @pl.when(any_tie)
    def _():
        upper = jnp.where(rows <= cols, 1.0, 0.0).astype(BF16)
        need_f = need.astype(F32)

        def body(j, run):
            kk = key_s[j]
            eq = kk == thr
            eqf = jnp.where(eq, 1.0, 0.0)
            rank = run + jnp.dot(eqf.astype(BF16), upper, preferred_element_type=F32)
            sel = ((kk > thr) | (eq & (rank <= need_f))) & (kk > KEY_NEG_INF)
            madd_s[j] = jnp.where(sel, 0.0, NEG_BIG)
            return run + jnp.broadcast_to(jnp.sum(eqf, axis=-1, keepdims=True), (TQ, LANES))
        lax.fori_loop(0, nch, body, jnp.zeros((TQ, LANES), F32))

    m_s[...] = jnp.full(m_s.shape, NEG_BIG, F32)
    l_s[...] = jnp.zeros(l_s.shape, F32)
    acc_s[...] = jnp.zeros(acc_s.shape, F32)
    q = q_ref[...]
    scale = HEAD_DIM ** -0.5

    def attend(j, bias_of):
        ks = pl.ds(pl.multiple_of(j * TQ, TQ), TQ)
        kb = k_ref[ks, :]
        vb = v_ref[ks, :]
        ma = madd_s[j]
        for h in range(H):
            hs = slice(h * HEAD_DIM, (h + 1) * HEAD_DIM)
            s = lax.dot_general(q[:, hs], kb[:, hs], NT, preferred_element_type=F32)
            s = s * scale + bias_of(h) + ma
            m_old = m_s[h]
            m_new = jnp.maximum(m_old, jnp.max(s, axis=-1, keepdims=True))
            alpha = jnp.exp(m_old - m_new)
            p = jnp.exp(s - m_new)
            l_s[h] = alpha * l_s[h] + jnp.sum(p, axis=-1, keepdims=True)
            acc_s[:, hs] = alpha * acc_s[:, hs] + jnp.dot(p.astype(BF16), vb[:, hs], preferred_element_type=F32)
            m_s[h] = m_new

    def far_body(j, c):
        attend(j, lambda h: relb_ref[N_BUCKETS - 1, h])
        return c

    lax.fori_loop(0, jnp.maximum(qb - 1, 0), far_body, 0)

    @pl.when(qb >= 1)
    def _():
        attend(qb - 1, lambda h: bias_s[h, 1])

    attend(qb, lambda h: bias_s[h, 0])

    outs = []
    for h in range(H):
        hs = slice(h * HEAD_DIM, (h + 1) * HEAD_DIM)
        o = acc_s[:, hs] / l_s[h]
        ms = jnp.mean(o * o, axis=-1, keepdims=True)
        outs.append(o * lax.rsqrt(ms + RMS_EPS))
    out_ref[...] = (jnp.concatenate(outs, axis=-1) * g_ref[...]).astype(out_ref.dtype)


def _dsa(qkv, qiw, kix, rel_bias, g, B, T):
    N = qkv.shape[0]
    TQ = QUERY_BLOCK
    nq = T // TQ
    A = ATTN_HEADS * HEAD_DIM
    ksel = min(TOPK_MAX, T // 4)
    assert MAX_DISTANCE <= TQ + 1
    return pl.pallas_call(
        functools.partial(_dsa_kernel, T, ksel),
        grid=(B, nq),
        in_specs=[pl.BlockSpec((TQ, A), lambda b, i: (b * nq + i, 0)),
                  pl.BlockSpec((T, A), lambda b, i: (b, 1)),
                  pl.BlockSpec((T, A), lambda b, i: (b, 2)),
                  pl.BlockSpec((TQ, IDX_COLS), lambda b, i: (b * nq + i, 0)),
                  pl.BlockSpec((T, 256), lambda b, i: (b, 0)),
                  pl.BlockSpec(memory_space=pltpu.SMEM),
                  pl.BlockSpec((1, A), lambda b, i: (0, 0))],
        out_specs=pl.BlockSpec((TQ, A), lambda b, i: (b * nq + i, 0)),
        out_shape=jax.ShapeDtypeStruct((N, A), BF16),
        scratch_shapes=[pltpu.VMEM((IDX_HEADS, TQ, 256), BF16),
                        pltpu.VMEM((IDX_HEADS, TQ, LANES), F32),
                        pltpu.VMEM((nq, TQ, TQ), I32),
                        pltpu.VMEM((nq, TQ, TQ), F32),
                        pltpu.VMEM((ATTN_HEADS, 2, TQ, TQ), F32),
                        pltpu.VMEM((ATTN_HEADS, TQ, 1), F32),
                        pltpu.VMEM((ATTN_HEADS, TQ, 1), F32),
                        pltpu.VMEM((TQ, A), F32)],
        compiler_params=_params("arbitrary", "arbitrary"),
        name="dsa",
    )(qkv, qkv, qkv, qiw, kix, rel_bias, g.reshape(1, A))


def _outproj_kernel(x_ref, rw_ref, att_ref, w_ref, gt_ref, o_ref):
    R = rw_ref.shape[1]
    mixed = (jnp.dot(rw_ref[...], w_ref[:R, :], preferred_element_type=F32)
             + jnp.dot(att_ref[...], w_ref[R:, :], preferred_element_type=F32))
    o_ref[...] = x_ref[...] + gt_ref[...] * mixed


def _outproj(x2d, rw, att, w, gt, T, tm=512):
    N, D = x2d.shape
    nb = T // tm
    row = lambda i: (i, 0)
    return pl.pallas_call(
        _outproj_kernel,
        grid=(N // tm,),
        in_specs=[pl.BlockSpec((tm, D), row),
                  pl.BlockSpec((tm, rw.shape[1]), row),
                  pl.BlockSpec((tm, att.shape[1]), row),
                  pl.BlockSpec(w.shape, lambda i: (0, 0)),
                  pl.BlockSpec((None, 1, D), lambda i: (i // nb, 0, 0))],
        out_specs=pl.BlockSpec((tm, D), row),
        out_shape=jax.ShapeDtypeStruct((N, D), F32),
        compiler_params=_params("arbitrary"),
        name="outproj",
    )(x2d, rw, att, w, gt)


def _mlp_kernel(final, x_ref, g_ref, sc_ref, sh_ref, gt_ref, w1_ref, w2_ref, fg_ref, o_ref, h_s, acc_s):
    j = pl.program_id(1)

    @pl.when(j == 0)
    def _():
        h_s[...] = _norm_mod(x_ref[...], g_ref[...], sc_ref[...], sh_ref[...]).astype(BF16)
        acc_s[...] = jnp.zeros_like(acc_s)

    u = jnp.dot(h_s[...], w1_ref[...], preferred_element_type=F32)
    u = jnp.square(jnp.maximum(u, 0.0))
    acc_s[...] += jnp.dot(u.astype(BF16), w2_ref[...], preferred_element_type=F32)

    @pl.when(j == pl.num_programs(1) - 1)
    def _():
        y = x_ref[...] + gt_ref[...] * acc_s[...]
        if final:
            ms = jnp.mean(y * y, axis=-1, keepdims=True)
            y = y * lax.rsqrt(ms + RMS_EPS) * fg_ref[...]
        o_ref[...] = y


def _mlp(x2d, g, sc, sh, gt, w1, w2, final_g, final, T, tm=512, fc=512):
    N, D = x2d.shape
    F = w1.shape[1]
    nb = T // tm
    row = lambda i, j: (i, 0)
    per_b = lambda i, j: (i // nb, 0, 0)
    const = lambda i, j: (0, 0)
    return pl.pallas_call(
        functools.partial(_mlp_kernel, final),
        grid=(N // tm, F // fc),
        in_specs=[pl.BlockSpec((tm, D), row),
                  pl.BlockSpec((1, D), const),
                  pl.BlockSpec((None, 1, D), per_b),
                  pl.BlockSpec((None, 1, D), per_b),
                  pl.BlockSpec((None, 1, D), per_b),
                  pl.BlockSpec((D, fc), lambda i, j: (0, j)),
                  pl.BlockSpec((fc, D), lambda i, j: (j, 0)),
                  pl.BlockSpec((1, D), const)],
        out_specs=pl.BlockSpec((tm, D), row),
        out_shape=jax.ShapeDtypeStruct((N, D), F32),
        scratch_shapes=[pltpu.VMEM((tm, D), BF16), pltpu.VMEM((tm, D), F32)],
        compiler_params=_params("arbitrary", "arbitrary"),
        name="mlp",
    )(x2d, g.reshape(1, D), sc, sh, gt, w1, w2, final_g.reshape(1, D))


def _pad_cols(w, n):
    return jnp.pad(w, ((0, 0), (0, n - w.shape[1])))


def _split_bf16(w):
    hi = w.astype(BF16)
    return hi, (w - hi.astype(F32)).astype(BF16)


def _in_weights(l, w_in, mu_lora, decay_a, iclr_a, gate_a, vres_mu, vres_a):
    D = w_in.shape[1]
    R = RWKV_HEADS * HEAD_DIM
    w = w_in[l]
    mats = [(decay_a[l], mu_lora[l, 0]), (iclr_a[l], mu_lora[l, 1]), (gate_a[l], mu_lora[l, 2])]
    if l > 0:
        mats.append((vres_a[l - 1], vres_mu[l - 1]))
    now = _pad_cols(jnp.concatenate([a * (1.0 - mu)[:, None] for a, mu in mats], axis=1), LORA_PAD)
    prev = _pad_cols(jnp.concatenate([a * mu[:, None] for a, mu in mats], axis=1), LORA_PAD)
    wa = jnp.concatenate([w[:, :3 * R], now, prev, w[:, 3 * R:6 * R]], axis=1).astype(BF16)
    wbh, wbl = _split_bf16(_pad_cols(w[:, 6 * R:], IDX_COLS))
    return wa, wbh, wbl


def kernel(x, c, w_ada, b_ada, norm1_g, norm2_g, w_in, mu_rkv, mu_lora, decay_w0, decay_a, decay_b, iclr_a0, iclr_a, iclr_b, gate_a, gate_b, k_k, k_a, r_k, lnx_g, lnx_b, vres_mu, vres_v0, vres_a, vres_b, attn_out_g, rel_bias, w_out, w_mlp1, w_mlp2, final_g):
    B, T, D = x.shape
    depth = w_in.shape[0]
    mod = _adaln(c, w_ada, b_ada)
    mod = mod.reshape(depth, B, 6, 1, D).transpose(0, 2, 1, 3, 4)
    x2d = x.reshape(B * T, D)
    v_first = None
    for l in range(depth):
        sh1, sc1, gt1, sh2, sc2, gt2 = (mod[l, i] for i in range(6))
        wa, wbh, wbl = _in_weights(l, w_in, mu_lora, decay_a, iclr_a, gate_a, vres_mu, vres_a)
        rkvl, qkv, qiw, kix = _inproj(x2d, norm1_g[l], sc1, sh1, wa, wbh, wbl, T)
        p = dict(mu_rkv=mu_rkv[l], decay_w0=decay_w0[l], decay_b=decay_b[l], iclr_a0=iclr_a0[l],
                 iclr_b=iclr_b[l], gate_b=gate_b[l], k_k=k_k[l], k_a=k_a[l], r_k=r_k[l],
                 lnx_g=lnx_g[l], lnx_b=lnx_b[l])
        if l > 0:
            p.update(vres_v0=vres_v0[l - 1], vres_b=vres_b[l - 1])
        rw, v_first = _rwkv(rkvl, v_first, p, B, T)
        att = _dsa(qkv, qiw, kix, rel_bias, attn_out_g[l], B, T)
        x2d = _outproj(x2d, rw, att, w_out[l].astype(BF16), gt1, T)
        x2d = _mlp(x2d, norm2_g[l], sc2, sh2, gt2, w_mlp1[l].astype(BF16), w_mlp2[l].astype(BF16),
                   final_g, l == depth - 1, T)
    return x2d.reshape(B, T, D)
```

```python
import functools
import math

import numpy as np
import jax
import jax.numpy as jnp
from jax import lax
from jax.experimental import pallas as pl
from jax.experimental.pallas import tpu as pltpu

F32 = jnp.float32
BF16 = jnp.bfloat16
I32 = jnp.int32

HEAD_DIM = 64
RWKV_HEADS = 8
ATTN_HEADS = 8
IDX_HEADS = 8
IDX_DIM = 64
TOPK_MAX = 256
QUERY_BLOCK = 128
N_BUCKETS = 32
MAX_DISTANCE = 128
RMS_EPS = 1e-6
LNX_EPS = 64e-5

LANES = 128
VMEM_LIMIT = 56 * 1024 * 1024
RWKV_CHUNK = 64
DSA_BLOCK = 256
LORA_PAD = 384
IDX_COLS = 640
NEG_BIG = -1e30
HI = lax.Precision.HIGHEST
NT = (((1,), (1,)), ((), ()))
TN = (((0,), (0,)), ((), ()))


def _bdot(a, b):
    return jnp.dot(a.astype(BF16), b.astype(BF16), preferred_element_type=F32)


def _hdot(a, b):
    return jnp.dot(a, b, precision=HI, preferred_element_type=F32)


def _params(*sem):
    return pltpu.CompilerParams(dimension_semantics=sem, vmem_limit_bytes=VMEM_LIMIT)


def _adaln_kernel(c_ref, w_ref, b_ref, o_ref):
    c = c_ref[...]
    c_act = c * jax.nn.sigmoid(c)
    o_ref[...] = _hdot(c_act, w_ref[...]) + b_ref[...]


def _adaln(c, w_ada, b_ada):
    L, D, D6 = w_ada.shape
    B = c.shape[0]
    cb = 1024
    return pl.pallas_call(
        _adaln_kernel,
        grid=(L, D6 // cb),
        in_specs=[pl.BlockSpec((B, D), lambda l, j: (0, 0)),
                  pl.BlockSpec((None, D, cb), lambda l, j: (l, 0, j)),
                  pl.BlockSpec((None, 1, cb), lambda l, j: (l, 0, j))],
        out_specs=pl.BlockSpec((None, B, cb), lambda l, j: (l, 0, j)),
        out_shape=jax.ShapeDtypeStruct((L, B, D6), F32),
        compiler_params=_params("arbitrary", "arbitrary"),
        name="adaln",
    )(c, w_ada, b_ada.reshape(L, 1, D6))


def _norm_mod(x, g, sc, sh):
    ms = jnp.mean(x * x, axis=-1, keepdims=True)
    return (x * lax.rsqrt(ms + RMS_EPS) * g) * (1.0 + sc) + sh


def _inproj_kernel(x_ref, g_ref, sc_ref, sh_ref, wa_ref, wvt_ref, wbh_ref, wbl_ref,
                   rkvl_ref, qk_ref, vt_ref, qiw_ref, kix_ref):
    h = _norm_mod(x_ref[...], g_ref[...], sc_ref[...], sh_ref[...])
    hb = h.astype(BF16)
    hl = (h - hb.astype(F32)).astype(BF16)
    pa = jnp.dot(hb, wa_ref[...], preferred_element_type=F32)
    nr = rkvl_ref.shape[1]
    rkvl_ref[...] = pa[:, :nr]
    qk_ref[...] = pa[:, nr:].astype(BF16)
    vt_ref[...] = lax.dot_general(wvt_ref[...], hb, NT, preferred_element_type=F32).astype(BF16)
    pb = (jnp.dot(hb, wbh_ref[...], preferred_element_type=F32)
          + jnp.dot(hb, wbl_ref[...], preferred_element_type=F32)
          + jnp.dot(hl, wbh_ref[...], preferred_element_type=F32))
    qiw_ref[...] = pb
    ki = pb[:, 512:576]
    kh = ki.astype(BF16)
    kl = (ki - kh.astype(F32)).astype(BF16)
    kix_ref[...] = jnp.concatenate([kh, kl, kh, jnp.zeros_like(kh)], axis=-1)


def _inproj(x2d, g, sc, sh, wa, wvt, wbh, wbl, T):
    N, D = x2d.shape
    tm = DSA_BLOCK
    na = wa.shape[1]
    A = wvt.shape[0]
    nr = 3 * 512 + 2 * LORA_PAD
    nb = T // tm
    row = lambda i: (i, 0)
    per_b = lambda i: (i // nb, 0, 0)
    const = lambda i: (0, 0)
    return pl.pallas_call(
        _inproj_kernel,
        grid=(N // tm,),
        in_specs=[pl.BlockSpec((tm, D), row),
                  pl.BlockSpec((1, D), const),
                  pl.BlockSpec((None, 1, D), per_b),
                  pl.BlockSpec((None, 1, D), per_b),
                  pl.BlockSpec((D, na), const),
                  pl.BlockSpec((A, D), const),
                  pl.BlockSpec((D, IDX_COLS), const),
                  pl.BlockSpec((D, IDX_COLS), const)],
        out_specs=[pl.BlockSpec((tm, nr), row),
                   pl.BlockSpec((tm, na - nr), row),
                   pl.BlockSpec((None, A, tm), lambda i: (i, 0, 0)),
                   pl.BlockSpec((tm, IDX_COLS), row),
                   pl.BlockSpec((tm, 256), row)],
        out_shape=[jax.ShapeDtypeStruct((N, nr), F32),
                   jax.ShapeDtypeStruct((N, na - nr), BF16),
                   jax.ShapeDtypeStruct((N // tm, A, tm), BF16),
                   jax.ShapeDtypeStruct((N, IDX_COLS), F32),
                   jax.ShapeDtypeStruct((N, 256), BF16)],
        compiler_params=_params("arbitrary"),
        name="inproj",
    )(x2d, g.reshape(1, D), sc, sh, wa, wvt, wbh, wbl)


def _head_ones(n):
    r = lax.broadcasted_iota(I32, (n, n), 0) // HEAD_DIM
    c = lax.broadcasted_iota(I32, (n, n), 1) // HEAD_DIM
    return jnp.where(r == c, 1.0, 0.0).astype(F32)


def _rwkv_kernel(has_vres, TT, *refs):
    C = RWKV_CHUNK
    R = RWKV_HEADS * HEAD_DIM
    if has_vres:
        (rkvl_ref, vf_ref, mu_ref, w0_ref, db_ref, a0_ref, ib_ref, gb_ref, kk_ref, ka_ref,
         rk_ref, lg_ref, lb_ref, v0_ref, vb_ref, out_ref,
         prev_ref, S_ref, r_s, k_s, v_s, kk_s, b_s, lw_s, y_s) = refs
    else:
        (rkvl_ref, mu_ref, w0_ref, db_ref, a0_ref, ib_ref, gb_ref, kk_ref, ka_ref,
         rk_ref, lg_ref, lb_ref, out_ref, vfo_ref,
         prev_ref, S_ref, r_s, k_s, v_s, kk_s, b_s, lw_s, y_s) = refs

    @pl.when(pl.program_id(1) == 0)
    def _():
        prev_ref[...] = jnp.zeros_like(prev_ref)
        S_ref[...] = jnp.zeros_like(S_ref)

    row0 = lax.broadcasted_iota(I32, (TT, 1), 0) == 0
    prev = prev_ref[...]

    def shift(z, p):
        return jnp.where(row0, p, pltpu.roll(z, 1, 0))

    rkv = rkvl_ref[:, :3 * R]
    p1 = rkvl_ref[:, 3 * R:3 * R + LORA_PAD]
    p2 = rkvl_ref[:, 3 * R + LORA_PAD:]
    rkv_new = rkv + (shift(rkv, prev[:, :3 * R]) - rkv) * mu_ref[...]
    lora = p1 + shift(p2, prev[:, 3 * R:])
    prev_ref[...] = jnp.concatenate([rkv[TT - 1:TT, :], p2[TT - 1:TT, :]], axis=-1)

    r = rkv_new[:, :R]
    k = rkv_new[:, R:2 * R]
    v = rkv_new[:, 2 * R:]
    ones_h = _head_ones(R)

    wlog = w0_ref[...] + _bdot(jnp.tanh(lora[:, 0:64]), db_ref[...])
    z = -wlog
    wlog = -(jnp.maximum(z, 0.0) + jnp.log1p(jnp.exp(-jnp.abs(z)))) - 0.5
    lw_s[...] = -jnp.exp(wlog)
    a = jax.nn.sigmoid(a0_ref[...] + _bdot(lora[:, 64:128], ib_ref[...]))
    g = _bdot(jax.nn.sigmoid(lora[:, 128:256]), gb_ref[...])
    kkr = k * kk_ref[...]
    n2 = _hdot(kkr * kkr, ones_h)
    kk = kkr / jnp.maximum(jnp.sqrt(n2), 1e-12)
    k = k * (1.0 + (a - 1.0) * ka_ref[...])
    if has_vres:
        v = v + (vf_ref[...] - v) * jax.nn.sigmoid(v0_ref[...] + _bdot(lora[:, 256:288], vb_ref[...]))
    else:
        vfo_ref[...] = v
    r_s[...] = r
    k_s[...] = k
    v_s[...] = v
    kk_s[...] = kk
    b_s[...] = kk * a

    ri = lax.broadcasted_iota(I32, (C, C), 0)
    ci = lax.broadcasted_iota(I32, (C, C), 1)
    incl = ci <= ri
    strict = ci < ri
    tri = jnp.where(incl, 1.0, 0.0).astype(F32)
    eye = jnp.where(ci == ri, 1.0, 0.0).astype(F32)
    dmask = ci == ri

    def chunk(c, carry):
        sl = pl.ds(pl.multiple_of(c * C, C), C)
        lw = lw_s[sl, :]
        cum = _hdot(tri, lw)
        cl = cum[C - 1:C, :]
        rd = r_s[sl, :] * jnp.exp(cum)
        kkd = kk_s[sl, :] * jnp.exp(cum - lw)
        pinv = jnp.exp(-cum)
        kc_all = k_s[sl, :]
        b_all = b_s[sl, :]
        kt = kc_all * pinv
        bt = b_all * pinv
        pc = jnp.exp(cl - cum)
        kc = kc_all * pc
        bc = b_all * pc
        pl_last = jnp.exp(cl)
        vv = v_s[sl, :]
        ys = []
        for h in range(RWKV_HEADS):
            hs = slice(h * HEAD_DIM, (h + 1) * HEAD_DIM)
            rd_h, kkd_h, kt_h, bt_h, kc_h, bc_h, v_h = (
                rd[:, hs], kkd[:, hs], kt[:, hs], bt[:, hs], kc[:, hs], bc[:, hs], vv[:, hs])
            akk = jnp.where(strict, lax.dot_general(kkd_h, kt_h, NT, precision=HI, preferred_element_type=F32), 0.0)
            akb = jnp.where(strict, lax.dot_general(kkd_h, bt_h, NT, precision=HI, preferred_element_type=F32), 0.0)
            ark = jnp.where(incl, lax.dot_general(rd_h, kt_h, NT, precision=HI, preferred_element_type=F32), 0.0)
            arb = jnp.where(incl, lax.dot_general(rd_h, bt_h, NT, precision=HI, preferred_element_type=F32), 0.0)
            npow = -akb
            x = eye + npow
            for _ in range(int(math.log2(C)) - 1):
                npow = _hdot(npow, npow)
                x = x + _hdot(x, npow)
            w1 = _hdot(x, kkd_h)
            u0 = _hdot(x, _hdot(akk, v_h))
            m_mat = (jnp.where(dmask, jnp.broadcast_to(pl_last[:, hs], (HEAD_DIM, HEAD_DIM)), 0.0)
                     - lax.dot_general(bc_h, w1, TN, precision=HI, preferred_element_type=F32))
            g_mat = (lax.dot_general(kc_h, v_h, TN, precision=HI, preferred_element_type=F32)
                     - lax.dot_general(bc_h, u0, TN, precision=HI, preferred_element_type=F32))
            q_mat = rd_h - _hdot(arb, w1)
            y0 = _hdot(ark, v_h) - _hdot(arb, u0)
            s_old = S_ref[h]
            ys.append(_hdot(q_mat, s_old) + y0)
            S_ref[h] = _hdot(m_mat, s_old) + g_mat
        y_s[sl, :] = jnp.concatenate(ys, axis=-1)
        return carry

    lax.fori_loop(0, TT // C, chunk, 0)

    y = y_s[...]
    inv_n = 1.0 / HEAD_DIM
    mean = _hdot(y, ones_h) * inv_n
    yc = y - mean
    var = _hdot(yc * yc, ones_h) * inv_n
    yn = yc * lax.rsqrt(var + LNX_EPS) * lg_ref[...] + lb_ref[...]
    r = r_s[...]
    bonus = _hdot(r * k_s[...] * rk_ref[...], ones_h) * v_s[...]
    out_ref[...] = ((yn + bonus) * g).astype(out_ref.dtype)


def _rwkv(rkvl, v_first, p, B, T, TT=256):
    N, nc = rkvl.shape
    R = RWKV_HEADS * HEAD_DIM
    nt = T // TT
    has_vres = v_first is not None
    row = lambda b, i: (b * nt + i, 0)
    const = lambda b, i: (0, 0)
    vec = lambda a: a.reshape(1, -1)
    ins = [rkvl]
    specs = [pl.BlockSpec((TT, nc), row)]
    if has_vres:
        ins.append(v_first)
        specs.append(pl.BlockSpec((TT, R), row))
    small = [vec(p["mu_rkv"]), vec(p["decay_w0"]), p["decay_b"].astype(BF16), vec(p["iclr_a0"]),
             p["iclr_b"].astype(BF16), p["gate_b"].astype(BF16), vec(p["k_k"]), vec(p["k_a"]),
             vec(p["r_k"]), vec(p["lnx_g"]), vec(p["lnx_b"])]
    if has_vres:
        small += [vec(p["vres_v0"]), p["vres_b"].astype(BF16)]
    ins += small
    specs += [pl.BlockSpec(a.shape, const) for a in small]
    out_shape = [jax.ShapeDtypeStruct((N, R), BF16)]
    out_specs = [pl.BlockSpec((TT, R), row)]
    if not has_vres:
        out_shape.append(jax.ShapeDtypeStruct((N, R), F32))
        out_specs.append(pl.BlockSpec((TT, R), row))
    scratch = [pltpu.VMEM((1, 3 * R + LORA_PAD), F32),
               pltpu.VMEM((RWKV_HEADS, HEAD_DIM, HEAD_DIM), F32)]
    scratch += [pltpu.VMEM((TT, R), F32) for _ in range(7)]
    res = pl.pallas_call(
        functools.partial(_rwkv_kernel, has_vres, TT),
        grid=(B, nt),
        in_specs=specs,
        out_specs=out_specs,
        out_shape=out_shape,
        scratch_shapes=scratch,
        compiler_params=_params("arbitrary", "arbitrary"),
        name="rwkv",
    )(*ins)
    if has_vres:
        return res[0], v_first
    return res[0], res[1]


def _bucket_boundaries():
    max_exact = N_BUCKETS // 2
    d = np.arange(0, 4 * MAX_DISTANCE, dtype=np.int64)
    nf = np.maximum(d, 1).astype(np.float32)
    large = max_exact + (np.log(nf / np.float32(max_exact)) / np.float32(math.log(MAX_DISTANCE / max_exact))
                         * np.float32(N_BUCKETS - max_exact)).astype(np.int32)
    large = np.minimum(large, N_BUCKETS - 1)
    bucket = np.where(d < max_exact, d, large)
    return [int(np.argmax(bucket >= j)) for j in range(max_exact + 1, N_BUCKETS)]


_BUCKET_STARTS = _bucket_boundaries()
KEY_NEG_INF = int(np.int32(np.array(-np.inf, np.float32).view(np.int32)) ^ np.int32(0x7FFFFFFF))
INT_MIN = -2 ** 31


def _sort_key(s):
    bits = pltpu.bitcast(s, I32)
    return bits ^ ((bits >> 31) & jnp.int32(0x7FFFFFFF))


def _dsa_kernel(T, ksel, q_ref, k_ref, vt_ref, qiw_ref, kix_ref, relb_ref, g_ref, out_ref,
                lhs_s, qpad_s, key_s, madd_s, bias_s, m_s, l_s, acc_s, s_s, p_s, cmax_s, alpha_s):
    TQ = TK = DSA_BLOCK
    H = ATTN_HEADS
    qb = pl.program_id(1)
    nch = qb + 1
    rows = lax.broadcasted_iota(I32, (TK, TQ), 0)
    cols = lax.broadcasted_iota(I32, (TK, TQ), 1)

    @pl.when((pl.program_id(0) == 0) & (qb == 0))
    def _():
        max_exact = N_BUCKETS // 2
        for off in range(2):
            d = cols - rows + off * TK
            bucket = jnp.where(d < max_exact, jnp.maximum(d, 0), max_exact)
            for start in _BUCKET_STARTS:
                bucket = bucket + jnp.where(d >= start, 1, 0)
            for h in range(H):
                far = relb_ref[N_BUCKETS - 1, h]
                tile = jnp.zeros((TK, TQ), F32)
                for bk in range(N_BUCKETS - 1):
                    tile = jnp.where(bucket == bk, relb_ref[bk, h] - far, tile)
                bias_s[h, off] = tile

    qi_t = (qiw_ref[:, :IDX_HEADS * IDX_DIM] * (IDX_DIM ** -0.5)).T
    for h in range(IDX_HEADS):
        qh = qi_t[h * IDX_DIM:(h + 1) * IDX_DIM, :]
        hi = qh.astype(BF16)
        lo = (qh - hi.astype(F32)).astype(BF16)
        lhs_s[h] = jnp.concatenate([hi, hi, lo, jnp.zeros_like(hi)], axis=0)
    w_t = qiw_ref[:, 512:640].T[64:64 + IDX_HEADS, :] * (IDX_HEADS ** -0.5)

    q_t = (q_ref[...].astype(F32) * (HEAD_DIM ** -0.5)).T
    zeros_h = jnp.zeros((HEAD_DIM, TQ), BF16)
    for h in range(H):
        qh = q_t[h * HEAD_DIM:(h + 1) * HEAD_DIM, :].astype(BF16)
        qpad_s[h] = jnp.concatenate([qh, zeros_h] if h % 2 == 0 else [zeros_h, qh], axis=0)

    def scores(j):
        kx = kix_ref[pl.ds(pl.multiple_of(j * TK, TK), TK), :]
        acc = jnp.zeros((TK, TQ), F32)
        for h in range(IDX_HEADS):
            d = jnp.dot(kx, lhs_s[h], preferred_element_type=F32)
            acc = acc + w_t[h:h + 1, :] * jnp.maximum(d, 0.0)
        return acc

    def score_body(j, c):
        key_s[j] = _sort_key(scores(j))
        return c

    lax.fori_loop(0, qb, score_body, 0)
    key_s[qb] = _sort_key(jnp.where(rows <= cols, scores(qb), -jnp.inf))

    def count(pred):
        def body(j, acc):
            hit = jnp.where(pred(key_s[j]), 1, 0)
            return acc + jnp.sum(hit.reshape(TK // 8, 8, TQ), axis=0)
        acc = lax.fori_loop(0, nch, body, jnp.zeros((8, TQ), I32))
        return jnp.sum(acc, axis=0, keepdims=True)

    zero = jnp.zeros((1, TQ), I32)
    base = jnp.where(count(lambda kk: kk >= zero) >= ksel, zero, jnp.full((1, TQ), INT_MIN, I32))

    def bit_body(i, base):
        cand = base | jnp.left_shift(jnp.int32(1), 30 - i)
        return jnp.where(count(lambda kk: kk >= cand) >= ksel, cand, base)

    thr = lax.fori_loop(0, 31, bit_body, base)
    n_gt = count(lambda kk: kk > thr)
    n_ge = count(lambda kk: kk >= thr)
    need = ksel - n_gt
    tie_rows = jnp.where((n_ge > ksel) & (thr > KEY_NEG_INF), 1, 0)
    any_tie = jnp.max(tie_rows) > 0

    @pl.when(jnp.logical_not(any_tie))
    def _():
        def body(j, c):
            kk = key_s[j]
            madd_s[j] = jnp.where((kk >= thr) & (kk > KEY_NEG_INF), 0.0, NEG_BIG)
            return c
        lax.fori_loop(0, nch, body, 0)

    @pl.when(any_tie)
    def _():
        kr = lax.broadcasted_iota(I32, (TK, TK), 0)
        kc = lax.broadcasted_iota(I32, (TK, TK), 1)
        lower = jnp.where(kc <= kr, 1.0, 0.0).astype(BF16)
        need_f = need.astype(F32)

        def body(j, run):
            kk = key_s[j]
            eq = kk == thr
            eqf = jnp.where(eq, 1.0, 0.0)
            rank = run + jnp.dot(lower, eqf.astype(BF16), preferred_element_type=F32)
            sel = ((kk > thr) | (eq & (rank <= need_f))) & (kk > KEY_NEG_INF)
            madd_s[j] = jnp.where(sel, 0.0, NEG_BIG)
            return run + jnp.sum(eqf, axis=0, keepdims=True)
        lax.fori_loop(0, nch, body, jnp.zeros((1, TQ), F32))

    m_s[...] = jnp.full(m_s.shape, NEG_BIG, F32)
    l_s[...] = jnp.zeros(l_s.shape, F32)
    acc_s[...] = jnp.zeros(acc_s.shape, F32)

    def attend(j, bias_of):
        ks = pl.ds(pl.multiple_of(j * TK, TK), TK)
        ma = madd_s[j]
        for h in range(H):
            kp = k_ref[ks, LANES * (h // 2):LANES * (h // 2 + 1)]
            s = jnp.dot(kp, qpad_s[h], preferred_element_type=F32) + ma
            bias = bias_of(h)
            if bias is not None:
                s = s + bias
            s_s[h] = s
            cmax_s[h] = jnp.max(s.reshape(TK // 8, 8, TQ), axis=0)
        for h in range(H):
            m_old = m_s[h]
            m_new = jnp.maximum(m_old, jnp.max(cmax_s[h], axis=0, keepdims=True))
            alpha_s[h] = jnp.exp(m_old - m_new)
            m_s[h] = m_new
            p = jnp.exp(s_s[h] - m_new)
            l_s[h] = alpha_s[h] * l_s[h] + jnp.sum(p, axis=0, keepdims=True)
            p_s[h] = p.astype(BF16)
        for h in range(H):
            hs = slice(h * HEAD_DIM, (h + 1) * HEAD_DIM)
            pv = jnp.dot(vt_ref[j, hs, :], p_s[h], preferred_element_type=F32)
            acc_s[hs, :] = alpha_s[h] * acc_s[hs, :] + pv

    def far_body(j, c):
        attend(j, lambda h: None)
        return c

    lax.fori_loop(0, jnp.maximum(qb - 1, 0), far_body, 0)

    @pl.when(qb >= 1)
    def _():
        attend(qb - 1, lambda h: bias_s[h, 1])

    attend(qb, lambda h: bias_s[h, 0])

    outs = []
    for h in range(H):
        hs = slice(h * HEAD_DIM, (h + 1) * HEAD_DIM)
        o = acc_s[hs, :] / l_s[h]
        ms = jnp.mean(o * o, axis=0, keepdims=True)
        outs.append(o * lax.rsqrt(ms + RMS_EPS))
    out_ref[...] = (jnp.concatenate(outs, axis=0).T * g_ref[...]).astype(out_ref.dtype)


def _dsa(qk, vt, qiw, kix, rel_bias, g, B, T):
    N = qk.shape[0]
    TQ = DSA_BLOCK
    nq = T // TQ
    A = ATTN_HEADS * HEAD_DIM
    ksel = min(TOPK_MAX, T // 4)
    assert MAX_DISTANCE <= TQ + 1
    return pl.pallas_call(
        functools.partial(_dsa_kernel, T, ksel),
        grid=(B, nq),
        in_specs=[pl.BlockSpec((TQ, A), lambda b, i: (b * nq + i, 0)),
                  pl.BlockSpec((T, A), lambda b, i: (b, 1)),
                  pl.BlockSpec((nq, A, TQ), lambda b, i: (b, 0, 0)),
                  pl.BlockSpec((TQ, IDX_COLS), lambda b, i: (b * nq + i, 0)),
                  pl.BlockSpec((T, 256), lambda b, i: (b, 0)),
                  pl.BlockSpec(memory_space=pltpu.SMEM),
                  pl.BlockSpec((1, A), lambda b, i: (0, 0))],
        out_specs=pl.BlockSpec((TQ, A), lambda b, i: (b * nq + i, 0)),
        out_shape=jax.ShapeDtypeStruct((N, A), BF16),
        scratch_shapes=[pltpu.VMEM((IDX_HEADS, 256, TQ), BF16),
                        pltpu.VMEM((ATTN_HEADS, LANES, TQ), BF16),
                        pltpu.VMEM((nq, TQ, TQ), I32),
                        pltpu.VMEM((nq, TQ, TQ), F32),
                        pltpu.VMEM((ATTN_HEADS, 2, TQ, TQ), F32),
                        pltpu.VMEM((ATTN_HEADS, 1, TQ), F32),
                        pltpu.VMEM((ATTN_HEADS, 1, TQ), F32),
                        pltpu.VMEM((A, TQ), F32),
                        pltpu.VMEM((ATTN_HEADS, TQ, TQ), F32),
                        pltpu.VMEM((ATTN_HEADS, TQ, TQ), BF16),
                        pltpu.VMEM((ATTN_HEADS, 8, TQ), F32),
                        pltpu.VMEM((ATTN_HEADS, 1, TQ), F32)],
        compiler_params=_params("arbitrary", "arbitrary"),
        name="dsa",
    )(qk, qk, vt, qiw, kix, rel_bias, g.reshape(1, A))


def _outproj_kernel(x_ref, rw_ref, att_ref, w_ref, gt_ref, o_ref):
    R = rw_ref.shape[1]
    mixed = (jnp.dot(rw_ref[...], w_ref[:R, :], preferred_element_type=F32)
             + jnp.dot(att_ref[...], w_ref[R:, :], preferred_element_type=F32))
    o_ref[...] = x_ref[...] + gt_ref[...] * mixed


def _outproj(x2d, rw, att, w, gt, T, tm=512):
    N, D = x2d.shape
    nb = T // tm
    row = lambda i: (i, 0)
    return pl.pallas_call(
        _outproj_kernel,
        grid=(N // tm,),
        in_specs=[pl.BlockSpec((tm, D), row),
                  pl.BlockSpec((tm, rw.shape[1]), row),
                  pl.BlockSpec((tm, att.shape[1]), row),
                  pl.BlockSpec(w.shape, lambda i: (0, 0)),
                  pl.BlockSpec((None, 1, D), lambda i: (i // nb, 0, 0))],
        out_specs=pl.BlockSpec((tm, D), row),
        out_shape=jax.ShapeDtypeStruct((N, D), F32),
        compiler_params=_params("arbitrary"),
        name="outproj",
    )(x2d, rw, att, w, gt)


def _mlp_kernel(final, x_ref, g_ref, sc_ref, sh_ref, gt_ref, w1_ref, w2_ref, fg_ref, o_ref, h_s, acc_s):
    j = pl.program_id(1)

    @pl.when(j == 0)
    def _():
        h_s[...] = _norm_mod(x_ref[...], g_ref[...], sc_ref[...], sh_ref[...]).astype(BF16)
        acc_s[...] = jnp.zeros_like(acc_s)

    u = jnp.dot(h_s[...], w1_ref[...], preferred_element_type=F32)
    u = jnp.square(jnp.maximum(u, 0.0))
    acc_s[...] += jnp.dot(u.astype(BF16), w2_ref[...], preferred_element_type=F32)

    @pl.when(j == pl.num_programs(1) - 1)
    def _():
        y = x_ref[...] + gt_ref[...] * acc_s[...]
        if final:
            ms = jnp.mean(y * y, axis=-1, keepdims=True)
            y = y * lax.rsqrt(ms + RMS_EPS) * fg_ref[...]
        o_ref[...] = y


def _mlp(x2d, g, sc, sh, gt, w1, w2, final_g, final, T, tm=512, fc=512):
    N, D = x2d.shape
    F = w1.shape[1]
    nb = T // tm
    row = lambda i, j: (i, 0)
    per_b = lambda i, j: (i // nb, 0, 0)
    const = lambda i, j: (0, 0)
    return pl.pallas_call(
        functools.partial(_mlp_kernel, final),
        grid=(N // tm, F // fc),
        in_specs=[pl.BlockSpec((tm, D), row),
                  pl.BlockSpec((1, D), const),
                  pl.BlockSpec((None, 1, D), per_b),
                  pl.BlockSpec((None, 1, D), per_b),
                  pl.BlockSpec((None, 1, D), per_b),
                  pl.BlockSpec((D, fc), lambda i, j: (0, j)),
                  pl.BlockSpec((fc, D), lambda i, j: (j, 0)),
                  pl.BlockSpec((1, D), const)],
        out_specs=pl.BlockSpec((tm, D), row),
        out_shape=jax.ShapeDtypeStruct((N, D), F32),
        scratch_shapes=[pltpu.VMEM((tm, D), BF16), pltpu.VMEM((tm, D), F32)],
        compiler_params=_params("arbitrary", "arbitrary"),
        name="mlp",
    )(x2d, g.reshape(1, D), sc, sh, gt, w1, w2, final_g.reshape(1, D))


def _pad_cols(w, n):
    return jnp.pad(w, ((0, 0), (0, n - w.shape[1])))


def _split_bf16(w):
    hi = w.astype(BF16)
    return hi, (w - hi.astype(F32)).astype(BF16)


def _in_weights(l, w_in, mu_lora, decay_a, iclr_a, gate_a, vres_mu, vres_a):
    D = w_in.shape[1]
    R = RWKV_HEADS * HEAD_DIM
    w = w_in[l]
    mats = [(decay_a[l], mu_lora[l, 0]), (iclr_a[l], mu_lora[l, 1]), (gate_a[l], mu_lora[l, 2])]
    if l > 0:
        mats.append((vres_a[l - 1], vres_mu[l - 1]))
    now = _pad_cols(jnp.concatenate([a * (1.0 - mu)[:, None] for a, mu in mats], axis=1), LORA_PAD)
    prev = _pad_cols(jnp.concatenate([a * mu[:, None] for a, mu in mats], axis=1), LORA_PAD)
    wa = jnp.concatenate([w[:, :3 * R], now, prev, w[:, 3 * R:5 * R]], axis=1).astype(BF16)
    wvt = w[:, 5 * R:6 * R].T.astype(BF16)
    wbh, wbl = _split_bf16(_pad_cols(w[:, 6 * R:], IDX_COLS))
    return wa, wvt, wbh, wbl


def kernel(x, c, w_ada, b_ada, norm1_g, norm2_g, w_in, mu_rkv, mu_lora, decay_w0, decay_a, decay_b, iclr_a0, iclr_a, iclr_b, gate_a, gate_b, k_k, k_a, r_k, lnx_g, lnx_b, vres_mu, vres_v0, vres_a, vres_b, attn_out_g, rel_bias, w_out, w_mlp1, w_mlp2, final_g):
    B, T, D = x.shape
    depth = w_in.shape[0]
    mod = _adaln(c, w_ada, b_ada)
    mod = mod.reshape(depth, B, 6, 1, D).transpose(0, 2, 1, 3, 4)
    x2d = x.reshape(B * T, D)
    v_first = None
    for l in range(depth):
        sh1, sc1, gt1, sh2, sc2, gt2 = (mod[l, i] for i in range(6))
        wa, wvt, wbh, wbl = _in_weights(l, w_in, mu_lora, decay_a, iclr_a, gate_a, vres_mu, vres_a)
        rkvl, qk, vt, qiw, kix = _inproj(x2d, norm1_g[l], sc1, sh1, wa, wvt, wbh, wbl, T)
        p = dict(mu_rkv=mu_rkv[l], decay_w0=decay_w0[l], decay_b=decay_b[l], iclr_a0=iclr_a0[l],
                 iclr_b=iclr_b[l], gate_b=gate_b[l], k_k=k_k[l], k_a=k_a[l], r_k=r_k[l],
                 lnx_g=lnx_g[l], lnx_b=lnx_b[l])
        if l > 0:
            p.update(vres_v0=vres_v0[l - 1], vres_b=vres_b[l - 1])
        rw, v_first = _rwkv(rkvl, v_first, p, B, T)
        att = _dsa(qk, vt, qiw, kix, rel_bias, attn_out_g[l], B, T)
        x2d = _outproj(x2d, rw, att, w_out[l].astype(BF16), gt1, T)
        x2d = _mlp(x2d, norm2_g[l], sc2, sh2, gt2, w_mlp1[l].astype(BF16), w_mlp2[l].astype(BF16),
                   final_g, l == depth - 1, T)
    return x2d.reshape(B, T, D)
```

```python
import functools
import math

import numpy as np
import jax
import jax.numpy as jnp
from jax import lax
from jax.experimental import pallas as pl
from jax.experimental.pallas import tpu as pltpu

F32 = jnp.float32
BF16 = jnp.bfloat16
I32 = jnp.int32

HEAD_DIM = 64
RWKV_HEADS = 8
ATTN_HEADS = 8
IDX_HEADS = 8
IDX_DIM = 64
TOPK_MAX = 256
QUERY_BLOCK = 128
N_BUCKETS = 32
MAX_DISTANCE = 128
RMS_EPS = 1e-6
LNX_EPS = 64e-5

LANES = 128
VMEM_LIMIT = 56 * 1024 * 1024
RWKV_CHUNK = 64
RWKV_GROUP = 256
RWKV_TILE = 512
RWKV_UNROLL = 4
DSA_BLOCK = 256
LORA_PAD = 384
IDX_COLS = 640
NEG_BIG = -1e30
HI = lax.Precision.HIGHEST
NT = (((1,), (1,)), ((), ()))


def _bdot(a, b):
    return jnp.dot(a.astype(BF16), b.astype(BF16), preferred_element_type=F32)


def _hdot(a, b):
    return jnp.dot(a, b, precision=HI, preferred_element_type=F32)


def _params(*sem):
    return pltpu.CompilerParams(dimension_semantics=sem, vmem_limit_bytes=VMEM_LIMIT)


def _adaln_kernel(c_ref, w_ref, b_ref, o_ref):
    c = c_ref[...]
    c_act = c * jax.nn.sigmoid(c)
    o_ref[...] = _hdot(c_act, w_ref[...]) + b_ref[...]


def _adaln(c, w_ada, b_ada):
    L, D, D6 = w_ada.shape
    B = c.shape[0]
    cb = 1024
    return pl.pallas_call(
        _adaln_kernel,
        grid=(L, D6 // cb),
        in_specs=[pl.BlockSpec((B, D), lambda l, j: (0, 0)),
                  pl.BlockSpec((None, D, cb), lambda l, j: (l, 0, j)),
                  pl.BlockSpec((None, 1, cb), lambda l, j: (l, 0, j))],
        out_specs=pl.BlockSpec((None, B, cb), lambda l, j: (l, 0, j)),
        out_shape=jax.ShapeDtypeStruct((L, B, D6), F32),
        compiler_params=_params("arbitrary", "arbitrary"),
        name="adaln",
    )(c, w_ada, b_ada.reshape(L, 1, D6))


def _norm_mod(x, g, sc, sh):
    ms = jnp.mean(x * x, axis=-1, keepdims=True)
    return (x * lax.rsqrt(ms + RMS_EPS) * g) * (1.0 + sc) + sh


def _inproj_kernel(x_ref, g_ref, sc_ref, sh_ref, wa_ref, wvt_ref, wbh_ref, wbl_ref,
                   rkvl_ref, qk_ref, vt_ref, qiw_ref, kix_ref):
    h = _norm_mod(x_ref[...], g_ref[...], sc_ref[...], sh_ref[...])
    hb = h.astype(BF16)
    hl = (h - hb.astype(F32)).astype(BF16)
    pa = jnp.dot(hb, wa_ref[...], preferred_element_type=F32)
    nr = rkvl_ref.shape[1]
    rkvl_ref[...] = pa[:, :nr]
    qk_ref[...] = pa[:, nr:].astype(BF16)
    vt_ref[...] = lax.dot_general(wvt_ref[...], hb, NT, preferred_element_type=F32).astype(BF16)
    pb = (jnp.dot(hb, wbh_ref[...], preferred_element_type=F32)
          + jnp.dot(hb, wbl_ref[...], preferred_element_type=F32)
          + jnp.dot(hl, wbh_ref[...], preferred_element_type=F32))
    qiw_ref[...] = pb
    ki = pb[:, 512:576]
    kh = ki.astype(BF16)
    kl = (ki - kh.astype(F32)).astype(BF16)
    kix_ref[...] = jnp.concatenate([kh, kl, kh, jnp.zeros_like(kh)], axis=-1)


def _inproj(x2d, g, sc, sh, wa, wvt, wbh, wbl, T):
    N, D = x2d.shape
    tm = DSA_BLOCK
    na = wa.shape[1]
    A = wvt.shape[0]
    nr = 3 * 512 + 2 * LORA_PAD
    nb = T // tm
    row = lambda i: (i, 0)
    per_b = lambda i: (i // nb, 0, 0)
    const = lambda i: (0, 0)
    return pl.pallas_call(
        _inproj_kernel,
        grid=(N // tm,),
        in_specs=[pl.BlockSpec((tm, D), row),
                  pl.BlockSpec((1, D), const),
                  pl.BlockSpec((None, 1, D), per_b),
                  pl.BlockSpec((None, 1, D), per_b),
                  pl.BlockSpec((D, na), const),
                  pl.BlockSpec((A, D), const),
                  pl.BlockSpec((D, IDX_COLS), const),
                  pl.BlockSpec((D, IDX_COLS), const)],
        out_specs=[pl.BlockSpec((tm, nr), row),
                   pl.BlockSpec((tm, na - nr), row),
                   pl.BlockSpec((None, A, tm), lambda i: (i, 0, 0)),
                   pl.BlockSpec((tm, IDX_COLS), row),
                   pl.BlockSpec((tm, 256), row)],
        out_shape=[jax.ShapeDtypeStruct((N, nr), F32),
                   jax.ShapeDtypeStruct((N, na - nr), BF16),
                   jax.ShapeDtypeStruct((N // tm, A, tm), BF16),
                   jax.ShapeDtypeStruct((N, IDX_COLS), F32),
                   jax.ShapeDtypeStruct((N, 256), BF16)],
        compiler_params=_params("arbitrary"),
        name="inproj",
    )(x2d, g.reshape(1, D), sc, sh, wa, wvt, wbh, wbl)


def _head_ones(n):
    r = lax.broadcasted_iota(I32, (n, n), 0) // HEAD_DIM
    c = lax.broadcasted_iota(I32, (n, n), 1) // HEAD_DIM
    return jnp.where(r == c, 1.0, 0.0).astype(F32)


def _split(x):
    hi = x.astype(BF16)
    return hi, (x - hi.astype(F32)).astype(BF16)


def _split3(x):
    hi = x.astype(BF16)
    r = x - hi.astype(F32)
    mid = r.astype(BF16)
    return hi, mid, (r - mid.astype(F32)).astype(BF16)


def _block_diag(w, blocks):
    w = w.astype(BF16)
    return jnp.concatenate([jnp.where(m, w, jnp.zeros_like(w)) for m in blocks], axis=0)


def _block_diag_t(w, blocks):
    return jnp.concatenate([jnp.where(m, w, 0.0) for m in blocks], axis=0).T


def _fold_blocks(w):
    n = w.shape[1] // HEAD_DIM
    out = w[:HEAD_DIM]
    for h in range(1, n):
        out = out + w[h * HEAD_DIM:(h + 1) * HEAD_DIM]
    return out


def _mm(a, w):
    return jnp.dot(a.astype(BF16), w, preferred_element_type=F32)


def _rwkv_kernel(has_vres, TT, *refs):
    C = RWKV_CHUNK
    R = RWKV_HEADS * HEAD_DIM
    G = RWKV_GROUP
    NG = R // G
    NC = TT // C
    if has_vres:
        (rkvl_ref, vf_ref, mu_ref, w0_ref, db_ref, a0_ref, ib_ref, gb_ref, kk_ref, ka_ref,
         rk_ref, lg_ref, lb_ref, v0_ref, vb_ref, out_ref,
         prev_ref, S_ref, r_s, k_s, v_s, kk_s, b_s, lw_s, cum_s, y_s, q_s, y0_s, m_s, g_s) = refs
    else:
        (rkvl_ref, mu_ref, w0_ref, db_ref, a0_ref, ib_ref, gb_ref, kk_ref, ka_ref,
         rk_ref, lg_ref, lb_ref, out_ref, vfo_ref,
         prev_ref, S_ref, r_s, k_s, v_s, kk_s, b_s, lw_s, cum_s, y_s, q_s, y0_s, m_s, g_s) = refs

    @pl.when(pl.program_id(1) == 0)
    def _():
        prev_ref[...] = jnp.zeros_like(prev_ref)
        S_ref[...] = jnp.zeros_like(S_ref)

    row0 = lax.broadcasted_iota(I32, (TT, 1), 0) == 0
    prev = prev_ref[...]

    def shift(z, p):
        return jnp.where(row0, p, pltpu.roll(z, 1, 0))

    rkv = rkvl_ref[:, :3 * R]
    p1 = rkvl_ref[:, 3 * R:3 * R + LORA_PAD]
    p2 = rkvl_ref[:, 3 * R + LORA_PAD:]
    rkv_new = rkv + (shift(rkv, prev[:, :3 * R]) - rkv) * mu_ref[...]
    lora = p1 + shift(p2, prev[:, 3 * R:])
    prev_ref[...] = jnp.concatenate([rkv[TT - 1:TT, :], p2[TT - 1:TT, :]], axis=-1)

    r = rkv_new[:, :R]
    k = rkv_new[:, R:2 * R]
    v = rkv_new[:, 2 * R:]
    ones_h = _head_ones(R).astype(BF16)

    def head_sum(x):
        hi, lo = _split(x)
        return (jnp.dot(hi, ones_h, preferred_element_type=F32)
                + jnp.dot(lo, ones_h, preferred_element_type=F32))

    wlog = w0_ref[...] + _bdot(jnp.tanh(lora[:, 0:64]), db_ref[...])
    z = -wlog
    wlog = -(jnp.maximum(z, 0.0) + jnp.log(1.0 + jnp.exp(-jnp.abs(z)))) - 0.5
    lw = -jnp.exp(wlog)
    lw_s[...] = lw
    slab = min(TT, 4 * C)
    tr = lax.broadcasted_iota(I32, (slab, slab), 0)
    tc = lax.broadcasted_iota(I32, (slab, slab), 1)
    tri = jnp.where((tc <= tr) & (tr // C == tc // C), 1.0, 0.0).astype(BF16)
    parts = _split3(lw)
    for s0 in range(0, TT, slab):
        cum_s[s0:s0 + slab, :] = sum(jnp.dot(tri, part[s0:s0 + slab], preferred_element_type=F32) for part in parts)
    a = jax.nn.sigmoid(a0_ref[...] + _bdot(lora[:, 64:128], ib_ref[...]))
    g = _bdot(jax.nn.sigmoid(lora[:, 128:256]), gb_ref[...])
    kkr = k * kk_ref[...]
    kk = kkr / jnp.maximum(jnp.sqrt(head_sum(kkr * kkr)), 1e-12)
    k = k * (1.0 + (a - 1.0) * ka_ref[...])
    if has_vres:
        v = v + (vf_ref[...] - v) * jax.nn.sigmoid(v0_ref[...] + _bdot(lora[:, 256:288], vb_ref[...]))
    else:
        vfo_ref[...] = v
    r_s[...] = r
    k_s[...] = k
    v_s[...] = v
    kk_s[...] = kk
    b_s[...] = kk * a

    lane_g = lax.broadcasted_iota(I32, (C, RWKV_GROUP), 1)
    row_g = lax.broadcasted_iota(I32, (C, RWKV_GROUP), 0)
    blocks = [lane_g // HEAD_DIM == h for h in range(RWKV_GROUP // HEAD_DIM)]
    local = lane_g % HEAD_DIM
    strict = local < row_g
    incl = local <= row_g
    diag = local == row_g
    eye_cat = jnp.where(diag, 1.0, 0.0).astype(F32)
    levels = int(math.log2(C)) - 1

    def local_chunk(it, carry):
        chains = [(u, slice(gi * G, (gi + 1) * G)) for u in range(RWKV_UNROLL) for gi in range(NG)]
        each = lambda f: [f(i) for i in range(len(chains))]
        rd, kkd, kt, bt, kc, bc, vv, plast = ([] for _ in range(8))
        for u in range(RWKV_UNROLL):
            sl = pl.ds(pl.multiple_of((it * RWKV_UNROLL + u) * C, C), C)
            lw = lw_s[sl, :]
            cum = cum_s[sl, :]
            cl = cum[C - 1:C, :]
            pinv = jnp.exp(-cum)
            pc = jnp.exp(cl - cum)
            k_a = k_s[sl, :]
            b_a = b_s[sl, :]
            full = (r_s[sl, :] * jnp.exp(cum), kk_s[sl, :] * jnp.exp(cum - lw), k_a * pinv, b_a * pinv,
                    k_a * pc, b_a * pc, v_s[sl, :], jnp.exp(cl))
            for dst, x in zip((rd, kkd, kt, bt, kc, bc, vv, plast), full):
                dst.extend(x[:, gs] for (uu, gs) in chains if uu == u)
        kt_w = each(lambda i: _block_diag_t(kt[i], blocks).astype(BF16))
        bt_w = each(lambda i: _block_diag_t(bt[i], blocks).astype(BF16))
        bc_t = each(lambda i: _fold_blocks(_block_diag_t(bc[i], blocks)))
        kc_t = each(lambda i: _fold_blocks(_block_diag_t(kc[i], blocks)))
        rr = each(lambda i: _mm(jnp.concatenate([kkd[i], rd[i]], axis=0),
                                jnp.concatenate([kt_w[i], bt_w[i]], axis=1)))
        akk = each(lambda i: jnp.where(strict, rr[i][:C, :G], 0.0))
        ark = each(lambda i: jnp.where(incl, rr[i][C:, :G], 0.0))
        arb = each(lambda i: jnp.where(incl, rr[i][C:, G:], 0.0))
        npow = each(lambda i: jnp.where(strict, -rr[i][:C, G:], 0.0))
        x = each(lambda i: eye_cat + npow[i])
        npow = each(lambda i: _mm(npow[i], _block_diag(npow[i], blocks)))
        for _ in range(levels - 1):
            rr = each(lambda i: _mm(jnp.concatenate([npow[i], x[i]], axis=0), _block_diag(npow[i], blocks)))
            npow = each(lambda i: rr[i][:C])
            x = each(lambda i: x[i] + rr[i][C:])
        rr = each(lambda i: _mm(x[i], _block_diag(npow[i], blocks)))
        x = each(lambda i: x[i] + rr[i])
        rv = each(lambda i: _mm(jnp.concatenate([akk[i], ark[i], kc_t[i]], axis=0), _block_diag(vv[i], blocks)))
        ru = each(lambda i: _mm(x[i], jnp.concatenate([_block_diag(kkd[i], blocks),
                                                       _block_diag(rv[i][:C], blocks)], axis=1)))
        rr = each(lambda i: _mm(jnp.concatenate([arb[i], bc_t[i]], axis=0),
                                jnp.concatenate([_block_diag(ru[i][:, :G], blocks),
                                                 _block_diag(ru[i][:, G:], blocks)], axis=1)))
        for i, (u, gs) in enumerate(chains):
            c = it * RWKV_UNROLL + u
            q_s[c, :, gs] = rd[i] - rr[i][:C, :G]
            y0_s[c, :, gs] = rv[i][C:2 * C] - rr[i][:C, G:]
            dterm = jnp.where(diag, jnp.broadcast_to(plast[i], (C, G)), 0.0)
            m_s[c, :, gs] = dterm - rr[i][C:, :G]
            g_s[c, :, gs] = rv[i][2 * C:] - rr[i][C:, G:]
        return carry

    lax.fori_loop(0, NC // RWKV_UNROLL, local_chunk, 0)

    def scan_chunk(c, carry):
        sl = pl.ds(pl.multiple_of(c * C, C), C)
        for gi in range(NG):
            gs = slice(gi * RWKV_GROUP, (gi + 1) * RWKV_GROUP)
            rr = _mm(jnp.concatenate([q_s[c, :, gs], m_s[c, :, gs]], axis=0), _block_diag(S_ref[:, gs], blocks))
            y_s[sl, gs] = rr[:C] + y0_s[c, :, gs]
            S_ref[:, gs] = rr[C:] + g_s[c, :, gs]
        return carry

    lax.fori_loop(0, NC, scan_chunk, 0)

    y = y_s[...]
    inv_n = 1.0 / HEAD_DIM
    mean = head_sum(y) * inv_n
    yc = y - mean
    var = head_sum(yc * yc) * inv_n
    yn = yc * lax.rsqrt(var + LNX_EPS) * lg_ref[...] + lb_ref[...]
    bonus = head_sum(r_s[...] * k_s[...] * rk_ref[...]) * v_s[...]
    out_ref[...] = ((yn + bonus) * g).astype(out_ref.dtype)


def _rwkv(rkvl, v_first, p, B, T):
    N, nc = rkvl.shape
    R = RWKV_HEADS * HEAD_DIM
    TT = min(RWKV_TILE, T)
    nt = T // TT
    NC = TT // RWKV_CHUNK
    has_vres = v_first is not None
    row = lambda b, i: (b * nt + i, 0)
    const = lambda b, i: (0, 0)
    vec = lambda a: a.reshape(1, -1)
    ins = [rkvl]
    specs = [pl.BlockSpec((TT, nc), row)]
    if has_vres:
        ins.append(v_first)
        specs.append(pl.BlockSpec((TT, R), row))
    small = [vec(p["mu_rkv"]), vec(p["decay_w0"]), p["decay_b"].astype(BF16), vec(p["iclr_a0"]),
             p["iclr_b"].astype(BF16), p["gate_b"].astype(BF16), vec(p["k_k"]), vec(p["k_a"]),
             vec(p["r_k"]), vec(p["lnx_g"]), vec(p["lnx_b"])]
    if has_vres:
        small += [vec(p["vres_v0"]), p["vres_b"].astype(BF16)]
    ins += small
    specs += [pl.BlockSpec(a.shape, const) for a in small]
    out_shape = [jax.ShapeDtypeStruct((N, R), BF16)]
    out_specs = [pl.BlockSpec((TT, R), row)]
    if not has_vres:
        out_shape.append(jax.ShapeDtypeStruct((N, R), F32))
        out_specs.append(pl.BlockSpec((TT, R), row))
    scratch = [pltpu.VMEM((1, 3 * R + LORA_PAD), F32),
               pltpu.VMEM((HEAD_DIM, R), F32)]
    scratch += [pltpu.VMEM((TT, R), F32) for _ in range(8)]
    scratch += [pltpu.VMEM((NC, RWKV_CHUNK, R), F32) for _ in range(4)]
    res = pl.pallas_call(
        functools.partial(_rwkv_kernel, has_vres, TT),
        grid=(B, nt),
        in_specs=specs,
        out_specs=out_specs,
        out_shape=out_shape,
        scratch_shapes=scratch,
        compiler_params=_params("arbitrary", "arbitrary"),
        name="rwkv",
    )(*ins)
    if has_vres:
        return res[0], v_first
    return res[0], res[1]


def _bucket_boundaries():
    max_exact = N_BUCKETS // 2
    d = np.arange(0, 4 * MAX_DISTANCE, dtype=np.int64)
    nf = np.maximum(d, 1).astype(np.float32)
    large = max_exact + (np.log(nf / np.float32(max_exact)) / np.float32(math.log(MAX_DISTANCE / max_exact))
                         * np.float32(N_BUCKETS - max_exact)).astype(np.int32)
    large = np.minimum(large, N_BUCKETS - 1)
    bucket = np.where(d < max_exact, d, large)
    return [int(np.argmax(bucket >= j)) for j in range(max_exact + 1, N_BUCKETS)]


_BUCKET_STARTS = _bucket_boundaries()
KEY_NEG_INF = int(np.int32(np.array(-np.inf, np.float32).view(np.int32)) ^ np.int32(0x7FFFFFFF))
INT_MIN = -2 ** 31


def _sort_key(s):
    bits = pltpu.bitcast(s, I32)
    return bits ^ ((bits >> 31) & jnp.int32(0x7FFFFFFF))


def _dsa_kernel(T, ksel, q_ref, k_ref, vt_ref, qiw_ref, kix_ref, relb_ref, g_ref, out_ref,
                lhs_s, qpad_s, key_s, madd_s, bias_s, m_s, l_s, acc_s, s_s, p_s, cmax_s, alpha_s):
    TQ = TK = DSA_BLOCK
    H = ATTN_HEADS
    qb = pl.program_id(1)
    nch = qb + 1
    rows = lax.broadcasted_iota(I32, (TK, TQ), 0)
    cols = lax.broadcasted_iota(I32, (TK, TQ), 1)

    @pl.when((pl.program_id(0) == 0) & (qb == 0))
    def _():
        max_exact = N_BUCKETS // 2
        for off in range(2):
            d = cols - rows + off * TK
            bucket = jnp.where(d < max_exact, jnp.maximum(d, 0), max_exact)
            for start in _BUCKET_STARTS:
                bucket = bucket + jnp.where(d >= start, 1, 0)
            for h in range(H):
                far = relb_ref[N_BUCKETS - 1, h]
                tile = jnp.zeros((TK, TQ), F32)
                for bk in range(N_BUCKETS - 1):
                    tile = jnp.where(bucket == bk, relb_ref[bk, h] - far, tile)
                bias_s[h, off] = tile

    qi_t = (qiw_ref[:, :IDX_HEADS * IDX_DIM] * (IDX_DIM ** -0.5)).T
    for h in range(IDX_HEADS):
        qh = qi_t[h * IDX_DIM:(h + 1) * IDX_DIM, :]
        hi = qh.astype(BF16)
        lo = (qh - hi.astype(F32)).astype(BF16)
        lhs_s[h] = jnp.concatenate([hi, hi, lo, jnp.zeros_like(hi)], axis=0)
    w_t = qiw_ref[:, 512:640].T[64:64 + IDX_HEADS, :] * (IDX_HEADS ** -0.5)

    q_t = (q_ref[...].astype(F32) * (HEAD_DIM ** -0.5)).T
    zeros_h = jnp.zeros((HEAD_DIM, TQ), BF16)
    for h in range(H):
        qh = q_t[h * HEAD_DIM:(h + 1) * HEAD_DIM, :].astype(BF16)
        qpad_s[h] = jnp.concatenate([qh, zeros_h] if h % 2 == 0 else [zeros_h, qh], axis=0)

    def scores(j):
        kx = kix_ref[pl.ds(pl.multiple_of(j * TK, TK), TK), :]
        acc = jnp.zeros((TK, TQ), F32)
        for h in range(IDX_HEADS):
            d = jnp.dot(kx, lhs_s[h], preferred_element_type=F32)
            acc = acc + w_t[h:h + 1, :] * jnp.maximum(d, 0.0)
        return acc

    def score_body(j, c):
        key_s[j] = _sort_key(scores(j))
        return c

    lax.fori_loop(0, qb, score_body, 0)
    key_s[qb] = _sort_key(jnp.where(rows <= cols, scores(qb), -jnp.inf))

    def count(pred):
        def body(j, acc):
            hit = jnp.where(pred(key_s[j]), 1, 0)
            return acc + jnp.sum(hit.reshape(TK // 8, 8, TQ), axis=0)
        acc = lax.fori_loop(0, nch, body, jnp.zeros((8, TQ), I32))
        return jnp.sum(acc, axis=0, keepdims=True)

    zero = jnp.zeros((1, TQ), I32)
    base = jnp.where(count(lambda kk: kk >= zero) >= ksel, zero, jnp.full((1, TQ), INT_MIN, I32))

    def bit_body(i, base):
        cand = base | jnp.left_shift(jnp.int32(1), 30 - i)
        return jnp.where(count(lambda kk: kk >= cand) >= ksel, cand, base)

    thr = lax.fori_loop(0, 31, bit_body, base)
    n_gt = count(lambda kk: kk > thr)
    n_ge = count(lambda kk: kk >= thr)
    need = ksel - n_gt
    tie_rows = jnp.where((n_ge > ksel) & (thr > KEY_NEG_INF), 1, 0)
    any_tie = jnp.max(tie_rows) > 0

    @pl.when(jnp.logical_not(any_tie))
    def _():
        def body(j, c):
            kk = key_s[j]
            madd_s[j] = jnp.where((kk >= thr) & (kk > KEY_NEG_INF), 0.0, NEG_BIG)
            return c
        lax.fori_loop(0, nch, body, 0)

    @pl.when(any_tie)
    def _():
        kr = lax.broadcasted_iota(I32, (TK, TK), 0)
        kc = lax.broadcasted_iota(I32, (TK, TK), 1)
        lower = jnp.where(kc <= kr, 1.0, 0.0).astype(BF16)
        need_f = need.astype(F32)

        def body(j, run):
            kk = key_s[j]
            eq = kk == thr
            eqf = jnp.where(eq, 1.0, 0.0)
            rank = run + jnp.dot(lower, eqf.astype(BF16), preferred_element_type=F32)
            sel = ((kk > thr) | (eq & (rank <= need_f))) & (kk > KEY_NEG_INF)
            madd_s[j] = jnp.where(sel, 0.0, NEG_BIG)
            return run + jnp.sum(eqf, axis=0, keepdims=True)
        lax.fori_loop(0, nch, body, jnp.zeros((1, TQ), F32))

    m_s[...] = jnp.full(m_s.shape, NEG_BIG, F32)
    l_s[...] = jnp.zeros(l_s.shape, F32)
    acc_s[...] = jnp.zeros(acc_s.shape, F32)

    def attend(j, bias_of):
        ks = pl.ds(pl.multiple_of(j * TK, TK), TK)
        ma = madd_s[j]
        for h in range(H):
            kp = k_ref[ks, LANES * (h // 2):LANES * (h // 2 + 1)]
            s = jnp.dot(kp, qpad_s[h], preferred_element_type=F32) + ma
            bias = bias_of(h)
            if bias is not None:
                s = s + bias
            s_s[h] = s
            cmax_s[h] = jnp.max(s.reshape(TK // 8, 8, TQ), axis=0)
        for h in range(H):
            m_old = m_s[h]
            m_new = jnp.maximum(m_old, jnp.max(cmax_s[h], axis=0, keepdims=True))
            alpha_s[h] = jnp.exp(m_old - m_new)
            m_s[h] = m_new
            p = jnp.exp(s_s[h] - m_new)
            l_s[h] = alpha_s[h] * l_s[h] + jnp.sum(p, axis=0, keepdims=True)
            p_s[h] = p.astype(BF16)
        for h in range(H):
            hs = slice(h * HEAD_DIM, (h + 1) * HEAD_DIM)
            pv = jnp.dot(vt_ref[j, hs, :], p_s[h], preferred_element_type=F32)
            acc_s[hs, :] = alpha_s[h] * acc_s[hs, :] + pv

    def far_body(j, c):
        attend(j, lambda h: None)
        return c

    lax.fori_loop(0, jnp.maximum(qb - 1, 0), far_body, 0)

    @pl.when(qb >= 1)
    def _():
        attend(qb - 1, lambda h: bias_s[h, 1])

    attend(qb, lambda h: bias_s[h, 0])

    outs = []
    for h in range(H):
        hs = slice(h * HEAD_DIM, (h + 1) * HEAD_DIM)
        o = acc_s[hs, :] / l_s[h]
        ms = jnp.mean(o * o, axis=0, keepdims=True)
        outs.append(o * lax.rsqrt(ms + RMS_EPS))
    out_ref[...] = (jnp.concatenate(outs, axis=0).T * g_ref[...]).astype(out_ref.dtype)


def _dsa(qk, vt, qiw, kix, rel_bias, g, B, T):
    N = qk.shape[0]
    TQ = DSA_BLOCK
    nq = T // TQ
    A = ATTN_HEADS * HEAD_DIM
    ksel = min(TOPK_MAX, T // 4)
    assert MAX_DISTANCE <= TQ + 1
    return pl.pallas_call(
        functools.partial(_dsa_kernel, T, ksel),
        grid=(B, nq),
        in_specs=[pl.BlockSpec((TQ, A), lambda b, i: (b * nq + i, 0)),
                  pl.BlockSpec((T, A), lambda b, i: (b, 1)),
                  pl.BlockSpec((nq, A, TQ), lambda b, i: (b, 0, 0)),
                  pl.BlockSpec((TQ, IDX_COLS), lambda b, i: (b * nq + i, 0)),
                  pl.BlockSpec((T, 256), lambda b, i: (b, 0)),
                  pl.BlockSpec(memory_space=pltpu.SMEM),
                  pl.BlockSpec((1, A), lambda b, i: (0, 0))],
        out_specs=pl.BlockSpec((TQ, A), lambda b, i: (b * nq + i, 0)),
        out_shape=jax.ShapeDtypeStruct((N, A), BF16),
        scratch_shapes=[pltpu.VMEM((IDX_HEADS, 256, TQ), BF16),
                        pltpu.VMEM((ATTN_HEADS, LANES, TQ), BF16),
                        pltpu.VMEM((nq, TQ, TQ), I32),
                        pltpu.VMEM((nq, TQ, TQ), F32),
                        pltpu.VMEM((ATTN_HEADS, 2, TQ, TQ), F32),
                        pltpu.VMEM((ATTN_HEADS, 1, TQ), F32),
                        pltpu.VMEM((ATTN_HEADS, 1, TQ), F32),
                        pltpu.VMEM((A, TQ), F32),
                        pltpu.VMEM((ATTN_HEADS, TQ, TQ), F32),
                        pltpu.VMEM((ATTN_HEADS, TQ, TQ), BF16),
                        pltpu.VMEM((ATTN_HEADS, 8, TQ), F32),
                        pltpu.VMEM((ATTN_HEADS, 1, TQ), F32)],
        compiler_params=_params("arbitrary", "arbitrary"),
        name="dsa",
    )(qk, qk, vt, qiw, kix, rel_bias, g.reshape(1, A))


def _outproj_kernel(x_ref, rw_ref, att_ref, w_ref, gt_ref, o_ref):
    R = rw_ref.shape[1]
    mixed = (jnp.dot(rw_ref[...], w_ref[:R, :], preferred_element_type=F32)
             + jnp.dot(att_ref[...], w_ref[R:, :], preferred_element_type=F32))
    o_ref[...] = x_ref[...] + gt_ref[...] * mixed


def _outproj(x2d, rw, att, w, gt, T, tm=512):
    N, D = x2d.shape
    nb = T // tm
    row = lambda i: (i, 0)
    return pl.pallas_call(
        _outproj_kernel,
        grid=(N // tm,),
        in_specs=[pl.BlockSpec((tm, D), row),
                  pl.BlockSpec((tm, rw.shape[1]), row),
                  pl.BlockSpec((tm, att.shape[1]), row),
                  pl.BlockSpec(w.shape, lambda i: (0, 0)),
                  pl.BlockSpec((None, 1, D), lambda i: (i // nb, 0, 0))],
        out_specs=pl.BlockSpec((tm, D), row),
        out_shape=jax.ShapeDtypeStruct((N, D), F32),
        compiler_params=_params("arbitrary"),
        name="outproj",
    )(x2d, rw, att, w, gt)


def _mlp_kernel(final, x_ref, g_ref, sc_ref, sh_ref, gt_ref, w1_ref, w2_ref, fg_ref, o_ref, h_s, acc_s):
    j = pl.program_id(1)

    @pl.when(j == 0)
    def _():
        h_s[...] = _norm_mod(x_ref[...], g_ref[...], sc_ref[...], sh_ref[...]).astype(BF16)
        acc_s[...] = jnp.zeros_like(acc_s)

    u = jnp.dot(h_s[...], w1_ref[...], preferred_element_type=F32)
    u = jnp.square(jnp.maximum(u, 0.0))
    acc_s[...] += jnp.dot(u.astype(BF16), w2_ref[...], preferred_element_type=F32)

    @pl.when(j == pl.num_programs(1) - 1)
    def _():
        y = x_ref[...] + gt_ref[...] * acc_s[...]
        if final:
            ms = jnp.mean(y * y, axis=-1, keepdims=True)
            y = y * lax.rsqrt(ms + RMS_EPS) * fg_ref[...]
        o_ref[...] = y


def _mlp(x2d, g, sc, sh, gt, w1, w2, final_g, final, T, tm=512, fc=512):
    N, D = x2d.shape
    F = w1.shape[1]
    nb = T // tm
    row = lambda i, j: (i, 0)
    per_b = lambda i, j: (i // nb, 0, 0)
    const = lambda i, j: (0, 0)
    return pl.pallas_call(
        functools.partial(_mlp_kernel, final),
        grid=(N // tm, F // fc),
        in_specs=[pl.BlockSpec((tm, D), row),
                  pl.BlockSpec((1, D), const),
                  pl.BlockSpec((None, 1, D), per_b),
                  pl.BlockSpec((None, 1, D), per_b),
                  pl.BlockSpec((None, 1, D), per_b),
                  pl.BlockSpec((D, fc), lambda i, j: (0, j)),
                  pl.BlockSpec((fc, D), lambda i, j: (j, 0)),
                  pl.BlockSpec((1, D), const)],
        out_specs=pl.BlockSpec((tm, D), row),
        out_shape=jax.ShapeDtypeStruct((N, D), F32),
        scratch_shapes=[pltpu.VMEM((tm, D), BF16), pltpu.VMEM((tm, D), F32)],
        compiler_params=_params("arbitrary", "arbitrary"),
        name="mlp",
    )(x2d, g.reshape(1, D), sc, sh, gt, w1, w2, final_g.reshape(1, D))


def _pad_cols(w, n):
    return jnp.pad(w, ((0, 0), (0, n - w.shape[1])))


def _split_bf16(w):
    hi = w.astype(BF16)
    return hi, (w - hi.astype(F32)).astype(BF16)


def _in_weights(l, w_in, mu_lora, decay_a, iclr_a, gate_a, vres_mu, vres_a):
    D = w_in.shape[1]
    R = RWKV_HEADS * HEAD_DIM
    w = w_in[l]
    mats = [(decay_a[l], mu_lora[l, 0]), (iclr_a[l], mu_lora[l, 1]), (gate_a[l], mu_lora[l, 2])]
    if l > 0:
        mats.append((vres_a[l - 1], vres_mu[l - 1]))
    now = _pad_cols(jnp.concatenate([a * (1.0 - mu)[:, None] for a, mu in mats], axis=1), LORA_PAD)
    prev = _pad_cols(jnp.concatenate([a * mu[:, None] for a, mu in mats], axis=1), LORA_PAD)
    wa = jnp.concatenate([w[:, :3 * R], now, prev, w[:, 3 * R:5 * R]], axis=1).astype(BF16)
    wvt = w[:, 5 * R:6 * R].T.astype(BF16)
    wbh, wbl = _split_bf16(_pad_cols(w[:, 6 * R:], IDX_COLS))
    return wa, wvt, wbh, wbl


def kernel(x, c, w_ada, b_ada, norm1_g, norm2_g, w_in, mu_rkv, mu_lora, decay_w0, decay_a, decay_b, iclr_a0, iclr_a, iclr_b, gate_a, gate_b, k_k, k_a, r_k, lnx_g, lnx_b, vres_mu, vres_v0, vres_a, vres_b, attn_out_g, rel_bias, w_out, w_mlp1, w_mlp2, final_g):
    B, T, D = x.shape
    depth = w_in.shape[0]
    mod = _adaln(c, w_ada, b_ada)
    mod = mod.reshape(depth, B, 6, 1, D).transpose(0, 2, 1, 3, 4)
    x2d = x.reshape(B * T, D)
    v_first = None
    for l in range(depth):
        sh1, sc1, gt1, sh2, sc2, gt2 = (mod[l, i] for i in range(6))
        wa, wvt, wbh, wbl = _in_weights(l, w_in, mu_lora, decay_a, iclr_a, gate_a, vres_mu, vres_a)
        rkvl, qk, vt, qiw, kix = _inproj(x2d, norm1_g[l], sc1, sh1, wa, wvt, wbh, wbl, T)
        p = dict(mu_rkv=mu_rkv[l], decay_w0=decay_w0[l], decay_b=decay_b[l], iclr_a0=iclr_a0[l],
                 iclr_b=iclr_b[l], gate_b=gate_b[l], k_k=k_k[l], k_a=k_a[l], r_k=r_k[l],
                 lnx_g=lnx_g[l], lnx_b=lnx_b[l])
        if l > 0:
            p.update(vres_v0=vres_v0[l - 1], vres_b=vres_b[l - 1])
        rw, v_first = _rwkv(rkvl, v_first, p, B, T)
        att = _dsa(qk, vt, qiw, kix, rel_bias, attn_out_g[l], B, T)
        x2d = _outproj(x2d, rw, att, w_out[l].astype(BF16), gt1, T)
        x2d = _mlp(x2d, norm2_g[l], sc2, sh2, gt2, w_mlp1[l].astype(BF16), w_mlp2[l].astype(BF16),
                   final_g, l == depth - 1, T)
    return x2d.reshape(B, T, D)
```

```python
import functools
import math

import numpy as np
import jax
import jax.numpy as jnp
from jax import lax
from jax.experimental import pallas as pl
from jax.experimental.pallas import tpu as pltpu

F32 = jnp.float32
BF16 = jnp.bfloat16
I32 = jnp.int32

HEAD_DIM = 64
RWKV_HEADS = 8
ATTN_HEADS = 8
IDX_HEADS = 8
IDX_DIM = 64
TOPK_MAX = 256
QUERY_BLOCK = 128
N_BUCKETS = 32
MAX_DISTANCE = 128
RMS_EPS = 1e-6
LNX_EPS = 64e-5

LANES = 128
VMEM_LIMIT = 56 * 1024 * 1024
RWKV_CHUNK = 64
RWKV_GROUP = 256
RWKV_TILE = 512
RWKV_UNROLL = 4
DSA_BLOCK = 256
VT_ROWS = 80
LOG2E = math.log2(math.e)
INPROJ_TILE = 512
MLP_TILE = 1024
LORA_PAD = 384
IDX_COLS = 640
NEG_BIG = -1e30
HI = lax.Precision.HIGHEST
NT = (((1,), (1,)), ((), ()))


def _bdot(a, b):
    return jnp.dot(a.astype(BF16), b.astype(BF16), preferred_element_type=F32)


def _hdot(a, b):
    return jnp.dot(a, b, precision=HI, preferred_element_type=F32)


def _params(*sem):
    return pltpu.CompilerParams(dimension_semantics=sem, vmem_limit_bytes=VMEM_LIMIT)


def _adaln_kernel(c_ref, w_ref, b_ref, o_ref):
    c = c_ref[...]
    c_act = c * jax.nn.sigmoid(c)
    o_ref[...] = _hdot(c_act, w_ref[...]) + b_ref[...]


def _adaln(c, w_ada, b_ada):
    L, D, D6 = w_ada.shape
    B = c.shape[0]
    cb = 1024
    return pl.pallas_call(
        _adaln_kernel,
        grid=(L, D6 // cb),
        in_specs=[pl.BlockSpec((B, D), lambda l, j: (0, 0)),
                  pl.BlockSpec((None, D, cb), lambda l, j: (l, 0, j)),
                  pl.BlockSpec((None, 1, cb), lambda l, j: (l, 0, j))],
        out_specs=pl.BlockSpec((None, B, cb), lambda l, j: (l, 0, j)),
        out_shape=jax.ShapeDtypeStruct((L, B, D6), F32),
        compiler_params=_params("arbitrary", "arbitrary"),
        name="adaln",
    )(c, w_ada, b_ada.reshape(L, 1, D6))


def _norm_mod(x, g, sc, sh):
    ms = jnp.mean(x * x, axis=-1, keepdims=True)
    return (x * lax.rsqrt(ms + RMS_EPS) * g) * (1.0 + sc) + sh


def _inproj_kernel(x_ref, g_ref, sc_ref, sh_ref, wa_ref, wvt_ref, wbh_ref, wbl_ref,
                   rkvl_ref, qk_ref, vt_ref, qiw_ref, kix_ref):
    h = _norm_mod(x_ref[...], g_ref[...], sc_ref[...], sh_ref[...])
    hb = h.astype(BF16)
    hl = (h - hb.astype(F32)).astype(BF16)
    pa = jnp.dot(hb, wa_ref[...], preferred_element_type=F32)
    nr = rkvl_ref.shape[1]
    rkvl_ref[...] = pa[:, :nr]
    qk_ref[...] = pa[:, nr:].astype(BF16)
    vt = lax.dot_general(wvt_ref[...], hb, NT, preferred_element_type=F32)
    ones_row = lax.broadcasted_iota(I32, vt.shape, 0) % VT_ROWS == HEAD_DIM
    vt = jnp.where(ones_row, 1.0, vt).astype(BF16)
    for j in range(vt_ref.shape[0]):
        vt_ref[j] = vt[:, j * DSA_BLOCK:(j + 1) * DSA_BLOCK]
    pb = (jnp.dot(hb, wbh_ref[...], preferred_element_type=F32)
          + jnp.dot(hb, wbl_ref[...], preferred_element_type=F32)
          + jnp.dot(hl, wbh_ref[...], preferred_element_type=F32))
    qiw_ref[...] = pb
    ki = pb[:, 512:576]
    kh = ki.astype(BF16)
    kl = (ki - kh.astype(F32)).astype(BF16)
    kix_ref[...] = jnp.concatenate([kh, kl, kh, jnp.zeros_like(kh)], axis=-1)


def _inproj(x2d, g, sc, sh, wa, wvt, wbh, wbl, T):
    N, D = x2d.shape
    nblk = INPROJ_TILE // DSA_BLOCK
    tm = INPROJ_TILE
    na = wa.shape[1]
    A = wvt.shape[0]
    nr = 3 * 512 + 2 * LORA_PAD
    nb = T // tm
    row = lambda i: (i, 0)
    per_b = lambda i: (i // nb, 0, 0)
    const = lambda i: (0, 0)
    once = pl.Buffered(1)
    return pl.pallas_call(
        _inproj_kernel,
        grid=(N // tm,),
        in_specs=[pl.BlockSpec((tm, D), row),
                  pl.BlockSpec((1, D), const),
                  pl.BlockSpec((None, 1, D), per_b),
                  pl.BlockSpec((None, 1, D), per_b),
                  pl.BlockSpec((D, na), const, pipeline_mode=once),
                  pl.BlockSpec((A, D), const, pipeline_mode=once),
                  pl.BlockSpec((D, IDX_COLS), const, pipeline_mode=once),
                  pl.BlockSpec((D, IDX_COLS), const, pipeline_mode=once)],
        out_specs=[pl.BlockSpec((tm, nr), row),
                   pl.BlockSpec((tm, na - nr), row),
                   pl.BlockSpec((nblk, A, DSA_BLOCK), lambda i: (i, 0, 0)),
                   pl.BlockSpec((tm, IDX_COLS), row),
                   pl.BlockSpec((tm, 256), row)],
        out_shape=[jax.ShapeDtypeStruct((N, nr), F32),
                   jax.ShapeDtypeStruct((N, na - nr), BF16),
                   jax.ShapeDtypeStruct((N // DSA_BLOCK, A, DSA_BLOCK), BF16),
                   jax.ShapeDtypeStruct((N, IDX_COLS), F32),
                   jax.ShapeDtypeStruct((N, 256), BF16)],
        compiler_params=_params("arbitrary"),
        name="inproj",
    )(x2d, g.reshape(1, D), sc, sh, wa, wvt, wbh, wbl)


def _head_ones(n):
    r = lax.broadcasted_iota(I32, (n, n), 0) // HEAD_DIM
    c = lax.broadcasted_iota(I32, (n, n), 1) // HEAD_DIM
    return jnp.where(r == c, 1.0, 0.0).astype(F32)


def _split(x):
    hi = x.astype(BF16)
    return hi, (x - hi.astype(F32)).astype(BF16)


def _split3(x):
    hi = x.astype(BF16)
    r = x - hi.astype(F32)
    mid = r.astype(BF16)
    return hi, mid, (r - mid.astype(F32)).astype(BF16)


def _block_diag(w, blocks):
    w = w.astype(BF16)
    return jnp.concatenate([jnp.where(m, w, jnp.zeros_like(w)) for m in blocks], axis=0)


def _block_diag_t(w, blocks):
    return jnp.concatenate([jnp.where(m, w, 0.0) for m in blocks], axis=0).T


def _fold_blocks(w):
    n = w.shape[1] // HEAD_DIM
    out = w[:HEAD_DIM]
    for h in range(1, n):
        out = out + w[h * HEAD_DIM:(h + 1) * HEAD_DIM]
    return out


def _mm(a, w):
    return jnp.dot(a.astype(BF16), w, preferred_element_type=F32)


def _rwkv_kernel(has_vres, TT, *refs):
    C = RWKV_CHUNK
    R = RWKV_HEADS * HEAD_DIM
    G = RWKV_GROUP
    NG = R // G
    NC = TT // C
    if has_vres:
        (rkvl_ref, vf_ref, mu_ref, w0_ref, db_ref, a0_ref, ib_ref, gb_ref, kk_ref, ka_ref,
         rk_ref, lg_ref, lb_ref, v0_ref, vb_ref, out_ref,
         prev_ref, S_ref, r_s, k_s, v_s, kk_s, b_s, lw_s, cum_s, y_s, q_s, y0_s, m_s, g_s) = refs
    else:
        (rkvl_ref, mu_ref, w0_ref, db_ref, a0_ref, ib_ref, gb_ref, kk_ref, ka_ref,
         rk_ref, lg_ref, lb_ref, out_ref, vfo_ref,
         prev_ref, S_ref, r_s, k_s, v_s, kk_s, b_s, lw_s, cum_s, y_s, q_s, y0_s, m_s, g_s) = refs

    @pl.when(pl.program_id(1) == 0)
    def _():
        prev_ref[...] = jnp.zeros_like(prev_ref)
        S_ref[...] = jnp.zeros_like(S_ref)

    row0 = lax.broadcasted_iota(I32, (TT, 1), 0) == 0
    prev = prev_ref[...]

    def shift(z, p):
        return jnp.where(row0, p, pltpu.roll(z, 1, 0))

    rkv = rkvl_ref[:, :3 * R]
    p1 = rkvl_ref[:, 3 * R:3 * R + LORA_PAD]
    p2 = rkvl_ref[:, 3 * R + LORA_PAD:]
    rkv_new = rkv + (shift(rkv, prev[:, :3 * R]) - rkv) * mu_ref[...]
    lora = p1 + shift(p2, prev[:, 3 * R:])
    prev_ref[...] = jnp.concatenate([rkv[TT - 1:TT, :], p2[TT - 1:TT, :]], axis=-1)

    r = rkv_new[:, :R]
    k = rkv_new[:, R:2 * R]
    v = rkv_new[:, 2 * R:]
    ones_h = _head_ones(R).astype(BF16)

    def head_sum(x):
        hi, lo = _split(x)
        return (jnp.dot(hi, ones_h, preferred_element_type=F32)
                + jnp.dot(lo, ones_h, preferred_element_type=F32))

    wlog = w0_ref[...] + _bdot(jnp.tanh(lora[:, 0:64]), db_ref[...])
    z = -wlog
    wlog = -(jnp.maximum(z, 0.0) + jnp.log(1.0 + jnp.exp(-jnp.abs(z)))) - 0.5
    lw = -jnp.exp(wlog)
    lw_s[...] = lw
    slab = min(TT, 4 * C)
    tr = lax.broadcasted_iota(I32, (slab, slab), 0)
    tc = lax.broadcasted_iota(I32, (slab, slab), 1)
    tri = jnp.where((tc <= tr) & (tr // C == tc // C), 1.0, 0.0).astype(BF16)
    parts = _split3(lw)
    for s0 in range(0, TT, slab):
        cum_s[s0:s0 + slab, :] = sum(jnp.dot(tri, part[s0:s0 + slab], preferred_element_type=F32) for part in parts)
    a = jax.nn.sigmoid(a0_ref[...] + _bdot(lora[:, 64:128], ib_ref[...]))
    g = _bdot(jax.nn.sigmoid(lora[:, 128:256]), gb_ref[...])
    kkr = k * kk_ref[...]
    kk = kkr / jnp.maximum(jnp.sqrt(head_sum(kkr * kkr)), 1e-12)
    k = k * (1.0 + (a - 1.0) * ka_ref[...])
    if has_vres:
        v = v + (vf_ref[...] - v) * jax.nn.sigmoid(v0_ref[...] + _bdot(lora[:, 256:288], vb_ref[...]))
    else:
        vfo_ref[...] = v
    r_s[...] = r
    k_s[...] = k
    v_s[...] = v
    kk_s[...] = kk
    b_s[...] = kk * a

    lane_g = lax.broadcasted_iota(I32, (C, RWKV_GROUP), 1)
    row_g = lax.broadcasted_iota(I32, (C, RWKV_GROUP), 0)
    blocks = [lane_g // HEAD_DIM == h for h in range(RWKV_GROUP // HEAD_DIM)]
    local = lane_g % HEAD_DIM
    strict = local < row_g
    incl = local <= row_g
    diag = local == row_g
    eye_cat = jnp.where(diag, 1.0, 0.0).astype(F32)
    levels = int(math.log2(C)) - 1

    def local_chunk(it, carry):
        chains = [(u, slice(gi * G, (gi + 1) * G)) for u in range(RWKV_UNROLL) for gi in range(NG)]
        each = lambda f: [f(i) for i in range(len(chains))]
        rd, kkd, kt, bt, kc, bc, vv, plast = ([] for _ in range(8))
        for u in range(RWKV_UNROLL):
            sl = pl.ds(pl.multiple_of((it * RWKV_UNROLL + u) * C, C), C)
            lw = lw_s[sl, :]
            cum = cum_s[sl, :]
            cl = cum[C - 1:C, :]
            pinv = jnp.exp(-cum)
            pc = jnp.exp(cl - cum)
            k_a = k_s[sl, :]
            b_a = b_s[sl, :]
            full = (r_s[sl, :] * jnp.exp(cum), kk_s[sl, :] * jnp.exp(cum - lw), k_a * pinv, b_a * pinv,
                    k_a * pc, b_a * pc, v_s[sl, :], jnp.exp(cl))
            for dst, x in zip((rd, kkd, kt, bt, kc, bc, vv, plast), full):
                dst.extend(x[:, gs] for (uu, gs) in chains if uu == u)
        kt_w = each(lambda i: _block_diag_t(kt[i], blocks).astype(BF16))
        bt_w = each(lambda i: _block_diag_t(bt[i], blocks).astype(BF16))
        bc_t = each(lambda i: _fold_blocks(_block_diag_t(bc[i], blocks)))
        kc_t = each(lambda i: _fold_blocks(_block_diag_t(kc[i], blocks)))
        rr = each(lambda i: _mm(jnp.concatenate([kkd[i], rd[i]], axis=0),
                                jnp.concatenate([kt_w[i], bt_w[i]], axis=1)))
        akk = each(lambda i: jnp.where(strict, rr[i][:C, :G], 0.0))
        ark = each(lambda i: jnp.where(incl, rr[i][C:, :G], 0.0))
        arb = each(lambda i: jnp.where(incl, rr[i][C:, G:], 0.0))
        npow = each(lambda i: jnp.where(strict, -rr[i][:C, G:], 0.0))
        x = each(lambda i: eye_cat + npow[i])
        npow = each(lambda i: _mm(npow[i], _block_diag(npow[i], blocks)))
        for _ in range(levels - 1):
            rr = each(lambda i: _mm(jnp.concatenate([npow[i], x[i]], axis=0), _block_diag(npow[i], blocks)))
            npow = each(lambda i: rr[i][:C])
            x = each(lambda i: x[i] + rr[i][C:])
        rr = each(lambda i: _mm(x[i], _block_diag(npow[i], blocks)))
        x = each(lambda i: x[i] + rr[i])
        rv = each(lambda i: _mm(jnp.concatenate([akk[i], ark[i], kc_t[i]], axis=0), _block_diag(vv[i], blocks)))
        ru = each(lambda i: _mm(x[i], jnp.concatenate([_block_diag(kkd[i], blocks),
                                                       _block_diag(rv[i][:C], blocks)], axis=1)))
        rr = each(lambda i: _mm(jnp.concatenate([arb[i], bc_t[i]], axis=0),
                                jnp.concatenate([_block_diag(ru[i][:, :G], blocks),
                                                 _block_diag(ru[i][:, G:], blocks)], axis=1)))
        for i, (u, gs) in enumerate(chains):
            c = it * RWKV_UNROLL + u
            q_s[c, :, gs] = rd[i] - rr[i][:C, :G]
            y0_s[c, :, gs] = rv[i][C:2 * C] - rr[i][:C, G:]
            dterm = jnp.where(diag, jnp.broadcast_to(plast[i], (C, G)), 0.0)
            m_s[c, :, gs] = dterm - rr[i][C:, :G]
            g_s[c, :, gs] = rv[i][2 * C:] - rr[i][C:, G:]
        return carry

    lax.fori_loop(0, NC // RWKV_UNROLL, local_chunk, 0)

    def scan_chunk(c, carry):
        sl = pl.ds(pl.multiple_of(c * C, C), C)
        for gi in range(NG):
            gs = slice(gi * RWKV_GROUP, (gi + 1) * RWKV_GROUP)
            rr = _mm(jnp.concatenate([q_s[c, :, gs], m_s[c, :, gs]], axis=0), _block_diag(S_ref[:, gs], blocks))
            y_s[sl, gs] = rr[:C] + y0_s[c, :, gs]
            S_ref[:, gs] = rr[C:] + g_s[c, :, gs]
        return carry

    lax.fori_loop(0, NC, scan_chunk, 0)

    y = y_s[...]
    inv_n = 1.0 / HEAD_DIM
    mean = head_sum(y) * inv_n
    yc = y - mean
    var = head_sum(yc * yc) * inv_n
    yn = yc * lax.rsqrt(var + LNX_EPS) * lg_ref[...] + lb_ref[...]
    bonus = head_sum(r_s[...] * k_s[...] * rk_ref[...]) * v_s[...]
    out_ref[...] = ((yn + bonus) * g).astype(out_ref.dtype)


def _rwkv(rkvl, v_first, p, B, T):
    N, nc = rkvl.shape
    R = RWKV_HEADS * HEAD_DIM
    TT = min(RWKV_TILE, T)
    nt = T // TT
    NC = TT // RWKV_CHUNK
    has_vres = v_first is not None
    row = lambda b, i: (b * nt + i, 0)
    const = lambda b, i: (0, 0)
    vec = lambda a: a.reshape(1, -1)
    ins = [rkvl]
    specs = [pl.BlockSpec((TT, nc), row)]
    if has_vres:
        ins.append(v_first)
        specs.append(pl.BlockSpec((TT, R), row))
    small = [vec(p["mu_rkv"]), vec(p["decay_w0"]), p["decay_b"].astype(BF16), vec(p["iclr_a0"]),
             p["iclr_b"].astype(BF16), p["gate_b"].astype(BF16), vec(p["k_k"]), vec(p["k_a"]),
             vec(p["r_k"]), vec(p["lnx_g"]), vec(p["lnx_b"])]
    if has_vres:
        small += [vec(p["vres_v0"]), p["vres_b"].astype(BF16)]
    ins += small
    specs += [pl.BlockSpec(a.shape, const) for a in small]
    out_shape = [jax.ShapeDtypeStruct((N, R), BF16)]
    out_specs = [pl.BlockSpec((TT, R), row)]
    if not has_vres:
        out_shape.append(jax.ShapeDtypeStruct((N, R), F32))
        out_specs.append(pl.BlockSpec((TT, R), row))
    scratch = [pltpu.VMEM((1, 3 * R + LORA_PAD), F32),
               pltpu.VMEM((HEAD_DIM, R), F32)]
    scratch += [pltpu.VMEM((TT, R), F32) for _ in range(8)]
    scratch += [pltpu.VMEM((NC, RWKV_CHUNK, R), F32) for _ in range(4)]
    res = pl.pallas_call(
        functools.partial(_rwkv_kernel, has_vres, TT),
        grid=(B, nt),
        in_specs=specs,
        out_specs=out_specs,
        out_shape=out_shape,
        scratch_shapes=scratch,
        compiler_params=_params("arbitrary", "arbitrary"),
        name="rwkv",
    )(*ins)
    if has_vres:
        return res[0], v_first
    return res[0], res[1]


def _bucket_boundaries():
    max_exact = N_BUCKETS // 2
    d = np.arange(0, 4 * MAX_DISTANCE, dtype=np.int64)
    nf = np.maximum(d, 1).astype(np.float32)
    large = max_exact + (np.log(nf / np.float32(max_exact)) / np.float32(math.log(MAX_DISTANCE / max_exact))
                         * np.float32(N_BUCKETS - max_exact)).astype(np.int32)
    large = np.minimum(large, N_BUCKETS - 1)
    bucket = np.where(d < max_exact, d, large)
    return [int(np.argmax(bucket >= j)) for j in range(max_exact + 1, N_BUCKETS)]


_BUCKET_STARTS = _bucket_boundaries()
KEY_NEG_INF = int(np.int32(np.array(-np.inf, np.float32).view(np.int32)) ^ np.int32(0x7FFFFFFF))
INT_MIN = -2 ** 31
INT_MAX = 2 ** 31 - 1
FAST_SELECT_STEPS = 2
FAST_SELECT_TRIPS = 10


def _sort_key(s):
    bits = pltpu.bitcast(s, I32)
    bits = jnp.where(bits == INT_MIN, 0, bits)
    return bits ^ ((bits >> 31) & jnp.int32(0x7FFFFFFF))


def _key_to_float(k):
    return pltpu.bitcast(k ^ ((k >> 31) & jnp.int32(0x7FFFFFFF)), F32)


def _dsa_kernel(T, ksel, q_ref, k_ref, vt_ref, qiw_ref, kix_ref, relb_ref, g_ref, out_ref,
                lhs_s, qpad_s, key_s, madd_s, bias_s, m_s, acc_s, s_s, p_s, cmax_s, alpha_s):
    TQ = TK = DSA_BLOCK
    H = ATTN_HEADS
    qb = pl.program_id(1)
    nch = qb + 1
    rows = lax.broadcasted_iota(I32, (TK, TQ), 0)
    cols = lax.broadcasted_iota(I32, (TK, TQ), 1)

    @pl.when((pl.program_id(0) == 0) & (qb == 0))
    def _():
        max_exact = N_BUCKETS // 2
        for off in range(2):
            d = cols - rows + off * TK
            bucket = jnp.where(d < max_exact, jnp.maximum(d, 0), max_exact)
            for start in _BUCKET_STARTS:
                bucket = bucket + jnp.where(d >= start, 1, 0)
            for h in range(H):
                far = relb_ref[N_BUCKETS - 1, h]
                tile = jnp.zeros((TK, TQ), F32)
                for bk in range(N_BUCKETS - 1):
                    tile = jnp.where(bucket == bk, (relb_ref[bk, h] - far) * LOG2E, tile)
                bias_s[h, off] = tile

    qi_t = (qiw_ref[:, :IDX_HEADS * IDX_DIM] * (IDX_DIM ** -0.5)).T
    for h in range(IDX_HEADS):
        qh = qi_t[h * IDX_DIM:(h + 1) * IDX_DIM, :]
        hi = qh.astype(BF16)
        lo = (qh - hi.astype(F32)).astype(BF16)
        lhs_s[h] = jnp.concatenate([hi, hi, lo, jnp.zeros_like(hi)], axis=0)
    w_t = qiw_ref[:, 512:640].T[64:64 + IDX_HEADS, :] * (IDX_HEADS ** -0.5)

    q_t = q_ref[...].astype(F32).T
    zeros_h = jnp.zeros((HEAD_DIM, TQ), BF16)
    for h in range(H):
        qh = q_t[h * HEAD_DIM:(h + 1) * HEAD_DIM, :].astype(BF16)
        qpad_s[h] = jnp.concatenate([qh, zeros_h] if h % 2 == 0 else [zeros_h, qh], axis=0)

    def scores(j):
        kx = kix_ref[pl.ds(pl.multiple_of(j * TK, TK), TK), :]
        acc = jnp.zeros((TK, TQ), F32)
        for h in range(IDX_HEADS):
            d = jnp.dot(kx, lhs_s[h], preferred_element_type=F32)
            acc = acc + w_t[h:h + 1, :] * jnp.maximum(d, 0.0)
        return acc

    def score_body(j, c):
        key_s[j] = _sort_key(scores(j))
        return c

    lax.fori_loop(0, qb, score_body, 0)
    key_s[qb] = _sort_key(jnp.where(rows <= cols, scores(qb), -jnp.inf))

    def count(pred):
        def body(j, acc):
            hit = jnp.where(pred(key_s[j]), 1, 0)
            return acc + jnp.sum(hit.reshape(TK // 8, 8, TQ), axis=0)
        acc = lax.fori_loop(0, nch, body, jnp.zeros((8, TQ), I32))
        return jnp.sum(acc, axis=0, keepdims=True)

    def write_mask(sel_of):
        def body(j, c):
            madd_s[j] = jnp.where(sel_of(key_s[j]), 0.0, NEG_BIG)
            return c
        lax.fori_loop(0, nch, body, 0)

    kf = float(ksel)
    n_adm = qb * TQ + lax.broadcasted_iota(I32, (1, TQ), 1) + 1
    trivial = n_adm <= ksel
    zero_k = jnp.zeros((1, TQ), I32)
    n_ge0 = count(lambda kk: kk >= zero_k)
    n_gt0 = count(lambda kk: kk > zero_k)
    positive = n_gt0 > ksel
    done0 = trivial | ((n_gt0 <= ksel) & (n_ge0 >= ksel))
    thr0 = jnp.where(trivial, KEY_NEG_INF, jnp.where(n_gt0 == ksel, 1, 0))
    need0 = jnp.where(n_gt0 < ksel, ksel - n_gt0, ksel)

    def minmax_body(j, c):
        kk = key_s[j].reshape(TK // 8, 8, TQ)
        return jnp.minimum(c[0], jnp.min(kk, axis=0)), jnp.maximum(c[1], jnp.max(kk, axis=0))

    mn, mx = lax.fori_loop(0, qb, minmax_body,
                           (jnp.full((8, TQ), INT_MAX, I32), jnp.full((8, TQ), INT_MIN, I32)))
    kd = key_s[qb].reshape(TK // 8, 8, TQ)
    mx = jnp.maximum(mx, jnp.max(kd, axis=0))
    mn = jnp.minimum(mn, jnp.min(jnp.where(kd > KEY_NEG_INF, kd, INT_MAX), axis=0))
    lo0 = _key_to_float(jnp.min(mn, axis=0, keepdims=True))
    hi0 = _key_to_float(jnp.max(mx, axis=0, keepdims=True) + 1)
    a0 = jnp.where(positive, 0.0, lo0)
    b0 = jnp.where(positive, hi0, 0.0)
    fa0 = jnp.where(positive, n_gt0, n_adm).astype(F32) - kf
    cb0 = jnp.where(positive, 0, n_ge0)
    as_int = lambda m: jnp.where(m, 1, 0)

    def any_lane(m):
        return jnp.max(as_int(m)) > 0

    def fast_cond(st):
        it, done, stuck = st[0], st[1], st[2]
        return (it < FAST_SELECT_TRIPS) & any_lane((done + stuck) == 0)

    def fast_body(st):
        it, st = st[0], st[1:]
        for _ in range(FAST_SELECT_STEPS):
            st = fast_step(st)
        return (it + 1,) + st

    def fast_step(st):
        done, stuck, thr, need, a, b, fa, fb, cb, last = st
        c = a + (b - a) * (fa / (fa - fb))
        c = jnp.where((c > a) & (c < b), c, 0.5 * a + 0.5 * b)
        inside = (c > a) & (c < b)
        ckey = _sort_key(c)
        cnt = count(lambda kk: kk >= ckey)
        fc = cnt.astype(F32) - kf
        hit = inside & (cnt == ksel) & (done == 0)
        thr = jnp.where(hit, ckey, thr)
        need = jnp.where(hit, ksel, need)
        done = jnp.where(hit, 1, done)
        stuck = jnp.where(inside, stuck, 1)
        up = inside & (cnt > ksel)
        dn = inside & (cnt < ksel)
        fb = jnp.where(up & (last == 1), 0.5 * fb, fb)
        fa = jnp.where(dn & (last == 0), 0.5 * fa, fa)
        a, fa = jnp.where(up, c, a), jnp.where(up, fc, fa)
        b, fb, cb = jnp.where(dn, c, b), jnp.where(dn, fc, fb), jnp.where(dn, cnt, cb)
        return done, stuck, thr, need, a, b, fa, fb, cb, jnp.where(up, 1, jnp.where(dn, 0, last))

    st = lax.while_loop(fast_cond, fast_body, (
        jnp.int32(0), as_int(done0), jnp.zeros((1, TQ), I32), thr0, need0,
        a0, b0, fa0, cb0.astype(F32) - kf, cb0, jnp.full((1, TQ), -1, I32)))
    done, thr, need, a, b, cb = st[1], st[3], st[4], st[5], st[6], st[9]
    open_lane = done == 0
    any_open = any_lane(open_lane)

    def close_cond(st):
        return any_lane((st[1] - st[0]) > 1)

    def close_body(st):
        ak, bk, cb = st
        mid = ak + ((bk - ak) >> 1)
        cnt = count(lambda kk: kk >= mid)
        ge = cnt >= ksel
        return jnp.where(ge, mid, ak), jnp.where(ge, bk, mid), jnp.where(ge, cb, cnt)

    ak, _, cb = lax.while_loop(close_cond, close_body, (
        jnp.where(open_lane, _sort_key(a), thr), jnp.where(open_lane, _sort_key(b), thr), cb))
    thr = jnp.where(open_lane, ak, thr)
    need = jnp.where(open_lane, ksel - cb, need)
    any_tie = any_open | any_lane(jnp.logical_not(trivial) & (n_gt0 < ksel) & (n_ge0 > ksel))

    @pl.when(jnp.logical_not(any_tie))
    def _():
        write_mask(lambda kk: (kk >= thr) & (kk > KEY_NEG_INF))

    @pl.when(any_tie)
    def _():
        kr = lax.broadcasted_iota(I32, (TK, TK), 0)
        kc = lax.broadcasted_iota(I32, (TK, TK), 1)
        lower = jnp.where(kc <= kr, 1.0, 0.0).astype(BF16)
        need_f = need.astype(F32)

        def body(j, run):
            kk = key_s[j]
            eq = kk == thr
            eqf = jnp.where(eq, 1.0, 0.0)
            rank = run + jnp.dot(lower, eqf.astype(BF16), preferred_element_type=F32)
            sel = ((kk > thr) | (eq & (rank <= need_f))) & (kk > KEY_NEG_INF)
            madd_s[j] = jnp.where(sel, 0.0, NEG_BIG)
            return run + jnp.sum(eqf, axis=0, keepdims=True)
        lax.fori_loop(0, nch, body, jnp.zeros((1, TQ), F32))

    m_s[...] = jnp.full(m_s.shape, NEG_BIG, F32)
    acc_s[...] = jnp.zeros(acc_s.shape, F32)

    def attend(j, bias_of):
        ks = pl.ds(pl.multiple_of(j * TK, TK), TK)
        ma = madd_s[j]
        for h in range(H):
            kp = k_ref[ks, LANES * (h // 2):LANES * (h // 2 + 1)]
            s = jnp.dot(kp, qpad_s[h], preferred_element_type=F32) + ma
            bias = bias_of(h)
            if bias is not None:
                s = s + bias
            s_s[h] = s
            cmax_s[h] = jnp.max(s.reshape(TK // 8, 8, TQ), axis=0)
        for h in range(H):
            m_old = m_s[h]
            m_new = jnp.maximum(m_old, jnp.max(cmax_s[h], axis=0, keepdims=True))
            alpha_s[h] = jnp.exp2(m_old - m_new)
            m_s[h] = m_new
            p_s[h] = jnp.exp2(s_s[h] - m_new).astype(BF16)
        for h in range(H):
            hs = slice(h * VT_ROWS, (h + 1) * VT_ROWS)
            pv = jnp.dot(vt_ref[j, hs, :], p_s[h], preferred_element_type=F32)
            acc_s[hs, :] = alpha_s[h] * acc_s[hs, :] + pv

    def far_body(j, c):
        attend(j, lambda h: None)
        return c

    lax.fori_loop(0, jnp.maximum(qb - 1, 0), far_body, 0)

    @pl.when(qb >= 1)
    def _():
        attend(qb - 1, lambda h: bias_s[h, 1])

    attend(qb, lambda h: bias_s[h, 0])

    outs = []
    for h in range(H):
        o = acc_s[h * VT_ROWS:h * VT_ROWS + HEAD_DIM, :] / acc_s[h * VT_ROWS + HEAD_DIM:h * VT_ROWS + HEAD_DIM + 1, :]
        ms = jnp.mean(o * o, axis=0, keepdims=True)
        outs.append(o * lax.rsqrt(ms + RMS_EPS))
    out_ref[...] = (jnp.concatenate(outs, axis=0).T * g_ref[...]).astype(out_ref.dtype)


def _dsa(qk, vt, qiw, kix, rel_bias, g, B, T):
    N = qk.shape[0]
    TQ = DSA_BLOCK
    nq = T // TQ
    A = ATTN_HEADS * HEAD_DIM
    ksel = min(TOPK_MAX, T // 4)
    assert MAX_DISTANCE <= TQ + 1
    return pl.pallas_call(
        functools.partial(_dsa_kernel, T, ksel),
        grid=(B, nq),
        in_specs=[pl.BlockSpec((TQ, A), lambda b, i: (b * nq + i, 0)),
                  pl.BlockSpec((T, A), lambda b, i: (b, 1)),
                  pl.BlockSpec((nq, ATTN_HEADS * VT_ROWS, TQ), lambda b, i: (b, 0, 0)),
                  pl.BlockSpec((TQ, IDX_COLS), lambda b, i: (b * nq + i, 0)),
                  pl.BlockSpec((T, 256), lambda b, i: (b, 0)),
                  pl.BlockSpec(memory_space=pltpu.SMEM),
                  pl.BlockSpec((1, A), lambda b, i: (0, 0))],
        out_specs=pl.BlockSpec((TQ, A), lambda b, i: (b * nq + i, 0)),
        out_shape=jax.ShapeDtypeStruct((N, A), BF16),
        scratch_shapes=[pltpu.VMEM((IDX_HEADS, 256, TQ), BF16),
                        pltpu.VMEM((ATTN_HEADS, LANES, TQ), BF16),
                        pltpu.VMEM((nq, TQ, TQ), I32),
                        pltpu.VMEM((nq, TQ, TQ), F32),
                        pltpu.VMEM((ATTN_HEADS, 2, TQ, TQ), F32),
                        pltpu.VMEM((ATTN_HEADS, 1, TQ), F32),
                        pltpu.VMEM((ATTN_HEADS * VT_ROWS, TQ), F32),
                        pltpu.VMEM((ATTN_HEADS, TQ, TQ), F32),
                        pltpu.VMEM((ATTN_HEADS, TQ, TQ), BF16),
                        pltpu.VMEM((ATTN_HEADS, 8, TQ), F32),
                        pltpu.VMEM((ATTN_HEADS, 1, TQ), F32)],
        compiler_params=_params("arbitrary", "arbitrary"),
        name="dsa",
    )(qk, qk, vt, qiw, kix, rel_bias, g.reshape(1, A))


def _outproj_kernel(x_ref, rw_ref, att_ref, w_ref, gt_ref, o_ref):
    R = rw_ref.shape[1]
    mixed = (jnp.dot(rw_ref[...], w_ref[:R, :], preferred_element_type=F32)
             + jnp.dot(att_ref[...], w_ref[R:, :], preferred_element_type=F32))
    o_ref[...] = x_ref[...] + gt_ref[...] * mixed


def _outproj(x2d, rw, att, w, gt, T, tm=512):
    N, D = x2d.shape
    nb = T // tm
    row = lambda i: (i, 0)
    return pl.pallas_call(
        _outproj_kernel,
        grid=(N // tm,),
        in_specs=[pl.BlockSpec((tm, D), row),
                  pl.BlockSpec((tm, rw.shape[1]), row),
                  pl.BlockSpec((tm, att.shape[1]), row),
                  pl.BlockSpec(w.shape, lambda i: (0, 0)),
                  pl.BlockSpec((None, 1, D), lambda i: (i // nb, 0, 0))],
        out_specs=pl.BlockSpec((tm, D), row),
        out_shape=jax.ShapeDtypeStruct((N, D), F32),
        compiler_params=_params("arbitrary"),
        name="outproj",
    )(x2d, rw, att, w, gt)


def _mlp_kernel(final, x_ref, g_ref, sc_ref, sh_ref, gt_ref, w1_ref, w2_ref, fg_ref, o_ref, h_s, acc_s):
    j = pl.program_id(1)

    @pl.when(j == 0)
    def _():
        h_s[...] = _norm_mod(x_ref[...], g_ref[...], sc_ref[...], sh_ref[...]).astype(BF16)
        acc_s[...] = jnp.zeros_like(acc_s)

    u = jnp.dot(h_s[...], w1_ref[...], preferred_element_type=F32)
    u = jnp.square(jnp.maximum(u, 0.0))
    acc_s[...] += jnp.dot(u.astype(BF16), w2_ref[...], preferred_element_type=F32)

    @pl.when(j == pl.num_programs(1) - 1)
    def _():
        y = x_ref[...] + gt_ref[...] * acc_s[...]
        if final:
            ms = jnp.mean(y * y, axis=-1, keepdims=True)
            y = y * lax.rsqrt(ms + RMS_EPS) * fg_ref[...]
        o_ref[...] = y


def _mlp(x2d, g, sc, sh, gt, w1, w2, final_g, final, T, fc=512):
    N, D = x2d.shape
    F = w1.shape[1]
    tm = min(MLP_TILE, T)
    nb = T // tm
    row = lambda i, j: (i, 0)
    per_b = lambda i, j: (i // nb, 0, 0)
    const = lambda i, j: (0, 0)
    return pl.pallas_call(
        functools.partial(_mlp_kernel, final),
        grid=(N // tm, F // fc),
        in_specs=[pl.BlockSpec((tm, D), row),
                  pl.BlockSpec((1, D), const),
                  pl.BlockSpec((None, 1, D), per_b),
                  pl.BlockSpec((None, 1, D), per_b),
                  pl.BlockSpec((None, 1, D), per_b),
                  pl.BlockSpec((D, fc), lambda i, j: (0, j)),
                  pl.BlockSpec((fc, D), lambda i, j: (j, 0)),
                  pl.BlockSpec((1, D), const)],
        out_specs=pl.BlockSpec((tm, D), row),
        out_shape=jax.ShapeDtypeStruct((N, D), F32),
        scratch_shapes=[pltpu.VMEM((tm, D), BF16), pltpu.VMEM((tm, D), F32)],
        compiler_params=_params("arbitrary", "arbitrary"),
        name="mlp",
    )(x2d, g.reshape(1, D), sc, sh, gt, w1, w2, final_g.reshape(1, D))


def _pad_cols(w, n):
    return jnp.pad(w, ((0, 0), (0, n - w.shape[1])))


def _split_bf16(w):
    hi = w.astype(BF16)
    return hi, (w - hi.astype(F32)).astype(BF16)


def _in_weights(l, w_in, mu_lora, decay_a, iclr_a, gate_a, vres_mu, vres_a):
    D = w_in.shape[1]
    R = RWKV_HEADS * HEAD_DIM
    w = w_in[l]
    mats = [(decay_a[l], mu_lora[l, 0]), (iclr_a[l], mu_lora[l, 1]), (gate_a[l], mu_lora[l, 2])]
    if l > 0:
        mats.append((vres_a[l - 1], vres_mu[l - 1]))
    now = _pad_cols(jnp.concatenate([a * (1.0 - mu)[:, None] for a, mu in mats], axis=1), LORA_PAD)
    prev = _pad_cols(jnp.concatenate([a * mu[:, None] for a, mu in mats], axis=1), LORA_PAD)
    wq = w[:, 3 * R:4 * R] * (HEAD_DIM ** -0.5 * LOG2E)
    wa = jnp.concatenate([w[:, :3 * R], now, prev, wq, w[:, 4 * R:5 * R]], axis=1).astype(BF16)
    wv = w[:, 5 * R:6 * R].reshape(D, ATTN_HEADS, HEAD_DIM)
    wv = jnp.pad(wv, ((0, 0), (0, 0), (0, VT_ROWS - HEAD_DIM))).reshape(D, ATTN_HEADS * VT_ROWS)
    wvt = wv.T.astype(BF16)
    wbh, wbl = _split_bf16(_pad_cols(w[:, 6 * R:], IDX_COLS))
    return wa, wvt, wbh, wbl


def kernel(x, c, w_ada, b_ada, norm1_g, norm2_g, w_in, mu_rkv, mu_lora, decay_w0, decay_a, decay_b, iclr_a0, iclr_a, iclr_b, gate_a, gate_b, k_k, k_a, r_k, lnx_g, lnx_b, vres_mu, vres_v0, vres_a, vres_b, attn_out_g, rel_bias, w_out, w_mlp1, w_mlp2, final_g):
    B, T, D = x.shape
    depth = w_in.shape[0]
    mod = _adaln(c, w_ada, b_ada)
    mod = mod.reshape(depth, B, 6, 1, D).transpose(0, 2, 1, 3, 4)
    x2d = x.reshape(B * T, D)
    v_first = None
    for l in range(depth):
        sh1, sc1, gt1, sh2, sc2, gt2 = (mod[l, i] for i in range(6))
        wa, wvt, wbh, wbl = _in_weights(l, w_in, mu_lora, decay_a, iclr_a, gate_a, vres_mu, vres_a)
        rkvl, qk, vt, qiw, kix = _inproj(x2d, norm1_g[l], sc1, sh1, wa, wvt, wbh, wbl, T)
        p = dict(mu_rkv=mu_rkv[l], decay_w0=decay_w0[l], decay_b=decay_b[l], iclr_a0=iclr_a0[l],
                 iclr_b=iclr_b[l], gate_b=gate_b[l], k_k=k_k[l], k_a=k_a[l], r_k=r_k[l],
                 lnx_g=lnx_g[l], lnx_b=lnx_b[l])
        if l > 0:
            p.update(vres_v0=vres_v0[l - 1], vres_b=vres_b[l - 1])
        rw, v_first = _rwkv(rkvl, v_first, p, B, T)
        att = _dsa(qk, vt, qiw, kix, rel_bias, attn_out_g[l], B, T)
        x2d = _outproj(x2d, rw, att, w_out[l].astype(BF16), gt1, T)
        x2d = _mlp(x2d, norm2_g[l], sc2, sh2, gt2, w_mlp1[l].astype(BF16), w_mlp2[l].astype(BF16),
                   final_g, l == depth - 1, T)
    return x2d.reshape(B, T, D)
```

```python
import functools
import math

import numpy as np
import jax
import jax.numpy as jnp
from jax import lax
from jax.experimental import pallas as pl
from jax.experimental.pallas import tpu as pltpu

F32 = jnp.float32
BF16 = jnp.bfloat16
I32 = jnp.int32

HEAD_DIM = 64
RWKV_HEADS = 8
ATTN_HEADS = 8
IDX_HEADS = 8
IDX_DIM = 64
TOPK_MAX = 256
QUERY_BLOCK = 128
N_BUCKETS = 32
MAX_DISTANCE = 128
RMS_EPS = 1e-6
LNX_EPS = 64e-5

LANES = 128
VMEM_LIMIT = 56 * 1024 * 1024
RWKV_CHUNK = 64
RWKV_GROUP = 256
RWKV_TILE = 512
RWKV_UNROLL = 8
DSA_BLOCK = 256
VT_ROWS = 80
LOG2E = math.log2(math.e)
INPROJ_TILE = 512
MLP_TILE = 1024
LORA_PAD = 384
IDX_COLS = 640
NEG_BIG = -1e30
HI = lax.Precision.HIGHEST
NT = (((1,), (1,)), ((), ()))


def _bdot(a, b):
    return jnp.dot(a.astype(BF16), b.astype(BF16), preferred_element_type=F32)


def _hdot(a, b):
    return jnp.dot(a, b, precision=HI, preferred_element_type=F32)


def _params(*sem):
    return pltpu.CompilerParams(dimension_semantics=sem, vmem_limit_bytes=VMEM_LIMIT)


def _adaln_kernel(c_ref, w_ref, b_ref, o_ref):
    c = c_ref[...]
    c_act = c * jax.nn.sigmoid(c)
    o_ref[...] = _hdot(c_act, w_ref[...]) + b_ref[...]


def _adaln(c, w_ada, b_ada):
    L, D, D6 = w_ada.shape
    B = c.shape[0]
    cb = 1024
    return pl.pallas_call(
        _adaln_kernel,
        grid=(L, D6 // cb),
        in_specs=[pl.BlockSpec((B, D), lambda l, j: (0, 0)),
                  pl.BlockSpec((None, D, cb), lambda l, j: (l, 0, j)),
                  pl.BlockSpec((None, 1, cb), lambda l, j: (l, 0, j))],
        out_specs=pl.BlockSpec((None, B, cb), lambda l, j: (l, 0, j)),
        out_shape=jax.ShapeDtypeStruct((L, B, D6), F32),
        compiler_params=_params("arbitrary", "arbitrary"),
        name="adaln",
    )(c, w_ada, b_ada.reshape(L, 1, D6))


def _norm_mod(x, g, sc, sh):
    ms = jnp.mean(x * x, axis=-1, keepdims=True)
    return (x * lax.rsqrt(ms + RMS_EPS) * g) * (1.0 + sc) + sh


def _inproj_kernel(x_ref, g_ref, sc_ref, sh_ref, wa_ref, wvt_ref, wbh_ref, wbl_ref,
                   rkvl_ref, qk_ref, vt_ref, qiw_ref, kix_ref):
    h = _norm_mod(x_ref[...], g_ref[...], sc_ref[...], sh_ref[...])
    hb = h.astype(BF16)
    hl = (h - hb.astype(F32)).astype(BF16)
    pa = jnp.dot(hb, wa_ref[...], preferred_element_type=F32)
    nr = rkvl_ref.shape[1]
    rkvl_ref[...] = pa[:, :nr]
    qk_ref[...] = pa[:, nr:].astype(BF16)
    vt = lax.dot_general(wvt_ref[...], hb, NT, preferred_element_type=F32)
    ones_row = lax.broadcasted_iota(I32, vt.shape, 0) % VT_ROWS == HEAD_DIM
    vt = jnp.where(ones_row, 1.0, vt).astype(BF16)
    for j in range(vt_ref.shape[0]):
        vt_ref[j] = vt[:, j * DSA_BLOCK:(j + 1) * DSA_BLOCK]
    pb = (jnp.dot(hb, wbh_ref[...], preferred_element_type=F32)
          + jnp.dot(hb, wbl_ref[...], preferred_element_type=F32)
          + jnp.dot(hl, wbh_ref[...], preferred_element_type=F32))
    qiw_ref[...] = pb
    ki = pb[:, 512:576]
    kh = ki.astype(BF16)
    kl = (ki - kh.astype(F32)).astype(BF16)
    kix_ref[...] = jnp.concatenate([kh, kl, kh, jnp.zeros_like(kh)], axis=-1)


def _inproj(x2d, g, sc, sh, wa, wvt, wbh, wbl, T):
    N, D = x2d.shape
    nblk = INPROJ_TILE // DSA_BLOCK
    tm = INPROJ_TILE
    na = wa.shape[1]
    A = wvt.shape[0]
    nr = 3 * 512 + 2 * LORA_PAD
    nb = T // tm
    row = lambda i: (i, 0)
    per_b = lambda i: (i // nb, 0, 0)
    const = lambda i: (0, 0)
    once = pl.Buffered(1)
    return pl.pallas_call(
        _inproj_kernel,
        grid=(N // tm,),
        in_specs=[pl.BlockSpec((tm, D), row),
                  pl.BlockSpec((1, D), const),
                  pl.BlockSpec((None, 1, D), per_b),
                  pl.BlockSpec((None, 1, D), per_b),
                  pl.BlockSpec((D, na), const, pipeline_mode=once),
                  pl.BlockSpec((A, D), const, pipeline_mode=once),
                  pl.BlockSpec((D, IDX_COLS), const, pipeline_mode=once),
                  pl.BlockSpec((D, IDX_COLS), const, pipeline_mode=once)],
        out_specs=[pl.BlockSpec((tm, nr), row),
                   pl.BlockSpec((tm, na - nr), row),
                   pl.BlockSpec((nblk, A, DSA_BLOCK), lambda i: (i, 0, 0)),
                   pl.BlockSpec((tm, IDX_COLS), row),
                   pl.BlockSpec((tm, 256), row)],
        out_shape=[jax.ShapeDtypeStruct((N, nr), F32),
                   jax.ShapeDtypeStruct((N, na - nr), BF16),
                   jax.ShapeDtypeStruct((N // DSA_BLOCK, A, DSA_BLOCK), BF16),
                   jax.ShapeDtypeStruct((N, IDX_COLS), F32),
                   jax.ShapeDtypeStruct((N, 256), BF16)],
        compiler_params=_params("arbitrary"),
        name="inproj",
    )(x2d, g.reshape(1, D), sc, sh, wa, wvt, wbh, wbl)


def _head_ones(n):
    r = lax.broadcasted_iota(I32, (n, n), 0) // HEAD_DIM
    c = lax.broadcasted_iota(I32, (n, n), 1) // HEAD_DIM
    return jnp.where(r == c, 1.0, 0.0).astype(F32)


def _split(x):
    hi = x.astype(BF16)
    return hi, (x - hi.astype(F32)).astype(BF16)


def _split3(x):
    hi = x.astype(BF16)
    r = x - hi.astype(F32)
    mid = r.astype(BF16)
    return hi, mid, (r - mid.astype(F32)).astype(BF16)


def _block_diag(w, blocks):
    w = w.astype(BF16)
    return jnp.concatenate([jnp.where(m, w, jnp.zeros_like(w)) for m in blocks], axis=0)


def _block_diag_t(w, blocks):
    return jnp.concatenate([jnp.where(m, w, 0.0) for m in blocks], axis=0).T


def _fold_blocks(w):
    n = w.shape[1] // HEAD_DIM
    out = w[:HEAD_DIM]
    for h in range(1, n):
        out = out + w[h * HEAD_DIM:(h + 1) * HEAD_DIM]
    return out


def _mm(a, w):
    return jnp.dot(a.astype(BF16), w, preferred_element_type=F32)


def _rwkv_kernel(has_vres, TT, *refs):
    C = RWKV_CHUNK
    R = RWKV_HEADS * HEAD_DIM
    G = RWKV_GROUP
    NG = R // G
    NC = TT // C
    if has_vres:
        (rkvl_ref, vf_ref, mu_ref, w0_ref, db_ref, a0_ref, ib_ref, gb_ref, kk_ref, ka_ref,
         rk_ref, lg_ref, lb_ref, v0_ref, vb_ref, out_ref,
         prev_ref, S_ref, r_s, k_s, v_s, kk_s, b_s, lw_s, cum_s, y_s, q_s, y0_s, m_s, g_s) = refs
    else:
        (rkvl_ref, mu_ref, w0_ref, db_ref, a0_ref, ib_ref, gb_ref, kk_ref, ka_ref,
         rk_ref, lg_ref, lb_ref, out_ref, vfo_ref,
         prev_ref, S_ref, r_s, k_s, v_s, kk_s, b_s, lw_s, cum_s, y_s, q_s, y0_s, m_s, g_s) = refs

    @pl.when(pl.program_id(1) == 0)
    def _():
        prev_ref[...] = jnp.zeros_like(prev_ref)
        S_ref[...] = jnp.zeros_like(S_ref)

    row0 = lax.broadcasted_iota(I32, (TT, 1), 0) == 0
    prev = prev_ref[...]

    def shift(z, p):
        return jnp.where(row0, p, pltpu.roll(z, 1, 0))

    rkv = rkvl_ref[:, :3 * R]
    p1 = rkvl_ref[:, 3 * R:3 * R + LORA_PAD]
    p2 = rkvl_ref[:, 3 * R + LORA_PAD:]
    rkv_new = rkv + (shift(rkv, prev[:, :3 * R]) - rkv) * mu_ref[...]
    lora = p1 + shift(p2, prev[:, 3 * R:])
    prev_ref[...] = jnp.concatenate([rkv[TT - 1:TT, :], p2[TT - 1:TT, :]], axis=-1)

    r = rkv_new[:, :R]
    k = rkv_new[:, R:2 * R]
    v = rkv_new[:, 2 * R:]
    ones_h = _head_ones(R).astype(BF16)

    def head_sum(x):
        hi, lo = _split(x)
        return (jnp.dot(hi, ones_h, preferred_element_type=F32)
                + jnp.dot(lo, ones_h, preferred_element_type=F32))

    wlog = w0_ref[...] + _bdot(jnp.tanh(lora[:, 0:64]), db_ref[...])
    z = -wlog
    wlog = -(jnp.maximum(z, 0.0) + jnp.log(1.0 + jnp.exp(-jnp.abs(z)))) - 0.5
    lw = -jnp.exp(wlog)
    lw_s[...] = lw
    slab = min(TT, 4 * C)
    tr = lax.broadcasted_iota(I32, (slab, slab), 0)
    tc = lax.broadcasted_iota(I32, (slab, slab), 1)
    tri = jnp.where((tc <= tr) & (tr // C == tc // C), 1.0, 0.0).astype(BF16)
    parts = _split3(lw)
    for s0 in range(0, TT, slab):
        cum_s[s0:s0 + slab, :] = sum(jnp.dot(tri, part[s0:s0 + slab], preferred_element_type=F32) for part in parts)
    a = jax.nn.sigmoid(a0_ref[...] + _bdot(lora[:, 64:128], ib_ref[...]))
    g = _bdot(jax.nn.sigmoid(lora[:, 128:256]), gb_ref[...])
    kkr = k * kk_ref[...]
    kk = kkr / jnp.maximum(jnp.sqrt(head_sum(kkr * kkr)), 1e-12)
    k = k * (1.0 + (a - 1.0) * ka_ref[...])
    if has_vres:
        v = v + (vf_ref[...] - v) * jax.nn.sigmoid(v0_ref[...] + _bdot(lora[:, 256:288], vb_ref[...]))
    else:
        vfo_ref[...] = v
    r_s[...] = r
    k_s[...] = k
    v_s[...] = v
    kk_s[...] = kk
    b_s[...] = kk * a

    lane_g = lax.broadcasted_iota(I32, (C, RWKV_GROUP), 1)
    row_g = lax.broadcasted_iota(I32, (C, RWKV_GROUP), 0)
    blocks = [lane_g // HEAD_DIM == h for h in range(RWKV_GROUP // HEAD_DIM)]
    local = lane_g % HEAD_DIM
    strict = local < row_g
    incl = local <= row_g
    diag = local == row_g
    eye_cat = jnp.where(diag, 1.0, 0.0).astype(F32)
    levels = int(math.log2(C)) - 1

    def local_chunk(it, carry):
        chains = [(u, slice(gi * G, (gi + 1) * G)) for u in range(RWKV_UNROLL) for gi in range(NG)]
        each = lambda f: [f(i) for i in range(len(chains))]
        rd, kkd, kt, bt, kc, bc, vv, plast = ([] for _ in range(8))
        for u in range(RWKV_UNROLL):
            sl = pl.ds(pl.multiple_of((it * RWKV_UNROLL + u) * C, C), C)
            lw = lw_s[sl, :]
            cum = cum_s[sl, :]
            cl = cum[C - 1:C, :]
            pinv = jnp.exp(-cum)
            pc = jnp.exp(cl - cum)
            k_a = k_s[sl, :]
            b_a = b_s[sl, :]
            full = (r_s[sl, :] * jnp.exp(cum), kk_s[sl, :] * jnp.exp(cum - lw), k_a * pinv, b_a * pinv,
                    k_a * pc, b_a * pc, v_s[sl, :], jnp.exp(cl))
            for dst, x in zip((rd, kkd, kt, bt, kc, bc, vv, plast), full):
                dst.extend(x[:, gs] for (uu, gs) in chains if uu == u)
        kt_w = each(lambda i: _block_diag_t(kt[i], blocks).astype(BF16))
        bt_w = each(lambda i: _block_diag_t(bt[i], blocks).astype(BF16))
        bc_t = each(lambda i: _fold_blocks(_block_diag_t(bc[i], blocks)))
        kc_t = each(lambda i: _fold_blocks(_block_diag_t(kc[i], blocks)))
        rr = each(lambda i: _mm(jnp.concatenate([kkd[i], rd[i]], axis=0),
                                jnp.concatenate([kt_w[i], bt_w[i]], axis=1)))
        akk = each(lambda i: jnp.where(strict, rr[i][:C, :G], 0.0))
        ark = each(lambda i: jnp.where(incl, rr[i][C:, :G], 0.0))
        arb = each(lambda i: jnp.where(incl, rr[i][C:, G:], 0.0))
        npow = each(lambda i: jnp.where(strict, -rr[i][:C, G:], 0.0))
        x = each(lambda i: eye_cat + npow[i])
        npow = each(lambda i: _mm(npow[i], _block_diag(npow[i], blocks)))
        for _ in range(levels - 1):
            rr = each(lambda i: _mm(jnp.concatenate([npow[i], x[i]], axis=0), _block_diag(npow[i], blocks)))
            npow = each(lambda i: rr[i][:C])
            x = each(lambda i: x[i] + rr[i][C:])
        rr = each(lambda i: _mm(x[i], _block_diag(npow[i], blocks)))
        x = each(lambda i: x[i] + rr[i])
        rv = each(lambda i: _mm(jnp.concatenate([akk[i], ark[i], kc_t[i]], axis=0), _block_diag(vv[i], blocks)))
        ru = each(lambda i: _mm(x[i], jnp.concatenate([_block_diag(kkd[i], blocks),
                                                       _block_diag(rv[i][:C], blocks)], axis=1)))
        rr = each(lambda i: _mm(jnp.concatenate([arb[i], bc_t[i]], axis=0),
                                jnp.concatenate([_block_diag(ru[i][:, :G], blocks),
                                                 _block_diag(ru[i][:, G:], blocks)], axis=1)))
        for i, (u, gs) in enumerate(chains):
            c = it * RWKV_UNROLL + u
            q_s[c, :, gs] = rd[i] - rr[i][:C, :G]
            y0_s[c, :, gs] = rv[i][C:2 * C] - rr[i][:C, G:]
            dterm = jnp.where(diag, jnp.broadcast_to(plast[i], (C, G)), 0.0)
            m_s[c, :, gs] = dterm - rr[i][C:, :G]
            g_s[c, :, gs] = rv[i][2 * C:] - rr[i][C:, G:]
        return carry

    lax.fori_loop(0, NC // RWKV_UNROLL, local_chunk, 0)

    def scan_chunk(c, carry):
        sl = pl.ds(pl.multiple_of(c * C, C), C)
        for gi in range(NG):
            gs = slice(gi * RWKV_GROUP, (gi + 1) * RWKV_GROUP)
            rr = _mm(jnp.concatenate([q_s[c, :, gs], m_s[c, :, gs]], axis=0), _block_diag(S_ref[:, gs], blocks))
            y_s[sl, gs] = rr[:C] + y0_s[c, :, gs]
            S_ref[:, gs] = rr[C:] + g_s[c, :, gs]
        return carry

    lax.fori_loop(0, NC, scan_chunk, 0)

    y = y_s[...]
    inv_n = 1.0 / HEAD_DIM
    mean = head_sum(y) * inv_n
    yc = y - mean
    var = head_sum(yc * yc) * inv_n
    yn = yc * lax.rsqrt(var + LNX_EPS) * lg_ref[...] + lb_ref[...]
    bonus = head_sum(r_s[...] * k_s[...] * rk_ref[...]) * v_s[...]
    out_ref[...] = ((yn + bonus) * g).astype(out_ref.dtype)


def _rwkv(rkvl, v_first, p, B, T):
    N, nc = rkvl.shape
    R = RWKV_HEADS * HEAD_DIM
    TT = min(RWKV_TILE, T)
    nt = T // TT
    NC = TT // RWKV_CHUNK
    has_vres = v_first is not None
    row = lambda b, i: (b * nt + i, 0)
    const = lambda b, i: (0, 0)
    vec = lambda a: a.reshape(1, -1)
    ins = [rkvl]
    specs = [pl.BlockSpec((TT, nc), row)]
    if has_vres:
        ins.append(v_first)
        specs.append(pl.BlockSpec((TT, R), row))
    small = [vec(p["mu_rkv"]), vec(p["decay_w0"]), p["decay_b"].astype(BF16), vec(p["iclr_a0"]),
             p["iclr_b"].astype(BF16), p["gate_b"].astype(BF16), vec(p["k_k"]), vec(p["k_a"]),
             vec(p["r_k"]), vec(p["lnx_g"]), vec(p["lnx_b"])]
    if has_vres:
        small += [vec(p["vres_v0"]), p["vres_b"].astype(BF16)]
    ins += small
    specs += [pl.BlockSpec(a.shape, const) for a in small]
    out_shape = [jax.ShapeDtypeStruct((N, R), BF16)]
    out_specs = [pl.BlockSpec((TT, R), row)]
    if not has_vres:
        out_shape.append(jax.ShapeDtypeStruct((N, R), F32))
        out_specs.append(pl.BlockSpec((TT, R), row))
    scratch = [pltpu.VMEM((1, 3 * R + LORA_PAD), F32),
               pltpu.VMEM((HEAD_DIM, R), F32)]
    scratch += [pltpu.VMEM((TT, R), F32) for _ in range(8)]
    scratch += [pltpu.VMEM((NC, RWKV_CHUNK, R), F32) for _ in range(4)]
    res = pl.pallas_call(
        functools.partial(_rwkv_kernel, has_vres, TT),
        grid=(B, nt),
        in_specs=specs,
        out_specs=out_specs,
        out_shape=out_shape,
        scratch_shapes=scratch,
        compiler_params=_params("arbitrary", "arbitrary"),
        name="rwkv",
    )(*ins)
    if has_vres:
        return res[0], v_first
    return res[0], res[1]


def _bucket_boundaries():
    max_exact = N_BUCKETS // 2
    d = np.arange(0, 4 * MAX_DISTANCE, dtype=np.int64)
    nf = np.maximum(d, 1).astype(np.float32)
    large = max_exact + (np.log(nf / np.float32(max_exact)) / np.float32(math.log(MAX_DISTANCE / max_exact))
                         * np.float32(N_BUCKETS - max_exact)).astype(np.int32)
    large = np.minimum(large, N_BUCKETS - 1)
    bucket = np.where(d < max_exact, d, large)
    return [int(np.argmax(bucket >= j)) for j in range(max_exact + 1, N_BUCKETS)]


_BUCKET_STARTS = _bucket_boundaries()
KEY_NEG_INF = int(np.int32(np.array(-np.inf, np.float32).view(np.int32)) ^ np.int32(0x7FFFFFFF))
INT_MIN = -2 ** 31
INT_MAX = 2 ** 31 - 1
FAST_SELECT_STEPS = 2
FAST_SELECT_TRIPS = 10


def _sort_key(s):
    bits = pltpu.bitcast(s, I32)
    bits = jnp.where(bits == INT_MIN, 0, bits)
    return bits ^ ((bits >> 31) & jnp.int32(0x7FFFFFFF))


def _key_to_float(k):
    return pltpu.bitcast(k ^ ((k >> 31) & jnp.int32(0x7FFFFFFF)), F32)


def _dsa_kernel(T, ksel, q_ref, k_ref, vt_ref, qiw_ref, kix_ref, relb_ref, g_ref, out_ref,
                lhs_s, qpad_s, key_s, madd_s, bias_s, m_s, acc_s, s_s, p_s, cmax_s, alpha_s):
    TQ = TK = DSA_BLOCK
    H = ATTN_HEADS
    qb = pl.program_id(1)
    nch = qb + 1
    rows = lax.broadcasted_iota(I32, (TK, TQ), 0)
    cols = lax.broadcasted_iota(I32, (TK, TQ), 1)

    @pl.when((pl.program_id(0) == 0) & (qb == 0))
    def _():
        max_exact = N_BUCKETS // 2
        for off in range(2):
            d = cols - rows + off * TK
            bucket = jnp.where(d < max_exact, jnp.maximum(d, 0), max_exact)
            for start in _BUCKET_STARTS:
                bucket = bucket + jnp.where(d >= start, 1, 0)
            for h in range(H):
                far = relb_ref[N_BUCKETS - 1, h]
                tile = jnp.zeros((TK, TQ), F32)
                for bk in range(N_BUCKETS - 1):
                    tile = jnp.where(bucket == bk, (relb_ref[bk, h] - far) * LOG2E, tile)
                bias_s[h, off] = tile

    qi_t = (qiw_ref[:, :IDX_HEADS * IDX_DIM] * (IDX_DIM ** -0.5)).T
    for h in range(IDX_HEADS):
        qh = qi_t[h * IDX_DIM:(h + 1) * IDX_DIM, :]
        hi = qh.astype(BF16)
        lo = (qh - hi.astype(F32)).astype(BF16)
        lhs_s[h] = jnp.concatenate([hi, hi, lo, jnp.zeros_like(hi)], axis=0)
    w_t = qiw_ref[:, 512:640].T[64:64 + IDX_HEADS, :] * (IDX_HEADS ** -0.5)

    q_t = q_ref[...].astype(F32).T
    zeros_h = jnp.zeros((HEAD_DIM, TQ), BF16)
    for h in range(H):
        qh = q_t[h * HEAD_DIM:(h + 1) * HEAD_DIM, :].astype(BF16)
        qpad_s[h] = jnp.concatenate([qh, zeros_h] if h % 2 == 0 else [zeros_h, qh], axis=0)

    def scores(j):
        kx = kix_ref[pl.ds(pl.multiple_of(j * TK, TK), TK), :]
        acc = jnp.zeros((TK, TQ), F32)
        for h in range(IDX_HEADS):
            d = jnp.dot(kx, lhs_s[h], preferred_element_type=F32)
            acc = acc + w_t[h:h + 1, :] * jnp.maximum(d, 0.0)
        return acc

    def stats(st, kk, mn_of):
        fold = lambda z: z.reshape(TK // 8, 8, TQ)
        ge0, gt0, mn, mx = st
        return (ge0 + jnp.sum(fold(jnp.where(kk >= 0, 1, 0)), axis=0),
                gt0 + jnp.sum(fold(jnp.where(kk > 0, 1, 0)), axis=0),
                jnp.minimum(mn, jnp.min(fold(mn_of(kk)), axis=0)),
                jnp.maximum(mx, jnp.max(fold(kk), axis=0)))

    def score_body(j, st):
        kk = _sort_key(scores(j))
        key_s[j] = kk
        return stats(st, kk, lambda z: z)

    st = lax.fori_loop(0, qb, score_body, (jnp.zeros((8, TQ), I32), jnp.zeros((8, TQ), I32),
                                           jnp.full((8, TQ), INT_MAX, I32), jnp.full((8, TQ), INT_MIN, I32)))
    kd = _sort_key(jnp.where(rows <= cols, scores(qb), -jnp.inf))
    key_s[qb] = kd
    st = stats(st, kd, lambda z: jnp.where(z > KEY_NEG_INF, z, INT_MAX))
    n_ge0, n_gt0 = (jnp.sum(z, axis=0, keepdims=True) for z in st[:2])
    lo0 = _key_to_float(jnp.min(st[2], axis=0, keepdims=True))
    hi0 = _key_to_float(jnp.max(st[3], axis=0, keepdims=True) + 1)

    def count(pred):
        def body(j, acc):
            hit = jnp.where(pred(key_s[j]), 1, 0)
            return acc + jnp.sum(hit.reshape(TK // 8, 8, TQ), axis=0)
        acc = lax.fori_loop(0, nch, body, jnp.zeros((8, TQ), I32))
        return jnp.sum(acc, axis=0, keepdims=True)

    def write_mask(sel_of):
        def body(j, c):
            madd_s[j] = jnp.where(sel_of(key_s[j]), 0.0, NEG_BIG)
            return c
        lax.fori_loop(0, nch, body, 0)

    kf = float(ksel)
    n_adm = qb * TQ + lax.broadcasted_iota(I32, (1, TQ), 1) + 1
    trivial = n_adm <= ksel
    positive = n_gt0 > ksel
    done0 = trivial | ((n_gt0 <= ksel) & (n_ge0 >= ksel))
    thr0 = jnp.where(trivial, KEY_NEG_INF, jnp.where(n_gt0 == ksel, 1, 0))
    need0 = jnp.where(n_gt0 < ksel, ksel - n_gt0, ksel)
    a0 = jnp.where(positive, 0.0, lo0)
    b0 = jnp.where(positive, hi0, 0.0)
    fa0 = jnp.where(positive, n_gt0, n_adm).astype(F32) - kf
    cb0 = jnp.where(positive, 0, n_ge0)
    as_int = lambda m: jnp.where(m, 1, 0)

    def any_lane(m):
        return jnp.max(as_int(m)) > 0

    def fast_cond(st):
        it, done, stuck = st[0], st[1], st[2]
        return (it < FAST_SELECT_TRIPS) & any_lane((done + stuck) == 0)

    def fast_body(st):
        it, st = st[0], st[1:]
        for _ in range(FAST_SELECT_STEPS):
            st = fast_step(st)
        return (it + 1,) + st

    def fast_step(st):
        done, stuck, thr, need, a, b, fa, fb, cb, last = st
        c = a + (b - a) * (fa / (fa - fb))
        c = jnp.where((c > a) & (c < b), c, 0.5 * a + 0.5 * b)
        inside = (c > a) & (c < b)
        ckey = _sort_key(c)
        cnt = count(lambda kk: kk >= ckey)
        fc = cnt.astype(F32) - kf
        hit = inside & (cnt == ksel) & (done == 0)
        thr = jnp.where(hit, ckey, thr)
        need = jnp.where(hit, ksel, need)
        done = jnp.where(hit, 1, done)
        stuck = jnp.where(inside, stuck, 1)
        up = inside & (cnt > ksel)
        dn = inside & (cnt < ksel)
        fb = jnp.where(up & (last == 1), 0.5 * fb, fb)
        fa = jnp.where(dn & (last == 0), 0.5 * fa, fa)
        a, fa = jnp.where(up, c, a), jnp.where(up, fc, fa)
        b, fb, cb = jnp.where(dn, c, b), jnp.where(dn, fc, fb), jnp.where(dn, cnt, cb)
        return done, stuck, thr, need, a, b, fa, fb, cb, jnp.where(up, 1, jnp.where(dn, 0, last))

    st = lax.while_loop(fast_cond, fast_body, (
        jnp.int32(0), as_int(done0), jnp.zeros((1, TQ), I32), thr0, need0,
        a0, b0, fa0, cb0.astype(F32) - kf, cb0, jnp.full((1, TQ), -1, I32)))
    done, thr, need, a, b, cb = st[1], st[3], st[4], st[5], st[6], st[9]
    open_lane = done == 0
    any_open = any_lane(open_lane)

    def close_cond(st):
        return any_lane((st[1] - st[0]) > 1)

    def close_body(st):
        ak, bk, cb = st
        mid = ak + ((bk - ak) >> 1)
        cnt = count(lambda kk: kk >= mid)
        ge = cnt >= ksel
        return jnp.where(ge, mid, ak), jnp.where(ge, bk, mid), jnp.where(ge, cb, cnt)

    ak, _, cb = lax.while_loop(close_cond, close_body, (
        jnp.where(open_lane, _sort_key(a), thr), jnp.where(open_lane, _sort_key(b), thr), cb))
    thr = jnp.where(open_lane, ak, thr)
    need = jnp.where(open_lane, ksel - cb, need)
    any_tie = any_open | any_lane(jnp.logical_not(trivial) & (n_gt0 < ksel) & (n_ge0 > ksel))

    @pl.when(jnp.logical_not(any_tie))
    def _():
        write_mask(lambda kk: (kk >= thr) & (kk > KEY_NEG_INF))

    @pl.when(any_tie)
    def _():
        kr = lax.broadcasted_iota(I32, (TK, TK), 0)
        kc = lax.broadcasted_iota(I32, (TK, TK), 1)
        lower = jnp.where(kc <= kr, 1.0, 0.0).astype(BF16)
        need_f = need.astype(F32)

        def body(j, run):
            kk = key_s[j]
            eq = kk == thr
            eqf = jnp.where(eq, 1.0, 0.0)
            rank = run + jnp.dot(lower, eqf.astype(BF16), preferred_element_type=F32)
            sel = ((kk > thr) | (eq & (rank <= need_f))) & (kk > KEY_NEG_INF)
            madd_s[j] = jnp.where(sel, 0.0, NEG_BIG)
            return run + jnp.sum(eqf, axis=0, keepdims=True)
        lax.fori_loop(0, nch, body, jnp.zeros((1, TQ), F32))

    m_s[...] = jnp.full(m_s.shape, NEG_BIG, F32)
    acc_s[...] = jnp.zeros(acc_s.shape, F32)

    def attend(j, bias_of):
        ks = pl.ds(pl.multiple_of(j * TK, TK), TK)
        ma = madd_s[j]
        for h in range(H):
            kp = k_ref[ks, LANES * (h // 2):LANES * (h // 2 + 1)]
            s = jnp.dot(kp, qpad_s[h], preferred_element_type=F32) + ma
            bias = bias_of(h)
            if bias is not None:
                s = s + bias
            s_s[h] = s
            cmax_s[h] = jnp.max(s.reshape(TK // 8, 8, TQ), axis=0)
        for h in range(H):
            m_old = m_s[h]
            m_new = jnp.maximum(m_old, jnp.max(cmax_s[h], axis=0, keepdims=True))
            alpha_s[h] = jnp.exp2(m_old - m_new)
            m_s[h] = m_new
            p_s[h] = jnp.exp2(s_s[h] - m_new).astype(BF16)
        for h in range(H):
            hs = slice(h * VT_ROWS, (h + 1) * VT_ROWS)
            pv = jnp.dot(vt_ref[j, hs, :], p_s[h], preferred_element_type=F32)
            acc_s[hs, :] = alpha_s[h] * acc_s[hs, :] + pv

    def far_body(j, c):
        attend(j, lambda h: None)
        return c

    lax.fori_loop(0, jnp.maximum(qb - 1, 0), far_body, 0)

    @pl.when(qb >= 1)
    def _():
        attend(qb - 1, lambda h: bias_s[h, 1])

    attend(qb, lambda h: bias_s[h, 0])

    outs = []
    for h in range(H):
        o = acc_s[h * VT_ROWS:h * VT_ROWS + HEAD_DIM, :] / acc_s[h * VT_ROWS + HEAD_DIM:h * VT_ROWS + HEAD_DIM + 1, :]
        ms = jnp.mean(o * o, axis=0, keepdims=True)
        outs.append(o * lax.rsqrt(ms + RMS_EPS))
    out_ref[...] = (jnp.concatenate(outs, axis=0).T * g_ref[...]).astype(out_ref.dtype)


def _dsa(qk, vt, qiw, kix, rel_bias, g, B, T):
    N = qk.shape[0]
    TQ = DSA_BLOCK
    nq = T // TQ
    A = ATTN_HEADS * HEAD_DIM
    ksel = min(TOPK_MAX, T // 4)
    assert MAX_DISTANCE <= TQ + 1
    return pl.pallas_call(
        functools.partial(_dsa_kernel, T, ksel),
        grid=(B, nq),
        in_specs=[pl.BlockSpec((TQ, A), lambda b, i: (b * nq + i, 0)),
                  pl.BlockSpec((T, A), lambda b, i: (b, 1)),
                  pl.BlockSpec((nq, ATTN_HEADS * VT_ROWS, TQ), lambda b, i: (b, 0, 0)),
                  pl.BlockSpec((TQ, IDX_COLS), lambda b, i: (b * nq + i, 0)),
                  pl.BlockSpec((T, 256), lambda b, i: (b, 0)),
                  pl.BlockSpec(memory_space=pltpu.SMEM),
                  pl.BlockSpec((1, A), lambda b, i: (0, 0))],
        out_specs=pl.BlockSpec((TQ, A), lambda b, i: (b * nq + i, 0)),
        out_shape=jax.ShapeDtypeStruct((N, A), BF16),
        scratch_shapes=[pltpu.VMEM((IDX_HEADS, 256, TQ), BF16),
                        pltpu.VMEM((ATTN_HEADS, LANES, TQ), BF16),
                        pltpu.VMEM((nq, TQ, TQ), I32),
                        pltpu.VMEM((nq, TQ, TQ), F32),
                        pltpu.VMEM((ATTN_HEADS, 2, TQ, TQ), F32),
                        pltpu.VMEM((ATTN_HEADS, 1, TQ), F32),
                        pltpu.VMEM((ATTN_HEADS * VT_ROWS, TQ), F32),
                        pltpu.VMEM((ATTN_HEADS, TQ, TQ), F32),
                        pltpu.VMEM((ATTN_HEADS, TQ, TQ), BF16),
                        pltpu.VMEM((ATTN_HEADS, 8, TQ), F32),
                        pltpu.VMEM((ATTN_HEADS, 1, TQ), F32)],
        compiler_params=_params("arbitrary", "arbitrary"),
        name="dsa",
    )(qk, qk, vt, qiw, kix, rel_bias, g.reshape(1, A))


def _mix_mlp_kernel(final, x_ref, rw_ref, att_ref, wo_ref, gt1_ref, g_ref, sc_ref, sh_ref, gt2_ref,
                    w1_ref, w2_ref, fg_ref, o_ref, x_s, h_s, acc_s):
    j = pl.program_id(1)

    @pl.when(j == 0)
    def _():
        R = rw_ref.shape[1]
        mixed = (jnp.dot(rw_ref[...], wo_ref[:R, :], preferred_element_type=F32)
                 + jnp.dot(att_ref[...], wo_ref[R:, :], preferred_element_type=F32))
        x1 = x_ref[...] + gt1_ref[...] * mixed
        x_s[...] = x1
        h_s[...] = _norm_mod(x1, g_ref[...], sc_ref[...], sh_ref[...]).astype(BF16)
        acc_s[...] = jnp.zeros_like(acc_s)

    u = jnp.dot(h_s[...], w1_ref[...], preferred_element_type=F32)
    u = jnp.square(jnp.maximum(u, 0.0))
    acc_s[...] += jnp.dot(u.astype(BF16), w2_ref[...], preferred_element_type=F32)

    @pl.when(j == pl.num_programs(1) - 1)
    def _():
        y = x_s[...] + gt2_ref[...] * acc_s[...]
        if final:
            ms = jnp.mean(y * y, axis=-1, keepdims=True)
            y = y * lax.rsqrt(ms + RMS_EPS) * fg_ref[...]
        o_ref[...] = y


def _mix_mlp(x2d, rw, att, wo, gt1, g, sc, sh, gt2, w1, w2, final_g, final, T, fc=512):
    N, D = x2d.shape
    F = w1.shape[1]
    tm = min(MLP_TILE, T)
    nb = T // tm
    row = lambda i, j: (i, 0)
    per_b = lambda i, j: (i // nb, 0, 0)
    const = lambda i, j: (0, 0)
    return pl.pallas_call(
        functools.partial(_mix_mlp_kernel, final),
        grid=(N // tm, F // fc),
        in_specs=[pl.BlockSpec((tm, D), row),
                  pl.BlockSpec((tm, rw.shape[1]), row),
                  pl.BlockSpec((tm, att.shape[1]), row),
                  pl.BlockSpec(wo.shape, const, pipeline_mode=pl.Buffered(1)),
                  pl.BlockSpec((None, 1, D), per_b),
                  pl.BlockSpec((1, D), const),
                  pl.BlockSpec((None, 1, D), per_b),
                  pl.BlockSpec((None, 1, D), per_b),
                  pl.BlockSpec((None, 1, D), per_b),
                  pl.BlockSpec((D, fc), lambda i, j: (0, j)),
                  pl.BlockSpec((fc, D), lambda i, j: (j, 0)),
                  pl.BlockSpec((1, D), const)],
        out_specs=pl.BlockSpec((tm, D), row),
        out_shape=jax.ShapeDtypeStruct((N, D), F32),
        scratch_shapes=[pltpu.VMEM((tm, D), F32), pltpu.VMEM((tm, D), BF16), pltpu.VMEM((tm, D), F32)],
        compiler_params=_params("arbitrary", "arbitrary"),
        name="mix_mlp",
    )(x2d, rw, att, wo, gt1, g.reshape(1, D), sc, sh, gt2, w1, w2, final_g.reshape(1, D))


def _pad_cols(w, n):
    return jnp.pad(w, ((0, 0), (0, n - w.shape[1])))


def _split_bf16(w):
    hi = w.astype(BF16)
    return hi, (w - hi.astype(F32)).astype(BF16)


def _in_weights(l, w_in, mu_lora, decay_a, iclr_a, gate_a, vres_mu, vres_a):
    D = w_in.shape[1]
    R = RWKV_HEADS * HEAD_DIM
    w = w_in[l]
    mats = [(decay_a[l], mu_lora[l, 0]), (iclr_a[l], mu_lora[l, 1]), (gate_a[l], mu_lora[l, 2])]
    if l > 0:
        mats.append((vres_a[l - 1], vres_mu[l - 1]))
    now = _pad_cols(jnp.concatenate([a * (1.0 - mu)[:, None] for a, mu in mats], axis=1), LORA_PAD)
    prev = _pad_cols(jnp.concatenate([a * mu[:, None] for a, mu in mats], axis=1), LORA_PAD)
    wq = w[:, 3 * R:4 * R] * (HEAD_DIM ** -0.5 * LOG2E)
    wa = jnp.concatenate([w[:, :3 * R], now, prev, wq, w[:, 4 * R:5 * R]], axis=1).astype(BF16)
    wv = w[:, 5 * R:6 * R].reshape(D, ATTN_HEADS, HEAD_DIM)
    wv = jnp.pad(wv, ((0, 0), (0, 0), (0, VT_ROWS - HEAD_DIM))).reshape(D, ATTN_HEADS * VT_ROWS)
    wvt = wv.T.astype(BF16)
    wbh, wbl = _split_bf16(_pad_cols(w[:, 6 * R:], IDX_COLS))
    return wa, wvt, wbh, wbl


def kernel(x, c, w_ada, b_ada, norm1_g, norm2_g, w_in, mu_rkv, mu_lora, decay_w0, decay_a, decay_b, iclr_a0, iclr_a, iclr_b, gate_a, gate_b, k_k, k_a, r_k, lnx_g, lnx_b, vres_mu, vres_v0, vres_a, vres_b, attn_out_g, rel_bias, w_out, w_mlp1, w_mlp2, final_g):
    B, T, D = x.shape
    depth = w_in.shape[0]
    mod = _adaln(c, w_ada, b_ada)
    mod = mod.reshape(depth, B, 6, 1, D).transpose(0, 2, 1, 3, 4)
    x2d = x.reshape(B * T, D)
    v_first = None
    for l in range(depth):
        sh1, sc1, gt1, sh2, sc2, gt2 = (mod[l, i] for i in range(6))
        wa, wvt, wbh, wbl = _in_weights(l, w_in, mu_lora, decay_a, iclr_a, gate_a, vres_mu, vres_a)
        rkvl, qk, vt, qiw, kix = _inproj(x2d, norm1_g[l], sc1, sh1, wa, wvt, wbh, wbl, T)
        p = dict(mu_rkv=mu_rkv[l], decay_w0=decay_w0[l], decay_b=decay_b[l], iclr_a0=iclr_a0[l],
                 iclr_b=iclr_b[l], gate_b=gate_b[l], k_k=k_k[l], k_a=k_a[l], r_k=r_k[l],
                 lnx_g=lnx_g[l], lnx_b=lnx_b[l])
        if l > 0:
            p.update(vres_v0=vres_v0[l - 1], vres_b=vres_b[l - 1])
        rw, v_first = _rwkv(rkvl, v_first, p, B, T)
        att = _dsa(qk, vt, qiw, kix, rel_bias, attn_out_g[l], B, T)
        x2d = _mix_mlp(x2d, rw, att, w_out[l].astype(BF16), gt1, norm2_g[l], sc2, sh2, gt2,
                       w_mlp1[l].astype(BF16), w_mlp2[l].astype(BF16), final_g, l == depth - 1, T)
    return x2d.reshape(B, T, D)
```

```python
import functools
import math

import numpy as np
import jax
import jax.numpy as jnp
from jax import lax
from jax.experimental import pallas as pl
from jax.experimental.pallas import tpu as pltpu

F32 = jnp.float32
BF16 = jnp.bfloat16
I32 = jnp.int32

HEAD_DIM = 64
RWKV_HEADS = 8
ATTN_HEADS = 8
IDX_HEADS = 8
IDX_DIM = 64
TOPK_MAX = 256
N_BUCKETS = 32
MAX_DISTANCE = 128
RMS_EPS = 1e-6
LNX_EPS = 64e-5

LANES = 128
VMEM_LIMIT = 56 * 1024 * 1024
RWKV_CHUNK = 64
RWKV_GROUP = 256
RWKV_TILE = 512
RWKV_UNROLL = 8
DSA_BLOCK = 256
VT_ROWS = 80
LOG2E = math.log2(math.e)
INPROJ_TILE = 512
MLP_TILE = 1024
LORA_PAD = 384
K_IDX_OFF = IDX_HEADS * IDX_DIM
W_IDX_OFF = K_IDX_OFF + IDX_DIM
IDX_COLS = 640
KIX_COLS = 4 * IDX_DIM
NEG_BIG = -1e30
HI = lax.Precision.HIGHEST
NT = (((1,), (1,)), ((), ()))


def _bdot(a, b):
    return jnp.dot(a.astype(BF16), b.astype(BF16), preferred_element_type=F32)


def _hdot(a, b):
    return jnp.dot(a, b, precision=HI, preferred_element_type=F32)


def _params(*sem):
    return pltpu.CompilerParams(dimension_semantics=sem, vmem_limit_bytes=VMEM_LIMIT)


def _adaln_kernel(c_ref, w_ref, b_ref, o_ref):
    c = c_ref[...]
    c_act = c * jax.nn.sigmoid(c)
    o_ref[...] = _hdot(c_act, w_ref[...]) + b_ref[...]


def _adaln(c, w_ada, b_ada):
    L, D, D6 = w_ada.shape
    B = c.shape[0]
    cb = 1024
    return pl.pallas_call(
        _adaln_kernel,
        grid=(L, D6 // cb),
        in_specs=[pl.BlockSpec((B, D), lambda l, j: (0, 0)),
                  pl.BlockSpec((None, D, cb), lambda l, j: (l, 0, j)),
                  pl.BlockSpec((None, 1, cb), lambda l, j: (l, 0, j))],
        out_specs=pl.BlockSpec((None, B, cb), lambda l, j: (l, 0, j)),
        out_shape=jax.ShapeDtypeStruct((L, B, D6), F32),
        compiler_params=_params("arbitrary", "arbitrary"),
        name="adaln",
    )(c, w_ada, b_ada.reshape(L, 1, D6))


def _norm_mod(x, g, sc, sh):
    ms = jnp.mean(x * x, axis=-1, keepdims=True)
    return (x * lax.rsqrt(ms + RMS_EPS) * g) * (1.0 + sc) + sh


def _inproj_kernel(x_ref, g_ref, sc_ref, sh_ref, wa_ref, wvt_ref, wbh_ref, wbl_ref,
                   rkvl_ref, qk_ref, vt_ref, qiw_ref, kix_ref):
    h = _norm_mod(x_ref[...], g_ref[...], sc_ref[...], sh_ref[...])
    hb = h.astype(BF16)
    hl = (h - hb.astype(F32)).astype(BF16)
    pa = jnp.dot(hb, wa_ref[...], preferred_element_type=F32)
    nr = rkvl_ref.shape[1]
    rkvl_ref[...] = pa[:, :nr]
    qk_ref[...] = pa[:, nr:].astype(BF16)
    vt = lax.dot_general(wvt_ref[...], hb, NT, preferred_element_type=F32)
    ones_row = lax.broadcasted_iota(I32, vt.shape, 0) % VT_ROWS == HEAD_DIM
    vt = jnp.where(ones_row, 1.0, vt).astype(BF16)
    for j in range(vt_ref.shape[0]):
        vt_ref[j] = vt[:, j * DSA_BLOCK:(j + 1) * DSA_BLOCK]
    pb = (jnp.dot(hb, wbh_ref[...], preferred_element_type=F32)
          + jnp.dot(hb, wbl_ref[...], preferred_element_type=F32)
          + jnp.dot(hl, wbh_ref[...], preferred_element_type=F32))
    qiw_ref[...] = pb
    ki = pb[:, K_IDX_OFF:W_IDX_OFF]
    kh = ki.astype(BF16)
    kl = (ki - kh.astype(F32)).astype(BF16)
    kix_ref[...] = jnp.concatenate([kh, kl, kh, jnp.zeros_like(kh)], axis=-1)


def _inproj(x2d, g, sc, sh, wa, wvt, wbh, wbl, T):
    N, D = x2d.shape
    nblk = INPROJ_TILE // DSA_BLOCK
    tm = INPROJ_TILE
    na = wa.shape[1]
    A = wvt.shape[0]
    nr = 3 * RWKV_HEADS * HEAD_DIM + 2 * LORA_PAD
    assert T % tm == 0 and (wa.shape[1] - nr) % LANES == 0
    nb = T // tm
    row = lambda i: (i, 0)
    per_b = lambda i: (i // nb, 0, 0)
    const = lambda i: (0, 0)
    once = pl.Buffered(1)
    return pl.pallas_call(
        _inproj_kernel,
        grid=(N // tm,),
        in_specs=[pl.BlockSpec((tm, D), row),
                  pl.BlockSpec((1, D), const),
                  pl.BlockSpec((None, 1, D), per_b),
                  pl.BlockSpec((None, 1, D), per_b),
                  pl.BlockSpec((D, na), const, pipeline_mode=once),
                  pl.BlockSpec((A, D), const, pipeline_mode=once),
                  pl.BlockSpec((D, IDX_COLS), const, pipeline_mode=once),
                  pl.BlockSpec((D, IDX_COLS), const, pipeline_mode=once)],
        out_specs=[pl.BlockSpec((tm, nr), row),
                   pl.BlockSpec((tm, na - nr), row),
                   pl.BlockSpec((nblk, A, DSA_BLOCK), lambda i: (i, 0, 0)),
                   pl.BlockSpec((tm, IDX_COLS), row),
                   pl.BlockSpec((tm, KIX_COLS), row)],
        out_shape=[jax.ShapeDtypeStruct((N, nr), F32),
                   jax.ShapeDtypeStruct((N, na - nr), BF16),
                   jax.ShapeDtypeStruct((N // DSA_BLOCK, A, DSA_BLOCK), BF16),
                   jax.ShapeDtypeStruct((N, IDX_COLS), F32),
                   jax.ShapeDtypeStruct((N, KIX_COLS), BF16)],
        compiler_params=_params("arbitrary"),
        name="inproj",
    )(x2d, g.reshape(1, D), sc, sh, wa, wvt, wbh, wbl)


def _head_ones(n):
    r = lax.broadcasted_iota(I32, (n, n), 0) // HEAD_DIM
    c = lax.broadcasted_iota(I32, (n, n), 1) // HEAD_DIM
    return jnp.where(r == c, 1.0, 0.0).astype(F32)


def _split(x):
    hi = x.astype(BF16)
    return hi, (x - hi.astype(F32)).astype(BF16)


def _split3(x):
    hi = x.astype(BF16)
    r = x - hi.astype(F32)
    mid = r.astype(BF16)
    return hi, mid, (r - mid.astype(F32)).astype(BF16)


def _block_diag(w, blocks):
    w = w.astype(BF16)
    return jnp.concatenate([jnp.where(m, w, jnp.zeros_like(w)) for m in blocks], axis=0)


def _block_diag_t(w, blocks):
    return jnp.concatenate([jnp.where(m, w, 0.0) for m in blocks], axis=0).T


def _fold_blocks(w):
    n = w.shape[1] // HEAD_DIM
    out = w[:HEAD_DIM]
    for h in range(1, n):
        out = out + w[h * HEAD_DIM:(h + 1) * HEAD_DIM]
    return out


def _mm(a, w):
    return jnp.dot(a.astype(BF16), w, preferred_element_type=F32)


def _rwkv_kernel(has_vres, TT, *refs):
    C = RWKV_CHUNK
    R = RWKV_HEADS * HEAD_DIM
    G = RWKV_GROUP
    NG = R // G
    NC = TT // C
    if has_vres:
        (rkvl_ref, vf_ref, mu_ref, w0_ref, db_ref, a0_ref, ib_ref, gb_ref, kk_ref, ka_ref,
         rk_ref, lg_ref, lb_ref, v0_ref, vb_ref, out_ref,
         prev_ref, S_ref, r_s, k_s, v_s, kk_s, b_s, lw_s, cum_s, y_s, q_s, y0_s, m_s, g_s) = refs
    else:
        (rkvl_ref, mu_ref, w0_ref, db_ref, a0_ref, ib_ref, gb_ref, kk_ref, ka_ref,
         rk_ref, lg_ref, lb_ref, out_ref, vfo_ref,
         prev_ref, S_ref, r_s, k_s, v_s, kk_s, b_s, lw_s, cum_s, y_s, q_s, y0_s, m_s, g_s) = refs

    @pl.when(pl.program_id(1) == 0)
    def _():
        prev_ref[...] = jnp.zeros_like(prev_ref)
        S_ref[...] = jnp.zeros_like(S_ref)

    row0 = lax.broadcasted_iota(I32, (TT, 1), 0) == 0
    prev = prev_ref[...]

    def shift(z, p):
        return jnp.where(row0, p, pltpu.roll(z, 1, 0))

    rkv = rkvl_ref[:, :3 * R]
    p1 = rkvl_ref[:, 3 * R:3 * R + LORA_PAD]
    p2 = rkvl_ref[:, 3 * R + LORA_PAD:]
    rkv_new = rkv + (shift(rkv, prev[:, :3 * R]) - rkv) * mu_ref[...]
    lora = p1 + shift(p2, prev[:, 3 * R:])
    prev_ref[...] = jnp.concatenate([rkv[TT - 1:TT, :], p2[TT - 1:TT, :]], axis=-1)

    r = rkv_new[:, :R]
    k = rkv_new[:, R:2 * R]
    v = rkv_new[:, 2 * R:]
    ones_h = _head_ones(G).astype(BF16)

    def head_sum(x):
        hi, lo = _split(x)
        return jnp.concatenate(
            [jnp.dot(hi[:, gi * G:(gi + 1) * G], ones_h, preferred_element_type=F32)
             + jnp.dot(lo[:, gi * G:(gi + 1) * G], ones_h, preferred_element_type=F32) for gi in range(NG)], axis=1)

    o_a = db_ref.shape[0]
    o_g = o_a + ib_ref.shape[0]
    o_v = o_g + gb_ref.shape[0]
    wlog = w0_ref[...] + _bdot(jnp.tanh(lora[:, :o_a]), db_ref[...])
    z = -wlog
    wlog = -(jnp.maximum(z, 0.0) + jnp.log(1.0 + jnp.exp(-jnp.abs(z)))) - 0.5
    lw = -jnp.exp(wlog)
    lw_s[...] = lw
    slab = min(TT, 4 * C)
    tr = lax.broadcasted_iota(I32, (slab, slab), 0)
    tc = lax.broadcasted_iota(I32, (slab, slab), 1)
    tri = jnp.where((tc <= tr) & (tr // C == tc // C), 1.0, 0.0).astype(BF16)
    parts = _split3(lw)
    for s0 in range(0, TT, slab):
        cum_s[s0:s0 + slab, :] = sum(jnp.dot(tri, part[s0:s0 + slab], preferred_element_type=F32) for part in parts)
    a = jax.nn.sigmoid(a0_ref[...] + _bdot(lora[:, o_a:o_g], ib_ref[...]))
    g = _bdot(jax.nn.sigmoid(lora[:, o_g:o_v]), gb_ref[...])
    kkr = k * kk_ref[...]
    kk = kkr / jnp.maximum(jnp.sqrt(head_sum(kkr * kkr)), 1e-12)
    k = k * (1.0 + (a - 1.0) * ka_ref[...])
    if has_vres:
        v = v + (vf_ref[...] - v) * jax.nn.sigmoid(
            v0_ref[...] + _bdot(lora[:, o_v:o_v + vb_ref.shape[0]], vb_ref[...]))
    else:
        vfo_ref[...] = v
    r_s[...] = r
    k_s[...] = k
    v_s[...] = v
    kk_s[...] = kk
    b_s[...] = kk * a

    lane_g = lax.broadcasted_iota(I32, (C, RWKV_GROUP), 1)
    row_g = lax.broadcasted_iota(I32, (C, RWKV_GROUP), 0)
    blocks = [lane_g // HEAD_DIM == h for h in range(RWKV_GROUP // HEAD_DIM)]
    local = lane_g % HEAD_DIM
    strict = local < row_g
    incl = local <= row_g
    diag = local == row_g
    eye_cat = jnp.where(diag, 1.0, 0.0).astype(F32)
    levels = int(math.log2(C)) - 1

    def local_chunk(it, carry):
        chains = [(u, slice(gi * G, (gi + 1) * G)) for u in range(RWKV_UNROLL) for gi in range(NG)]
        each = lambda f: [f(i) for i in range(len(chains))]
        rd, kkd, kt, bt, kc, bc, vv, plast = ([] for _ in range(8))
        for u in range(RWKV_UNROLL):
            sl = pl.ds(pl.multiple_of((it * RWKV_UNROLL + u) * C, C), C)
            lw = lw_s[sl, :]
            cum = cum_s[sl, :]
            cl = cum[C - 1:C, :]
            pinv = jnp.exp(-cum)
            pc = jnp.exp(cl - cum)
            k_a = k_s[sl, :]
            b_a = b_s[sl, :]
            full = (r_s[sl, :] * jnp.exp(cum), kk_s[sl, :] * jnp.exp(cum - lw), k_a * pinv, b_a * pinv,
                    k_a * pc, b_a * pc, v_s[sl, :], jnp.exp(cl))
            for dst, x in zip((rd, kkd, kt, bt, kc, bc, vv, plast), full):
                dst.extend(x[:, gs] for (uu, gs) in chains if uu == u)
        kt_w = each(lambda i: _block_diag_t(kt[i], blocks).astype(BF16))
        bt_w = each(lambda i: _block_diag_t(bt[i], blocks).astype(BF16))
        bc_t = each(lambda i: _fold_blocks(_block_diag_t(bc[i], blocks)))
        kc_t = each(lambda i: _fold_blocks(_block_diag_t(kc[i], blocks)))
        rr = each(lambda i: _mm(jnp.concatenate([kkd[i], rd[i]], axis=0),
                                jnp.concatenate([kt_w[i], bt_w[i]], axis=1)))
        akk = each(lambda i: jnp.where(strict, rr[i][:C, :G], 0.0))
        ark = each(lambda i: jnp.where(incl, rr[i][C:, :G], 0.0))
        arb = each(lambda i: jnp.where(incl, rr[i][C:, G:], 0.0))
        npow = each(lambda i: jnp.where(strict, -rr[i][:C, G:], 0.0))
        x = each(lambda i: eye_cat + npow[i])
        npow = each(lambda i: _mm(npow[i], _block_diag(npow[i], blocks)))
        for _ in range(levels - 1):
            rr = each(lambda i: _mm(jnp.concatenate([npow[i], x[i]], axis=0), _block_diag(npow[i], blocks)))
            npow = each(lambda i: rr[i][:C])
            x = each(lambda i: x[i] + rr[i][C:])
        rr = each(lambda i: _mm(x[i], _block_diag(npow[i], blocks)))
        x = each(lambda i: x[i] + rr[i])
        rv = each(lambda i: _mm(jnp.concatenate([akk[i], ark[i], kc_t[i]], axis=0), _block_diag(vv[i], blocks)))
        ru = each(lambda i: _mm(x[i], jnp.concatenate([_block_diag(kkd[i], blocks),
                                                       _block_diag(rv[i][:C], blocks)], axis=1)))
        rr = each(lambda i: _mm(jnp.concatenate([arb[i], bc_t[i]], axis=0),
                                jnp.concatenate([_block_diag(ru[i][:, :G], blocks),
                                                 _block_diag(ru[i][:, G:], blocks)], axis=1)))
        for i, (u, gs) in enumerate(chains):
            c = it * RWKV_UNROLL + u
            q_s[c, :, gs] = rd[i] - rr[i][:C, :G]
            y0_s[c, :, gs] = rv[i][C:2 * C] - rr[i][:C, G:]
            dterm = jnp.where(diag, jnp.broadcast_to(plast[i], (C, G)), 0.0)
            m_s[c, :, gs] = dterm - rr[i][C:, :G]
            g_s[c, :, gs] = rv[i][2 * C:] - rr[i][C:, G:]
        return carry

    lax.fori_loop(0, NC // RWKV_UNROLL, local_chunk, 0)

    def scan_chunk(c, carry):
        sl = pl.ds(pl.multiple_of(c * C, C), C)
        for gi in range(NG):
            gs = slice(gi * RWKV_GROUP, (gi + 1) * RWKV_GROUP)
            rr = _mm(jnp.concatenate([q_s[c, :, gs], m_s[c, :, gs]], axis=0), _block_diag(S_ref[:, gs], blocks))
            y_s[sl, gs] = rr[:C] + y0_s[c, :, gs]
            S_ref[:, gs] = rr[C:] + g_s[c, :, gs]
        return carry

    lax.fori_loop(0, NC, scan_chunk, 0)

    y = y_s[...]
    inv_n = 1.0 / HEAD_DIM
    mean = head_sum(y) * inv_n
    yc = y - mean
    var = head_sum(yc * yc) * inv_n
    yn = yc * lax.rsqrt(var + LNX_EPS) * lg_ref[...] + lb_ref[...]
    bonus = head_sum(r_s[...] * k_s[...] * rk_ref[...]) * v_s[...]
    out_ref[...] = ((yn + bonus) * g).astype(out_ref.dtype)


def _rwkv(rkvl, v_first, p, B, T):
    N, nc = rkvl.shape
    R = RWKV_HEADS * HEAD_DIM
    TT = min(RWKV_TILE, T)
    nt = T // TT
    NC = TT // RWKV_CHUNK
    has_vres = v_first is not None
    row = lambda b, i: (b * nt + i, 0)
    const = lambda b, i: (0, 0)
    vec = lambda a: a.reshape(1, -1)
    ins = [rkvl]
    specs = [pl.BlockSpec((TT, nc), row)]
    if has_vres:
        ins.append(v_first)
        specs.append(pl.BlockSpec((TT, R), row))
    small = [vec(p["mu_rkv"]), vec(p["decay_w0"]), p["decay_b"].astype(BF16), vec(p["iclr_a0"]),
             p["iclr_b"].astype(BF16), p["gate_b"].astype(BF16), vec(p["k_k"]), vec(p["k_a"]),
             vec(p["r_k"]), vec(p["lnx_g"]), vec(p["lnx_b"])]
    if has_vres:
        small += [vec(p["vres_v0"]), p["vres_b"].astype(BF16)]
    ins += small
    specs += [pl.BlockSpec(a.shape, const) for a in small]
    out_shape = [jax.ShapeDtypeStruct((N, R), BF16)]
    out_specs = [pl.BlockSpec((TT, R), row)]
    if not has_vres:
        out_shape.append(jax.ShapeDtypeStruct((N, R), F32))
        out_specs.append(pl.BlockSpec((TT, R), row))
    scratch = [pltpu.VMEM((1, 3 * R + LORA_PAD), F32),
               pltpu.VMEM((HEAD_DIM, R), F32)]
    scratch += [pltpu.VMEM((TT, R), F32) for _ in range(8)]
    scratch += [pltpu.VMEM((NC, RWKV_CHUNK, R), F32) for _ in range(4)]
    res = pl.pallas_call(
        functools.partial(_rwkv_kernel, has_vres, TT),
        grid=(B, nt),
        in_specs=specs,
        out_specs=out_specs,
        out_shape=out_shape,
        scratch_shapes=scratch,
        compiler_params=_params("arbitrary", "arbitrary"),
        name="rwkv",
    )(*ins)
    if has_vres:
        return res[0], v_first
    return res[0], res[1]


def _bucket_boundaries():
    max_exact = N_BUCKETS // 2
    d = np.arange(0, 4 * MAX_DISTANCE, dtype=np.int64)
    nf = np.maximum(d, 1).astype(np.float32)
    large = max_exact + (np.log(nf / np.float32(max_exact)) / np.float32(math.log(MAX_DISTANCE / max_exact))
                         * np.float32(N_BUCKETS - max_exact)).astype(np.int32)
    large = np.minimum(large, N_BUCKETS - 1)
    bucket = np.where(d < max_exact, d, large)
    return [int(np.argmax(bucket >= j)) for j in range(max_exact + 1, N_BUCKETS)]


_BUCKET_STARTS = _bucket_boundaries()
KEY_NEG_INF = int(np.int32(np.array(-np.inf, np.float32).view(np.int32)) ^ np.int32(0x7FFFFFFF))
INT_MIN = -2 ** 31
INT_MAX = 2 ** 31 - 1
FAST_SELECT_STEPS = 2
FAST_SELECT_TRIPS = 10


def _sort_key(s):
    bits = pltpu.bitcast(s, I32)
    bits = jnp.where(bits == INT_MIN, 0, bits)
    return bits ^ ((bits >> 31) & jnp.int32(0x7FFFFFFF))


def _key_to_float(k):
    return pltpu.bitcast(k ^ ((k >> 31) & jnp.int32(0x7FFFFFFF)), F32)


def _dsa_kernel(T, ksel, q_ref, k_ref, vt_ref, qiw_ref, kix_ref, relb_ref, g_ref, out_ref,
                lhs_s, qpad_s, key_s, madd_s, bias_s, m_s, acc_s, s_s, p_s, cmax_s, alpha_s):
    TQ = TK = DSA_BLOCK
    H = ATTN_HEADS
    qb = pl.program_id(1)
    nch = qb + 1
    rows = lax.broadcasted_iota(I32, (TK, TQ), 0)
    cols = lax.broadcasted_iota(I32, (TK, TQ), 1)

    @pl.when((pl.program_id(0) == 0) & (qb == 0))
    def _():
        max_exact = N_BUCKETS // 2
        for off in range(2):
            d = cols - rows + off * TK
            bucket = jnp.where(d < max_exact, jnp.maximum(d, 0), max_exact)
            for start in _BUCKET_STARTS:
                bucket = bucket + jnp.where(d >= start, 1, 0)
            for h in range(H):
                far = relb_ref[N_BUCKETS - 1, h]
                tile = jnp.zeros((TK, TQ), F32)
                for bk in range(N_BUCKETS - 1):
                    tile = jnp.where(bucket == bk, (relb_ref[bk, h] - far) * LOG2E, tile)
                bias_s[h, off] = tile

    qi_t = (qiw_ref[:, :IDX_HEADS * IDX_DIM] * (IDX_DIM ** -0.5)).T
    for h in range(IDX_HEADS):
        qh = qi_t[h * IDX_DIM:(h + 1) * IDX_DIM, :]
        hi = qh.astype(BF16)
        lo = (qh - hi.astype(F32)).astype(BF16)
        lhs_s[h] = jnp.concatenate([hi, hi, lo, jnp.zeros_like(hi)], axis=0)
    w_t = qiw_ref[:, K_IDX_OFF:IDX_COLS].T[IDX_DIM:IDX_DIM + IDX_HEADS, :] * (IDX_HEADS ** -0.5)

    q_t = q_ref[...].astype(F32).T
    zeros_h = jnp.zeros((HEAD_DIM, TQ), BF16)
    for h in range(H):
        qh = q_t[h * HEAD_DIM:(h + 1) * HEAD_DIM, :].astype(BF16)
        qpad_s[h] = jnp.concatenate([qh, zeros_h] if h % 2 == 0 else [zeros_h, qh], axis=0)

    def scores(j):
        kx = kix_ref[pl.ds(pl.multiple_of(j * TK, TK), TK), :]
        acc = jnp.zeros((TK, TQ), F32)
        for h in range(IDX_HEADS):
            d = jnp.dot(kx, lhs_s[h], preferred_element_type=F32)
            acc = acc + w_t[h:h + 1, :] * jnp.maximum(d, 0.0)
        return acc

    def stats(st, kk, mn_of):
        fold = lambda z: z.reshape(TK // 8, 8, TQ)
        ge0, gt0, mn, mx = st
        return (ge0 + jnp.sum(fold(jnp.where(kk >= 0, 1, 0)), axis=0),
                gt0 + jnp.sum(fold(jnp.where(kk > 0, 1, 0)), axis=0),
                jnp.minimum(mn, jnp.min(fold(mn_of(kk)), axis=0)),
                jnp.maximum(mx, jnp.max(fold(kk), axis=0)))

    def score_body(j, st):
        kk = _sort_key(scores(j))
        key_s[j] = kk
        return stats(st, kk, lambda z: z)

    st = lax.fori_loop(0, qb, score_body, (jnp.zeros((8, TQ), I32), jnp.zeros((8, TQ), I32),
                                           jnp.full((8, TQ), INT_MAX, I32), jnp.full((8, TQ), INT_MIN, I32)))
    kd = _sort_key(jnp.where(rows <= cols, scores(qb), -jnp.inf))
    key_s[qb] = kd
    st = stats(st, kd, lambda z: jnp.where(z > KEY_NEG_INF, z, INT_MAX))
    n_ge0, n_gt0 = (jnp.sum(z, axis=0, keepdims=True) for z in st[:2])
    lo0 = _key_to_float(jnp.min(st[2], axis=0, keepdims=True))
    hi0 = _key_to_float(jnp.max(st[3], axis=0, keepdims=True) + 1)

    def count(pred):
        def body(j, acc):
            hit = jnp.where(pred(key_s[j]), 1, 0)
            return acc + jnp.sum(hit.reshape(TK // 8, 8, TQ), axis=0)
        acc = lax.fori_loop(0, nch, body, jnp.zeros((8, TQ), I32))
        return jnp.sum(acc, axis=0, keepdims=True)

    def write_mask(sel_of):
        def body(j, c):
            madd_s[j] = jnp.where(sel_of(key_s[j]), 0.0, NEG_BIG)
            return c
        lax.fori_loop(0, nch, body, 0)

    kf = float(ksel)
    n_adm = qb * TQ + lax.broadcasted_iota(I32, (1, TQ), 1) + 1
    trivial = n_adm <= ksel
    positive = n_gt0 > ksel
    done0 = trivial | ((n_gt0 <= ksel) & (n_ge0 >= ksel))
    thr0 = jnp.where(trivial, KEY_NEG_INF, jnp.where(n_gt0 == ksel, 1, 0))
    need0 = jnp.where(n_gt0 < ksel, ksel - n_gt0, ksel)
    a0 = jnp.where(positive, 0.0, lo0)
    b0 = jnp.where(positive, hi0, 0.0)
    fa0 = jnp.where(positive, n_gt0, n_adm).astype(F32) - kf
    cb0 = jnp.where(positive, 0, n_ge0)
    as_int = lambda m: jnp.where(m, 1, 0)

    def any_lane(m):
        return jnp.max(as_int(m)) > 0

    def fast_cond(st):
        it, done, stuck = st[0], st[1], st[2]
        return (it < FAST_SELECT_TRIPS) & any_lane((done + stuck) == 0)

    def fast_body(st):
        it, st = st[0], st[1:]
        for _ in range(FAST_SELECT_STEPS):
            st = fast_step(st)
        return (it + 1,) + st

    def fast_step(st):
        done, stuck, thr, need, a, b, fa, fb, cb, last = st
        c = a + (b - a) * (fa / (fa - fb))
        c = jnp.where((c > a) & (c < b), c, 0.5 * a + 0.5 * b)
        inside = (c > a) & (c < b)
        ckey = _sort_key(c)
        cnt = count(lambda kk: kk >= ckey)
        fc = cnt.astype(F32) - kf
        hit = inside & (cnt == ksel) & (done == 0)
        thr = jnp.where(hit, ckey, thr)
        need = jnp.where(hit, ksel, need)
        done = jnp.where(hit, 1, done)
        stuck = jnp.where(inside, stuck, 1)
        up = inside & (cnt > ksel)
        dn = inside & (cnt < ksel)
        fb = jnp.where(up & (last == 1), 0.5 * fb, fb)
        fa = jnp.where(dn & (last == 0), 0.5 * fa, fa)
        a, fa = jnp.where(up, c, a), jnp.where(up, fc, fa)
        b, fb, cb = jnp.where(dn, c, b), jnp.where(dn, fc, fb), jnp.where(dn, cnt, cb)
        return done, stuck, thr, need, a, b, fa, fb, cb, jnp.where(up, 1, jnp.where(dn, 0, last))

    st = lax.while_loop(fast_cond, fast_body, (
        jnp.int32(0), as_int(done0), jnp.zeros((1, TQ), I32), thr0, need0,
        a0, b0, fa0, cb0.astype(F32) - kf, cb0, jnp.full((1, TQ), -1, I32)))
    done, thr, need, a, b, cb = st[1], st[3], st[4], st[5], st[6], st[9]
    open_lane = done == 0
    any_open = any_lane(open_lane)

    def close_cond(st):
        return any_lane((st[1] - st[0]) > 1)

    def close_body(st):
        ak, bk, cb = st
        mid = ak + ((bk - ak) >> 1)
        cnt = count(lambda kk: kk >= mid)
        ge = cnt >= ksel
        return jnp.where(ge, mid, ak), jnp.where(ge, bk, mid), jnp.where(ge, cb, cnt)

    ak, _, cb = lax.while_loop(close_cond, close_body, (
        jnp.where(open_lane, _sort_key(a), thr), jnp.where(open_lane, _sort_key(b), thr), cb))
    thr = jnp.where(open_lane, ak, thr)
    need = jnp.where(open_lane, ksel - cb, need)
    any_tie = any_open | any_lane(jnp.logical_not(trivial) & (n_gt0 < ksel) & (n_ge0 > ksel))

    @pl.when(jnp.logical_not(any_tie))
    def _():
        write_mask(lambda kk: (kk >= thr) & (kk > KEY_NEG_INF))

    @pl.when(any_tie)
    def _():
        kr = lax.broadcasted_iota(I32, (TK, TK), 0)
        kc = lax.broadcasted_iota(I32, (TK, TK), 1)
        lower = jnp.where(kc <= kr, 1.0, 0.0).astype(BF16)
        need_f = need.astype(F32)

        def body(j, run):
            kk = key_s[j]
            eq = kk == thr
            eqf = jnp.where(eq, 1.0, 0.0)
            rank = run + jnp.dot(lower, eqf.astype(BF16), preferred_element_type=F32)
            sel = ((kk > thr) | (eq & (rank <= need_f))) & (kk > KEY_NEG_INF)
            madd_s[j] = jnp.where(sel, 0.0, NEG_BIG)
            return run + jnp.sum(eqf, axis=0, keepdims=True)
        lax.fori_loop(0, nch, body, jnp.zeros((1, TQ), F32))

    m_s[...] = jnp.full(m_s.shape, NEG_BIG, F32)
    acc_s[...] = jnp.zeros(acc_s.shape, F32)

    def attend(j, bias_of):
        ks = pl.ds(pl.multiple_of(j * TK, TK), TK)
        ma = madd_s[j]
        for h in range(H):
            kp = k_ref[ks, LANES * (h // 2):LANES * (h // 2 + 1)]
            s = jnp.dot(kp, qpad_s[h], preferred_element_type=F32) + ma
            bias = bias_of(h)
            if bias is not None:
                s = s + bias
            s_s[h] = s
            cmax_s[h] = jnp.max(s.reshape(TK // 8, 8, TQ), axis=0)
        for h in range(H):
            m_old = m_s[h]
            m_new = jnp.maximum(m_old, jnp.max(cmax_s[h], axis=0, keepdims=True))
            alpha_s[h] = jnp.exp2(m_old - m_new)
            m_s[h] = m_new
            p_s[h] = jnp.exp2(s_s[h] - m_new).astype(BF16)
        for h in range(H):
            hs = slice(h * VT_ROWS, (h + 1) * VT_ROWS)
            pv = jnp.dot(vt_ref[j, hs, :], p_s[h], preferred_element_type=F32)
            acc_s[hs, :] = alpha_s[h] * acc_s[hs, :] + pv

    def far_body(j, c):
        attend(j, lambda h: None)
        return c

    lax.fori_loop(0, jnp.maximum(qb - 1, 0), far_body, 0)

    @pl.when(qb >= 1)
    def _():
        attend(qb - 1, lambda h: bias_s[h, 1])

    attend(qb, lambda h: bias_s[h, 0])

    outs = []
    for h in range(H):
        o = acc_s[h * VT_ROWS:h * VT_ROWS + HEAD_DIM, :] / acc_s[h * VT_ROWS + HEAD_DIM:h * VT_ROWS + HEAD_DIM + 1, :]
        ms = jnp.mean(o * o, axis=0, keepdims=True)
        outs.append(o * lax.rsqrt(ms + RMS_EPS))
    out_ref[...] = (jnp.concatenate(outs, axis=0).T * g_ref[...]).astype(out_ref.dtype)


def _dsa(qk, vt, qiw, kix, rel_bias, g, B, T):
    N = qk.shape[0]
    TQ = DSA_BLOCK
    nq = T // TQ
    A = ATTN_HEADS * HEAD_DIM
    ksel = min(TOPK_MAX, T // 4)
    assert MAX_DISTANCE <= TQ + 1
    return pl.pallas_call(
        functools.partial(_dsa_kernel, T, ksel),
        grid=(B, nq),
        in_specs=[pl.BlockSpec((TQ, A), lambda b, i: (b * nq + i, 0)),
                  pl.BlockSpec((T, A), lambda b, i: (b, 1)),
                  pl.BlockSpec((nq, ATTN_HEADS * VT_ROWS, TQ), lambda b, i: (b, 0, 0)),
                  pl.BlockSpec((TQ, IDX_COLS), lambda b, i: (b * nq + i, 0)),
                  pl.BlockSpec((T, KIX_COLS), lambda b, i: (b, 0)),
                  pl.BlockSpec(memory_space=pltpu.SMEM),
                  pl.BlockSpec((1, A), lambda b, i: (0, 0))],
        out_specs=pl.BlockSpec((TQ, A), lambda b, i: (b * nq + i, 0)),
        out_shape=jax.ShapeDtypeStruct((N, A), BF16),
        scratch_shapes=[pltpu.VMEM((IDX_HEADS, KIX_COLS, TQ), BF16),
                        pltpu.VMEM((ATTN_HEADS, LANES, TQ), BF16),
                        pltpu.VMEM((nq, TQ, TQ), I32),
                        pltpu.VMEM((nq, TQ, TQ), F32),
                        pltpu.VMEM((ATTN_HEADS, 2, TQ, TQ), F32),
                        pltpu.VMEM((ATTN_HEADS, 1, TQ), F32),
                        pltpu.VMEM((ATTN_HEADS * VT_ROWS, TQ), F32),
                        pltpu.VMEM((ATTN_HEADS, TQ, TQ), F32),
                        pltpu.VMEM((ATTN_HEADS, TQ, TQ), BF16),
                        pltpu.VMEM((ATTN_HEADS, 8, TQ), F32),
                        pltpu.VMEM((ATTN_HEADS, 1, TQ), F32)],
        compiler_params=_params("arbitrary", "arbitrary"),
        name="dsa",
    )(qk, qk, vt, qiw, kix, rel_bias, g.reshape(1, A))


def _mix_mlp_kernel(final, x_ref, rw_ref, att_ref, wo_ref, gt1_ref, g_ref, sc_ref, sh_ref, gt2_ref,
                    w1_ref, w2_ref, fg_ref, o_ref, x_s, h_s, acc_s):
    j = pl.program_id(1)

    @pl.when(j == 0)
    def _():
        R = rw_ref.shape[1]
        mixed = (jnp.dot(rw_ref[...], wo_ref[:R, :], preferred_element_type=F32)
                 + jnp.dot(att_ref[...], wo_ref[R:, :], preferred_element_type=F32))
        x1 = x_ref[...] + gt1_ref[...] * mixed
        x_s[...] = x1
        h_s[...] = _norm_mod(x1, g_ref[...], sc_ref[...], sh_ref[...]).astype(BF16)
        acc_s[...] = jnp.zeros_like(acc_s)

    u = jnp.dot(h_s[...], w1_ref[...], preferred_element_type=F32)
    u = jnp.square(jnp.maximum(u, 0.0))
    acc_s[...] += jnp.dot(u.astype(BF16), w2_ref[...], preferred_element_type=F32)

    @pl.when(j == pl.num_programs(1) - 1)
    def _():
        y = x_s[...] + gt2_ref[...] * acc_s[...]
        if final:
            ms = jnp.mean(y * y, axis=-1, keepdims=True)
            y = y * lax.rsqrt(ms + RMS_EPS) * fg_ref[...]
        o_ref[...] = y


def _mix_mlp(x2d, rw, att, wo, gt1, g, sc, sh, gt2, w1, w2, final_g, final, T, fc=512):
    N, D = x2d.shape
    F = w1.shape[1]
    tm = min(MLP_TILE, T)
    nb = T // tm
    row = lambda i, j: (i, 0)
    per_b = lambda i, j: (i // nb, 0, 0)
    const = lambda i, j: (0, 0)
    return pl.pallas_call(
        functools.partial(_mix_mlp_kernel, final),
        grid=(N // tm, F // fc),
        in_specs=[pl.BlockSpec((tm, D), row),
                  pl.BlockSpec((tm, rw.shape[1]), row),
                  pl.BlockSpec((tm, att.shape[1]), row),
                  pl.BlockSpec(wo.shape, const, pipeline_mode=pl.Buffered(1)),
                  pl.BlockSpec((None, 1, D), per_b),
                  pl.BlockSpec((1, D), const),
                  pl.BlockSpec((None, 1, D), per_b),
                  pl.BlockSpec((None, 1, D), per_b),
                  pl.BlockSpec((None, 1, D), per_b),
                  pl.BlockSpec((D, fc), lambda i, j: (0, j)),
                  pl.BlockSpec((fc, D), lambda i, j: (j, 0)),
                  pl.BlockSpec((1, D), const)],
        out_specs=pl.BlockSpec((tm, D), row),
        out_shape=jax.ShapeDtypeStruct((N, D), F32),
        scratch_shapes=[pltpu.VMEM((tm, D), F32), pltpu.VMEM((tm, D), BF16), pltpu.VMEM((tm, D), F32)],
        compiler_params=_params("arbitrary", "arbitrary"),
        name="mix_mlp",
    )(x2d, rw, att, wo, gt1, g.reshape(1, D), sc, sh, gt2, w1, w2, final_g.reshape(1, D))


def _pad_cols(w, n):
    return jnp.pad(w, ((0, 0), (0, n - w.shape[1])))


def _split_bf16(w):
    hi = w.astype(BF16)
    return hi, (w - hi.astype(F32)).astype(BF16)


def _in_weights(l, w_in, mu_lora, decay_a, iclr_a, gate_a, vres_mu, vres_a):
    D = w_in.shape[1]
    R = RWKV_HEADS * HEAD_DIM
    w = w_in[l]
    mats = [(decay_a[l], mu_lora[l, 0]), (iclr_a[l], mu_lora[l, 1]), (gate_a[l], mu_lora[l, 2])]
    if l > 0:
        mats.append((vres_a[l - 1], vres_mu[l - 1]))
    now = _pad_cols(jnp.concatenate([a * (1.0 - mu)[:, None] for a, mu in mats], axis=1), LORA_PAD)
    prev = _pad_cols(jnp.concatenate([a * mu[:, None] for a, mu in mats], axis=1), LORA_PAD)
    wq = w[:, 3 * R:4 * R] * (HEAD_DIM ** -0.5 * LOG2E)
    wa = jnp.concatenate([w[:, :3 * R], now, prev, wq, w[:, 4 * R:5 * R]], axis=1).astype(BF16)
    wv = w[:, 5 * R:6 * R].reshape(D, ATTN_HEADS, HEAD_DIM)
    wv = jnp.pad(wv, ((0, 0), (0, 0), (0, VT_ROWS - HEAD_DIM))).reshape(D, ATTN_HEADS * VT_ROWS)
    wvt = wv.T.astype(BF16)
    wbh, wbl = _split_bf16(_pad_cols(w[:, 6 * R:], IDX_COLS))
    return wa, wvt, wbh, wbl


def kernel(x, c, w_ada, b_ada, norm1_g, norm2_g, w_in, mu_rkv, mu_lora, decay_w0, decay_a, decay_b, iclr_a0, iclr_a, iclr_b, gate_a, gate_b, k_k, k_a, r_k, lnx_g, lnx_b, vres_mu, vres_v0, vres_a, vres_b, attn_out_g, rel_bias, w_out, w_mlp1, w_mlp2, final_g):
    B, T, D = x.shape
    depth = w_in.shape[0]
    mod = _adaln(c, w_ada, b_ada)
    mod = mod.reshape(depth, B, 6, 1, D).transpose(0, 2, 1, 3, 4)
    x2d = x.reshape(B * T, D)
    v_first = None
    for l in range(depth):
        sh1, sc1, gt1, sh2, sc2, gt2 = (mod[l, i] for i in range(6))
        wa, wvt, wbh, wbl = _in_weights(l, w_in, mu_lora, decay_a, iclr_a, gate_a, vres_mu, vres_a)
        rkvl, qk, vt, qiw, kix = _inproj(x2d, norm1_g[l], sc1, sh1, wa, wvt, wbh, wbl, T)
        p = dict(mu_rkv=mu_rkv[l], decay_w0=decay_w0[l], decay_b=decay_b[l], iclr_a0=iclr_a0[l],
                 iclr_b=iclr_b[l], gate_b=gate_b[l], k_k=k_k[l], k_a=k_a[l], r_k=r_k[l],
                 lnx_g=lnx_g[l], lnx_b=lnx_b[l])
        if l > 0:
            p.update(vres_v0=vres_v0[l - 1], vres_b=vres_b[l - 1])
        rw, v_first = _rwkv(rkvl, v_first, p, B, T)
        att = _dsa(qk, vt, qiw, kix, rel_bias, attn_out_g[l], B, T)
        x2d = _mix_mlp(x2d, rw, att, w_out[l].astype(BF16), gt1, norm2_g[l], sc2, sh2, gt2,
                       w_mlp1[l].astype(BF16), w_mlp2[l].astype(BF16), final_g, l == depth - 1, T)
    return x2d.reshape(B, T, D)
```

```python
import functools
import math

import numpy as np
import jax
import jax.numpy as jnp
from jax import lax
from jax.experimental import pallas as pl
from jax.experimental.pallas import tpu as pltpu

F32 = jnp.float32
BF16 = jnp.bfloat16
I32 = jnp.int32

HEAD_DIM = 64
RWKV_HEADS = 8
ATTN_HEADS = 8
IDX_HEADS = 8
IDX_DIM = 64
TOPK_MAX = 256
N_BUCKETS = 32
MAX_DISTANCE = 128
RMS_EPS = 1e-6
LNX_EPS = 64e-5

LANES = 128
VMEM_LIMIT = 56 * 1024 * 1024
RWKV_CHUNK = 64
RWKV_GROUP = 256
RWKV_TILE = 512
RWKV_UNROLL = 8
DSA_BLOCK = 256
VT_ROWS = 80
LOG2E = math.log2(math.e)
INPROJ_TILE = 512
MLP_TILE = 1024
LORA_PAD = 384
K_IDX_OFF = IDX_HEADS * IDX_DIM
W_IDX_OFF = K_IDX_OFF + IDX_DIM
IDX_COLS = 640
KIX_COLS = 4 * IDX_DIM
NEG_BIG = -1e30
HI = lax.Precision.HIGHEST
NT = (((1,), (1,)), ((), ()))


def _bdot(a, b):
    return jnp.dot(a.astype(BF16), b.astype(BF16), preferred_element_type=F32)


def _hdot(a, b):
    return jnp.dot(a, b, precision=HI, preferred_element_type=F32)


def _params(*sem):
    return pltpu.CompilerParams(dimension_semantics=sem, vmem_limit_bytes=VMEM_LIMIT)


def _adaln_kernel(c_ref, w_ref, b_ref, o_ref):
    c = c_ref[...]
    c_act = c * jax.nn.sigmoid(c)
    o_ref[...] = _hdot(c_act, w_ref[...]) + b_ref[...]


def _adaln(c, w_ada, b_ada):
    L, D, D6 = w_ada.shape
    B = c.shape[0]
    cb = 1024
    return pl.pallas_call(
        _adaln_kernel,
        grid=(L, D6 // cb),
        in_specs=[pl.BlockSpec((B, D), lambda l, j: (0, 0)),
                  pl.BlockSpec((None, D, cb), lambda l, j: (l, 0, j)),
                  pl.BlockSpec((None, 1, cb), lambda l, j: (l, 0, j))],
        out_specs=pl.BlockSpec((None, B, cb), lambda l, j: (l, 0, j)),
        out_shape=jax.ShapeDtypeStruct((L, B, D6), F32),
        compiler_params=_params("arbitrary", "arbitrary"),
        name="adaln",
    )(c, w_ada, b_ada.reshape(L, 1, D6))


def _norm_mod(x, g, sc, sh):
    ms = jnp.mean(x * x, axis=-1, keepdims=True)
    return (x * lax.rsqrt(ms + RMS_EPS) * g) * (1.0 + sc) + sh


def _inproj_kernel(x_ref, g_ref, sc_ref, sh_ref, wa_ref, wvt_ref, wbh_ref, wbl_ref,
                   rkvl_ref, qk_ref, vt_ref, qiw_ref, kix_ref):
    h = _norm_mod(x_ref[...], g_ref[...], sc_ref[...], sh_ref[...])
    hb = h.astype(BF16)
    hl = (h - hb.astype(F32)).astype(BF16)
    pa = jnp.dot(hb, wa_ref[...], preferred_element_type=F32)
    nr = rkvl_ref.shape[1]
    rkvl_ref[...] = pa[:, :nr]
    qk_ref[...] = pa[:, nr:].astype(BF16)
    vt = lax.dot_general(wvt_ref[...], hb, NT, preferred_element_type=F32)
    ones_row = lax.broadcasted_iota(I32, vt.shape, 0) % VT_ROWS == HEAD_DIM
    vt = jnp.where(ones_row, 1.0, vt).astype(BF16)
    for j in range(vt_ref.shape[0]):
        vt_ref[j] = vt[:, j * DSA_BLOCK:(j + 1) * DSA_BLOCK]
    pb = (jnp.dot(hb, wbh_ref[...], preferred_element_type=F32)
          + jnp.dot(hb, wbl_ref[...], preferred_element_type=F32)
          + jnp.dot(hl, wbh_ref[...], preferred_element_type=F32))
    qiw_ref[...] = pb
    ki = pb[:, K_IDX_OFF:W_IDX_OFF]
    kh = ki.astype(BF16)
    kl = (ki - kh.astype(F32)).astype(BF16)
    kix_ref[...] = jnp.concatenate([kh, kl, kh, jnp.zeros_like(kh)], axis=-1)


def _inproj(x2d, g, sc, sh, wa, wvt, wbh, wbl, T):
    N, D = x2d.shape
    nblk = INPROJ_TILE // DSA_BLOCK
    tm = INPROJ_TILE
    na = wa.shape[1]
    A = wvt.shape[0]
    nr = 3 * RWKV_HEADS * HEAD_DIM + 2 * LORA_PAD
    assert T % tm == 0 and (wa.shape[1] - nr) % LANES == 0
    nb = T // tm
    row = lambda i: (i, 0)
    per_b = lambda i: (i // nb, 0, 0)
    const = lambda i: (0, 0)
    once = pl.Buffered(1)
    return pl.pallas_call(
        _inproj_kernel,
        grid=(N // tm,),
        in_specs=[pl.BlockSpec((tm, D), row),
                  pl.BlockSpec((1, D), const),
                  pl.BlockSpec((None, 1, D), per_b),
                  pl.BlockSpec((None, 1, D), per_b),
                  pl.BlockSpec((D, na), const, pipeline_mode=once),
                  pl.BlockSpec((A, D), const, pipeline_mode=once),
                  pl.BlockSpec((D, IDX_COLS), const, pipeline_mode=once),
                  pl.BlockSpec((D, IDX_COLS), const, pipeline_mode=once)],
        out_specs=[pl.BlockSpec((tm, nr), row),
                   pl.BlockSpec((tm, na - nr), row),
                   pl.BlockSpec((nblk, A, DSA_BLOCK), lambda i: (i, 0, 0)),
                   pl.BlockSpec((tm, IDX_COLS), row),
                   pl.BlockSpec((tm, KIX_COLS), row)],
        out_shape=[jax.ShapeDtypeStruct((N, nr), F32),
                   jax.ShapeDtypeStruct((N, na - nr), BF16),
                   jax.ShapeDtypeStruct((N // DSA_BLOCK, A, DSA_BLOCK), BF16),
                   jax.ShapeDtypeStruct((N, IDX_COLS), F32),
                   jax.ShapeDtypeStruct((N, KIX_COLS), BF16)],
        compiler_params=_params("arbitrary"),
        name="inproj",
    )(x2d, g.reshape(1, D), sc, sh, wa, wvt, wbh, wbl)


def _head_ones(n):
    r = lax.broadcasted_iota(I32, (n, n), 0) // HEAD_DIM
    c = lax.broadcasted_iota(I32, (n, n), 1) // HEAD_DIM
    return jnp.where(r == c, 1.0, 0.0).astype(F32)


def _split(x):
    hi = x.astype(BF16)
    return hi, (x - hi.astype(F32)).astype(BF16)


def _split3(x):
    hi = x.astype(BF16)
    r = x - hi.astype(F32)
    mid = r.astype(BF16)
    return hi, mid, (r - mid.astype(F32)).astype(BF16)


def _block_diag(w, blocks):
    w = w.astype(BF16)
    return jnp.concatenate([jnp.where(m, w, jnp.zeros_like(w)) for m in blocks], axis=0)


def _block_diag_t(w, blocks):
    return jnp.concatenate([jnp.where(m, w, 0.0) for m in blocks], axis=0).T


def _fold_blocks(w):
    n = w.shape[1] // HEAD_DIM
    out = w[:HEAD_DIM]
    for h in range(1, n):
        out = out + w[h * HEAD_DIM:(h + 1) * HEAD_DIM]
    return out


def _mm(a, w):
    return jnp.dot(a.astype(BF16), w, preferred_element_type=F32)


def _rwkv_kernel(has_vres, TT, *refs):
    C = RWKV_CHUNK
    R = RWKV_HEADS * HEAD_DIM
    G = RWKV_GROUP
    NG = R // G
    NC = TT // C
    if has_vres:
        (rkvl_ref, vf_ref, mu_ref, w0_ref, db_ref, a0_ref, ib_ref, gb_ref, kk_ref, ka_ref,
         rk_ref, lg_ref, lb_ref, v0_ref, vb_ref, out_ref,
         prev_ref, S_ref, r_s, k_s, v_s, kk_s, b_s, lw_s, cum_s, y_s, q_s, y0_s, m_s, g_s) = refs
    else:
        (rkvl_ref, mu_ref, w0_ref, db_ref, a0_ref, ib_ref, gb_ref, kk_ref, ka_ref,
         rk_ref, lg_ref, lb_ref, out_ref, vfo_ref,
         prev_ref, S_ref, r_s, k_s, v_s, kk_s, b_s, lw_s, cum_s, y_s, q_s, y0_s, m_s, g_s) = refs

    @pl.when(pl.program_id(1) == 0)
    def _():
        prev_ref[...] = jnp.zeros_like(prev_ref)
        S_ref[...] = jnp.zeros_like(S_ref)

    row0 = lax.broadcasted_iota(I32, (TT, 1), 0) == 0
    prev = prev_ref[...]

    def shift(z, p):
        return jnp.where(row0, p, pltpu.roll(z, 1, 0))

    rkv = rkvl_ref[:, :3 * R]
    p1 = rkvl_ref[:, 3 * R:3 * R + LORA_PAD]
    p2 = rkvl_ref[:, 3 * R + LORA_PAD:]
    rkv_new = rkv + (shift(rkv, prev[:, :3 * R]) - rkv) * mu_ref[...]
    lora = p1 + shift(p2, prev[:, 3 * R:])
    prev_ref[...] = jnp.concatenate([rkv[TT - 1:TT, :], p2[TT - 1:TT, :]], axis=-1)

    r = rkv_new[:, :R]
    k = rkv_new[:, R:2 * R]
    v = rkv_new[:, 2 * R:]
    ones_h = _head_ones(G).astype(BF16)

    def head_sum(x):
        hi, lo = _split(x)
        return jnp.concatenate(
            [jnp.dot(hi[:, gi * G:(gi + 1) * G], ones_h, preferred_element_type=F32)
             + jnp.dot(lo[:, gi * G:(gi + 1) * G], ones_h, preferred_element_type=F32) for gi in range(NG)], axis=1)

    o_a = db_ref.shape[0]
    o_g = o_a + ib_ref.shape[0]
    o_v = o_g + gb_ref.shape[0]
    wlog = w0_ref[...] + _bdot(jnp.tanh(lora[:, :o_a]), db_ref[...])
    z = -wlog
    wlog = -(jnp.maximum(z, 0.0) + jnp.log(1.0 + jnp.exp(-jnp.abs(z)))) - 0.5
    lw = -jnp.exp(wlog)
    lw_s[...] = lw
    slab = min(TT, 4 * C)
    tr = lax.broadcasted_iota(I32, (slab, slab), 0)
    tc = lax.broadcasted_iota(I32, (slab, slab), 1)
    tri = jnp.where((tc <= tr) & (tr // C == tc // C), 1.0, 0.0).astype(BF16)
    parts = _split3(lw)
    for s0 in range(0, TT, slab):
        cum_s[s0:s0 + slab, :] = sum(jnp.dot(tri, part[s0:s0 + slab], preferred_element_type=F32) for part in parts)
    a = jax.nn.sigmoid(a0_ref[...] + _bdot(lora[:, o_a:o_g], ib_ref[...]))
    g = _bdot(jax.nn.sigmoid(lora[:, o_g:o_v]), gb_ref[...])
    kkr = k * kk_ref[...]
    kk = kkr * lax.rsqrt(jnp.maximum(head_sum(kkr * kkr), 1e-24))
    k = k * (1.0 + (a - 1.0) * ka_ref[...])
    if has_vres:
        v = v + (vf_ref[...] - v) * jax.nn.sigmoid(
            v0_ref[...] + _bdot(lora[:, o_v:o_v + vb_ref.shape[0]], vb_ref[...]))
    else:
        vfo_ref[...] = v
    r_s[...] = r
    k_s[...] = k
    v_s[...] = v
    kk_s[...] = kk
    b_s[...] = kk * a

    lane_g = lax.broadcasted_iota(I32, (C, RWKV_GROUP), 1)
    row_g = lax.broadcasted_iota(I32, (C, RWKV_GROUP), 0)
    blocks = [lane_g // HEAD_DIM == h for h in range(RWKV_GROUP // HEAD_DIM)]
    local = lane_g % HEAD_DIM
    strict = local < row_g
    incl = local <= row_g
    diag = local == row_g
    eye_cat = jnp.where(diag, 1.0, 0.0).astype(F32)
    levels = int(math.log2(C)) - 1

    def local_chunk(it, carry):
        chains = [(u, slice(gi * G, (gi + 1) * G)) for u in range(RWKV_UNROLL) for gi in range(NG)]
        each = lambda f: [f(i) for i in range(len(chains))]
        rd, kkd, kt, bt, kc, bc, vv, plast = ([] for _ in range(8))
        for u in range(RWKV_UNROLL):
            sl = pl.ds(pl.multiple_of((it * RWKV_UNROLL + u) * C, C), C)
            lw = lw_s[sl, :]
            cum = cum_s[sl, :]
            cl = cum[C - 1:C, :]
            pinv = jnp.exp(-cum)
            pc = jnp.exp(cl - cum)
            k_a = k_s[sl, :]
            b_a = b_s[sl, :]
            full = (r_s[sl, :] * jnp.exp(cum), kk_s[sl, :] * jnp.exp(cum - lw), k_a * pinv, b_a * pinv,
                    k_a * pc, b_a * pc, v_s[sl, :], jnp.exp(cl))
            for dst, x in zip((rd, kkd, kt, bt, kc, bc, vv, plast), full):
                dst.extend(x[:, gs] for (uu, gs) in chains if uu == u)
        kt_w = each(lambda i: _block_diag_t(kt[i], blocks).astype(BF16))
        bt_w = each(lambda i: _block_diag_t(bt[i], blocks).astype(BF16))
        bc_t = each(lambda i: _fold_blocks(_block_diag_t(bc[i], blocks)))
        kc_t = each(lambda i: _fold_blocks(_block_diag_t(kc[i], blocks)))
        rr = each(lambda i: _mm(jnp.concatenate([kkd[i], rd[i]], axis=0),
                                jnp.concatenate([kt_w[i], bt_w[i]], axis=1)))
        akk = each(lambda i: jnp.where(strict, rr[i][:C, :G], 0.0))
        ark = each(lambda i: jnp.where(incl, rr[i][C:, :G], 0.0))
        arb = each(lambda i: jnp.where(incl, rr[i][C:, G:], 0.0))
        npow = each(lambda i: jnp.where(strict, -rr[i][:C, G:], 0.0))
        x = each(lambda i: eye_cat + npow[i])
        npow = each(lambda i: _mm(npow[i], _block_diag(npow[i], blocks)))
        for _ in range(levels - 1):
            rr = each(lambda i: _mm(jnp.concatenate([npow[i], x[i]], axis=0), _block_diag(npow[i], blocks)))
            npow = each(lambda i: rr[i][:C])
            x = each(lambda i: x[i] + rr[i][C:])
        rr = each(lambda i: _mm(x[i], _block_diag(npow[i], blocks)))
        x = each(lambda i: x[i] + rr[i])
        rv = each(lambda i: _mm(jnp.concatenate([akk[i], ark[i], kc_t[i]], axis=0), _block_diag(vv[i], blocks)))
        ru = each(lambda i: _mm(x[i], jnp.concatenate([_block_diag(kkd[i], blocks),
                                                       _block_diag(rv[i][:C], blocks)], axis=1)))
        rr = each(lambda i: _mm(jnp.concatenate([arb[i], bc_t[i]], axis=0),
                                jnp.concatenate([_block_diag(ru[i][:, :G], blocks),
                                                 _block_diag(ru[i][:, G:], blocks)], axis=1)))
        for i, (u, gs) in enumerate(chains):
            c = it * RWKV_UNROLL + u
            q_s[c, :, gs] = rd[i] - rr[i][:C, :G]
            y0_s[c, :, gs] = rv[i][C:2 * C] - rr[i][:C, G:]
            dterm = jnp.where(diag, jnp.broadcast_to(plast[i], (C, G)), 0.0)
            m_s[c, :, gs] = dterm - rr[i][C:, :G]
            g_s[c, :, gs] = rv[i][2 * C:] - rr[i][C:, G:]
        return carry

    lax.fori_loop(0, NC // RWKV_UNROLL, local_chunk, 0)

    def scan_chunk(c, carry):
        sl = pl.ds(pl.multiple_of(c * C, C), C)
        for gi in range(NG):
            gs = slice(gi * RWKV_GROUP, (gi + 1) * RWKV_GROUP)
            rr = _mm(jnp.concatenate([q_s[c, :, gs], m_s[c, :, gs]], axis=0), _block_diag(S_ref[:, gs], blocks))
            y_s[sl, gs] = rr[:C] + y0_s[c, :, gs]
            S_ref[:, gs] = rr[C:] + g_s[c, :, gs]
        return carry

    lax.fori_loop(0, NC, scan_chunk, 0)

    y = y_s[...]
    inv_n = 1.0 / HEAD_DIM
    mean = head_sum(y) * inv_n
    yc = y - mean
    var = head_sum(yc * yc) * inv_n
    yn = yc * lax.rsqrt(var + LNX_EPS) * lg_ref[...] + lb_ref[...]
    bonus = head_sum(r_s[...] * k_s[...] * rk_ref[...]) * v_s[...]
    out_ref[...] = ((yn + bonus) * g).astype(out_ref.dtype)


def _rwkv(rkvl, v_first, p, B, T):
    N, nc = rkvl.shape
    R = RWKV_HEADS * HEAD_DIM
    TT = min(RWKV_TILE, T)
    nt = T // TT
    NC = TT // RWKV_CHUNK
    has_vres = v_first is not None
    row = lambda b, i: (b * nt + i, 0)
    const = lambda b, i: (0, 0)
    vec = lambda a: a.reshape(1, -1)
    ins = [rkvl]
    specs = [pl.BlockSpec((TT, nc), row)]
    if has_vres:
        ins.append(v_first)
        specs.append(pl.BlockSpec((TT, R), row))
    small = [vec(p["mu_rkv"]), vec(p["decay_w0"]), p["decay_b"].astype(BF16), vec(p["iclr_a0"]),
             p["iclr_b"].astype(BF16), p["gate_b"].astype(BF16), vec(p["k_k"]), vec(p["k_a"]),
             vec(p["r_k"]), vec(p["lnx_g"]), vec(p["lnx_b"])]
    if has_vres:
        small += [vec(p["vres_v0"]), p["vres_b"].astype(BF16)]
    ins += small
    specs += [pl.BlockSpec(a.shape, const) for a in small]
    out_shape = [jax.ShapeDtypeStruct((N, R), BF16)]
    out_specs = [pl.BlockSpec((TT, R), row)]
    if not has_vres:
        out_shape.append(jax.ShapeDtypeStruct((N, R), F32))
        out_specs.append(pl.BlockSpec((TT, R), row))
    scratch = [pltpu.VMEM((1, 3 * R + LORA_PAD), F32),
               pltpu.VMEM((HEAD_DIM, R), F32)]
    scratch += [pltpu.VMEM((TT, R), F32) for _ in range(8)]
    scratch += [pltpu.VMEM((NC, RWKV_CHUNK, R), F32) for _ in range(4)]
    res = pl.pallas_call(
        functools.partial(_rwkv_kernel, has_vres, TT),
        grid=(B, nt),
        in_specs=specs,
        out_specs=out_specs,
        out_shape=out_shape,
        scratch_shapes=scratch,
        compiler_params=_params("arbitrary", "arbitrary"),
        name="rwkv",
    )(*ins)
    if has_vres:
        return res[0], v_first
    return res[0], res[1]


def _bucket_boundaries():
    max_exact = N_BUCKETS // 2
    d = np.arange(0, 4 * MAX_DISTANCE, dtype=np.int64)
    nf = np.maximum(d, 1).astype(np.float32)
    large = max_exact + (np.log(nf / np.float32(max_exact)) / np.float32(math.log(MAX_DISTANCE / max_exact))
                         * np.float32(N_BUCKETS - max_exact)).astype(np.int32)
    large = np.minimum(large, N_BUCKETS - 1)
    bucket = np.where(d < max_exact, d, large)
    return [int(np.argmax(bucket >= j)) for j in range(max_exact + 1, N_BUCKETS)]


_BUCKET_STARTS = _bucket_boundaries()
KEY_NEG_INF = int(np.int32(np.array(-np.inf, np.float32).view(np.int32)) ^ np.int32(0x7FFFFFFF))
INT_MIN = -2 ** 31
INT_MAX = 2 ** 31 - 1
FAST_SELECT_STEPS = 2
FAST_SELECT_TRIPS = 10


def _sort_key(s):
    bits = pltpu.bitcast(s, I32)
    bits = jnp.where(bits == INT_MIN, 0, bits)
    return bits ^ ((bits >> 31) & jnp.int32(0x7FFFFFFF))


def _key_to_float(k):
    return pltpu.bitcast(k ^ ((k >> 31) & jnp.int32(0x7FFFFFFF)), F32)


def _dsa_kernel(T, ksel, q_ref, k_ref, vt_ref, qiw_ref, kix_ref, relb_ref, g_ref, out_ref,
                lhs_s, qpad_s, key_s, madd_s, bias_s, m_s, acc_s, s_s, p_s, cmax_s, alpha_s):
    TQ = TK = DSA_BLOCK
    H = ATTN_HEADS
    qb = pl.program_id(1)
    nch = qb + 1
    rows = lax.broadcasted_iota(I32, (TK, TQ), 0)
    cols = lax.broadcasted_iota(I32, (TK, TQ), 1)

    @pl.when((pl.program_id(0) == 0) & (qb == 0))
    def _():
        max_exact = N_BUCKETS // 2
        for off in range(2):
            d = cols - rows + off * TK
            bucket = jnp.where(d < max_exact, jnp.maximum(d, 0), max_exact)
            for start in _BUCKET_STARTS:
                bucket = bucket + jnp.where(d >= start, 1, 0)
            for h in range(H):
                far = relb_ref[N_BUCKETS - 1, h]
                tile = jnp.zeros((TK, TQ), F32)
                for bk in range(N_BUCKETS - 1):
                    tile = jnp.where(bucket == bk, (relb_ref[bk, h] - far) * LOG2E, tile)
                bias_s[h, off] = tile

    qi_t = (qiw_ref[:, :IDX_HEADS * IDX_DIM] * (IDX_DIM ** -0.5)).T
    for h in range(IDX_HEADS):
        qh = qi_t[h * IDX_DIM:(h + 1) * IDX_DIM, :]
        hi = qh.astype(BF16)
        lo = (qh - hi.astype(F32)).astype(BF16)
        lhs_s[h] = jnp.concatenate([hi, hi, lo, jnp.zeros_like(hi)], axis=0)
    w_t = qiw_ref[:, K_IDX_OFF:IDX_COLS].T[IDX_DIM:IDX_DIM + IDX_HEADS, :] * (IDX_HEADS ** -0.5)

    q_t = q_ref[...].astype(F32).T
    zeros_h = jnp.zeros((HEAD_DIM, TQ), BF16)
    for h in range(H):
        qh = q_t[h * HEAD_DIM:(h + 1) * HEAD_DIM, :].astype(BF16)
        qpad_s[h] = jnp.concatenate([qh, zeros_h] if h % 2 == 0 else [zeros_h, qh], axis=0)

    def scores(j):
        kx = kix_ref[pl.ds(pl.multiple_of(j * TK, TK), TK), :]
        acc = jnp.zeros((TK, TQ), F32)
        for h in range(IDX_HEADS):
            d = jnp.dot(kx, lhs_s[h], preferred_element_type=F32)
            acc = acc + w_t[h:h + 1, :] * jnp.maximum(d, 0.0)
        return acc

    def stats(st, kk, mn_of):
        fold = lambda z: z.reshape(TK // 8, 8, TQ)
        ge0, gt0, mn, mx = st
        return (ge0 + jnp.sum(fold(jnp.where(kk >= 0, 1, 0)), axis=0),
                gt0 + jnp.sum(fold(jnp.where(kk > 0, 1, 0)), axis=0),
                jnp.minimum(mn, jnp.min(fold(mn_of(kk)), axis=0)),
                jnp.maximum(mx, jnp.max(fold(kk), axis=0)))

    def score_body(j, st):
        kk = _sort_key(scores(j))
        key_s[j] = kk
        return stats(st, kk, lambda z: z)

    st = lax.fori_loop(0, qb, score_body, (jnp.zeros((8, TQ), I32), jnp.zeros((8, TQ), I32),
                                           jnp.full((8, TQ), INT_MAX, I32), jnp.full((8, TQ), INT_MIN, I32)))
    kd = _sort_key(jnp.where(rows <= cols, scores(qb), -jnp.inf))
    key_s[qb] = kd
    st = stats(st, kd, lambda z: jnp.where(z > KEY_NEG_INF, z, INT_MAX))
    n_ge0, n_gt0 = (jnp.sum(z, axis=0, keepdims=True) for z in st[:2])
    lo0 = _key_to_float(jnp.min(st[2], axis=0, keepdims=True))
    hi0 = _key_to_float(jnp.max(st[3], axis=0, keepdims=True) + 1)

    def count(pred):
        def body(j, acc):
            hit = jnp.where(pred(key_s[j]), 1, 0)
            return acc + jnp.sum(hit.reshape(TK // 8, 8, TQ), axis=0)
        acc = lax.fori_loop(0, nch, body, jnp.zeros((8, TQ), I32))
        return jnp.sum(acc, axis=0, keepdims=True)

    def write_mask(sel_of):
        def body(j, c):
            madd_s[j] = jnp.where(sel_of(key_s[j]), 0.0, NEG_BIG)
            return c
        lax.fori_loop(0, nch, body, 0)

    kf = float(ksel)
    n_adm = qb * TQ + lax.broadcasted_iota(I32, (1, TQ), 1) + 1
    trivial = n_adm <= ksel
    positive = n_gt0 > ksel
    done0 = trivial | ((n_gt0 <= ksel) & (n_ge0 >= ksel))
    thr0 = jnp.where(trivial, KEY_NEG_INF + 1, jnp.where(n_gt0 == ksel, 1, 0))
    need0 = jnp.where(n_gt0 < ksel, ksel - n_gt0, ksel)
    a0 = jnp.where(positive, 0.0, lo0)
    b0 = jnp.where(positive, hi0, 0.0)
    fa0 = jnp.where(positive, n_gt0, n_adm).astype(F32) - kf
    cb0 = jnp.where(positive, 0, n_ge0)
    as_int = lambda m: jnp.where(m, 1, 0)

    def any_lane(m):
        return jnp.max(as_int(m)) > 0

    def fast_cond(st):
        it, done, stuck = st[0], st[1], st[2]
        return (it < FAST_SELECT_TRIPS) & any_lane((done + stuck) == 0)

    def fast_body(st):
        it, st = st[0], st[1:]
        for _ in range(FAST_SELECT_STEPS):
            st = fast_step(st)
        return (it + 1,) + st

    def fast_step(st):
        done, stuck, thr, need, a, b, fa, fb, cb, last = st
        c = a + (b - a) * (fa / (fa - fb))
        c = jnp.where((c > a) & (c < b), c, 0.5 * a + 0.5 * b)
        inside = (c > a) & (c < b)
        ckey = _sort_key(c)
        cnt = count(lambda kk: kk >= ckey)
        fc = cnt.astype(F32) - kf
        hit = inside & (cnt == ksel) & (done == 0)
        thr = jnp.where(hit, ckey, thr)
        need = jnp.where(hit, ksel, need)
        done = jnp.where(hit, 1, done)
        stuck = jnp.where(inside, stuck, 1)
        up = inside & (cnt > ksel)
        dn = inside & (cnt < ksel)
        fb = jnp.where(up & (last == 1), 0.5 * fb, fb)
        fa = jnp.where(dn & (last == 0), 0.5 * fa, fa)
        a, fa = jnp.where(up, c, a), jnp.where(up, fc, fa)
        b, fb, cb = jnp.where(dn, c, b), jnp.where(dn, fc, fb), jnp.where(dn, cnt, cb)
        return done, stuck, thr, need, a, b, fa, fb, cb, jnp.where(up, 1, jnp.where(dn, 0, last))

    st = lax.while_loop(fast_cond, fast_body, (
        jnp.int32(0), as_int(done0), jnp.zeros((1, TQ), I32), thr0, need0,
        a0, b0, fa0, cb0.astype(F32) - kf, cb0, jnp.full((1, TQ), -1, I32)))
    done, thr, need, a, b, cb = st[1], st[3], st[4], st[5], st[6], st[9]
    open_lane = done == 0
    any_open = any_lane(open_lane)

    def close_cond(st):
        return any_lane((st[1] - st[0]) > 1)

    def close_body(st):
        ak, bk, cb = st
        mid = ak + ((bk - ak) >> 1)
        cnt = count(lambda kk: kk >= mid)
        ge = cnt >= ksel
        return jnp.where(ge, mid, ak), jnp.where(ge, bk, mid), jnp.where(ge, cb, cnt)

    ak, _, cb = lax.while_loop(close_cond, close_body, (
        jnp.where(open_lane, _sort_key(a), thr), jnp.where(open_lane, _sort_key(b), thr), cb))
    thr = jnp.where(open_lane, ak, thr)
    need = jnp.where(open_lane, ksel - cb, need)
    any_tie = any_open | any_lane(jnp.logical_not(trivial) & (n_gt0 < ksel) & (n_ge0 > ksel))

    @pl.when(jnp.logical_not(any_tie))
    def _():
        write_mask(lambda kk: kk >= thr)

    @pl.when(any_tie)
    def _():
        kr = lax.broadcasted_iota(I32, (TK, TK), 0)
        kc = lax.broadcasted_iota(I32, (TK, TK), 1)
        lower = jnp.where(kc <= kr, 1.0, 0.0).astype(BF16)
        need_f = need.astype(F32)

        def body(j, run):
            kk = key_s[j]
            eq = kk == thr
            eqf = jnp.where(eq, 1.0, 0.0)
            rank = run + jnp.dot(lower, eqf.astype(BF16), preferred_element_type=F32)
            tied_in = jnp.where(eq, rank, float(TOPK_MAX + 1)) <= need_f
            madd_s[j] = jnp.where(kk > thr, 0.0, jnp.where(tied_in, 0.0, NEG_BIG))
            return run + jnp.sum(eqf, axis=0, keepdims=True)
        lax.fori_loop(0, nch, body, jnp.zeros((1, TQ), F32))

    m_s[...] = jnp.full(m_s.shape, NEG_BIG, F32)
    acc_s[...] = jnp.zeros(acc_s.shape, F32)

    def attend(j, bias_of):
        ks = pl.ds(pl.multiple_of(j * TK, TK), TK)
        ma = madd_s[j]
        for h in range(H):
            kp = k_ref[ks, LANES * (h // 2):LANES * (h // 2 + 1)]
            s = jnp.dot(kp, qpad_s[h], preferred_element_type=F32) + ma
            bias = bias_of(h)
            if bias is not None:
                s = s + bias
            s_s[h] = s
            cmax_s[h] = jnp.max(s.reshape(TK // 8, 8, TQ), axis=0)
        for h in range(H):
            m_old = m_s[h]
            m_new = jnp.maximum(m_old, jnp.max(cmax_s[h], axis=0, keepdims=True))
            alpha_s[h] = jnp.exp2(m_old - m_new)
            m_s[h] = m_new
            p_s[h] = jnp.exp2(s_s[h] - m_new).astype(BF16)
        for h in range(H):
            hs = slice(h * VT_ROWS, (h + 1) * VT_ROWS)
            pv = jnp.dot(vt_ref[j, hs, :], p_s[h], preferred_element_type=F32)
            acc_s[hs, :] = alpha_s[h] * acc_s[hs, :] + pv

    def far_body(j, c):
        attend(j, lambda h: None)
        return c

    lax.fori_loop(0, jnp.maximum(qb - 1, 0), far_body, 0)

    @pl.when(qb >= 1)
    def _():
        attend(qb - 1, lambda h: bias_s[h, 1])

    attend(qb, lambda h: bias_s[h, 0])

    outs = []
    for h in range(H):
        o = acc_s[h * VT_ROWS:h * VT_ROWS + HEAD_DIM, :] / acc_s[h * VT_ROWS + HEAD_DIM:h * VT_ROWS + HEAD_DIM + 1, :]
        ms = jnp.mean(o * o, axis=0, keepdims=True)
        outs.append(o * lax.rsqrt(ms + RMS_EPS))
    out_ref[...] = (jnp.concatenate(outs, axis=0).T * g_ref[...]).astype(out_ref.dtype)


def _dsa(qk, vt, qiw, kix, rel_bias, g, B, T):
    N = qk.shape[0]
    TQ = DSA_BLOCK
    nq = T // TQ
    A = ATTN_HEADS * HEAD_DIM
    ksel = min(TOPK_MAX, T // 4)
    assert MAX_DISTANCE <= TQ + 1
    return pl.pallas_call(
        functools.partial(_dsa_kernel, T, ksel),
        grid=(B, nq),
        in_specs=[pl.BlockSpec((TQ, A), lambda b, i: (b * nq + i, 0)),
                  pl.BlockSpec((T, A), lambda b, i: (b, 1)),
                  pl.BlockSpec((nq, ATTN_HEADS * VT_ROWS, TQ), lambda b, i: (b, 0, 0)),
                  pl.BlockSpec((TQ, IDX_COLS), lambda b, i: (b * nq + i, 0)),
                  pl.BlockSpec((T, KIX_COLS), lambda b, i: (b, 0)),
                  pl.BlockSpec(memory_space=pltpu.SMEM),
                  pl.BlockSpec((1, A), lambda b, i: (0, 0))],
        out_specs=pl.BlockSpec((TQ, A), lambda b, i: (b * nq + i, 0)),
        out_shape=jax.ShapeDtypeStruct((N, A), BF16),
        scratch_shapes=[pltpu.VMEM((IDX_HEADS, KIX_COLS, TQ), BF16),
                        pltpu.VMEM((ATTN_HEADS, LANES, TQ), BF16),
                        pltpu.VMEM((nq, TQ, TQ), I32),
                        pltpu.VMEM((nq, TQ, TQ), F32),
                        pltpu.VMEM((ATTN_HEADS, 2, TQ, TQ), F32),
                        pltpu.VMEM((ATTN_HEADS, 1, TQ), F32),
                        pltpu.VMEM((ATTN_HEADS * VT_ROWS, TQ), F32),
                        pltpu.VMEM((ATTN_HEADS, TQ, TQ), F32),
                        pltpu.VMEM((ATTN_HEADS, TQ, TQ), BF16),
                        pltpu.VMEM((ATTN_HEADS, 8, TQ), F32),
                        pltpu.VMEM((ATTN_HEADS, 1, TQ), F32)],
        compiler_params=_params("arbitrary", "arbitrary"),
        name="dsa",
    )(qk, qk, vt, qiw, kix, rel_bias, g.reshape(1, A))


def _mix_mlp_kernel(final, x_ref, rw_ref, att_ref, wo_ref, gt1_ref, g_ref, sc_ref, sh_ref, gt2_ref,
                    w1_ref, w2_ref, fg_ref, o_ref, x_s, h_s, acc_s):
    j = pl.program_id(1)

    @pl.when(j == 0)
    def _():
        R = rw_ref.shape[1]
        mixed = (jnp.dot(rw_ref[...], wo_ref[:R, :], preferred_element_type=F32)
                 + jnp.dot(att_ref[...], wo_ref[R:, :], preferred_element_type=F32))
        x1 = x_ref[...] + gt1_ref[...] * mixed
        x_s[...] = x1
        h_s[...] = _norm_mod(x1, g_ref[...], sc_ref[...], sh_ref[...]).astype(BF16)
        acc_s[...] = jnp.zeros_like(acc_s)

    u = jnp.dot(h_s[...], w1_ref[...], preferred_element_type=F32)
    u = jnp.square(jnp.maximum(u, 0.0))
    acc_s[...] += jnp.dot(u.astype(BF16), w2_ref[...], preferred_element_type=F32)

    @pl.when(j == pl.num_programs(1) - 1)
    def _():
        y = x_s[...] + gt2_ref[...] * acc_s[...]
        if final:
            ms = jnp.mean(y * y, axis=-1, keepdims=True)
            y = y * lax.rsqrt(ms + RMS_EPS) * fg_ref[...]
        o_ref[...] = y


def _mix_mlp(x2d, rw, att, wo, gt1, g, sc, sh, gt2, w1, w2, final_g, final, T, fc=512):
    N, D = x2d.shape
    F = w1.shape[1]
    tm = min(MLP_TILE, T)
    nb = T // tm
    row = lambda i, j: (i, 0)
    per_b = lambda i, j: (i // nb, 0, 0)
    const = lambda i, j: (0, 0)
    return pl.pallas_call(
        functools.partial(_mix_mlp_kernel, final),
        grid=(N // tm, F // fc),
        in_specs=[pl.BlockSpec((tm, D), row),
                  pl.BlockSpec((tm, rw.shape[1]), row),
                  pl.BlockSpec((tm, att.shape[1]), row),
                  pl.BlockSpec(wo.shape, const, pipeline_mode=pl.Buffered(1)),
                  pl.BlockSpec((None, 1, D), per_b),
                  pl.BlockSpec((1, D), const),
                  pl.BlockSpec((None, 1, D), per_b),
                  pl.BlockSpec((None, 1, D), per_b),
                  pl.BlockSpec((None, 1, D), per_b),
                  pl.BlockSpec((D, fc), lambda i, j: (0, j)),
                  pl.BlockSpec((fc, D), lambda i, j: (j, 0)),
                  pl.BlockSpec((1, D), const)],
        out_specs=pl.BlockSpec((tm, D), row),
        out_shape=jax.ShapeDtypeStruct((N, D), F32),
        scratch_shapes=[pltpu.VMEM((tm, D), F32), pltpu.VMEM((tm, D), BF16), pltpu.VMEM((tm, D), F32)],
        compiler_params=_params("arbitrary", "arbitrary"),
        name="mix_mlp",
    )(x2d, rw, att, wo, gt1, g.reshape(1, D), sc, sh, gt2, w1, w2, final_g.reshape(1, D))


def _pad_cols(w, n):
    return jnp.pad(w, ((0, 0), (0, n - w.shape[1])))


def _split_bf16(w):
    hi = w.astype(BF16)
    return hi, (w - hi.astype(F32)).astype(BF16)


def _in_weights(l, w_in, mu_lora, decay_a, iclr_a, gate_a, vres_mu, vres_a):
    D = w_in.shape[1]
    R = RWKV_HEADS * HEAD_DIM
    w = w_in[l]
    mats = [(decay_a[l], mu_lora[l, 0]), (iclr_a[l], mu_lora[l, 1]), (gate_a[l], mu_lora[l, 2])]
    if l > 0:
        mats.append((vres_a[l - 1], vres_mu[l - 1]))
    now = _pad_cols(jnp.concatenate([a * (1.0 - mu)[:, None] for a, mu in mats], axis=1), LORA_PAD)
    prev = _pad_cols(jnp.concatenate([a * mu[:, None] for a, mu in mats], axis=1), LORA_PAD)
    wq = w[:, 3 * R:4 * R] * (HEAD_DIM ** -0.5 * LOG2E)
    wa = jnp.concatenate([w[:, :3 * R], now, prev, wq, w[:, 4 * R:5 * R]], axis=1).astype(BF16)
    wv = w[:, 5 * R:6 * R].reshape(D, ATTN_HEADS, HEAD_DIM)
    wv = jnp.pad(wv, ((0, 0), (0, 0), (0, VT_ROWS - HEAD_DIM))).reshape(D, ATTN_HEADS * VT_ROWS)
    wvt = wv.T.astype(BF16)
    wbh, wbl = _split_bf16(_pad_cols(w[:, 6 * R:], IDX_COLS))
    return wa, wvt, wbh, wbl


def kernel(x, c, w_ada, b_ada, norm1_g, norm2_g, w_in, mu_rkv, mu_lora, decay_w0, decay_a, decay_b, iclr_a0, iclr_a, iclr_b, gate_a, gate_b, k_k, k_a, r_k, lnx_g, lnx_b, vres_mu, vres_v0, vres_a, vres_b, attn_out_g, rel_bias, w_out, w_mlp1, w_mlp2, final_g):
    B, T, D = x.shape
    depth = w_in.shape[0]
    mod = _adaln(c, w_ada, b_ada)
    mod = mod.reshape(depth, B, 6, 1, D).transpose(0, 2, 1, 3, 4)
    x2d = x.reshape(B * T, D)
    v_first = None
    for l in range(depth):
        sh1, sc1, gt1, sh2, sc2, gt2 = (mod[l, i] for i in range(6))
        wa, wvt, wbh, wbl = _in_weights(l, w_in, mu_lora, decay_a, iclr_a, gate_a, vres_mu, vres_a)
        rkvl, qk, vt, qiw, kix = _inproj(x2d, norm1_g[l], sc1, sh1, wa, wvt, wbh, wbl, T)
        p = dict(mu_rkv=mu_rkv[l], decay_w0=decay_w0[l], decay_b=decay_b[l], iclr_a0=iclr_a0[l],
                 iclr_b=iclr_b[l], gate_b=gate_b[l], k_k=k_k[l], k_a=k_a[l], r_k=r_k[l],
                 lnx_g=lnx_g[l], lnx_b=lnx_b[l])
        if l > 0:
            p.update(vres_v0=vres_v0[l - 1], vres_b=vres_b[l - 1])
        rw, v_first = _rwkv(rkvl, v_first, p, B, T)
        att = _dsa(qk, vt, qiw, kix, rel_bias, attn_out_g[l], B, T)
        x2d = _mix_mlp(x2d, rw, att, w_out[l].astype(BF16), gt1, norm2_g[l], sc2, sh2, gt2,
                       w_mlp1[l].astype(BF16), w_mlp2[l].astype(BF16), final_g, l == depth - 1, T)
    return x2d.reshape(B, T, D)
```

```python
import functools
import math

import numpy as np
import jax
import jax.numpy as jnp
from jax import lax
from jax.experimental import pallas as pl
from jax.experimental.pallas import tpu as pltpu

F32 = jnp.float32
BF16 = jnp.bfloat16
I32 = jnp.int32

HEAD_DIM = 64
RWKV_HEADS = 8
ATTN_HEADS = 8
IDX_HEADS = 8
IDX_DIM = 64
TOPK_MAX = 256
N_BUCKETS = 32
MAX_DISTANCE = 128
RMS_EPS = 1e-6
LNX_EPS = 64e-5

LANES = 128
VMEM_LIMIT = 56 * 1024 * 1024
RWKV_CHUNK = 64
RWKV_GROUP = 256
RWKV_TILE = 512
RWKV_UNROLL = 8
DSA_BLOCK = 256
VT_ROWS = 80
LOG2E = math.log2(math.e)
INPROJ_TILE = 512
MLP_TILE = 1024
LORA_PAD = 384
K_IDX_OFF = IDX_HEADS * IDX_DIM
W_IDX_OFF = K_IDX_OFF + IDX_DIM
IDX_COLS = 640
KIX_COLS = 4 * IDX_DIM
NEG_BIG = -1e30
HI = lax.Precision.HIGHEST
NT = (((1,), (1,)), ((), ()))


def _bdot(a, b):
    return jnp.dot(a.astype(BF16), b.astype(BF16), preferred_element_type=F32)


def _hdot(a, b):
    return jnp.dot(a, b, precision=HI, preferred_element_type=F32)


def _params(*sem):
    return pltpu.CompilerParams(dimension_semantics=sem, vmem_limit_bytes=VMEM_LIMIT)


def _adaln_kernel(c_ref, w_ref, b_ref, o_ref):
    c = c_ref[...]
    c_act = c * jax.nn.sigmoid(c)
    o_ref[...] = _hdot(c_act, w_ref[...]) + b_ref[...]


def _adaln(c, w_ada, b_ada):
    L, D, D6 = w_ada.shape
    B = c.shape[0]
    cb = 1024
    return pl.pallas_call(
        _adaln_kernel,
        grid=(L, D6 // cb),
        in_specs=[pl.BlockSpec((B, D), lambda l, j: (0, 0)),
                  pl.BlockSpec((None, D, cb), lambda l, j: (l, 0, j)),
                  pl.BlockSpec((None, 1, cb), lambda l, j: (l, 0, j))],
        out_specs=pl.BlockSpec((None, B, cb), lambda l, j: (l, 0, j)),
        out_shape=jax.ShapeDtypeStruct((L, B, D6), F32),
        compiler_params=_params("arbitrary", "arbitrary"),
        name="adaln",
    )(c, w_ada, b_ada.reshape(L, 1, D6))


def _norm_mod(x, g, sc, sh):
    ms = jnp.mean(x * x, axis=-1, keepdims=True)
    return (x * lax.rsqrt(ms + RMS_EPS) * g) * (1.0 + sc) + sh


def _inproj_kernel(x_ref, g_ref, sc_ref, sh_ref, wa_ref, wvt_ref, wbh_ref, wbl_ref,
                   rkvl_ref, qk_ref, vt_ref, qiw_ref, kix_ref):
    h = _norm_mod(x_ref[...], g_ref[...], sc_ref[...], sh_ref[...])
    hb = h.astype(BF16)
    hl = (h - hb.astype(F32)).astype(BF16)
    pa = jnp.dot(hb, wa_ref[...], preferred_element_type=F32)
    nr = rkvl_ref.shape[1]
    rkvl_ref[...] = pa[:, :nr]
    qk_ref[...] = pa[:, nr:].astype(BF16)
    vt = lax.dot_general(wvt_ref[...], hb, NT, preferred_element_type=F32)
    ones_row = lax.broadcasted_iota(I32, vt.shape, 0) % VT_ROWS == HEAD_DIM
    vt = jnp.where(ones_row, 1.0, vt).astype(BF16)
    for j in range(vt_ref.shape[0]):
        vt_ref[j] = vt[:, j * DSA_BLOCK:(j + 1) * DSA_BLOCK]
    pb = (jnp.dot(hb, wbh_ref[...], preferred_element_type=F32)
          + jnp.dot(hb, wbl_ref[...], preferred_element_type=F32)
          + jnp.dot(hl, wbh_ref[...], preferred_element_type=F32))
    qiw_ref[...] = pb
    ki = pb[:, K_IDX_OFF:W_IDX_OFF]
    kh = ki.astype(BF16)
    kl = (ki - kh.astype(F32)).astype(BF16)
    kix_ref[...] = jnp.concatenate([kh, kl, kh, jnp.zeros_like(kh)], axis=-1)


def _inproj(x2d, g, sc, sh, wa, wvt, wbh, wbl, T):
    N, D = x2d.shape
    nblk = INPROJ_TILE // DSA_BLOCK
    tm = INPROJ_TILE
    na = wa.shape[1]
    A = wvt.shape[0]
    nr = 3 * RWKV_HEADS * HEAD_DIM + 2 * LORA_PAD
    assert T % tm == 0 and (wa.shape[1] - nr) % LANES == 0
    nb = T // tm
    row = lambda i: (i, 0)
    per_b = lambda i: (i // nb, 0, 0)
    const = lambda i: (0, 0)
    once = pl.Buffered(1)
    return pl.pallas_call(
        _inproj_kernel,
        grid=(N // tm,),
        in_specs=[pl.BlockSpec((tm, D), row),
                  pl.BlockSpec((1, D), const),
                  pl.BlockSpec((None, 1, D), per_b),
                  pl.BlockSpec((None, 1, D), per_b),
                  pl.BlockSpec((D, na), const, pipeline_mode=once),
                  pl.BlockSpec((A, D), const, pipeline_mode=once),
                  pl.BlockSpec((D, IDX_COLS), const, pipeline_mode=once),
                  pl.BlockSpec((D, IDX_COLS), const, pipeline_mode=once)],
        out_specs=[pl.BlockSpec((tm, nr), row),
                   pl.BlockSpec((tm, na - nr), row),
                   pl.BlockSpec((nblk, A, DSA_BLOCK), lambda i: (i, 0, 0)),
                   pl.BlockSpec((tm, IDX_COLS), row),
                   pl.BlockSpec((tm, KIX_COLS), row)],
        out_shape=[jax.ShapeDtypeStruct((N, nr), F32),
                   jax.ShapeDtypeStruct((N, na - nr), BF16),
                   jax.ShapeDtypeStruct((N // DSA_BLOCK, A, DSA_BLOCK), BF16),
                   jax.ShapeDtypeStruct((N, IDX_COLS), F32),
                   jax.ShapeDtypeStruct((N, KIX_COLS), BF16)],
        compiler_params=_params("arbitrary"),
        name="inproj",
    )(x2d, g.reshape(1, D), sc, sh, wa, wvt, wbh, wbl)


def _head_ones(n):
    r = lax.broadcasted_iota(I32, (n, n), 0) // HEAD_DIM
    c = lax.broadcasted_iota(I32, (n, n), 1) // HEAD_DIM
    return jnp.where(r == c, 1.0, 0.0).astype(F32)


def _split(x):
    hi = x.astype(BF16)
    return hi, (x - hi.astype(F32)).astype(BF16)


def _split3(x):
    hi = x.astype(BF16)
    r = x - hi.astype(F32)
    mid = r.astype(BF16)
    return hi, mid, (r - mid.astype(F32)).astype(BF16)


def _block_diag(w, blocks):
    w = w.astype(BF16)
    return jnp.concatenate([jnp.where(m, w, jnp.zeros_like(w)) for m in blocks], axis=0)


def _block_diag_t(w, blocks):
    return jnp.concatenate([jnp.where(m, w, 0.0) for m in blocks], axis=0).T


def _fold_blocks(w):
    n = w.shape[1] // HEAD_DIM
    out = w[:HEAD_DIM]
    for h in range(1, n):
        out = out + w[h * HEAD_DIM:(h + 1) * HEAD_DIM]
    return out


def _mm(a, w):
    return jnp.dot(a.astype(BF16), w, preferred_element_type=F32)


def _rwkv_kernel(has_vres, TT, *refs):
    C = RWKV_CHUNK
    R = RWKV_HEADS * HEAD_DIM
    G = RWKV_GROUP
    NG = R // G
    NC = TT // C
    if has_vres:
        (rkvl_ref, vf_ref, mu_ref, w0_ref, db_ref, a0_ref, ib_ref, gb_ref, kk_ref, ka_ref,
         rk_ref, lg_ref, lb_ref, v0_ref, vb_ref, out_ref,
         prev_ref, S_ref, r_s, k_s, v_s, kk_s, b_s, lw_s, cum_s, y_s, q_s, y0_s, m_s, g_s) = refs
    else:
        (rkvl_ref, mu_ref, w0_ref, db_ref, a0_ref, ib_ref, gb_ref, kk_ref, ka_ref,
         rk_ref, lg_ref, lb_ref, out_ref, vfo_ref,
         prev_ref, S_ref, r_s, k_s, v_s, kk_s, b_s, lw_s, cum_s, y_s, q_s, y0_s, m_s, g_s) = refs

    @pl.when(pl.program_id(1) == 0)
    def _():
        prev_ref[...] = jnp.zeros_like(prev_ref)
        S_ref[...] = jnp.zeros_like(S_ref)

    row0 = lax.broadcasted_iota(I32, (TT, 1), 0) == 0
    prev = prev_ref[...]

    def shift(z, p):
        return jnp.where(row0, p, pltpu.roll(z, 1, 0))

    rkv = rkvl_ref[:, :3 * R]
    p1 = rkvl_ref[:, 3 * R:3 * R + LORA_PAD]
    p2 = rkvl_ref[:, 3 * R + LORA_PAD:]
    rkv_new = rkv + (shift(rkv, prev[:, :3 * R]) - rkv) * mu_ref[...]
    lora = p1 + shift(p2, prev[:, 3 * R:])
    prev_ref[...] = jnp.concatenate([rkv[TT - 1:TT, :], p2[TT - 1:TT, :]], axis=-1)

    r = rkv_new[:, :R]
    k = rkv_new[:, R:2 * R]
    v = rkv_new[:, 2 * R:]
    ones_h = _head_ones(G).astype(BF16)

    def head_sum(x):
        hi, lo = _split(x)
        return jnp.concatenate(
            [jnp.dot(hi[:, gi * G:(gi + 1) * G], ones_h, preferred_element_type=F32)
             + jnp.dot(lo[:, gi * G:(gi + 1) * G], ones_h, preferred_element_type=F32) for gi in range(NG)], axis=1)

    o_a = db_ref.shape[0]
    o_g = o_a + ib_ref.shape[0]
    o_v = o_g + gb_ref.shape[0]
    wlog = w0_ref[...] + _bdot(jnp.tanh(lora[:, :o_a]), db_ref[...])
    z = -wlog
    wlog = -(jnp.maximum(z, 0.0) + jnp.log(1.0 + jnp.exp(-jnp.abs(z)))) - 0.5
    lw = -jnp.exp(wlog)
    lw_s[...] = lw
    slab = min(TT, 4 * C)
    tr = lax.broadcasted_iota(I32, (slab, slab), 0)
    tc = lax.broadcasted_iota(I32, (slab, slab), 1)
    tri = jnp.where((tc <= tr) & (tr // C == tc // C), 1.0, 0.0).astype(BF16)
    parts = _split3(lw)
    for s0 in range(0, TT, slab):
        cum_s[s0:s0 + slab, :] = sum(jnp.dot(tri, part[s0:s0 + slab], preferred_element_type=F32) for part in parts)
    a = jax.nn.sigmoid(a0_ref[...] + _bdot(lora[:, o_a:o_g], ib_ref[...]))
    g = _bdot(jax.nn.sigmoid(lora[:, o_g:o_v]), gb_ref[...])
    kkr = k * kk_ref[...]
    kk = kkr * lax.rsqrt(jnp.maximum(head_sum(kkr * kkr), 1e-24))
    k = k * (1.0 + (a - 1.0) * ka_ref[...])
    if has_vres:
        v = v + (vf_ref[...] - v) * jax.nn.sigmoid(
            v0_ref[...] + _bdot(lora[:, o_v:o_v + vb_ref.shape[0]], vb_ref[...]))
    else:
        vfo_ref[...] = v
    r_s[...] = r
    k_s[...] = k
    v_s[...] = v
    kk_s[...] = kk
    b_s[...] = kk * a

    lane_g = lax.broadcasted_iota(I32, (C, RWKV_GROUP), 1)
    row_g = lax.broadcasted_iota(I32, (C, RWKV_GROUP), 0)
    blocks = [lane_g // HEAD_DIM == h for h in range(RWKV_GROUP // HEAD_DIM)]
    local = lane_g % HEAD_DIM
    strict = local < row_g
    incl = local <= row_g
    diag = local == row_g
    eye_cat = jnp.where(diag, 1.0, 0.0).astype(F32)
    levels = int(math.log2(C)) - 1

    def local_chunk(it, carry):
        chains = [(u, slice(gi * G, (gi + 1) * G)) for u in range(RWKV_UNROLL) for gi in range(NG)]
        each = lambda f: [f(i) for i in range(len(chains))]
        rd, kkd, kt, bt, kc, bc, vv, plast = ([] for _ in range(8))
        for u in range(RWKV_UNROLL):
            sl = pl.ds(pl.multiple_of((it * RWKV_UNROLL + u) * C, C), C)
            lw = lw_s[sl, :]
            cum = cum_s[sl, :]
            cl = cum[C - 1:C, :]
            pinv = jnp.exp(-cum)
            pc = jnp.exp(cl - cum)
            k_a = k_s[sl, :]
            b_a = b_s[sl, :]
            full = (r_s[sl, :] * jnp.exp(cum), kk_s[sl, :] * jnp.exp(cum - lw), k_a * pinv, b_a * pinv,
                    k_a * pc, b_a * pc, v_s[sl, :], jnp.exp(cl))
            for dst, x in zip((rd, kkd, kt, bt, kc, bc, vv, plast), full):
                dst.extend(x[:, gs] for (uu, gs) in chains if uu == u)
        kt_w = each(lambda i: _block_diag_t(kt[i], blocks).astype(BF16))
        bt_w = each(lambda i: _block_diag_t(bt[i], blocks).astype(BF16))
        bc_t = each(lambda i: _fold_blocks(_block_diag_t(bc[i], blocks)))
        kc_t = each(lambda i: _fold_blocks(_block_diag_t(kc[i], blocks)))
        rr = each(lambda i: _mm(jnp.concatenate([kkd[i], rd[i]], axis=0),
                                jnp.concatenate([kt_w[i], bt_w[i]], axis=1)))
        akk = each(lambda i: jnp.where(strict, rr[i][:C, :G], 0.0))
        ark = each(lambda i: jnp.where(incl, rr[i][C:, :G], 0.0))
        arb = each(lambda i: jnp.where(incl, rr[i][C:, G:], 0.0))
        npow = each(lambda i: jnp.where(strict, -rr[i][:C, G:], 0.0))
        x = each(lambda i: eye_cat + npow[i])
        npow = each(lambda i: _mm(npow[i], _block_diag(npow[i], blocks)))
        for _ in range(levels - 1):
            rr = each(lambda i: _mm(jnp.concatenate([npow[i], x[i]], axis=0), _block_diag(npow[i], blocks)))
            npow = each(lambda i: rr[i][:C])
            x = each(lambda i: x[i] + rr[i][C:])
        rr = each(lambda i: _mm(x[i], _block_diag(npow[i], blocks)))
        x = each(lambda i: x[i] + rr[i])
        rv = each(lambda i: _mm(jnp.concatenate([akk[i], ark[i], kc_t[i]], axis=0), _block_diag(vv[i], blocks)))
        ru = each(lambda i: _mm(x[i], jnp.concatenate([_block_diag(kkd[i], blocks),
                                                       _block_diag(rv[i][:C], blocks)], axis=1)))
        rr = each(lambda i: _mm(jnp.concatenate([arb[i], bc_t[i]], axis=0),
                                jnp.concatenate([_block_diag(ru[i][:, :G], blocks),
                                                 _block_diag(ru[i][:, G:], blocks)], axis=1)))
        for i, (u, gs) in enumerate(chains):
            c = it * RWKV_UNROLL + u
            q_s[c, :, gs] = rd[i] - rr[i][:C, :G]
            y0_s[c, :, gs] = rv[i][C:2 * C] - rr[i][:C, G:]
            dterm = jnp.where(diag, jnp.broadcast_to(plast[i], (C, G)), 0.0)
            m_s[c, :, gs] = dterm - rr[i][C:, :G]
            g_s[c, :, gs] = rv[i][2 * C:] - rr[i][C:, G:]
        return carry

    lax.fori_loop(0, NC // RWKV_UNROLL, local_chunk, 0)

    def scan_chunk(c, carry):
        sl = pl.ds(pl.multiple_of(c * C, C), C)
        for gi in range(NG):
            gs = slice(gi * RWKV_GROUP, (gi + 1) * RWKV_GROUP)
            rr = _mm(jnp.concatenate([q_s[c, :, gs], m_s[c, :, gs]], axis=0), _block_diag(S_ref[:, gs], blocks))
            y_s[sl, gs] = rr[:C] + y0_s[c, :, gs]
            S_ref[:, gs] = rr[C:] + g_s[c, :, gs]
        return carry

    lax.fori_loop(0, NC, scan_chunk, 0)

    y = y_s[...]
    inv_n = 1.0 / HEAD_DIM
    mean = head_sum(y) * inv_n
    yc = y - mean
    var = head_sum(yc * yc) * inv_n
    yn = yc * lax.rsqrt(var + LNX_EPS) * lg_ref[...] + lb_ref[...]
    bonus = head_sum(r_s[...] * k_s[...] * rk_ref[...]) * v_s[...]
    out_ref[...] = ((yn + bonus) * g).astype(out_ref.dtype)


def _rwkv(rkvl, v_first, p, B, T):
    N, nc = rkvl.shape
    R = RWKV_HEADS * HEAD_DIM
    TT = min(RWKV_TILE, T)
    nt = T // TT
    NC = TT // RWKV_CHUNK
    has_vres = v_first is not None
    row = lambda b, i: (b * nt + i, 0)
    const = lambda b, i: (0, 0)
    vec = lambda a: a.reshape(1, -1)
    ins = [rkvl]
    specs = [pl.BlockSpec((TT, nc), row)]
    if has_vres:
        ins.append(v_first)
        specs.append(pl.BlockSpec((TT, R), row))
    small = [vec(p["mu_rkv"]), vec(p["decay_w0"]), p["decay_b"].astype(BF16), vec(p["iclr_a0"]),
             p["iclr_b"].astype(BF16), p["gate_b"].astype(BF16), vec(p["k_k"]), vec(p["k_a"]),
             vec(p["r_k"]), vec(p["lnx_g"]), vec(p["lnx_b"])]
    if has_vres:
        small += [vec(p["vres_v0"]), p["vres_b"].astype(BF16)]
    ins += small
    specs += [pl.BlockSpec(a.shape, const) for a in small]
    out_shape = [jax.ShapeDtypeStruct((N, R), BF16)]
    out_specs = [pl.BlockSpec((TT, R), row)]
    if not has_vres:
        out_shape.append(jax.ShapeDtypeStruct((N, R), F32))
        out_specs.append(pl.BlockSpec((TT, R), row))
    scratch = [pltpu.VMEM((1, 3 * R + LORA_PAD), F32),
               pltpu.VMEM((HEAD_DIM, R), F32)]
    scratch += [pltpu.VMEM((TT, R), F32) for _ in range(8)]
    scratch += [pltpu.VMEM((NC, RWKV_CHUNK, R), F32) for _ in range(4)]
    res = pl.pallas_call(
        functools.partial(_rwkv_kernel, has_vres, TT),
        grid=(B, nt),
        in_specs=specs,
        out_specs=out_specs,
        out_shape=out_shape,
        scratch_shapes=scratch,
        compiler_params=_params("arbitrary", "arbitrary"),
        name="rwkv",
    )(*ins)
    if has_vres:
        return res[0], v_first
    return res[0], res[1]


def _bucket_boundaries():
    max_exact = N_BUCKETS // 2
    d = np.arange(0, 4 * MAX_DISTANCE, dtype=np.int64)
    nf = np.maximum(d, 1).astype(np.float32)
    large = max_exact + (np.log(nf / np.float32(max_exact)) / np.float32(math.log(MAX_DISTANCE / max_exact))
                         * np.float32(N_BUCKETS - max_exact)).astype(np.int32)
    large = np.minimum(large, N_BUCKETS - 1)
    bucket = np.where(d < max_exact, d, large)
    return [int(np.argmax(bucket >= j)) for j in range(max_exact + 1, N_BUCKETS)]


_BUCKET_STARTS = _bucket_boundaries()
KEY_NEG_INF = int(np.int32(np.array(-np.inf, np.float32).view(np.int32)) ^ np.int32(0x7FFFFFFF))
INT_MIN = -2 ** 31
INT_MAX = 2 ** 31 - 1
FAST_SELECT_BLIND = 14
FAST_SELECT_STEPS = 2
FAST_SELECT_TRIPS = 10


def _sort_key(s):
    bits = pltpu.bitcast(s, I32)
    bits = jnp.where(bits == INT_MIN, 0, bits)
    return bits ^ ((bits >> 31) & jnp.int32(0x7FFFFFFF))


def _key_to_float(k):
    return pltpu.bitcast(k ^ ((k >> 31) & jnp.int32(0x7FFFFFFF)), F32)


def _dsa_kernel(T, ksel, q_ref, k_ref, vt_ref, qiw_ref, kix_ref, relb_ref, g_ref, out_ref,
                lhs_s, qpad_s, key_s, madd_s, bias_s, m_s, acc_s, s_s, p_s, cmax_s, alpha_s):
    TQ = TK = DSA_BLOCK
    H = ATTN_HEADS
    qb = pl.program_id(1)
    nch = qb + 1
    rows = lax.broadcasted_iota(I32, (TK, TQ), 0)
    cols = lax.broadcasted_iota(I32, (TK, TQ), 1)

    @pl.when((pl.program_id(0) == 0) & (qb == 0))
    def _():
        max_exact = N_BUCKETS // 2
        for off in range(2):
            d = cols - rows + off * TK
            bucket = jnp.where(d < max_exact, jnp.maximum(d, 0), max_exact)
            for start in _BUCKET_STARTS:
                bucket = bucket + jnp.where(d >= start, 1, 0)
            for h in range(H):
                far = relb_ref[N_BUCKETS - 1, h]
                tile = jnp.zeros((TK, TQ), F32)
                for bk in range(N_BUCKETS - 1):
                    tile = jnp.where(bucket == bk, (relb_ref[bk, h] - far) * LOG2E, tile)
                bias_s[h, off] = tile

    qi_t = (qiw_ref[:, :IDX_HEADS * IDX_DIM] * (IDX_DIM ** -0.5)).T
    for h in range(IDX_HEADS):
        qh = qi_t[h * IDX_DIM:(h + 1) * IDX_DIM, :]
        hi = qh.astype(BF16)
        lo = (qh - hi.astype(F32)).astype(BF16)
        lhs_s[h] = jnp.concatenate([hi, hi, lo, jnp.zeros_like(hi)], axis=0)
    w_t = qiw_ref[:, K_IDX_OFF:IDX_COLS].T[IDX_DIM:IDX_DIM + IDX_HEADS, :] * (IDX_HEADS ** -0.5)

    q_t = q_ref[...].astype(F32).T
    zeros_h = jnp.zeros((HEAD_DIM, TQ), BF16)
    for h in range(H):
        qh = q_t[h * HEAD_DIM:(h + 1) * HEAD_DIM, :].astype(BF16)
        qpad_s[h] = jnp.concatenate([qh, zeros_h] if h % 2 == 0 else [zeros_h, qh], axis=0)

    def scores(j):
        kx = kix_ref[pl.ds(pl.multiple_of(j * TK, TK), TK), :]
        acc = jnp.zeros((TK, TQ), F32)
        for h in range(IDX_HEADS):
            d = jnp.dot(kx, lhs_s[h], preferred_element_type=F32)
            acc = acc + w_t[h:h + 1, :] * jnp.maximum(d, 0.0)
        return acc

    def stats(st, kk, mn_of):
        fold = lambda z: z.reshape(TK // 8, 8, TQ)
        ge0, gt0, mn, mx = st
        return (ge0 + jnp.sum(fold(jnp.where(kk >= 0, 1, 0)), axis=0),
                gt0 + jnp.sum(fold(jnp.where(kk > 0, 1, 0)), axis=0),
                jnp.minimum(mn, jnp.min(fold(mn_of(kk)), axis=0)),
                jnp.maximum(mx, jnp.max(fold(kk), axis=0)))

    def score_body(j, st):
        kk = _sort_key(scores(j))
        key_s[j] = kk
        return stats(st, kk, lambda z: z)

    st = lax.fori_loop(0, qb, score_body, (jnp.zeros((8, TQ), I32), jnp.zeros((8, TQ), I32),
                                           jnp.full((8, TQ), INT_MAX, I32), jnp.full((8, TQ), INT_MIN, I32)))
    kd = _sort_key(jnp.where(rows <= cols, scores(qb), -jnp.inf))
    key_s[qb] = kd
    st = stats(st, kd, lambda z: jnp.where(z > KEY_NEG_INF, z, INT_MAX))
    n_ge0, n_gt0 = (jnp.sum(z, axis=0, keepdims=True) for z in st[:2])
    lo0 = _key_to_float(jnp.min(st[2], axis=0, keepdims=True))
    hi0 = _key_to_float(jnp.max(st[3], axis=0, keepdims=True) + 1)

    def count(pred):
        def body(j, acc):
            hit = jnp.where(pred(key_s[j]), 1, 0)
            return acc + jnp.sum(hit.reshape(TK // 8, 8, TQ), axis=0)
        acc = lax.fori_loop(0, nch, body, jnp.zeros((8, TQ), I32))
        return jnp.sum(acc, axis=0, keepdims=True)

    def write_mask(sel_of):
        def body(j, c):
            madd_s[j] = jnp.where(sel_of(key_s[j]), 0.0, NEG_BIG)
            return c
        lax.fori_loop(0, nch, body, 0)

    kf = float(ksel)
    n_adm = qb * TQ + lax.broadcasted_iota(I32, (1, TQ), 1) + 1
    trivial = n_adm <= ksel
    positive = n_gt0 > ksel
    done0 = trivial | ((n_gt0 <= ksel) & (n_ge0 >= ksel))
    thr0 = jnp.where(trivial, KEY_NEG_INF + 1, jnp.where(n_gt0 == ksel, 1, 0))
    need0 = jnp.where(n_gt0 < ksel, ksel - n_gt0, ksel)
    a0 = jnp.where(positive, 0.0, lo0)
    b0 = jnp.where(positive, hi0, 0.0)
    fa0 = jnp.where(positive, n_gt0, n_adm).astype(F32) - kf
    cb0 = jnp.where(positive, 0, n_ge0)
    as_int = lambda m: jnp.where(m, 1, 0)

    def any_lane(m):
        return jnp.max(as_int(m)) > 0

    def fast_cond(st):
        it, done, stuck = st[0], st[1], st[2]
        return (it < FAST_SELECT_TRIPS) & any_lane((done + stuck) == 0)

    def fast_body(st):
        it, st = st[0], st[1:]
        for _ in range(FAST_SELECT_STEPS):
            st = fast_step(st)
        return (it + 1,) + st

    def fast_step(st):
        done, stuck, thr, need, a, b, fa, fb, cb, last = st
        c = a + (b - a) * (fa / (fa - fb))
        c = jnp.where((c > a) & (c < b), c, 0.5 * a + 0.5 * b)
        inside = (c > a) & (c < b)
        ckey = _sort_key(c)
        cnt = count(lambda kk: kk >= ckey)
        fc = cnt.astype(F32) - kf
        hit = inside & (cnt == ksel) & (done == 0)
        thr = jnp.where(hit, ckey, thr)
        need = jnp.where(hit, ksel, need)
        done = jnp.where(hit, 1, done)
        stuck = jnp.where(inside, stuck, 1)
        up = inside & (cnt > ksel)
        dn = inside & (cnt < ksel)
        fb = jnp.where(up & (last == 1), 0.5 * fb, fb)
        fa = jnp.where(dn & (last == 0), 0.5 * fa, fa)
        a, fa = jnp.where(up, c, a), jnp.where(up, fc, fa)
        b, fb, cb = jnp.where(dn, c, b), jnp.where(dn, fc, fb), jnp.where(dn, cnt, cb)
        return done, stuck, thr, need, a, b, fa, fb, cb, jnp.where(up, 1, jnp.where(dn, 0, last))

    st = (as_int(done0), jnp.zeros((1, TQ), I32), thr0, need0,
          a0, b0, fa0, cb0.astype(F32) - kf, cb0, jnp.full((1, TQ), -1, I32))
    blind = jnp.where((qb + 1) * TQ <= ksel, 0, FAST_SELECT_BLIND)
    st = lax.fori_loop(0, blind, lambda i, s: fast_step(s), st)
    st = lax.while_loop(fast_cond, fast_body, (jnp.int32(0),) + st)
    done, thr, need, a, b, cb = st[1], st[3], st[4], st[5], st[6], st[9]
    open_lane = done == 0
    any_open = any_lane(open_lane)

    def close_cond(st):
        return any_lane((st[1] - st[0]) > 1)

    def close_body(st):
        ak, bk, cb = st
        mid = ak + ((bk - ak) >> 1)
        cnt = count(lambda kk: kk >= mid)
        ge = cnt >= ksel
        return jnp.where(ge, mid, ak), jnp.where(ge, bk, mid), jnp.where(ge, cb, cnt)

    ak, _, cb = lax.while_loop(close_cond, close_body, (
        jnp.where(open_lane, _sort_key(a), thr), jnp.where(open_lane, _sort_key(b), thr), cb))
    thr = jnp.where(open_lane, ak, thr)
    need = jnp.where(open_lane, ksel - cb, need)
    any_tie = any_open | any_lane(jnp.logical_not(trivial) & (n_gt0 < ksel) & (n_ge0 > ksel))

    @pl.when(jnp.logical_not(any_tie))
    def _():
        write_mask(lambda kk: kk >= thr)

    @pl.when(any_tie)
    def _():
        kr = lax.broadcasted_iota(I32, (TK, TK), 0)
        kc = lax.broadcasted_iota(I32, (TK, TK), 1)
        lower = jnp.where(kc <= kr, 1.0, 0.0).astype(BF16)
        need_f = need.astype(F32)

        def body(j, run):
            kk = key_s[j]
            eq = kk == thr
            eqf = jnp.where(eq, 1.0, 0.0)
            rank = run + jnp.dot(lower, eqf.astype(BF16), preferred_element_type=F32)
            tied_in = jnp.where(eq, rank, float(TOPK_MAX + 1)) <= need_f
            madd_s[j] = jnp.where(kk > thr, 0.0, jnp.where(tied_in, 0.0, NEG_BIG))
            return run + jnp.sum(eqf, axis=0, keepdims=True)
        lax.fori_loop(0, nch, body, jnp.zeros((1, TQ), F32))

    m_s[...] = jnp.full(m_s.shape, NEG_BIG, F32)
    acc_s[...] = jnp.zeros(acc_s.shape, F32)

    def attend(j, bias_of):
        ks = pl.ds(pl.multiple_of(j * TK, TK), TK)
        ma = madd_s[j]
        for h in range(H):
            kp = k_ref[ks, LANES * (h // 2):LANES * (h // 2 + 1)]
            s = jnp.dot(kp, qpad_s[h], preferred_element_type=F32) + ma
            bias = bias_of(h)
            if bias is not None:
                s = s + bias
            s_s[h] = s
            cmax_s[h] = jnp.max(s.reshape(TK // 8, 8, TQ), axis=0)
        for h in range(H):
            m_old = m_s[h]
            m_new = jnp.maximum(m_old, jnp.max(cmax_s[h], axis=0, keepdims=True))
            alpha_s[h] = jnp.exp2(m_old - m_new)
            m_s[h] = m_new
            p_s[h] = jnp.exp2(s_s[h] - m_new).astype(BF16)
        for h in range(H):
            hs = slice(h * VT_ROWS, (h + 1) * VT_ROWS)
            pv = jnp.dot(vt_ref[j, hs, :], p_s[h], preferred_element_type=F32)
            acc_s[hs, :] = alpha_s[h] * acc_s[hs, :] + pv

    def far_body(j, c):
        attend(j, lambda h: None)
        return c

    lax.fori_loop(0, jnp.maximum(qb - 1, 0), far_body, 0)

    @pl.when(qb >= 1)
    def _():
        attend(qb - 1, lambda h: bias_s[h, 1])

    attend(qb, lambda h: bias_s[h, 0])

    outs = []
    for h in range(H):
        o = acc_s[h * VT_ROWS:h * VT_ROWS + HEAD_DIM, :] / acc_s[h * VT_ROWS + HEAD_DIM:h * VT_ROWS + HEAD_DIM + 1, :]
        ms = jnp.mean(o * o, axis=0, keepdims=True)
        outs.append(o * lax.rsqrt(ms + RMS_EPS))
    out_ref[...] = (jnp.concatenate(outs, axis=0).T * g_ref[...]).astype(out_ref.dtype)


def _dsa(qk, vt, qiw, kix, rel_bias, g, B, T):
    N = qk.shape[0]
    TQ = DSA_BLOCK
    nq = T // TQ
    A = ATTN_HEADS * HEAD_DIM
    ksel = min(TOPK_MAX, T // 4)
    assert MAX_DISTANCE <= TQ + 1
    return pl.pallas_call(
        functools.partial(_dsa_kernel, T, ksel),
        grid=(B, nq),
        in_specs=[pl.BlockSpec((TQ, A), lambda b, i: (b * nq + i, 0)),
                  pl.BlockSpec((T, A), lambda b, i: (b, 1)),
                  pl.BlockSpec((nq, ATTN_HEADS * VT_ROWS, TQ), lambda b, i: (b, 0, 0)),
                  pl.BlockSpec((TQ, IDX_COLS), lambda b, i: (b * nq + i, 0)),
                  pl.BlockSpec((T, KIX_COLS), lambda b, i: (b, 0)),
                  pl.BlockSpec(memory_space=pltpu.SMEM),
                  pl.BlockSpec((1, A), lambda b, i: (0, 0))],
        out_specs=pl.BlockSpec((TQ, A), lambda b, i: (b * nq + i, 0)),
        out_shape=jax.ShapeDtypeStruct((N, A), BF16),
        scratch_shapes=[pltpu.VMEM((IDX_HEADS, KIX_COLS, TQ), BF16),
                        pltpu.VMEM((ATTN_HEADS, LANES, TQ), BF16),
                        pltpu.VMEM((nq, TQ, TQ), I32),
                        pltpu.VMEM((nq, TQ, TQ), F32),
                        pltpu.VMEM((ATTN_HEADS, 2, TQ, TQ), F32),
                        pltpu.VMEM((ATTN_HEADS, 1, TQ), F32),
                        pltpu.VMEM((ATTN_HEADS * VT_ROWS, TQ), F32),
                        pltpu.VMEM((ATTN_HEADS, TQ, TQ), F32),
                        pltpu.VMEM((ATTN_HEADS, TQ, TQ), BF16),
                        pltpu.VMEM((ATTN_HEADS, 8, TQ), F32),
                        pltpu.VMEM((ATTN_HEADS, 1, TQ), F32)],
        compiler_params=_params("arbitrary", "arbitrary"),
        name="dsa",
    )(qk, qk, vt, qiw, kix, rel_bias, g.reshape(1, A))


def _mix_mlp_kernel(final, x_ref, rw_ref, att_ref, wo_ref, gt1_ref, g_ref, sc_ref, sh_ref, gt2_ref,
                    w1_ref, w2_ref, fg_ref, o_ref, x_s, h_s, acc_s):
    j = pl.program_id(1)

    @pl.when(j == 0)
    def _():
        R = rw_ref.shape[1]
        mixed = (jnp.dot(rw_ref[...], wo_ref[:R, :], preferred_element_type=F32)
                 + jnp.dot(att_ref[...], wo_ref[R:, :], preferred_element_type=F32))
        x1 = x_ref[...] + gt1_ref[...] * mixed
        x_s[...] = x1
        h_s[...] = _norm_mod(x1, g_ref[...], sc_ref[...], sh_ref[...]).astype(BF16)
        acc_s[...] = jnp.zeros_like(acc_s)

    u = jnp.dot(h_s[...], w1_ref[...], preferred_element_type=F32)
    u = jnp.square(jnp.maximum(u, 0.0))
    acc_s[...] += jnp.dot(u.astype(BF16), w2_ref[...], preferred_element_type=F32)

    @pl.when(j == pl.num_programs(1) - 1)
    def _():
        y = x_s[...] + gt2_ref[...] * acc_s[...]
        if final:
            ms = jnp.mean(y * y, axis=-1, keepdims=True)
            y = y * lax.rsqrt(ms + RMS_EPS) * fg_ref[...]
        o_ref[...] = y


def _mix_mlp(x2d, rw, att, wo, gt1, g, sc, sh, gt2, w1, w2, final_g, final, T, fc=512):
    N, D = x2d.shape
    F = w1.shape[1]
    tm = min(MLP_TILE, T)
    nb = T // tm
    row = lambda i, j: (i, 0)
    per_b = lambda i, j: (i // nb, 0, 0)
    const = lambda i, j: (0, 0)
    return pl.pallas_call(
        functools.partial(_mix_mlp_kernel, final),
        grid=(N // tm, F // fc),
        in_specs=[pl.BlockSpec((tm, D), row),
                  pl.BlockSpec((tm, rw.shape[1]), row),
                  pl.BlockSpec((tm, att.shape[1]), row),
                  pl.BlockSpec(wo.shape, const, pipeline_mode=pl.Buffered(1)),
                  pl.BlockSpec((None, 1, D), per_b),
                  pl.BlockSpec((1, D), const),
                  pl.BlockSpec((None, 1, D), per_b),
                  pl.BlockSpec((None, 1, D), per_b),
                  pl.BlockSpec((None, 1, D), per_b),
                  pl.BlockSpec((D, fc), lambda i, j: (0, j)),
                  pl.BlockSpec((fc, D), lambda i, j: (j, 0)),
                  pl.BlockSpec((1, D), const)],
        out_specs=pl.BlockSpec((tm, D), row),
        out_shape=jax.ShapeDtypeStruct((N, D), F32),
        scratch_shapes=[pltpu.VMEM((tm, D), F32), pltpu.VMEM((tm, D), BF16), pltpu.VMEM((tm, D), F32)],
        compiler_params=_params("arbitrary", "arbitrary"),
        name="mix_mlp",
    )(x2d, rw, att, wo, gt1, g.reshape(1, D), sc, sh, gt2, w1, w2, final_g.reshape(1, D))


def _pad_cols(w, n):
    return jnp.pad(w, ((0, 0), (0, n - w.shape[1])))


def _split_bf16(w):
    hi = w.astype(BF16)
    return hi, (w - hi.astype(F32)).astype(BF16)


def _in_weights(l, w_in, mu_lora, decay_a, iclr_a, gate_a, vres_mu, vres_a):
    D = w_in.shape[1]
    R = RWKV_HEADS * HEAD_DIM
    w = w_in[l]
    mats = [(decay_a[l], mu_lora[l, 0]), (iclr_a[l], mu_lora[l, 1]), (gate_a[l], mu_lora[l, 2])]
    if l > 0:
        mats.append((vres_a[l - 1], vres_mu[l - 1]))
    now = _pad_cols(jnp.concatenate([a * (1.0 - mu)[:, None] for a, mu in mats], axis=1), LORA_PAD)
    prev = _pad_cols(jnp.concatenate([a * mu[:, None] for a, mu in mats], axis=1), LORA_PAD)
    wq = w[:, 3 * R:4 * R] * (HEAD_DIM ** -0.5 * LOG2E)
    wa = jnp.concatenate([w[:, :3 * R], now, prev, wq, w[:, 4 * R:5 * R]], axis=1).astype(BF16)
    wv = w[:, 5 * R:6 * R].reshape(D, ATTN_HEADS, HEAD_DIM)
    wv = jnp.pad(wv, ((0, 0), (0, 0), (0, VT_ROWS - HEAD_DIM))).reshape(D, ATTN_HEADS * VT_ROWS)
    wvt = wv.T.astype(BF16)
    wbh, wbl = _split_bf16(_pad_cols(w[:, 6 * R:], IDX_COLS))
    return wa, wvt, wbh, wbl


def kernel(x, c, w_ada, b_ada, norm1_g, norm2_g, w_in, mu_rkv, mu_lora, decay_w0, decay_a, decay_b, iclr_a0, iclr_a, iclr_b, gate_a, gate_b, k_k, k_a, r_k, lnx_g, lnx_b, vres_mu, vres_v0, vres_a, vres_b, attn_out_g, rel_bias, w_out, w_mlp1, w_mlp2, final_g):
    B, T, D = x.shape
    depth = w_in.shape[0]
    mod = _adaln(c, w_ada, b_ada)
    mod = mod.reshape(depth, B, 6, 1, D).transpose(0, 2, 1, 3, 4)
    x2d = x.reshape(B * T, D)
    v_first = None
    for l in range(depth):
        sh1, sc1, gt1, sh2, sc2, gt2 = (mod[l, i] for i in range(6))
        wa, wvt, wbh, wbl = _in_weights(l, w_in, mu_lora, decay_a, iclr_a, gate_a, vres_mu, vres_a)
        rkvl, qk, vt, qiw, kix = _inproj(x2d, norm1_g[l], sc1, sh1, wa, wvt, wbh, wbl, T)
        p = dict(mu_rkv=mu_rkv[l], decay_w0=decay_w0[l], decay_b=decay_b[l], iclr_a0=iclr_a0[l],
                 iclr_b=iclr_b[l], gate_b=gate_b[l], k_k=k_k[l], k_a=k_a[l], r_k=r_k[l],
                 lnx_g=lnx_g[l], lnx_b=lnx_b[l])
        if l > 0:
            p.update(vres_v0=vres_v0[l - 1], vres_b=vres_b[l - 1])
        rw, v_first = _rwkv(rkvl, v_first, p, B, T)
        att = _dsa(qk, vt, qiw, kix, rel_bias, attn_out_g[l], B, T)
        x2d = _mix_mlp(x2d, rw, att, w_out[l].astype(BF16), gt1, norm2_g[l], sc2, sh2, gt2,
                       w_mlp1[l].astype(BF16), w_mlp2[l].astype(BF16), final_g, l == depth - 1, T)
    return x2d.reshape(B, T, D)
```

```python
import functools
import math

import numpy as np
import jax
import jax.numpy as jnp
from jax import lax
from jax.experimental import pallas as pl
from jax.experimental.pallas import tpu as pltpu

F32 = jnp.float32
BF16 = jnp.bfloat16
I32 = jnp.int32

HEAD_DIM = 64
RWKV_HEADS = 8
ATTN_HEADS = 8
IDX_HEADS = 8
IDX_DIM = 64
TOPK_MAX = 256
N_BUCKETS = 32
MAX_DISTANCE = 128
RMS_EPS = 1e-6
LNX_EPS = 64e-5

LANES = 128
VMEM_LIMIT = 56 * 1024 * 1024
RWKV_CHUNK = 64
RWKV_GROUP = 256
RWKV_TILE = 512
RWKV_UNROLL = 8
DSA_BLOCK = 256
VT_ROWS = 80
LOG2E = math.log2(math.e)
INPROJ_TILE = 512
MLP_TILE = 1024
LORA_PAD = 384
K_IDX_OFF = IDX_HEADS * IDX_DIM
W_IDX_OFF = K_IDX_OFF + IDX_DIM
IDX_COLS = 640
KIX_COLS = 4 * IDX_DIM
NEG_BIG = -1e30
HI = lax.Precision.HIGHEST
NT = (((1,), (1,)), ((), ()))


def _bdot(a, b):
    return jnp.dot(a.astype(BF16), b.astype(BF16), preferred_element_type=F32)


def _hdot(a, b):
    return jnp.dot(a, b, precision=HI, preferred_element_type=F32)


def _params(*sem):
    return pltpu.CompilerParams(dimension_semantics=sem, vmem_limit_bytes=VMEM_LIMIT)


def _adaln_kernel(c_ref, w_ref, b_ref, o_ref):
    c = c_ref[...]
    c_act = c * jax.nn.sigmoid(c)
    o_ref[...] = _hdot(c_act, w_ref[...]) + b_ref[...]


def _adaln(c, w_ada, b_ada):
    L, D, D6 = w_ada.shape
    B = c.shape[0]
    cb = 1024
    return pl.pallas_call(
        _adaln_kernel,
        grid=(L, D6 // cb),
        in_specs=[pl.BlockSpec((B, D), lambda l, j: (0, 0)),
                  pl.BlockSpec((None, D, cb), lambda l, j: (l, 0, j)),
                  pl.BlockSpec((None, 1, cb), lambda l, j: (l, 0, j))],
        out_specs=pl.BlockSpec((None, B, cb), lambda l, j: (l, 0, j)),
        out_shape=jax.ShapeDtypeStruct((L, B, D6), F32),
        compiler_params=_params("arbitrary", "arbitrary"),
        name="adaln",
    )(c, w_ada, b_ada.reshape(L, 1, D6))


def _norm_mod(x, g, sc, sh):
    ms = jnp.mean(x * x, axis=-1, keepdims=True)
    return (x * lax.rsqrt(ms + RMS_EPS) * g) * (1.0 + sc) + sh


def _inproj_kernel(x_ref, g_ref, sc_ref, sh_ref, wa_ref, wvt_ref, wbh_ref, wbl_ref,
                   rkvl_ref, qk_ref, vt_ref, qiw_ref, kix_ref):
    h = _norm_mod(x_ref[...], g_ref[...], sc_ref[...], sh_ref[...])
    hb = h.astype(BF16)
    hl = (h - hb.astype(F32)).astype(BF16)
    pa = jnp.dot(hb, wa_ref[...], preferred_element_type=F32)
    nr = rkvl_ref.shape[1]
    rkvl_ref[...] = pa[:, :nr]
    qk_ref[...] = pa[:, nr:].astype(BF16)
    vt = lax.dot_general(wvt_ref[...], hb, NT, preferred_element_type=F32)
    ones_row = lax.broadcasted_iota(I32, vt.shape, 0) % VT_ROWS == HEAD_DIM
    vt = jnp.where(ones_row, 1.0, vt).astype(BF16)
    for j in range(vt_ref.shape[0]):
        vt_ref[j] = vt[:, j * DSA_BLOCK:(j + 1) * DSA_BLOCK]
    pb = (jnp.dot(hb, wbh_ref[...], preferred_element_type=F32)
          + jnp.dot(hb, wbl_ref[...], preferred_element_type=F32)
          + jnp.dot(hl, wbh_ref[...], preferred_element_type=F32))
    qiw_ref[...] = pb
    ki = pb[:, K_IDX_OFF:W_IDX_OFF]
    kh = ki.astype(BF16)
    kl = (ki - kh.astype(F32)).astype(BF16)
    kix_ref[...] = jnp.concatenate([kh, kl, kh, jnp.zeros_like(kh)], axis=-1)


def _inproj(x2d, g, sc, sh, wa, wvt, wbh, wbl, T):
    N, D = x2d.shape
    nblk = INPROJ_TILE // DSA_BLOCK
    tm = INPROJ_TILE
    na = wa.shape[1]
    A = wvt.shape[0]
    nr = 3 * RWKV_HEADS * HEAD_DIM + 2 * LORA_PAD
    assert T % tm == 0 and (wa.shape[1] - nr) % LANES == 0
    nb = T // tm
    row = lambda i: (i, 0)
    per_b = lambda i: (i // nb, 0, 0)
    const = lambda i: (0, 0)
    once = pl.Buffered(1)
    return pl.pallas_call(
        _inproj_kernel,
        grid=(N // tm,),
        in_specs=[pl.BlockSpec((tm, D), row),
                  pl.BlockSpec((1, D), const),
                  pl.BlockSpec((None, 1, D), per_b),
                  pl.BlockSpec((None, 1, D), per_b),
                  pl.BlockSpec((D, na), const, pipeline_mode=once),
                  pl.BlockSpec((A, D), const, pipeline_mode=once),
                  pl.BlockSpec((D, IDX_COLS), const, pipeline_mode=once),
                  pl.BlockSpec((D, IDX_COLS), const, pipeline_mode=once)],
        out_specs=[pl.BlockSpec((tm, nr), row),
                   pl.BlockSpec((tm, na - nr), row),
                   pl.BlockSpec((nblk, A, DSA_BLOCK), lambda i: (i, 0, 0)),
                   pl.BlockSpec((tm, IDX_COLS), row),
                   pl.BlockSpec((tm, KIX_COLS), row)],
        out_shape=[jax.ShapeDtypeStruct((N, nr), F32),
                   jax.ShapeDtypeStruct((N, na - nr), BF16),
                   jax.ShapeDtypeStruct((N // DSA_BLOCK, A, DSA_BLOCK), BF16),
                   jax.ShapeDtypeStruct((N, IDX_COLS), F32),
                   jax.ShapeDtypeStruct((N, KIX_COLS), BF16)],
        compiler_params=_params("arbitrary"),
        name="inproj",
    )(x2d, g.reshape(1, D), sc, sh, wa, wvt, wbh, wbl)


def _head_ones(n):
    r = lax.broadcasted_iota(I32, (n, n), 0) // HEAD_DIM
    c = lax.broadcasted_iota(I32, (n, n), 1) // HEAD_DIM
    return jnp.where(r == c, 1.0, 0.0).astype(F32)


def _split(x):
    hi = x.astype(BF16)
    return hi, (x - hi.astype(F32)).astype(BF16)


def _split3(x):
    hi = x.astype(BF16)
    r = x - hi.astype(F32)
    mid = r.astype(BF16)
    return hi, mid, (r - mid.astype(F32)).astype(BF16)


def _block_diag(w, blocks):
    w = w.astype(BF16)
    return jnp.concatenate([jnp.where(m, w, jnp.zeros_like(w)) for m in blocks], axis=0)


def _block_diag_t(w, blocks):
    return jnp.concatenate([jnp.where(m, w, 0.0) for m in blocks], axis=0).T


def _fold_blocks(w):
    n = w.shape[1] // HEAD_DIM
    out = w[:HEAD_DIM]
    for h in range(1, n):
        out = out + w[h * HEAD_DIM:(h + 1) * HEAD_DIM]
    return out


def _mm(a, w):
    return jnp.dot(a.astype(BF16), w, preferred_element_type=F32)


def _rwkv_kernel(has_vres, TT, *refs):
    C = RWKV_CHUNK
    R = RWKV_HEADS * HEAD_DIM
    G = RWKV_GROUP
    NG = R // G
    NC = TT // C
    if has_vres:
        (rkvl_ref, vf_ref, mu_ref, w0_ref, db_ref, a0_ref, ib_ref, gb_ref, kk_ref, ka_ref,
         rk_ref, lg_ref, lb_ref, v0_ref, vb_ref, out_ref,
         prev_ref, S_ref, r_s, k_s, v_s, kk_s, b_s, lw_s, cum_s, y_s, q_s, y0_s, m_s, g_s) = refs
    else:
        (rkvl_ref, mu_ref, w0_ref, db_ref, a0_ref, ib_ref, gb_ref, kk_ref, ka_ref,
         rk_ref, lg_ref, lb_ref, out_ref, vfo_ref,
         prev_ref, S_ref, r_s, k_s, v_s, kk_s, b_s, lw_s, cum_s, y_s, q_s, y0_s, m_s, g_s) = refs

    @pl.when(pl.program_id(1) == 0)
    def _():
        prev_ref[...] = jnp.zeros_like(prev_ref)
        S_ref[...] = jnp.zeros_like(S_ref)

    row0 = lax.broadcasted_iota(I32, (TT, 1), 0) == 0
    prev = prev_ref[...]

    def shift(z, p):
        return jnp.where(row0, p, pltpu.roll(z, 1, 0))

    rkv = rkvl_ref[:, :3 * R]
    p1 = rkvl_ref[:, 3 * R:3 * R + LORA_PAD]
    p2 = rkvl_ref[:, 3 * R + LORA_PAD:]
    rkv_new = rkv + (shift(rkv, prev[:, :3 * R]) - rkv) * mu_ref[...]
    lora = p1 + shift(p2, prev[:, 3 * R:])
    prev_ref[...] = jnp.concatenate([rkv[TT - 1:TT, :], p2[TT - 1:TT, :]], axis=-1)

    r = rkv_new[:, :R]
    k = rkv_new[:, R:2 * R]
    v = rkv_new[:, 2 * R:]
    ones_h = _head_ones(G).astype(BF16)

    def head_sum(x):
        hi, lo = _split(x)
        return jnp.concatenate(
            [jnp.dot(hi[:, gi * G:(gi + 1) * G], ones_h, preferred_element_type=F32)
             + jnp.dot(lo[:, gi * G:(gi + 1) * G], ones_h, preferred_element_type=F32) for gi in range(NG)], axis=1)

    o_a = db_ref.shape[0]
    o_g = o_a + ib_ref.shape[0]
    o_v = o_g + gb_ref.shape[0]
    wlog = w0_ref[...] + _bdot(jnp.tanh(lora[:, :o_a]), db_ref[...])
    z = -wlog
    wlog = -(jnp.maximum(z, 0.0) + jnp.log(1.0 + jnp.exp(-jnp.abs(z)))) - 0.5
    lw = -jnp.exp(wlog)
    lw_s[...] = lw
    slab = min(TT, 4 * C)
    tr = lax.broadcasted_iota(I32, (slab, slab), 0)
    tc = lax.broadcasted_iota(I32, (slab, slab), 1)
    tri = jnp.where((tc <= tr) & (tr // C == tc // C), 1.0, 0.0).astype(BF16)
    parts = _split3(lw)
    for s0 in range(0, TT, slab):
        cum_s[s0:s0 + slab, :] = sum(jnp.dot(tri, part[s0:s0 + slab], preferred_element_type=F32) for part in parts)
    a = jax.nn.sigmoid(a0_ref[...] + _bdot(lora[:, o_a:o_g], ib_ref[...]))
    g = _bdot(jax.nn.sigmoid(lora[:, o_g:o_v]), gb_ref[...])
    kkr = k * kk_ref[...]
    kk = kkr * lax.rsqrt(jnp.maximum(head_sum(kkr * kkr), 1e-24))
    k = k * (1.0 + (a - 1.0) * ka_ref[...])
    if has_vres:
        v = v + (vf_ref[...] - v) * jax.nn.sigmoid(
            v0_ref[...] + _bdot(lora[:, o_v:o_v + vb_ref.shape[0]], vb_ref[...]))
    else:
        vfo_ref[...] = v
    r_s[...] = r
    k_s[...] = k
    v_s[...] = v
    kk_s[...] = kk
    b_s[...] = kk * a

    lane_g = lax.broadcasted_iota(I32, (C, RWKV_GROUP), 1)
    row_g = lax.broadcasted_iota(I32, (C, RWKV_GROUP), 0)
    blocks = [lane_g // HEAD_DIM == h for h in range(RWKV_GROUP // HEAD_DIM)]
    local = lane_g % HEAD_DIM
    strict = local < row_g
    incl = local <= row_g
    diag = local == row_g
    eye_cat = jnp.where(diag, 1.0, 0.0).astype(F32)
    levels = int(math.log2(C)) - 1

    def local_chunk(it, carry):
        chains = [(u, slice(gi * G, (gi + 1) * G)) for u in range(RWKV_UNROLL) for gi in range(NG)]
        each = lambda f: [f(i) for i in range(len(chains))]
        rd, kkd, kt, bt, kc, bc, vv, plast = ([] for _ in range(8))
        for u in range(RWKV_UNROLL):
            sl = pl.ds(pl.multiple_of((it * RWKV_UNROLL + u) * C, C), C)
            lw = lw_s[sl, :]
            cum = cum_s[sl, :]
            cl = cum[C - 1:C, :]
            pinv = jnp.exp(-cum)
            pc = jnp.exp(cl - cum)
            k_a = k_s[sl, :]
            b_a = b_s[sl, :]
            full = (r_s[sl, :] * jnp.exp(cum), kk_s[sl, :] * jnp.exp(cum - lw), k_a * pinv, b_a * pinv,
                    k_a * pc, b_a * pc, v_s[sl, :], jnp.exp(cl))
            for dst, x in zip((rd, kkd, kt, bt, kc, bc, vv, plast), full):
                dst.extend(x[:, gs] for (uu, gs) in chains if uu == u)
        kt_w = each(lambda i: _block_diag_t(kt[i], blocks).astype(BF16))
        bt_w = each(lambda i: _block_diag_t(bt[i], blocks).astype(BF16))
        bc_t = each(lambda i: _fold_blocks(_block_diag_t(bc[i], blocks)))
        kc_t = each(lambda i: _fold_blocks(_block_diag_t(kc[i], blocks)))
        rr = each(lambda i: _mm(jnp.concatenate([kkd[i], rd[i]], axis=0),
                                jnp.concatenate([kt_w[i], bt_w[i]], axis=1)))
        akk = each(lambda i: jnp.where(strict, rr[i][:C, :G], 0.0))
        ark = each(lambda i: jnp.where(incl, rr[i][C:, :G], 0.0))
        arb = each(lambda i: jnp.where(incl, rr[i][C:, G:], 0.0))
        npow = each(lambda i: jnp.where(strict, -rr[i][:C, G:], 0.0))
        x = each(lambda i: eye_cat + npow[i])
        npow = each(lambda i: _mm(npow[i], _block_diag(npow[i], blocks)))
        for _ in range(levels - 1):
            rr = each(lambda i: _mm(jnp.concatenate([npow[i], x[i]], axis=0), _block_diag(npow[i], blocks)))
            npow = each(lambda i: rr[i][:C])
            x = each(lambda i: x[i] + rr[i][C:])
        rr = each(lambda i: _mm(x[i], _block_diag(npow[i], blocks)))
        x = each(lambda i: x[i] + rr[i])
        rv = each(lambda i: _mm(jnp.concatenate([akk[i], ark[i], kc_t[i]], axis=0), _block_diag(vv[i], blocks)))
        ru = each(lambda i: _mm(x[i], jnp.concatenate([_block_diag(kkd[i], blocks),
                                                       _block_diag(rv[i][:C], blocks)], axis=1)))
        rr = each(lambda i: _mm(jnp.concatenate([arb[i], bc_t[i]], axis=0),
                                jnp.concatenate([_block_diag(ru[i][:, :G], blocks),
                                                 _block_diag(ru[i][:, G:], blocks)], axis=1)))
        for i, (u, gs) in enumerate(chains):
            c = it * RWKV_UNROLL + u
            q_s[c, :, gs] = rd[i] - rr[i][:C, :G]
            y0_s[c, :, gs] = rv[i][C:2 * C] - rr[i][:C, G:]
            dterm = jnp.where(diag, jnp.broadcast_to(plast[i], (C, G)), 0.0)
            m_s[c, :, gs] = dterm - rr[i][C:, :G]
            g_s[c, :, gs] = rv[i][2 * C:] - rr[i][C:, G:]
        return carry

    lax.fori_loop(0, NC // RWKV_UNROLL, local_chunk, 0)

    def scan_chunk(c, carry):
        sl = pl.ds(pl.multiple_of(c * C, C), C)
        for gi in range(NG):
            gs = slice(gi * RWKV_GROUP, (gi + 1) * RWKV_GROUP)
            rr = _mm(jnp.concatenate([q_s[c, :, gs], m_s[c, :, gs]], axis=0), _block_diag(S_ref[:, gs], blocks))
            y_s[sl, gs] = rr[:C] + y0_s[c, :, gs]
            S_ref[:, gs] = rr[C:] + g_s[c, :, gs]
        return carry

    lax.fori_loop(0, NC, scan_chunk, 0)

    y = y_s[...]
    inv_n = 1.0 / HEAD_DIM
    mean = head_sum(y) * inv_n
    yc = y - mean
    var = head_sum(yc * yc) * inv_n
    yn = yc * lax.rsqrt(var + LNX_EPS) * lg_ref[...] + lb_ref[...]
    bonus = head_sum(r_s[...] * k_s[...] * rk_ref[...]) * v_s[...]
    out_ref[...] = ((yn + bonus) * g).astype(out_ref.dtype)


def _rwkv(rkvl, v_first, p, B, T):
    N, nc = rkvl.shape
    R = RWKV_HEADS * HEAD_DIM
    TT = min(RWKV_TILE, T)
    nt = T // TT
    NC = TT // RWKV_CHUNK
    has_vres = v_first is not None
    row = lambda b, i: (b * nt + i, 0)
    const = lambda b, i: (0, 0)
    vec = lambda a: a.reshape(1, -1)
    ins = [rkvl]
    specs = [pl.BlockSpec((TT, nc), row)]
    if has_vres:
        ins.append(v_first)
        specs.append(pl.BlockSpec((TT, R), row))
    small = [vec(p["mu_rkv"]), vec(p["decay_w0"]), p["decay_b"].astype(BF16), vec(p["iclr_a0"]),
             p["iclr_b"].astype(BF16), p["gate_b"].astype(BF16), vec(p["k_k"]), vec(p["k_a"]),
             vec(p["r_k"]), vec(p["lnx_g"]), vec(p["lnx_b"])]
    if has_vres:
        small += [vec(p["vres_v0"]), p["vres_b"].astype(BF16)]
    ins += small
    specs += [pl.BlockSpec(a.shape, const) for a in small]
    out_shape = [jax.ShapeDtypeStruct((N, R), BF16)]
    out_specs = [pl.BlockSpec((TT, R), row)]
    if not has_vres:
        out_shape.append(jax.ShapeDtypeStruct((N, R), F32))
        out_specs.append(pl.BlockSpec((TT, R), row))
    scratch = [pltpu.VMEM((1, 3 * R + LORA_PAD), F32),
               pltpu.VMEM((HEAD_DIM, R), F32)]
    scratch += [pltpu.VMEM((TT, R), F32) for _ in range(8)]
    scratch += [pltpu.VMEM((NC, RWKV_CHUNK, R), F32) for _ in range(4)]
    res = pl.pallas_call(
        functools.partial(_rwkv_kernel, has_vres, TT),
        grid=(B, nt),
        in_specs=specs,
        out_specs=out_specs,
        out_shape=out_shape,
        scratch_shapes=scratch,
        compiler_params=_params("arbitrary", "arbitrary"),
        name="rwkv",
    )(*ins)
    if has_vres:
        return res[0], v_first
    return res[0], res[1]


def _bucket_boundaries():
    max_exact = N_BUCKETS // 2
    d = np.arange(0, 4 * MAX_DISTANCE, dtype=np.int64)
    nf = np.maximum(d, 1).astype(np.float32)
    large = max_exact + (np.log(nf / np.float32(max_exact)) / np.float32(math.log(MAX_DISTANCE / max_exact))
                         * np.float32(N_BUCKETS - max_exact)).astype(np.int32)
    large = np.minimum(large, N_BUCKETS - 1)
    bucket = np.where(d < max_exact, d, large)
    return [int(np.argmax(bucket >= j)) for j in range(max_exact + 1, N_BUCKETS)]


_BUCKET_STARTS = _bucket_boundaries()
KEY_NEG_INF = int(np.int32(np.array(-np.inf, np.float32).view(np.int32)) ^ np.int32(0x7FFFFFFF))
INT_MIN = -2 ** 31
INT_MAX = 2 ** 31 - 1
FAST_SELECT_BLIND = 16
FAST_SELECT_STEPS = 2
FAST_SELECT_TRIPS = 10


def _sort_key(s):
    bits = pltpu.bitcast(s, I32)
    bits = jnp.where(bits == INT_MIN, 0, bits)
    return bits ^ ((bits >> 31) & jnp.int32(0x7FFFFFFF))


def _key_to_float(k):
    return pltpu.bitcast(k ^ ((k >> 31) & jnp.int32(0x7FFFFFFF)), F32)


def _dsa_kernel(T, ksel, q_ref, k_ref, vt_ref, qiw_ref, kix_ref, relb_ref, g_ref, out_ref,
                lhs_s, qpad_s, key_s, madd_s, bias_s, m_s, acc_s, s_s, p_s, cmax_s, alpha_s):
    TQ = TK = DSA_BLOCK
    H = ATTN_HEADS
    qb = pl.program_id(1)
    nch = qb + 1
    rows = lax.broadcasted_iota(I32, (TK, TQ), 0)
    cols = lax.broadcasted_iota(I32, (TK, TQ), 1)

    @pl.when((pl.program_id(0) == 0) & (qb == 0))
    def _():
        max_exact = N_BUCKETS // 2
        for off in range(2):
            d = cols - rows + off * TK
            bucket = jnp.where(d < max_exact, jnp.maximum(d, 0), max_exact)
            for start in _BUCKET_STARTS:
                bucket = bucket + jnp.where(d >= start, 1, 0)
            for h in range(H):
                far = relb_ref[N_BUCKETS - 1, h]
                tile = jnp.zeros((TK, TQ), F32)
                for bk in range(N_BUCKETS - 1):
                    tile = jnp.where(bucket == bk, (relb_ref[bk, h] - far) * LOG2E, tile)
                bias_s[h, off] = tile

    qi_t = (qiw_ref[:, :IDX_HEADS * IDX_DIM] * (IDX_DIM ** -0.5)).T
    for h in range(IDX_HEADS):
        qh = qi_t[h * IDX_DIM:(h + 1) * IDX_DIM, :]
        hi = qh.astype(BF16)
        lo = (qh - hi.astype(F32)).astype(BF16)
        lhs_s[h] = jnp.concatenate([hi, hi, lo, jnp.zeros_like(hi)], axis=0)
    w_t = qiw_ref[:, K_IDX_OFF:IDX_COLS].T[IDX_DIM:IDX_DIM + IDX_HEADS, :] * (IDX_HEADS ** -0.5)

    q_t = q_ref[...].astype(F32).T
    zeros_h = jnp.zeros((HEAD_DIM, TQ), BF16)
    for h in range(H):
        qh = q_t[h * HEAD_DIM:(h + 1) * HEAD_DIM, :].astype(BF16)
        qpad_s[h] = jnp.concatenate([qh, zeros_h] if h % 2 == 0 else [zeros_h, qh], axis=0)

    def scores(j):
        kx = kix_ref[pl.ds(pl.multiple_of(j * TK, TK), TK), :]
        acc = jnp.zeros((TK, TQ), F32)
        for h in range(IDX_HEADS):
            d = jnp.dot(kx, lhs_s[h], preferred_element_type=F32)
            acc = acc + w_t[h:h + 1, :] * jnp.maximum(d, 0.0)
        return acc

    def stats(st, kk, mn_of):
        fold = lambda z: z.reshape(TK // 8, 8, TQ)
        ge0, gt0, mn, mx = st
        return (ge0 + jnp.sum(fold(jnp.where(kk >= 0, 1, 0)), axis=0),
                gt0 + jnp.sum(fold(jnp.where(kk > 0, 1, 0)), axis=0),
                jnp.minimum(mn, jnp.min(fold(mn_of(kk)), axis=0)),
                jnp.maximum(mx, jnp.max(fold(kk), axis=0)))

    def score_body(j, st):
        kk = _sort_key(scores(j))
        key_s[j] = kk
        return stats(st, kk, lambda z: z)

    st = lax.fori_loop(0, qb, score_body, (jnp.zeros((8, TQ), I32), jnp.zeros((8, TQ), I32),
                                           jnp.full((8, TQ), INT_MAX, I32), jnp.full((8, TQ), INT_MIN, I32)))
    kd = _sort_key(jnp.where(rows <= cols, scores(qb), -jnp.inf))
    key_s[qb] = kd
    st = stats(st, kd, lambda z: jnp.where(z > KEY_NEG_INF, z, INT_MAX))
    n_ge0, n_gt0 = (jnp.sum(z, axis=0, keepdims=True) for z in st[:2])
    lo0 = _key_to_float(jnp.min(st[2], axis=0, keepdims=True))
    hi0 = _key_to_float(jnp.max(st[3], axis=0, keepdims=True) + 1)

    def count(pred):
        def body(j, acc):
            hit = jnp.where(pred(key_s[j]), 1, 0)
            return acc + jnp.sum(hit.reshape(TK // 8, 8, TQ), axis=0)
        acc = lax.fori_loop(0, nch, body, jnp.zeros((8, TQ), I32))
        return jnp.sum(acc, axis=0, keepdims=True)

    def write_mask(sel_of):
        def body(j, c):
            madd_s[j] = jnp.where(sel_of(key_s[j]), 0.0, NEG_BIG)
            return c
        lax.fori_loop(0, nch, body, 0)

    kf = float(ksel)
    n_adm = qb * TQ + lax.broadcasted_iota(I32, (1, TQ), 1) + 1
    trivial = n_adm <= ksel
    positive = n_gt0 > ksel
    done0 = trivial | ((n_gt0 <= ksel) & (n_ge0 >= ksel))
    thr0 = jnp.where(trivial, KEY_NEG_INF + 1, jnp.where(n_gt0 == ksel, 1, 0))
    need0 = jnp.where(n_gt0 < ksel, ksel - n_gt0, ksel)
    a0 = jnp.where(positive, 0.0, lo0)
    b0 = jnp.where(positive, hi0, 0.0)
    fa0 = jnp.where(positive, n_gt0, n_adm).astype(F32) - kf
    cb0 = jnp.where(positive, 0, n_ge0)
    as_int = lambda m: jnp.where(m, 1, 0)

    def any_lane(m):
        return jnp.max(as_int(m)) > 0

    def fast_cond(st):
        it, done, stuck = st[0], st[1], st[2]
        return (it < FAST_SELECT_TRIPS) & any_lane((done + stuck) == 0)

    def fast_body(st):
        it, st = st[0], st[1:]
        for _ in range(FAST_SELECT_STEPS):
            st = fast_step(st)
        return (it + 1,) + st

    def fast_step(st):
        done, stuck, thr, need, a, b, fa, fb, cb, last = st
        c = a + (b - a) * (fa / (fa - fb))
        c = jnp.where((c > a) & (c < b), c, 0.5 * a + 0.5 * b)
        inside = (c > a) & (c < b)
        ckey = _sort_key(c)
        cnt = count(lambda kk: kk >= ckey)
        fc = cnt.astype(F32) - kf
        hit = inside & (cnt == ksel) & (done == 0)
        thr = jnp.where(hit, ckey, thr)
        need = jnp.where(hit, ksel, need)
        done = jnp.where(hit, 1, done)
        stuck = jnp.where(inside, stuck, 1)
        up = inside & (cnt > ksel)
        dn = inside & (cnt < ksel)
        fb = jnp.where(up & (last == 1), 0.5 * fb, fb)
        fa = jnp.where(dn & (last == 0), 0.5 * fa, fa)
        a, fa = jnp.where(up, c, a), jnp.where(up, fc, fa)
        b, fb, cb = jnp.where(dn, c, b), jnp.where(dn, fc, fb), jnp.where(dn, cnt, cb)
        return done, stuck, thr, need, a, b, fa, fb, cb, jnp.where(up, 1, jnp.where(dn, 0, last))

    st = (as_int(done0), jnp.zeros((1, TQ), I32), thr0, need0,
          a0, b0, fa0, cb0.astype(F32) - kf, cb0, jnp.full((1, TQ), -1, I32))
    blind = jnp.where((qb + 1) * TQ <= ksel, 0, FAST_SELECT_BLIND)
    st = lax.fori_loop(0, blind, lambda i, s: fast_step(s), st)
    st = lax.while_loop(fast_cond, fast_body, (jnp.int32(0),) + st)
    done, thr, need, a, b, cb = st[1], st[3], st[4], st[5], st[6], st[9]
    open_lane = done == 0

    def close_cond(st):
        return any_lane((st[1] - st[0]) > 1)

    def close_body(st):
        ak, bk, cb = st
        mid = ak + ((bk - ak) >> 1)
        cnt = count(lambda kk: kk >= mid)
        ge = cnt >= ksel
        return jnp.where(ge, mid, ak), jnp.where(ge, bk, mid), jnp.where(ge, cb, cnt)

    ak, _, cb = lax.while_loop(close_cond, close_body, (
        jnp.where(open_lane, _sort_key(a), thr), jnp.where(open_lane, _sort_key(b), thr), cb))
    thr = jnp.where(open_lane, ak, thr)
    need = jnp.where(open_lane, ksel - cb, need)
    any_tie = any_lane(open_lane | (jnp.logical_not(trivial) & (n_gt0 < ksel) & (n_ge0 > ksel)))

    @pl.when(jnp.logical_not(any_tie))
    def _():
        write_mask(lambda kk: kk >= thr)

    @pl.when(any_tie)
    def _():
        kr = lax.broadcasted_iota(I32, (TK, TK), 0)
        kc = lax.broadcasted_iota(I32, (TK, TK), 1)
        lower = jnp.where(kc <= kr, 1.0, 0.0).astype(BF16)
        need_f = need.astype(F32)

        def body(j, run):
            kk = key_s[j]
            eq = kk == thr
            eqf = jnp.where(eq, 1.0, 0.0)
            rank = run + jnp.dot(lower, eqf.astype(BF16), preferred_element_type=F32)
            tied_in = jnp.where(eq, rank, float(TOPK_MAX + 1)) <= need_f
            madd_s[j] = jnp.where(kk > thr, 0.0, jnp.where(tied_in, 0.0, NEG_BIG))
            return run + jnp.sum(eqf, axis=0, keepdims=True)
        lax.fori_loop(0, nch, body, jnp.zeros((1, TQ), F32))

    m_s[...] = jnp.full(m_s.shape, NEG_BIG, F32)
    acc_s[...] = jnp.zeros(acc_s.shape, F32)

    def attend(j, bias_of):
        ks = pl.ds(pl.multiple_of(j * TK, TK), TK)
        ma = madd_s[j]
        for h in range(H):
            kp = k_ref[ks, LANES * (h // 2):LANES * (h // 2 + 1)]
            s = jnp.dot(kp, qpad_s[h], preferred_element_type=F32) + ma
            bias = bias_of(h)
            if bias is not None:
                s = s + bias
            s_s[h] = s
            cmax_s[h] = jnp.max(s.reshape(TK // 8, 8, TQ), axis=0)
        for h in range(H):
            m_old = m_s[h]
            m_new = jnp.maximum(m_old, jnp.max(cmax_s[h], axis=0, keepdims=True))
            alpha_s[h] = jnp.exp2(m_old - m_new)
            m_s[h] = m_new
            p_s[h] = jnp.exp2(s_s[h] - m_new).astype(BF16)
        for h in range(H):
            hs = slice(h * VT_ROWS, (h + 1) * VT_ROWS)
            pv = jnp.dot(vt_ref[j, hs, :], p_s[h], preferred_element_type=F32)
            acc_s[hs, :] = alpha_s[h] * acc_s[hs, :] + pv

    def far_body(j, c):
        attend(j, lambda h: None)
        return c

    lax.fori_loop(0, jnp.maximum(qb - 1, 0), far_body, 0)

    @pl.when(qb >= 1)
    def _():
        attend(qb - 1, lambda h: bias_s[h, 1])

    attend(qb, lambda h: bias_s[h, 0])

    outs = []
    for h in range(H):
        o = acc_s[h * VT_ROWS:h * VT_ROWS + HEAD_DIM, :] / acc_s[h * VT_ROWS + HEAD_DIM:h * VT_ROWS + HEAD_DIM + 1, :]
        ms = jnp.mean(o * o, axis=0, keepdims=True)
        outs.append(o * lax.rsqrt(ms + RMS_EPS))
    out_ref[...] = (jnp.concatenate(outs, axis=0).T * g_ref[...]).astype(out_ref.dtype)


def _dsa(qk, vt, qiw, kix, rel_bias, g, B, T):
    N = qk.shape[0]
    TQ = DSA_BLOCK
    nq = T // TQ
    A = ATTN_HEADS * HEAD_DIM
    ksel = min(TOPK_MAX, T // 4)
    assert MAX_DISTANCE <= TQ + 1
    return pl.pallas_call(
        functools.partial(_dsa_kernel, T, ksel),
        grid=(B, nq),
        in_specs=[pl.BlockSpec((TQ, A), lambda b, i: (b * nq + i, 0)),
                  pl.BlockSpec((T, A), lambda b, i: (b, 1)),
                  pl.BlockSpec((nq, ATTN_HEADS * VT_ROWS, TQ), lambda b, i: (b, 0, 0)),
                  pl.BlockSpec((TQ, IDX_COLS), lambda b, i: (b * nq + i, 0)),
                  pl.BlockSpec((T, KIX_COLS), lambda b, i: (b, 0)),
                  pl.BlockSpec(memory_space=pltpu.SMEM),
                  pl.BlockSpec((1, A), lambda b, i: (0, 0))],
        out_specs=pl.BlockSpec((TQ, A), lambda b, i: (b * nq + i, 0)),
        out_shape=jax.ShapeDtypeStruct((N, A), BF16),
        scratch_shapes=[pltpu.VMEM((IDX_HEADS, KIX_COLS, TQ), BF16),
                        pltpu.VMEM((ATTN_HEADS, LANES, TQ), BF16),
                        pltpu.VMEM((nq, TQ, TQ), I32),
                        pltpu.VMEM((nq, TQ, TQ), F32),
                        pltpu.VMEM((ATTN_HEADS, 2, TQ, TQ), F32),
                        pltpu.VMEM((ATTN_HEADS, 1, TQ), F32),
                        pltpu.VMEM((ATTN_HEADS * VT_ROWS, TQ), F32),
                        pltpu.VMEM((ATTN_HEADS, TQ, TQ), F32),
                        pltpu.VMEM((ATTN_HEADS, TQ, TQ), BF16),
                        pltpu.VMEM((ATTN_HEADS, 8, TQ), F32),
                        pltpu.VMEM((ATTN_HEADS, 1, TQ), F32)],
        compiler_params=_params("arbitrary", "arbitrary"),
        name="dsa",
    )(qk, qk, vt, qiw, kix, rel_bias, g.reshape(1, A))


def _mix_mlp_kernel(final, x_ref, rw_ref, att_ref, wo_ref, gt1_ref, g_ref, sc_ref, sh_ref, gt2_ref,
                    w1_ref, w2_ref, fg_ref, o_ref, x_s, h_s, acc_s):
    j = pl.program_id(1)

    @pl.when(j == 0)
    def _():
        R = rw_ref.shape[1]
        mixed = (jnp.dot(rw_ref[...], wo_ref[:R, :], preferred_element_type=F32)
                 + jnp.dot(att_ref[...], wo_ref[R:, :], preferred_element_type=F32))
        x1 = x_ref[...] + gt1_ref[...] * mixed
        x_s[...] = x1
        h_s[...] = _norm_mod(x1, g_ref[...], sc_ref[...], sh_ref[...]).astype(BF16)
        acc_s[...] = jnp.zeros_like(acc_s)

    u = jnp.dot(h_s[...], w1_ref[...], preferred_element_type=F32)
    u = jnp.square(jnp.maximum(u, 0.0))
    acc_s[...] += jnp.dot(u.astype(BF16), w2_ref[...], preferred_element_type=F32)

    @pl.when(j == pl.num_programs(1) - 1)
    def _():
        y = x_s[...] + gt2_ref[...] * acc_s[...]
        if final:
            ms = jnp.mean(y * y, axis=-1, keepdims=True)
            y = y * lax.rsqrt(ms + RMS_EPS) * fg_ref[...]
        o_ref[...] = y


def _mix_mlp(x2d, rw, att, wo, gt1, g, sc, sh, gt2, w1, w2, final_g, final, T, fc=512):
    N, D = x2d.shape
    F = w1.shape[1]
    tm = min(MLP_TILE, T)
    nb = T // tm
    row = lambda i, j: (i, 0)
    per_b = lambda i, j: (i // nb, 0, 0)
    const = lambda i, j: (0, 0)
    return pl.pallas_call(
        functools.partial(_mix_mlp_kernel, final),
        grid=(N // tm, F // fc),
        in_specs=[pl.BlockSpec((tm, D), row),
                  pl.BlockSpec((tm, rw.shape[1]), row),
                  pl.BlockSpec((tm, att.shape[1]), row),
                  pl.BlockSpec(wo.shape, const, pipeline_mode=pl.Buffered(1)),
                  pl.BlockSpec((None, 1, D), per_b),
                  pl.BlockSpec((1, D), const),
                  pl.BlockSpec((None, 1, D), per_b),
                  pl.BlockSpec((None, 1, D), per_b),
                  pl.BlockSpec((None, 1, D), per_b),
                  pl.BlockSpec((D, fc), lambda i, j: (0, j)),
                  pl.BlockSpec((fc, D), lambda i, j: (j, 0)),
                  pl.BlockSpec((1, D), const)],
        out_specs=pl.BlockSpec((tm, D), row),
        out_shape=jax.ShapeDtypeStruct((N, D), F32),
        scratch_shapes=[pltpu.VMEM((tm, D), F32), pltpu.VMEM((tm, D), BF16), pltpu.VMEM((tm, D), F32)],
        compiler_params=_params("arbitrary", "arbitrary"),
        name="mix_mlp",
    )(x2d, rw, att, wo, gt1, g.reshape(1, D), sc, sh, gt2, w1, w2, final_g.reshape(1, D))


def _pad_cols(w, n):
    return jnp.pad(w, ((0, 0), (0, n - w.shape[1])))


def _split_bf16(w):
    hi = w.astype(BF16)
    return hi, (w - hi.astype(F32)).astype(BF16)


def _in_weights(l, w_in, mu_lora, decay_a, iclr_a, gate_a, vres_mu, vres_a):
    D = w_in.shape[1]
    R = RWKV_HEADS * HEAD_DIM
    w = w_in[l]
    mats = [(decay_a[l], mu_lora[l, 0]), (iclr_a[l], mu_lora[l, 1]), (gate_a[l], mu_lora[l, 2])]
    if l > 0:
        mats.append((vres_a[l - 1], vres_mu[l - 1]))
    now = _pad_cols(jnp.concatenate([a * (1.0 - mu)[:, None] for a, mu in mats], axis=1), LORA_PAD)
    prev = _pad_cols(jnp.concatenate([a * mu[:, None] for a, mu in mats], axis=1), LORA_PAD)
    wq = w[:, 3 * R:4 * R] * (HEAD_DIM ** -0.5 * LOG2E)
    wa = jnp.concatenate([w[:, :3 * R], now, prev, wq, w[:, 4 * R:5 * R]], axis=1).astype(BF16)
    wv = w[:, 5 * R:6 * R].reshape(D, ATTN_HEADS, HEAD_DIM)
    wv = jnp.pad(wv, ((0, 0), (0, 0), (0, VT_ROWS - HEAD_DIM))).reshape(D, ATTN_HEADS * VT_ROWS)
    wvt = wv.T.astype(BF16)
    wbh, wbl = _split_bf16(_pad_cols(w[:, 6 * R:], IDX_COLS))
    return wa, wvt, wbh, wbl


def kernel(x, c, w_ada, b_ada, norm1_g, norm2_g, w_in, mu_rkv, mu_lora, decay_w0, decay_a, decay_b, iclr_a0, iclr_a, iclr_b, gate_a, gate_b, k_k, k_a, r_k, lnx_g, lnx_b, vres_mu, vres_v0, vres_a, vres_b, attn_out_g, rel_bias, w_out, w_mlp1, w_mlp2, final_g):
    B, T, D = x.shape
    depth = w_in.shape[0]
    mod = _adaln(c, w_ada, b_ada)
    mod = mod.reshape(depth, B, 6, 1, D).transpose(0, 2, 1, 3, 4)
    x2d = x.reshape(B * T, D)
    v_first = None
    for l in range(depth):
        sh1, sc1, gt1, sh2, sc2, gt2 = (mod[l, i] for i in range(6))
        wa, wvt, wbh, wbl = _in_weights(l, w_in, mu_lora, decay_a, iclr_a, gate_a, vres_mu, vres_a)
        rkvl, qk, vt, qiw, kix = _inproj(x2d, norm1_g[l], sc1, sh1, wa, wvt, wbh, wbl, T)
        p = dict(mu_rkv=mu_rkv[l], decay_w0=decay_w0[l], decay_b=decay_b[l], iclr_a0=iclr_a0[l],
                 iclr_b=iclr_b[l], gate_b=gate_b[l], k_k=k_k[l], k_a=k_a[l], r_k=r_k[l],
                 lnx_g=lnx_g[l], lnx_b=lnx_b[l])
        if l > 0:
            p.update(vres_v0=vres_v0[l - 1], vres_b=vres_b[l - 1])
        rw, v_first = _rwkv(rkvl, v_first, p, B, T)
        att = _dsa(qk, vt, qiw, kix, rel_bias, attn_out_g[l], B, T)
        x2d = _mix_mlp(x2d, rw, att, w_out[l].astype(BF16), gt1, norm2_g[l], sc2, sh2, gt2,
                       w_mlp1[l].astype(BF16), w_mlp2[l].astype(BF16), final_g, l == depth - 1, T)
    return x2d.reshape(B, T, D)
```

```python
import functools
import math

import numpy as np
import jax
import jax.numpy as jnp
from jax import lax
from jax.experimental import pallas as pl
from jax.experimental.pallas import tpu as pltpu

F32 = jnp.float32
BF16 = jnp.bfloat16
I32 = jnp.int32

HEAD_DIM = 64
RWKV_HEADS = 8
ATTN_HEADS = 8
IDX_HEADS = 8
IDX_DIM = 64
TOPK_MAX = 256
N_BUCKETS = 32
MAX_DISTANCE = 128
RMS_EPS = 1e-6
LNX_EPS = 64e-5

LANES = 128
VMEM_LIMIT = 56 * 1024 * 1024
RWKV_CHUNK = 64
RWKV_GROUP = 256
RWKV_TILE = 512
RWKV_UNROLL = 8
DSA_BLOCK = 256
VT_ROWS = 80
LOG2E = math.log2(math.e)
INPROJ_TILE = 512
MLP_TILE = 1024
MLP_FF_CHUNK = 1024
LORA_PAD = 384
K_IDX_OFF = IDX_HEADS * IDX_DIM
W_IDX_OFF = K_IDX_OFF + IDX_DIM
IDX_COLS = 640
KIX_COLS = 4 * IDX_DIM
NEG_BIG = -1e30
HI = lax.Precision.HIGHEST
NT = (((1,), (1,)), ((), ()))


def _bdot(a, b):
    return jnp.dot(a.astype(BF16), b.astype(BF16), preferred_element_type=F32)


def _hdot(a, b):
    return jnp.dot(a, b, precision=HI, preferred_element_type=F32)


def _params(*sem):
    return pltpu.CompilerParams(dimension_semantics=sem, vmem_limit_bytes=VMEM_LIMIT)


def _adaln_kernel(c_ref, w_ref, b_ref, o_ref):
    c = c_ref[...]
    c_act = c * jax.nn.sigmoid(c)
    o_ref[...] = _hdot(c_act, w_ref[...]) + b_ref[...]


def _adaln(c, w_ada, b_ada):
    L, D, D6 = w_ada.shape
    B = c.shape[0]
    cb = 1024
    return pl.pallas_call(
        _adaln_kernel,
        grid=(L, D6 // cb),
        in_specs=[pl.BlockSpec((B, D), lambda l, j: (0, 0)),
                  pl.BlockSpec((None, D, cb), lambda l, j: (l, 0, j)),
                  pl.BlockSpec((None, 1, cb), lambda l, j: (l, 0, j))],
        out_specs=pl.BlockSpec((None, B, cb), lambda l, j: (l, 0, j)),
        out_shape=jax.ShapeDtypeStruct((L, B, D6), F32),
        compiler_params=_params("arbitrary", "arbitrary"),
        name="adaln",
    )(c, w_ada, b_ada.reshape(L, 1, D6))


def _norm_mod(x, g, sc, sh):
    ms = jnp.mean(x * x, axis=-1, keepdims=True)
    return (x * lax.rsqrt(ms + RMS_EPS) * g) * (1.0 + sc) + sh


def _inproj_kernel(x_ref, g_ref, sc_ref, sh_ref, wa_ref, wvt_ref, wbh_ref, wbl_ref,
                   rkvl_ref, qk_ref, vt_ref, qiw_ref, kix_ref):
    h = _norm_mod(x_ref[...], g_ref[...], sc_ref[...], sh_ref[...])
    hb = h.astype(BF16)
    hl = (h - hb.astype(F32)).astype(BF16)
    pa = jnp.dot(hb, wa_ref[...], preferred_element_type=F32)
    nr = rkvl_ref.shape[1]
    rkvl_ref[...] = pa[:, :nr]
    qk_ref[...] = pa[:, nr:].astype(BF16)
    vt = lax.dot_general(wvt_ref[...], hb, NT, preferred_element_type=F32)
    ones_row = lax.broadcasted_iota(I32, vt.shape, 0) % VT_ROWS == HEAD_DIM
    vt = jnp.where(ones_row, 1.0, vt).astype(BF16)
    for j in range(vt_ref.shape[0]):
        vt_ref[j] = vt[:, j * DSA_BLOCK:(j + 1) * DSA_BLOCK]
    pb = (jnp.dot(hb, wbh_ref[...], preferred_element_type=F32)
          + jnp.dot(hb, wbl_ref[...], preferred_element_type=F32)
          + jnp.dot(hl, wbh_ref[...], preferred_element_type=F32))
    qiw_ref[...] = pb
    ki = pb[:, K_IDX_OFF:W_IDX_OFF]
    kh = ki.astype(BF16)
    kl = (ki - kh.astype(F32)).astype(BF16)
    kix_ref[...] = jnp.concatenate([kh, kl, kh, jnp.zeros_like(kh)], axis=-1)


def _inproj(x2d, g, sc, sh, wa, wvt, wbh, wbl, T):
    N, D = x2d.shape
    nblk = INPROJ_TILE // DSA_BLOCK
    tm = INPROJ_TILE
    na = wa.shape[1]
    A = wvt.shape[0]
    nr = 3 * RWKV_HEADS * HEAD_DIM + 2 * LORA_PAD
    assert T % tm == 0 and (wa.shape[1] - nr) % LANES == 0
    nb = T // tm
    row = lambda i: (i, 0)
    per_b = lambda i: (i // nb, 0, 0)
    const = lambda i: (0, 0)
    once = pl.Buffered(1)
    return pl.pallas_call(
        _inproj_kernel,
        grid=(N // tm,),
        in_specs=[pl.BlockSpec((tm, D), row),
                  pl.BlockSpec((1, D), const),
                  pl.BlockSpec((None, 1, D), per_b),
                  pl.BlockSpec((None, 1, D), per_b),
                  pl.BlockSpec((D, na), const, pipeline_mode=once),
                  pl.BlockSpec((A, D), const, pipeline_mode=once),
                  pl.BlockSpec((D, IDX_COLS), const, pipeline_mode=once),
                  pl.BlockSpec((D, IDX_COLS), const, pipeline_mode=once)],
        out_specs=[pl.BlockSpec((tm, nr), row),
                   pl.BlockSpec((tm, na - nr), row),
                   pl.BlockSpec((nblk, A, DSA_BLOCK), lambda i: (i, 0, 0)),
                   pl.BlockSpec((tm, IDX_COLS), row),
                   pl.BlockSpec((tm, KIX_COLS), row)],
        out_shape=[jax.ShapeDtypeStruct((N, nr), F32),
                   jax.ShapeDtypeStruct((N, na - nr), BF16),
                   jax.ShapeDtypeStruct((N // DSA_BLOCK, A, DSA_BLOCK), BF16),
                   jax.ShapeDtypeStruct((N, IDX_COLS), F32),
                   jax.ShapeDtypeStruct((N, KIX_COLS), BF16)],
        compiler_params=_params("arbitrary"),
        name="inproj",
    )(x2d, g.reshape(1, D), sc, sh, wa, wvt, wbh, wbl)


def _head_ones(n):
    r = lax.broadcasted_iota(I32, (n, n), 0) // HEAD_DIM
    c = lax.broadcasted_iota(I32, (n, n), 1) // HEAD_DIM
    return jnp.where(r == c, 1.0, 0.0).astype(F32)


def _split(x):
    hi = x.astype(BF16)
    return hi, (x - hi.astype(F32)).astype(BF16)


def _split3(x):
    hi = x.astype(BF16)
    r = x - hi.astype(F32)
    mid = r.astype(BF16)
    return hi, mid, (r - mid.astype(F32)).astype(BF16)


def _block_diag(w, blocks):
    w = w.astype(BF16)
    return jnp.concatenate([jnp.where(m, w, jnp.zeros_like(w)) for m in blocks], axis=0)


def _block_diag_t(w, blocks):
    return jnp.concatenate([jnp.where(m, w, 0.0) for m in blocks], axis=0).T


def _fold_blocks(w):
    n = w.shape[1] // HEAD_DIM
    out = w[:HEAD_DIM]
    for h in range(1, n):
        out = out + w[h * HEAD_DIM:(h + 1) * HEAD_DIM]
    return out


def _mm(a, w):
    return jnp.dot(a.astype(BF16), w, preferred_element_type=F32)


def _rwkv_kernel(has_vres, TT, *refs):
    C = RWKV_CHUNK
    R = RWKV_HEADS * HEAD_DIM
    G = RWKV_GROUP
    NG = R // G
    NC = TT // C
    if has_vres:
        (rkvl_ref, vf_ref, mu_ref, w0_ref, db_ref, a0_ref, ib_ref, gb_ref, kk_ref, ka_ref,
         rk_ref, lg_ref, lb_ref, v0_ref, vb_ref, out_ref,
         prev_ref, S_ref, r_s, k_s, v_s, kk_s, b_s, lw_s, cum_s, y_s, q_s, y0_s, m_s, g_s) = refs
    else:
        (rkvl_ref, mu_ref, w0_ref, db_ref, a0_ref, ib_ref, gb_ref, kk_ref, ka_ref,
         rk_ref, lg_ref, lb_ref, out_ref, vfo_ref,
         prev_ref, S_ref, r_s, k_s, v_s, kk_s, b_s, lw_s, cum_s, y_s, q_s, y0_s, m_s, g_s) = refs

    @pl.when(pl.program_id(1) == 0)
    def _():
        prev_ref[...] = jnp.zeros_like(prev_ref)
        S_ref[...] = jnp.zeros_like(S_ref)

    row0 = lax.broadcasted_iota(I32, (TT, 1), 0) == 0
    prev = prev_ref[...]

    def shift(z, p):
        return jnp.where(row0, p, pltpu.roll(z, 1, 0))

    rkv = rkvl_ref[:, :3 * R]
    p1 = rkvl_ref[:, 3 * R:3 * R + LORA_PAD]
    p2 = rkvl_ref[:, 3 * R + LORA_PAD:]
    rkv_new = rkv + (shift(rkv, prev[:, :3 * R]) - rkv) * mu_ref[...]
    lora = p1 + shift(p2, prev[:, 3 * R:])
    prev_ref[...] = jnp.concatenate([rkv[TT - 1:TT, :], p2[TT - 1:TT, :]], axis=-1)

    r = rkv_new[:, :R]
    k = rkv_new[:, R:2 * R]
    v = rkv_new[:, 2 * R:]
    ones_h = _head_ones(G).astype(BF16)

    def head_sum(x):
        hi, lo = _split(x)
        return jnp.concatenate(
            [jnp.dot(hi[:, gi * G:(gi + 1) * G], ones_h, preferred_element_type=F32)
             + jnp.dot(lo[:, gi * G:(gi + 1) * G], ones_h, preferred_element_type=F32) for gi in range(NG)], axis=1)

    o_a = db_ref.shape[0]
    o_g = o_a + ib_ref.shape[0]
    o_v = o_g + gb_ref.shape[0]
    wlog = w0_ref[...] + _bdot(jnp.tanh(lora[:, :o_a]), db_ref[...])
    z = -wlog
    wlog = -(jnp.maximum(z, 0.0) + jnp.log(1.0 + jnp.exp(-jnp.abs(z)))) - 0.5
    lw = -jnp.exp(wlog)
    lw_s[...] = lw
    slab = min(TT, 4 * C)
    tr = lax.broadcasted_iota(I32, (slab, slab), 0)
    tc = lax.broadcasted_iota(I32, (slab, slab), 1)
    tri = jnp.where((tc <= tr) & (tr // C == tc // C), 1.0, 0.0).astype(BF16)
    parts = _split3(lw)
    for s0 in range(0, TT, slab):
        cum_s[s0:s0 + slab, :] = sum(jnp.dot(tri, part[s0:s0 + slab], preferred_element_type=F32) for part in parts)
    a = jax.nn.sigmoid(a0_ref[...] + _bdot(lora[:, o_a:o_g], ib_ref[...]))
    g = _bdot(jax.nn.sigmoid(lora[:, o_g:o_v]), gb_ref[...])
    kkr = k * kk_ref[...]
    kk = kkr * lax.rsqrt(jnp.maximum(head_sum(kkr * kkr), 1e-24))
    k = k * (1.0 + (a - 1.0) * ka_ref[...])
    if has_vres:
        v = v + (vf_ref[...] - v) * jax.nn.sigmoid(
            v0_ref[...] + _bdot(lora[:, o_v:o_v + vb_ref.shape[0]], vb_ref[...]))
    else:
        vfo_ref[...] = v
    r_s[...] = r
    k_s[...] = k
    v_s[...] = v
    kk_s[...] = kk
    b_s[...] = kk * a

    lane_g = lax.broadcasted_iota(I32, (C, RWKV_GROUP), 1)
    row_g = lax.broadcasted_iota(I32, (C, RWKV_GROUP), 0)
    blocks = [lane_g // HEAD_DIM == h for h in range(RWKV_GROUP // HEAD_DIM)]
    local = lane_g % HEAD_DIM
    strict = local < row_g
    incl = local <= row_g
    diag = local == row_g
    eye_cat = jnp.where(diag, 1.0, 0.0).astype(F32)
    levels = int(math.log2(C)) - 1

    def local_chunk(it, carry):
        chains = [(u, slice(gi * G, (gi + 1) * G)) for u in range(RWKV_UNROLL) for gi in range(NG)]
        each = lambda f: [f(i) for i in range(len(chains))]
        rd, kkd, kt, bt, kc, bc, vv, plast = ([] for _ in range(8))
        for u in range(RWKV_UNROLL):
            sl = pl.ds(pl.multiple_of((it * RWKV_UNROLL + u) * C, C), C)
            lw = lw_s[sl, :]
            cum = cum_s[sl, :]
            cl = cum[C - 1:C, :]
            pinv = jnp.exp(-cum)
            pc = jnp.exp(cl - cum)
            k_a = k_s[sl, :]
            b_a = b_s[sl, :]
            full = (r_s[sl, :] * jnp.exp(cum), kk_s[sl, :] * jnp.exp(cum - lw), k_a * pinv, b_a * pinv,
                    k_a * pc, b_a * pc, v_s[sl, :], jnp.exp(cl))
            for dst, x in zip((rd, kkd, kt, bt, kc, bc, vv, plast), full):
                dst.extend(x[:, gs] for (uu, gs) in chains if uu == u)
        kt_w = each(lambda i: _block_diag_t(kt[i], blocks).astype(BF16))
        bt_w = each(lambda i: _block_diag_t(bt[i], blocks).astype(BF16))
        bc_t = each(lambda i: _fold_blocks(_block_diag_t(bc[i], blocks)))
        kc_t = each(lambda i: _fold_blocks(_block_diag_t(kc[i], blocks)))
        rr = each(lambda i: _mm(jnp.concatenate([kkd[i], rd[i]], axis=0),
                                jnp.concatenate([kt_w[i], bt_w[i]], axis=1)))
        akk = each(lambda i: jnp.where(strict, rr[i][:C, :G], 0.0))
        ark = each(lambda i: jnp.where(incl, rr[i][C:, :G], 0.0))
        arb = each(lambda i: jnp.where(incl, rr[i][C:, G:], 0.0))
        npow = each(lambda i: jnp.where(strict, -rr[i][:C, G:], 0.0))
        x = each(lambda i: eye_cat + npow[i])
        npow = each(lambda i: _mm(npow[i], _block_diag(npow[i], blocks)))
        for _ in range(levels - 1):
            rr = each(lambda i: _mm(jnp.concatenate([npow[i], x[i]], axis=0), _block_diag(npow[i], blocks)))
            npow = each(lambda i: rr[i][:C])
            x = each(lambda i: x[i] + rr[i][C:])
        rr = each(lambda i: _mm(x[i], _block_diag(npow[i], blocks)))
        x = each(lambda i: x[i] + rr[i])
        rv = each(lambda i: _mm(jnp.concatenate([akk[i], ark[i], kc_t[i]], axis=0), _block_diag(vv[i], blocks)))
        ru = each(lambda i: _mm(x[i], jnp.concatenate([_block_diag(kkd[i], blocks),
                                                       _block_diag(rv[i][:C], blocks)], axis=1)))
        rr = each(lambda i: _mm(jnp.concatenate([arb[i], bc_t[i]], axis=0),
                                jnp.concatenate([_block_diag(ru[i][:, :G], blocks),
                                                 _block_diag(ru[i][:, G:], blocks)], axis=1)))
        for i, (u, gs) in enumerate(chains):
            c = it * RWKV_UNROLL + u
            q_s[c, :, gs] = rd[i] - rr[i][:C, :G]
            y0_s[c, :, gs] = rv[i][C:2 * C] - rr[i][:C, G:]
            dterm = jnp.where(diag, jnp.broadcast_to(plast[i], (C, G)), 0.0)
            m_s[c, :, gs] = dterm - rr[i][C:, :G]
            g_s[c, :, gs] = rv[i][2 * C:] - rr[i][C:, G:]
        return carry

    lax.fori_loop(0, NC // RWKV_UNROLL, local_chunk, 0)

    def scan_chunk(c, carry):
        sl = pl.ds(pl.multiple_of(c * C, C), C)
        for gi in range(NG):
            gs = slice(gi * RWKV_GROUP, (gi + 1) * RWKV_GROUP)
            rr = _mm(jnp.concatenate([q_s[c, :, gs], m_s[c, :, gs]], axis=0), _block_diag(S_ref[:, gs], blocks))
            y_s[sl, gs] = rr[:C] + y0_s[c, :, gs]
            S_ref[:, gs] = rr[C:] + g_s[c, :, gs]
        return carry

    lax.fori_loop(0, NC, scan_chunk, 0)

    y = y_s[...]
    inv_n = 1.0 / HEAD_DIM
    mean = head_sum(y) * inv_n
    yc = y - mean
    var = head_sum(yc * yc) * inv_n
    yn = yc * lax.rsqrt(var + LNX_EPS) * lg_ref[...] + lb_ref[...]
    bonus = head_sum(r_s[...] * k_s[...] * rk_ref[...]) * v_s[...]
    out_ref[...] = ((yn + bonus) * g).astype(out_ref.dtype)


def _rwkv(rkvl, v_first, p, B, T):
    N, nc = rkvl.shape
    R = RWKV_HEADS * HEAD_DIM
    TT = min(RWKV_TILE, T)
    nt = T // TT
    NC = TT // RWKV_CHUNK
    has_vres = v_first is not None
    row = lambda b, i: (b * nt + i, 0)
    const = lambda b, i: (0, 0)
    vec = lambda a: a.reshape(1, -1)
    ins = [rkvl]
    specs = [pl.BlockSpec((TT, nc), row)]
    if has_vres:
        ins.append(v_first)
        specs.append(pl.BlockSpec((TT, R), row))
    small = [vec(p["mu_rkv"]), vec(p["decay_w0"]), p["decay_b"].astype(BF16), vec(p["iclr_a0"]),
             p["iclr_b"].astype(BF16), p["gate_b"].astype(BF16), vec(p["k_k"]), vec(p["k_a"]),
             vec(p["r_k"]), vec(p["lnx_g"]), vec(p["lnx_b"])]
    if has_vres:
        small += [vec(p["vres_v0"]), p["vres_b"].astype(BF16)]
    ins += small
    specs += [pl.BlockSpec(a.shape, const) for a in small]
    out_shape = [jax.ShapeDtypeStruct((N, R), BF16)]
    out_specs = [pl.BlockSpec((TT, R), row)]
    if not has_vres:
        out_shape.append(jax.ShapeDtypeStruct((N, R), F32))
        out_specs.append(pl.BlockSpec((TT, R), row))
    scratch = [pltpu.VMEM((1, 3 * R + LORA_PAD), F32),
               pltpu.VMEM((HEAD_DIM, R), F32)]
    scratch += [pltpu.VMEM((TT, R), F32) for _ in range(8)]
    scratch += [pltpu.VMEM((NC, RWKV_CHUNK, R), F32) for _ in range(4)]
    res = pl.pallas_call(
        functools.partial(_rwkv_kernel, has_vres, TT),
        grid=(B, nt),
        in_specs=specs,
        out_specs=out_specs,
        out_shape=out_shape,
        scratch_shapes=scratch,
        compiler_params=_params("arbitrary", "arbitrary"),
        name="rwkv",
    )(*ins)
    if has_vres:
        return res[0], v_first
    return res[0], res[1]


def _bucket_boundaries():
    max_exact = N_BUCKETS // 2
    d = np.arange(0, 4 * MAX_DISTANCE, dtype=np.int64)
    nf = np.maximum(d, 1).astype(np.float32)
    large = max_exact + (np.log(nf / np.float32(max_exact)) / np.float32(math.log(MAX_DISTANCE / max_exact))
                         * np.float32(N_BUCKETS - max_exact)).astype(np.int32)
    large = np.minimum(large, N_BUCKETS - 1)
    bucket = np.where(d < max_exact, d, large)
    return [int(np.argmax(bucket >= j)) for j in range(max_exact + 1, N_BUCKETS)]


_BUCKET_STARTS = _bucket_boundaries()
KEY_NEG_INF = int(np.int32(np.array(-np.inf, np.float32).view(np.int32)) ^ np.int32(0x7FFFFFFF))
INT_MIN = -2 ** 31
INT_MAX = 2 ** 31 - 1
FAST_SELECT_BLIND = 16
FAST_SELECT_STEPS = 2
FAST_SELECT_TRIPS = 10


def _sort_key(s):
    bits = pltpu.bitcast(s, I32)
    bits = jnp.where(bits == INT_MIN, 0, bits)
    return bits ^ ((bits >> 31) & jnp.int32(0x7FFFFFFF))


def _key_to_float(k):
    return pltpu.bitcast(k ^ ((k >> 31) & jnp.int32(0x7FFFFFFF)), F32)


def _dsa_kernel(T, ksel, q_ref, k_ref, vt_ref, qiw_ref, kix_ref, relb_ref, g_ref, out_ref,
                lhs_s, qpad_s, key_s, madd_s, bias_s, m_s, acc_s, s_s, p_s, cmax_s, alpha_s):
    TQ = TK = DSA_BLOCK
    H = ATTN_HEADS
    qb = pl.program_id(1)
    nch = qb + 1
    rows = lax.broadcasted_iota(I32, (TK, TQ), 0)
    cols = lax.broadcasted_iota(I32, (TK, TQ), 1)

    @pl.when((pl.program_id(0) == 0) & (qb == 0))
    def _():
        max_exact = N_BUCKETS // 2
        for off in range(2):
            d = cols - rows + off * TK
            bucket = jnp.where(d < max_exact, jnp.maximum(d, 0), max_exact)
            for start in _BUCKET_STARTS:
                bucket = bucket + jnp.where(d >= start, 1, 0)
            for h in range(H):
                far = relb_ref[N_BUCKETS - 1, h]
                tile = jnp.zeros((TK, TQ), F32)
                for bk in range(N_BUCKETS - 1):
                    tile = jnp.where(bucket == bk, (relb_ref[bk, h] - far) * LOG2E, tile)
                bias_s[h, off] = tile

    qi_t = (qiw_ref[:, :IDX_HEADS * IDX_DIM] * (IDX_DIM ** -0.5)).T
    for h in range(IDX_HEADS):
        qh = qi_t[h * IDX_DIM:(h + 1) * IDX_DIM, :]
        hi = qh.astype(BF16)
        lo = (qh - hi.astype(F32)).astype(BF16)
        lhs_s[h] = jnp.concatenate([hi, hi, lo, jnp.zeros_like(hi)], axis=0)
    w_t = qiw_ref[:, K_IDX_OFF:IDX_COLS].T[IDX_DIM:IDX_DIM + IDX_HEADS, :] * (IDX_HEADS ** -0.5)

    q_t = q_ref[...].astype(F32).T
    zeros_h = jnp.zeros((HEAD_DIM, TQ), BF16)
    for h in range(H):
        qh = q_t[h * HEAD_DIM:(h + 1) * HEAD_DIM, :].astype(BF16)
        qpad_s[h] = jnp.concatenate([qh, zeros_h] if h % 2 == 0 else [zeros_h, qh], axis=0)

    def scores(j):
        kx = kix_ref[pl.ds(pl.multiple_of(j * TK, TK), TK), :]
        acc = jnp.zeros((TK, TQ), F32)
        for h in range(IDX_HEADS):
            d = jnp.dot(kx, lhs_s[h], preferred_element_type=F32)
            acc = acc + w_t[h:h + 1, :] * jnp.maximum(d, 0.0)
        return acc

    def stats(st, kk, mn_of):
        fold = lambda z: z.reshape(TK // 8, 8, TQ)
        ge0, gt0, mn, mx = st
        return (ge0 + jnp.sum(fold(jnp.where(kk >= 0, 1, 0)), axis=0),
                gt0 + jnp.sum(fold(jnp.where(kk > 0, 1, 0)), axis=0),
                jnp.minimum(mn, jnp.min(fold(mn_of(kk)), axis=0)),
                jnp.maximum(mx, jnp.max(fold(kk), axis=0)))

    def score_body(j, st):
        kk = _sort_key(scores(j))
        key_s[j] = kk
        return stats(st, kk, lambda z: z)

    st = lax.fori_loop(0, qb, score_body, (jnp.zeros((8, TQ), I32), jnp.zeros((8, TQ), I32),
                                           jnp.full((8, TQ), INT_MAX, I32), jnp.full((8, TQ), INT_MIN, I32)))
    kd = _sort_key(jnp.where(rows <= cols, scores(qb), -jnp.inf))
    key_s[qb] = kd
    st = stats(st, kd, lambda z: jnp.where(z > KEY_NEG_INF, z, INT_MAX))
    n_ge0, n_gt0 = (jnp.sum(z, axis=0, keepdims=True) for z in st[:2])
    lo0 = _key_to_float(jnp.min(st[2], axis=0, keepdims=True))
    hi0 = _key_to_float(jnp.max(st[3], axis=0, keepdims=True) + 1)

    def count(pred):
        def body(j, acc):
            hit = jnp.where(pred(key_s[j]), 1, 0)
            return acc + jnp.sum(hit.reshape(TK // 8, 8, TQ), axis=0)
        acc = lax.fori_loop(0, nch, body, jnp.zeros((8, TQ), I32))
        return jnp.sum(acc, axis=0, keepdims=True)

    def write_mask(sel_of):
        def body(j, c):
            madd_s[j] = jnp.where(sel_of(key_s[j]), 0.0, NEG_BIG)
            return c
        lax.fori_loop(0, nch, body, 0)

    kf = float(ksel)
    n_adm = qb * TQ + lax.broadcasted_iota(I32, (1, TQ), 1) + 1
    trivial = n_adm <= ksel
    positive = n_gt0 > ksel
    done0 = trivial | ((n_gt0 <= ksel) & (n_ge0 >= ksel))
    thr0 = jnp.where(trivial, KEY_NEG_INF + 1, jnp.where(n_gt0 == ksel, 1, 0))
    need0 = jnp.where(n_gt0 < ksel, ksel - n_gt0, ksel)
    a0 = jnp.where(positive, 0.0, lo0)
    b0 = jnp.where(positive, hi0, 0.0)
    fa0 = jnp.where(positive, n_gt0, n_adm).astype(F32) - kf
    cb0 = jnp.where(positive, 0, n_ge0)
    as_int = lambda m: jnp.where(m, 1, 0)

    def any_lane(m):
        return jnp.max(as_int(m)) > 0

    def fast_cond(st):
        it, done, stuck = st[0], st[1], st[2]
        return (it < FAST_SELECT_TRIPS) & any_lane((done + stuck) == 0)

    def fast_body(st):
        it, st = st[0], st[1:]
        for _ in range(FAST_SELECT_STEPS):
            st = fast_step(st)
        return (it + 1,) + st

    def fast_step(st):
        done, stuck, thr, need, a, b, fa, fb, cb, last = st
        c = a + (b - a) * (fa / (fa - fb))
        c = jnp.where((c > a) & (c < b), c, 0.5 * a + 0.5 * b)
        inside = (c > a) & (c < b)
        ckey = _sort_key(c)
        cnt = count(lambda kk: kk >= ckey)
        fc = cnt.astype(F32) - kf
        hit = inside & (cnt == ksel) & (done == 0)
        thr = jnp.where(hit, ckey, thr)
        need = jnp.where(hit, ksel, need)
        done = jnp.where(hit, 1, done)
        stuck = jnp.where(inside, stuck, 1)
        up = inside & (cnt > ksel)
        dn = inside & (cnt < ksel)
        fb = jnp.where(up & (last == 1), 0.5 * fb, fb)
        fa = jnp.where(dn & (last == 0), 0.5 * fa, fa)
        a, fa = jnp.where(up, c, a), jnp.where(up, fc, fa)
        b, fb, cb = jnp.where(dn, c, b), jnp.where(dn, fc, fb), jnp.where(dn, cnt, cb)
        return done, stuck, thr, need, a, b, fa, fb, cb, jnp.where(up, 1, jnp.where(dn, 0, last))

    st = (as_int(done0), jnp.zeros((1, TQ), I32), thr0, need0,
          a0, b0, fa0, cb0.astype(F32) - kf, cb0, jnp.full((1, TQ), -1, I32))
    blind = jnp.where((qb + 1) * TQ <= ksel, 0, FAST_SELECT_BLIND)
    st = lax.fori_loop(0, blind, lambda i, s: fast_step(s), st)
    st = lax.while_loop(fast_cond, fast_body, (jnp.int32(0),) + st)
    done, thr, need, a, b, cb = st[1], st[3], st[4], st[5], st[6], st[9]
    open_lane = done == 0

    def close_cond(st):
        return any_lane((st[1] - st[0]) > 1)

    def close_body(st):
        ak, bk, cb = st
        mid = ak + ((bk - ak) >> 1)
        cnt = count(lambda kk: kk >= mid)
        ge = cnt >= ksel
        return jnp.where(ge, mid, ak), jnp.where(ge, bk, mid), jnp.where(ge, cb, cnt)

    ak, _, cb = lax.while_loop(close_cond, close_body, (
        jnp.where(open_lane, _sort_key(a), thr), jnp.where(open_lane, _sort_key(b), thr), cb))
    thr = jnp.where(open_lane, ak, thr)
    need = jnp.where(open_lane, ksel - cb, need)
    any_tie = any_lane(open_lane | (jnp.logical_not(trivial) & (n_gt0 < ksel) & (n_ge0 > ksel)))

    @pl.when(jnp.logical_not(any_tie))
    def _():
        write_mask(lambda kk: kk >= thr)

    @pl.when(any_tie)
    def _():
        kr = lax.broadcasted_iota(I32, (TK, TK), 0)
        kc = lax.broadcasted_iota(I32, (TK, TK), 1)
        lower = jnp.where(kc <= kr, 1.0, 0.0).astype(BF16)
        need_f = need.astype(F32)

        def body(j, run):
            kk = key_s[j]
            eq = kk == thr
            eqf = jnp.where(eq, 1.0, 0.0)
            rank = run + jnp.dot(lower, eqf.astype(BF16), preferred_element_type=F32)
            tied_in = jnp.where(eq, rank, float(TOPK_MAX + 1)) <= need_f
            madd_s[j] = jnp.where(kk > thr, 0.0, jnp.where(tied_in, 0.0, NEG_BIG))
            return run + jnp.sum(eqf, axis=0, keepdims=True)
        lax.fori_loop(0, nch, body, jnp.zeros((1, TQ), F32))

    m_s[...] = jnp.full(m_s.shape, NEG_BIG, F32)
    acc_s[...] = jnp.zeros(acc_s.shape, F32)

    def attend(j, bias_of):
        ks = pl.ds(pl.multiple_of(j * TK, TK), TK)
        ma = madd_s[j]
        for h in range(H):
            kp = k_ref[ks, LANES * (h // 2):LANES * (h // 2 + 1)]
            s = jnp.dot(kp, qpad_s[h], preferred_element_type=F32) + ma
            bias = bias_of(h)
            if bias is not None:
                s = s + bias
            s_s[h] = s
            cmax_s[h] = jnp.max(s.reshape(TK // 8, 8, TQ), axis=0)
        for h in range(H):
            m_old = m_s[h]
            m_new = jnp.maximum(m_old, jnp.max(cmax_s[h], axis=0, keepdims=True))
            alpha_s[h] = jnp.exp2(m_old - m_new)
            m_s[h] = m_new
            p_s[h] = jnp.exp2(s_s[h] - m_new).astype(BF16)
        for h in range(H):
            hs = slice(h * VT_ROWS, (h + 1) * VT_ROWS)
            pv = jnp.dot(vt_ref[j, hs, :], p_s[h], preferred_element_type=F32)
            acc_s[hs, :] = alpha_s[h] * acc_s[hs, :] + pv

    def far_body(j, c):
        attend(j, lambda h: None)
        return c

    lax.fori_loop(0, jnp.maximum(qb - 1, 0), far_body, 0)

    @pl.when(qb >= 1)
    def _():
        attend(qb - 1, lambda h: bias_s[h, 1])

    attend(qb, lambda h: bias_s[h, 0])

    outs = []
    for h in range(H):
        o = acc_s[h * VT_ROWS:h * VT_ROWS + HEAD_DIM, :] / acc_s[h * VT_ROWS + HEAD_DIM:h * VT_ROWS + HEAD_DIM + 1, :]
        ms = jnp.mean(o * o, axis=0, keepdims=True)
        outs.append(o * lax.rsqrt(ms + RMS_EPS))
    out_ref[...] = (jnp.concatenate(outs, axis=0).T * g_ref[...]).astype(out_ref.dtype)


def _dsa(qk, vt, qiw, kix, rel_bias, g, B, T):
    N = qk.shape[0]
    TQ = DSA_BLOCK
    nq = T // TQ
    A = ATTN_HEADS * HEAD_DIM
    ksel = min(TOPK_MAX, T // 4)
    assert MAX_DISTANCE <= TQ + 1
    return pl.pallas_call(
        functools.partial(_dsa_kernel, T, ksel),
        grid=(B, nq),
        in_specs=[pl.BlockSpec((TQ, A), lambda b, i: (b * nq + i, 0)),
                  pl.BlockSpec((T, A), lambda b, i: (b, 1)),
                  pl.BlockSpec((nq, ATTN_HEADS * VT_ROWS, TQ), lambda b, i: (b, 0, 0)),
                  pl.BlockSpec((TQ, IDX_COLS), lambda b, i: (b * nq + i, 0)),
                  pl.BlockSpec((T, KIX_COLS), lambda b, i: (b, 0)),
                  pl.BlockSpec(memory_space=pltpu.SMEM),
                  pl.BlockSpec((1, A), lambda b, i: (0, 0))],
        out_specs=pl.BlockSpec((TQ, A), lambda b, i: (b * nq + i, 0)),
        out_shape=jax.ShapeDtypeStruct((N, A), BF16),
        scratch_shapes=[pltpu.VMEM((IDX_HEADS, KIX_COLS, TQ), BF16),
                        pltpu.VMEM((ATTN_HEADS, LANES, TQ), BF16),
                        pltpu.VMEM((nq, TQ, TQ), I32),
                        pltpu.VMEM((nq, TQ, TQ), F32),
                        pltpu.VMEM((ATTN_HEADS, 2, TQ, TQ), F32),
                        pltpu.VMEM((ATTN_HEADS, 1, TQ), F32),
                        pltpu.VMEM((ATTN_HEADS * VT_ROWS, TQ), F32),
                        pltpu.VMEM((ATTN_HEADS, TQ, TQ), F32),
                        pltpu.VMEM((ATTN_HEADS, TQ, TQ), BF16),
                        pltpu.VMEM((ATTN_HEADS, 8, TQ), F32),
                        pltpu.VMEM((ATTN_HEADS, 1, TQ), F32)],
        compiler_params=_params("arbitrary", "arbitrary"),
        name="dsa",
    )(qk, qk, vt, qiw, kix, rel_bias, g.reshape(1, A))


def _mix_mlp_kernel(final, x_ref, rw_ref, att_ref, wo_ref, gt1_ref, g_ref, sc_ref, sh_ref, gt2_ref,
                    w1_ref, w2_ref, fg_ref, o_ref, x_s, h_s, acc_s):
    j = pl.program_id(1)

    @pl.when(j == 0)
    def _():
        R = rw_ref.shape[1]
        mixed = (jnp.dot(rw_ref[...], wo_ref[:R, :], preferred_element_type=F32)
                 + jnp.dot(att_ref[...], wo_ref[R:, :], preferred_element_type=F32))
        x1 = x_ref[...] + gt1_ref[...] * mixed
        x_s[...] = x1
        h_s[...] = _norm_mod(x1, g_ref[...], sc_ref[...], sh_ref[...]).astype(BF16)
        acc_s[...] = jnp.zeros_like(acc_s)

    u = jnp.dot(h_s[...], w1_ref[...], preferred_element_type=F32)
    u = jnp.square(jnp.maximum(u, 0.0))
    acc_s[...] += jnp.dot(u.astype(BF16), w2_ref[...], preferred_element_type=F32)

    @pl.when(j == pl.num_programs(1) - 1)
    def _():
        y = x_s[...] + gt2_ref[...] * acc_s[...]
        if final:
            ms = jnp.mean(y * y, axis=-1, keepdims=True)
            y = y * lax.rsqrt(ms + RMS_EPS) * fg_ref[...]
        o_ref[...] = y


def _mix_mlp(x2d, rw, att, wo, gt1, g, sc, sh, gt2, w1, w2, final_g, final, T):
    N, D = x2d.shape
    F = w1.shape[1]
    fc = min(MLP_FF_CHUNK, F)
    tm = min(MLP_TILE, T)
    nb = T // tm
    row = lambda i, j: (i, 0)
    per_b = lambda i, j: (i // nb, 0, 0)
    const = lambda i, j: (0, 0)
    return pl.pallas_call(
        functools.partial(_mix_mlp_kernel, final),
        grid=(N // tm, F // fc),
        in_specs=[pl.BlockSpec((tm, D), row),
                  pl.BlockSpec((tm, rw.shape[1]), row),
                  pl.BlockSpec((tm, att.shape[1]), row),
                  pl.BlockSpec(wo.shape, const, pipeline_mode=pl.Buffered(1)),
                  pl.BlockSpec((None, 1, D), per_b),
                  pl.BlockSpec((1, D), const),
                  pl.BlockSpec((None, 1, D), per_b),
                  pl.BlockSpec((None, 1, D), per_b),
                  pl.BlockSpec((None, 1, D), per_b),
                  pl.BlockSpec((D, fc), lambda i, j: (0, j)),
                  pl.BlockSpec((fc, D), lambda i, j: (j, 0)),
                  pl.BlockSpec((1, D), const)],
        out_specs=pl.BlockSpec((tm, D), row),
        out_shape=jax.ShapeDtypeStruct((N, D), F32),
        scratch_shapes=[pltpu.VMEM((tm, D), F32), pltpu.VMEM((tm, D), BF16), pltpu.VMEM((tm, D), F32)],
        compiler_params=_params("arbitrary", "arbitrary"),
        name="mix_mlp",
    )(x2d, rw, att, wo, gt1, g.reshape(1, D), sc, sh, gt2, w1, w2, final_g.reshape(1, D))


def _pad_cols(w, n):
    return jnp.pad(w, ((0, 0), (0, n - w.shape[1])))


def _split_bf16(w):
    hi = w.astype(BF16)
    return hi, (w - hi.astype(F32)).astype(BF16)


def _in_weights(l, w_in, mu_lora, decay_a, iclr_a, gate_a, vres_mu, vres_a):
    D = w_in.shape[1]
    R = RWKV_HEADS * HEAD_DIM
    w = w_in[l]
    mats = [(decay_a[l], mu_lora[l, 0]), (iclr_a[l], mu_lora[l, 1]), (gate_a[l], mu_lora[l, 2])]
    if l > 0:
        mats.append((vres_a[l - 1], vres_mu[l - 1]))
    now = _pad_cols(jnp.concatenate([a * (1.0 - mu)[:, None] for a, mu in mats], axis=1), LORA_PAD)
    prev = _pad_cols(jnp.concatenate([a * mu[:, None] for a, mu in mats], axis=1), LORA_PAD)
    wq = w[:, 3 * R:4 * R] * (HEAD_DIM ** -0.5 * LOG2E)
    wa = jnp.concatenate([w[:, :3 * R], now, prev, wq, w[:, 4 * R:5 * R]], axis=1).astype(BF16)
    wv = w[:, 5 * R:6 * R].reshape(D, ATTN_HEADS, HEAD_DIM)
    wv = jnp.pad(wv, ((0, 0), (0, 0), (0, VT_ROWS - HEAD_DIM))).reshape(D, ATTN_HEADS * VT_ROWS)
    wvt = wv.T.astype(BF16)
    wbh, wbl = _split_bf16(_pad_cols(w[:, 6 * R:], IDX_COLS))
    return wa, wvt, wbh, wbl


def kernel(x, c, w_ada, b_ada, norm1_g, norm2_g, w_in, mu_rkv, mu_lora, decay_w0, decay_a, decay_b, iclr_a0, iclr_a, iclr_b, gate_a, gate_b, k_k, k_a, r_k, lnx_g, lnx_b, vres_mu, vres_v0, vres_a, vres_b, attn_out_g, rel_bias, w_out, w_mlp1, w_mlp2, final_g):
    B, T, D = x.shape
    depth = w_in.shape[0]
    mod = _adaln(c, w_ada, b_ada)
    mod = mod.reshape(depth, B, 6, 1, D).transpose(0, 2, 1, 3, 4)
    x2d = x.reshape(B * T, D)
    v_first = None
    for l in range(depth):
        sh1, sc1, gt1, sh2, sc2, gt2 = (mod[l, i] for i in range(6))
        wa, wvt, wbh, wbl = _in_weights(l, w_in, mu_lora, decay_a, iclr_a, gate_a, vres_mu, vres_a)
        rkvl, qk, vt, qiw, kix = _inproj(x2d, norm1_g[l], sc1, sh1, wa, wvt, wbh, wbl, T)
        p = dict(mu_rkv=mu_rkv[l], decay_w0=decay_w0[l], decay_b=decay_b[l], iclr_a0=iclr_a0[l],
                 iclr_b=iclr_b[l], gate_b=gate_b[l], k_k=k_k[l], k_a=k_a[l], r_k=r_k[l],
                 lnx_g=lnx_g[l], lnx_b=lnx_b[l])
        if l > 0:
            p.update(vres_v0=vres_v0[l - 1], vres_b=vres_b[l - 1])
        rw, v_first = _rwkv(rkvl, v_first, p, B, T)
        att = _dsa(qk, vt, qiw, kix, rel_bias, attn_out_g[l], B, T)
        x2d = _mix_mlp(x2d, rw, att, w_out[l].astype(BF16), gt1, norm2_g[l], sc2, sh2, gt2,
                       w_mlp1[l].astype(BF16), w_mlp2[l].astype(BF16), final_g, l == depth - 1, T)
    return x2d.reshape(B, T, D)
```

```python
import functools
import math

import numpy as np
import jax
import jax.numpy as jnp
from jax import lax
from jax.experimental import pallas as pl
from jax.experimental.pallas import tpu as pltpu

F32 = jnp.float32
BF16 = jnp.bfloat16
I32 = jnp.int32

HEAD_DIM = 64
RWKV_HEADS = 8
ATTN_HEADS = 8
IDX_HEADS = 8
IDX_DIM = 64
TOPK_MAX = 256
N_BUCKETS = 32
MAX_DISTANCE = 128
RMS_EPS = 1e-6
LNX_EPS = 64e-5

LANES = 128
VMEM_LIMIT = 56 * 1024 * 1024
RWKV_CHUNK = 64
RWKV_GROUP = 256
RWKV_TILE = 512
RWKV_UNROLL = 8
DSA_BLOCK = 256
VT_ROWS = 80
LOG2E = math.log2(math.e)
INPROJ_TILE = 512
MLP_TILE = 1024
MLP_FF_CHUNK = 1024
LORA_PAD = 384
K_IDX_OFF = IDX_HEADS * IDX_DIM
W_IDX_OFF = K_IDX_OFF + IDX_DIM
IDX_COLS = 640
KIX_COLS = 4 * IDX_DIM
NEG_BIG = -1e30
HI = lax.Precision.HIGHEST
NT = (((1,), (1,)), ((), ()))


def _bdot(a, b):
    return jnp.dot(a.astype(BF16), b.astype(BF16), preferred_element_type=F32)


def _hdot(a, b):
    return jnp.dot(a, b, precision=HI, preferred_element_type=F32)


def _params(*sem):
    return pltpu.CompilerParams(dimension_semantics=sem, vmem_limit_bytes=VMEM_LIMIT)


def _adaln_kernel(c_ref, w_ref, b_ref, o_ref):
    c = c_ref[...]
    c_act = c * jax.nn.sigmoid(c)
    o_ref[...] = _hdot(c_act, w_ref[...]) + b_ref[...]


def _adaln(c, w_ada, b_ada):
    L, D, D6 = w_ada.shape
    B = c.shape[0]
    cb = 1024
    return pl.pallas_call(
        _adaln_kernel,
        grid=(L, D6 // cb),
        in_specs=[pl.BlockSpec((B, D), lambda l, j: (0, 0)),
                  pl.BlockSpec((None, D, cb), lambda l, j: (l, 0, j)),
                  pl.BlockSpec((None, 1, cb), lambda l, j: (l, 0, j))],
        out_specs=pl.BlockSpec((None, B, cb), lambda l, j: (l, 0, j)),
        out_shape=jax.ShapeDtypeStruct((L, B, D6), F32),
        compiler_params=_params("arbitrary", "arbitrary"),
        name="adaln",
    )(c, w_ada, b_ada.reshape(L, 1, D6))


def _norm_mod(x, g, sc, sh):
    ms = jnp.mean(x * x, axis=-1, keepdims=True)
    return (x * lax.rsqrt(ms + RMS_EPS) * g) * (1.0 + sc) + sh


def _inproj_kernel(x_ref, g_ref, sc_ref, sh_ref, wa_ref, wvt_ref, wbh_ref, wbl_ref,
                   rkvl_ref, qk_ref, vt_ref, qiw_ref, kix_ref):
    h = _norm_mod(x_ref[...], g_ref[...], sc_ref[...], sh_ref[...])
    hb = h.astype(BF16)
    hl = (h - hb.astype(F32)).astype(BF16)
    pa = jnp.dot(hb, wa_ref[...], preferred_element_type=F32)
    nr = rkvl_ref.shape[1]
    rkvl_ref[...] = pa[:, :nr]
    qk_ref[...] = pa[:, nr:].astype(BF16)
    vt = lax.dot_general(wvt_ref[...], hb, NT, preferred_element_type=F32)
    ones_row = lax.broadcasted_iota(I32, vt.shape, 0) % VT_ROWS == HEAD_DIM
    vt = jnp.where(ones_row, 1.0, vt).astype(BF16)
    for j in range(vt_ref.shape[0]):
        vt_ref[j] = vt[:, j * DSA_BLOCK:(j + 1) * DSA_BLOCK]
    pb = (jnp.dot(hb, wbh_ref[...], preferred_element_type=F32)
          + jnp.dot(hb, wbl_ref[...], preferred_element_type=F32)
          + jnp.dot(hl, wbh_ref[...], preferred_element_type=F32))
    qiw_ref[...] = pb
    ki = pb[:, K_IDX_OFF:W_IDX_OFF]
    kh = ki.astype(BF16)
    kl = (ki - kh.astype(F32)).astype(BF16)
    kix_ref[...] = jnp.concatenate([kh, kl, kh, jnp.zeros_like(kh)], axis=-1)


def _inproj(x2d, g, sc, sh, wa, wvt, wbh, wbl, T):
    N, D = x2d.shape
    nblk = INPROJ_TILE // DSA_BLOCK
    tm = INPROJ_TILE
    na = wa.shape[1]
    A = wvt.shape[0]
    nr = 3 * RWKV_HEADS * HEAD_DIM + 2 * LORA_PAD
    assert T % tm == 0 and (wa.shape[1] - nr) % LANES == 0
    nb = T // tm
    row = lambda i: (i, 0)
    per_b = lambda i: (i // nb, 0, 0)
    const = lambda i: (0, 0)
    once = pl.Buffered(1)
    return pl.pallas_call(
        _inproj_kernel,
        grid=(N // tm,),
        in_specs=[pl.BlockSpec((tm, D), row),
                  pl.BlockSpec((1, D), const),
                  pl.BlockSpec((None, 1, D), per_b),
                  pl.BlockSpec((None, 1, D), per_b),
                  pl.BlockSpec((D, na), const, pipeline_mode=once),
                  pl.BlockSpec((A, D), const, pipeline_mode=once),
                  pl.BlockSpec((D, IDX_COLS), const, pipeline_mode=once),
                  pl.BlockSpec((D, IDX_COLS), const, pipeline_mode=once)],
        out_specs=[pl.BlockSpec((tm, nr), row),
                   pl.BlockSpec((tm, na - nr), row),
                   pl.BlockSpec((nblk, A, DSA_BLOCK), lambda i: (i, 0, 0)),
                   pl.BlockSpec((tm, IDX_COLS), row),
                   pl.BlockSpec((tm, KIX_COLS), row)],
        out_shape=[jax.ShapeDtypeStruct((N, nr), F32),
                   jax.ShapeDtypeStruct((N, na - nr), BF16),
                   jax.ShapeDtypeStruct((N // DSA_BLOCK, A, DSA_BLOCK), BF16),
                   jax.ShapeDtypeStruct((N, IDX_COLS), F32),
                   jax.ShapeDtypeStruct((N, KIX_COLS), BF16)],
        compiler_params=_params("arbitrary"),
        name="inproj",
    )(x2d, g.reshape(1, D), sc, sh, wa, wvt, wbh, wbl)


def _head_ones(n):
    r = lax.broadcasted_iota(I32, (n, n), 0) // HEAD_DIM
    c = lax.broadcasted_iota(I32, (n, n), 1) // HEAD_DIM
    return jnp.where(r == c, 1.0, 0.0).astype(F32)


def _split(x):
    hi = x.astype(BF16)
    return hi, (x - hi.astype(F32)).astype(BF16)


def _split3(x):
    hi = x.astype(BF16)
    r = x - hi.astype(F32)
    mid = r.astype(BF16)
    return hi, mid, (r - mid.astype(F32)).astype(BF16)


def _block_diag(w, blocks):
    w = w.astype(BF16)
    return jnp.concatenate([jnp.where(m, w, jnp.zeros_like(w)) for m in blocks], axis=0)


def _block_diag_t(w, blocks):
    return jnp.concatenate([jnp.where(m, w, 0.0) for m in blocks], axis=0).T


def _fold_blocks(w):
    n = w.shape[1] // HEAD_DIM
    out = w[:HEAD_DIM]
    for h in range(1, n):
        out = out + w[h * HEAD_DIM:(h + 1) * HEAD_DIM]
    return out


def _mm(a, w):
    return jnp.dot(a.astype(BF16), w, preferred_element_type=F32)


def _rwkv_kernel(has_vres, TT, *refs):
    C = RWKV_CHUNK
    R = RWKV_HEADS * HEAD_DIM
    G = RWKV_GROUP
    NG = R // G
    NC = TT // C
    if has_vres:
        (rkvl_ref, vf_ref, mu_ref, w0_ref, db_ref, a0_ref, ib_ref, gb_ref, kk_ref, ka_ref,
         rk_ref, lg_ref, lb_ref, v0_ref, vb_ref, out_ref,
         prev_ref, S_ref, r_s, k_s, v_s, kk_s, b_s, lw_s, cum_s, y_s, q_s, y0_s, m_s, g_s) = refs
    else:
        (rkvl_ref, mu_ref, w0_ref, db_ref, a0_ref, ib_ref, gb_ref, kk_ref, ka_ref,
         rk_ref, lg_ref, lb_ref, out_ref, vfo_ref,
         prev_ref, S_ref, r_s, k_s, v_s, kk_s, b_s, lw_s, cum_s, y_s, q_s, y0_s, m_s, g_s) = refs

    @pl.when(pl.program_id(1) == 0)
    def _():
        prev_ref[...] = jnp.zeros_like(prev_ref)
        S_ref[...] = jnp.zeros_like(S_ref)

    row0 = lax.broadcasted_iota(I32, (TT, 1), 0) == 0
    prev = prev_ref[...]

    def shift(z, p):
        return jnp.where(row0, p, pltpu.roll(z, 1, 0))

    rkv = rkvl_ref[:, :3 * R]
    p1 = rkvl_ref[:, 3 * R:3 * R + LORA_PAD]
    p2 = rkvl_ref[:, 3 * R + LORA_PAD:]
    rkv_new = rkv + (shift(rkv, prev[:, :3 * R]) - rkv) * mu_ref[...]
    lora = p1 + shift(p2, prev[:, 3 * R:])
    prev_ref[...] = jnp.concatenate([rkv[TT - 1:TT, :], p2[TT - 1:TT, :]], axis=-1)

    r = rkv_new[:, :R]
    k = rkv_new[:, R:2 * R]
    v = rkv_new[:, 2 * R:]
    ones_h = _head_ones(G).astype(BF16)

    def head_sum(x):
        hi, lo = _split(x)
        return jnp.concatenate(
            [jnp.dot(hi[:, gi * G:(gi + 1) * G], ones_h, preferred_element_type=F32)
             + jnp.dot(lo[:, gi * G:(gi + 1) * G], ones_h, preferred_element_type=F32) for gi in range(NG)], axis=1)

    o_a = db_ref.shape[0]
    o_g = o_a + ib_ref.shape[0]
    o_v = o_g + gb_ref.shape[0]
    wlog = w0_ref[...] + _bdot(jnp.tanh(lora[:, :o_a]), db_ref[...])
    z = -wlog
    wlog = -(jnp.maximum(z, 0.0) + jnp.log(1.0 + jnp.exp(-jnp.abs(z)))) - 0.5
    lw = -jnp.exp(wlog)
    lw_s[...] = lw
    slab = min(TT, 4 * C)
    tr = lax.broadcasted_iota(I32, (slab, slab), 0)
    tc = lax.broadcasted_iota(I32, (slab, slab), 1)
    tri = jnp.where((tc <= tr) & (tr // C == tc // C), 1.0, 0.0).astype(BF16)
    parts = _split3(lw)
    for s0 in range(0, TT, slab):
        cum_s[s0:s0 + slab, :] = sum(jnp.dot(tri, part[s0:s0 + slab], preferred_element_type=F32) for part in parts)
    a = jax.nn.sigmoid(a0_ref[...] + _bdot(lora[:, o_a:o_g], ib_ref[...]))
    g = _bdot(jax.nn.sigmoid(lora[:, o_g:o_v]), gb_ref[...])
    kkr = k * kk_ref[...]
    kk = kkr * lax.rsqrt(jnp.maximum(head_sum(kkr * kkr), 1e-24))
    k = k * (1.0 + (a - 1.0) * ka_ref[...])
    if has_vres:
        v = v + (vf_ref[...] - v) * jax.nn.sigmoid(
            v0_ref[...] + _bdot(lora[:, o_v:o_v + vb_ref.shape[0]], vb_ref[...]))
    else:
        vfo_ref[...] = v
    r_s[...] = r
    k_s[...] = k
    v_s[...] = v
    kk_s[...] = kk
    b_s[...] = kk * a

    lane_g = lax.broadcasted_iota(I32, (C, RWKV_GROUP), 1)
    row_g = lax.broadcasted_iota(I32, (C, RWKV_GROUP), 0)
    blocks = [lane_g // HEAD_DIM == h for h in range(RWKV_GROUP // HEAD_DIM)]
    local = lane_g % HEAD_DIM
    strict = local < row_g
    incl = local <= row_g
    diag = local == row_g
    eye_cat = jnp.where(diag, 1.0, 0.0).astype(F32)
    levels = int(math.log2(C)) - 1

    def local_chunk(it, carry):
        chains = [(u, slice(gi * G, (gi + 1) * G)) for u in range(RWKV_UNROLL) for gi in range(NG)]
        each = lambda f: [f(i) for i in range(len(chains))]
        rd, kkd, kt, bt, kc, bc, vv, plast = ([] for _ in range(8))
        for u in range(RWKV_UNROLL):
            sl = pl.ds(pl.multiple_of((it * RWKV_UNROLL + u) * C, C), C)
            lw = lw_s[sl, :]
            cum = cum_s[sl, :]
            cl = cum[C - 1:C, :]
            pinv = jnp.exp(-cum)
            pc = jnp.exp(cl - cum)
            k_a = k_s[sl, :]
            b_a = b_s[sl, :]
            full = (r_s[sl, :] * jnp.exp(cum), kk_s[sl, :] * jnp.exp(cum - lw), k_a * pinv, b_a * pinv,
                    k_a * pc, b_a * pc, v_s[sl, :], jnp.exp(cl))
            for dst, x in zip((rd, kkd, kt, bt, kc, bc, vv, plast), full):
                dst.extend(x[:, gs] for (uu, gs) in chains if uu == u)
        kt_w = each(lambda i: _block_diag_t(kt[i], blocks).astype(BF16))
        bt_w = each(lambda i: _block_diag_t(bt[i], blocks).astype(BF16))
        bc_t = each(lambda i: _fold_blocks(_block_diag_t(bc[i], blocks)))
        kc_t = each(lambda i: _fold_blocks(_block_diag_t(kc[i], blocks)))
        rr = each(lambda i: _mm(jnp.concatenate([kkd[i], rd[i]], axis=0),
                                jnp.concatenate([kt_w[i], bt_w[i]], axis=1)))
        akk = each(lambda i: jnp.where(strict, rr[i][:C, :G], 0.0))
        ark = each(lambda i: jnp.where(incl, rr[i][C:, :G], 0.0))
        arb = each(lambda i: jnp.where(incl, rr[i][C:, G:], 0.0))
        npow = each(lambda i: jnp.where(strict, -rr[i][:C, G:], 0.0))
        x = each(lambda i: eye_cat + npow[i])
        npow = each(lambda i: _mm(npow[i], _block_diag(npow[i], blocks)))
        for _ in range(levels - 1):
            rr = each(lambda i: _mm(jnp.concatenate([npow[i], x[i]], axis=0), _block_diag(npow[i], blocks)))
            npow = each(lambda i: rr[i][:C])
            x = each(lambda i: x[i] + rr[i][C:])
        rr = each(lambda i: _mm(x[i], _block_diag(npow[i], blocks)))
        x = each(lambda i: x[i] + rr[i])
        rv = each(lambda i: _mm(jnp.concatenate([akk[i], ark[i], kc_t[i]], axis=0), _block_diag(vv[i], blocks)))
        ru = each(lambda i: _mm(x[i], jnp.concatenate([_block_diag(kkd[i], blocks),
                                                       _block_diag(rv[i][:C], blocks)], axis=1)))
        rr = each(lambda i: _mm(jnp.concatenate([arb[i], bc_t[i]], axis=0),
                                jnp.concatenate([_block_diag(ru[i][:, :G], blocks),
                                                 _block_diag(ru[i][:, G:], blocks)], axis=1)))
        for i, (u, gs) in enumerate(chains):
            c = it * RWKV_UNROLL + u
            q_s[c, :, gs] = rd[i] - rr[i][:C, :G]
            y0_s[c, :, gs] = rv[i][C:2 * C] - rr[i][:C, G:]
            dterm = jnp.where(diag, jnp.broadcast_to(plast[i], (C, G)), 0.0)
            m_s[c, :, gs] = dterm - rr[i][C:, :G]
            g_s[c, :, gs] = rv[i][2 * C:] - rr[i][C:, G:]
        return carry

    lax.fori_loop(0, NC // RWKV_UNROLL, local_chunk, 0)

    def scan_chunk(c, carry):
        sl = pl.ds(pl.multiple_of(c * C, C), C)
        for gi in range(NG):
            gs = slice(gi * RWKV_GROUP, (gi + 1) * RWKV_GROUP)
            rr = _mm(jnp.concatenate([q_s[c, :, gs], m_s[c, :, gs]], axis=0), _block_diag(S_ref[:, gs], blocks))
            y_s[sl, gs] = rr[:C] + y0_s[c, :, gs]
            S_ref[:, gs] = rr[C:] + g_s[c, :, gs]
        return carry

    lax.fori_loop(0, NC, scan_chunk, 0)

    y = y_s[...]
    inv_n = 1.0 / HEAD_DIM
    mean = head_sum(y) * inv_n
    yc = y - mean
    var = head_sum(yc * yc) * inv_n
    yn = yc * lax.rsqrt(var + LNX_EPS) * lg_ref[...] + lb_ref[...]
    bonus = head_sum(r_s[...] * k_s[...] * rk_ref[...]) * v_s[...]
    out_ref[...] = ((yn + bonus) * g).astype(out_ref.dtype)


def _rwkv(rkvl, v_first, p, B, T):
    N, nc = rkvl.shape
    R = RWKV_HEADS * HEAD_DIM
    TT = min(RWKV_TILE, T)
    nt = T // TT
    NC = TT // RWKV_CHUNK
    has_vres = v_first is not None
    row = lambda b, i: (b * nt + i, 0)
    const = lambda b, i: (0, 0)
    vec = lambda a: a.reshape(1, -1)
    ins = [rkvl]
    specs = [pl.BlockSpec((TT, nc), row)]
    if has_vres:
        ins.append(v_first)
        specs.append(pl.BlockSpec((TT, R), row))
    small = [vec(p["mu_rkv"]), vec(p["decay_w0"]), p["decay_b"].astype(BF16), vec(p["iclr_a0"]),
             p["iclr_b"].astype(BF16), p["gate_b"].astype(BF16), vec(p["k_k"]), vec(p["k_a"]),
             vec(p["r_k"]), vec(p["lnx_g"]), vec(p["lnx_b"])]
    if has_vres:
        small += [vec(p["vres_v0"]), p["vres_b"].astype(BF16)]
    ins += small
    specs += [pl.BlockSpec(a.shape, const) for a in small]
    out_shape = [jax.ShapeDtypeStruct((N, R), BF16)]
    out_specs = [pl.BlockSpec((TT, R), row)]
    if not has_vres:
        out_shape.append(jax.ShapeDtypeStruct((N, R), F32))
        out_specs.append(pl.BlockSpec((TT, R), row))
    scratch = [pltpu.VMEM((1, 3 * R + LORA_PAD), F32),
               pltpu.VMEM((HEAD_DIM, R), F32)]
    scratch += [pltpu.VMEM((TT, R), F32) for _ in range(8)]
    scratch += [pltpu.VMEM((NC, RWKV_CHUNK, R), F32) for _ in range(4)]
    res = pl.pallas_call(
        functools.partial(_rwkv_kernel, has_vres, TT),
        grid=(B, nt),
        in_specs=specs,
        out_specs=out_specs,
        out_shape=out_shape,
        scratch_shapes=scratch,
        compiler_params=_params("arbitrary", "arbitrary"),
        name="rwkv",
    )(*ins)
    if has_vres:
        return res[0], v_first
    return res[0], res[1]


def _bucket_boundaries():
    max_exact = N_BUCKETS // 2
    d = np.arange(0, 4 * MAX_DISTANCE, dtype=np.int64)
    nf = np.maximum(d, 1).astype(np.float32)
    large = max_exact + (np.log(nf / np.float32(max_exact)) / np.float32(math.log(MAX_DISTANCE / max_exact))
                         * np.float32(N_BUCKETS - max_exact)).astype(np.int32)
    large = np.minimum(large, N_BUCKETS - 1)
    bucket = np.where(d < max_exact, d, large)
    return [int(np.argmax(bucket >= j)) for j in range(max_exact + 1, N_BUCKETS)]


_BUCKET_STARTS = _bucket_boundaries()
KEY_NEG_INF = int(np.int32(np.array(-np.inf, np.float32).view(np.int32)) ^ np.int32(0x7FFFFFFF))
INT_MIN = -2 ** 31
INT_MAX = 2 ** 31 - 1
FAST_SELECT_BLIND = 14
FAST_SELECT_STEPS = 2
FAST_SELECT_TRIPS = 10


def _sort_key(s):
    bits = pltpu.bitcast(s, I32)
    bits = jnp.where(bits == INT_MIN, 0, bits)
    return bits ^ ((bits >> 31) & jnp.int32(0x7FFFFFFF))


def _key_to_float(k):
    return pltpu.bitcast(k ^ ((k >> 31) & jnp.int32(0x7FFFFFFF)), F32)


def _dsa_kernel(T, ksel, q_ref, k_ref, vt_ref, qiw_ref, kix_ref, relb_ref, g_ref, out_ref,
                lhs_s, qpad_s, key_s, madd_s, bias_s, m_s, acc_s, s_s, p_s, cmax_s, alpha_s):
    TQ = TK = DSA_BLOCK
    H = ATTN_HEADS
    qb = pl.program_id(1)
    nch = qb + 1
    rows = lax.broadcasted_iota(I32, (TK, TQ), 0)
    cols = lax.broadcasted_iota(I32, (TK, TQ), 1)

    @pl.when((pl.program_id(0) == 0) & (qb == 0))
    def _():
        max_exact = N_BUCKETS // 2
        for off in range(2):
            d = cols - rows + off * TK
            bucket = jnp.where(d < max_exact, jnp.maximum(d, 0), max_exact)
            for start in _BUCKET_STARTS:
                bucket = bucket + jnp.where(d >= start, 1, 0)
            for h in range(H):
                far = relb_ref[N_BUCKETS - 1, h]
                tile = jnp.zeros((TK, TQ), F32)
                for bk in range(N_BUCKETS - 1):
                    tile = jnp.where(bucket == bk, (relb_ref[bk, h] - far) * LOG2E, tile)
                bias_s[h, off] = tile

    qi_t = (qiw_ref[:, :IDX_HEADS * IDX_DIM] * (IDX_DIM ** -0.5)).T
    for h in range(IDX_HEADS):
        qh = qi_t[h * IDX_DIM:(h + 1) * IDX_DIM, :]
        hi = qh.astype(BF16)
        lo = (qh - hi.astype(F32)).astype(BF16)
        lhs_s[h] = jnp.concatenate([hi, hi, lo, jnp.zeros_like(hi)], axis=0)
    w_t = qiw_ref[:, K_IDX_OFF:IDX_COLS].T[IDX_DIM:IDX_DIM + IDX_HEADS, :] * (IDX_HEADS ** -0.5)

    q_t = q_ref[...].astype(F32).T
    zeros_h = jnp.zeros((HEAD_DIM, TQ), BF16)
    for h in range(H):
        qh = q_t[h * HEAD_DIM:(h + 1) * HEAD_DIM, :].astype(BF16)
        qpad_s[h] = jnp.concatenate([qh, zeros_h] if h % 2 == 0 else [zeros_h, qh], axis=0)

    def scores(j):
        kx = kix_ref[pl.ds(pl.multiple_of(j * TK, TK), TK), :]
        acc = jnp.zeros((TK, TQ), F32)
        for h in range(IDX_HEADS):
            d = jnp.dot(kx, lhs_s[h], preferred_element_type=F32)
            acc = acc + w_t[h:h + 1, :] * jnp.maximum(d, 0.0)
        return acc

    def stats(st, kk, mn_of):
        fold = lambda z: z.reshape(TK // 8, 8, TQ)
        ge0, gt0, mn, mx = st
        return (ge0 + jnp.sum(fold(jnp.where(kk >= 0, 1, 0)), axis=0),
                gt0 + jnp.sum(fold(jnp.where(kk > 0, 1, 0)), axis=0),
                jnp.minimum(mn, jnp.min(fold(mn_of(kk)), axis=0)),
                jnp.maximum(mx, jnp.max(fold(kk), axis=0)))

    def score_body(j, st):
        kk = _sort_key(scores(j))
        key_s[j] = kk
        return stats(st, kk, lambda z: z)

    st = lax.fori_loop(0, qb, score_body, (jnp.zeros((8, TQ), I32), jnp.zeros((8, TQ), I32),
                                           jnp.full((8, TQ), INT_MAX, I32), jnp.full((8, TQ), INT_MIN, I32)))
    kd = _sort_key(jnp.where(rows <= cols, scores(qb), -jnp.inf))
    key_s[qb] = kd
    st = stats(st, kd, lambda z: jnp.where(z > KEY_NEG_INF, z, INT_MAX))
    n_ge0, n_gt0 = (jnp.sum(z, axis=0, keepdims=True) for z in st[:2])
    lo0 = _key_to_float(jnp.min(st[2], axis=0, keepdims=True))
    hi0 = _key_to_float(jnp.max(st[3], axis=0, keepdims=True) + 1)

    def count(pred):
        def body(j, acc):
            hit = jnp.where(pred(key_s[j]), 1, 0)
            return acc + jnp.sum(hit.reshape(TK // 8, 8, TQ), axis=0)
        acc = lax.fori_loop(0, nch, body, jnp.zeros((8, TQ), I32))
        return jnp.sum(acc, axis=0, keepdims=True)

    def write_mask(sel_of):
        def body(j, c):
            madd_s[j] = jnp.where(sel_of(key_s[j]), 0.0, NEG_BIG)
            return c
        lax.fori_loop(0, nch, body, 0)

    kf = float(ksel)
    n_adm = qb * TQ + lax.broadcasted_iota(I32, (1, TQ), 1) + 1
    trivial = n_adm <= ksel
    positive = n_gt0 > ksel
    done0 = trivial | ((n_gt0 <= ksel) & (n_ge0 >= ksel))
    thr0 = jnp.where(trivial, KEY_NEG_INF + 1, jnp.where(n_gt0 == ksel, 1, 0))
    need0 = jnp.where(n_gt0 < ksel, ksel - n_gt0, ksel)
    a0 = jnp.where(positive, 0.0, lo0)
    b0 = jnp.where(positive, hi0, 0.0)
    cb0 = jnp.where(positive, 0, n_ge0)

    def excess(cnt):
        return jnp.log(cnt.astype(F32) + 0.5) - math.log(kf + 0.5)

    fa0 = excess(jnp.where(positive, n_gt0, n_adm))
    as_int = lambda m: jnp.where(m, 1, 0)

    def any_lane(m):
        return jnp.max(as_int(m)) > 0

    def fast_cond(st):
        it, done, stuck = st[0], st[1], st[2]
        return (it < FAST_SELECT_TRIPS) & any_lane((done + stuck) == 0)

    def fast_body(st):
        it, st = st[0], st[1:]
        for _ in range(FAST_SELECT_STEPS):
            st = fast_step(st)
        return (it + 1,) + st

    def fast_step(st):
        done, stuck, thr, need, a, b, fa, fb, cb, last = st
        c = a + (b - a) * (fa / (fa - fb))
        c = jnp.where((c > a) & (c < b), c, 0.5 * a + 0.5 * b)
        inside = (c > a) & (c < b)
        ckey = _sort_key(c)
        cnt = count(lambda kk: kk >= ckey)
        fc = excess(cnt)
        hit = inside & (cnt == ksel) & (done == 0)
        thr = jnp.where(hit, ckey, thr)
        need = jnp.where(hit, ksel, need)
        done = jnp.where(hit, 1, done)
        stuck = jnp.where(inside, stuck, 1)
        up = inside & (cnt > ksel)
        dn = inside & (cnt < ksel)
        fb = jnp.where(up & (last == 1), 0.5 * fb, fb)
        fa = jnp.where(dn & (last == 0), 0.5 * fa, fa)
        a, fa = jnp.where(up, c, a), jnp.where(up, fc, fa)
        b, fb, cb = jnp.where(dn, c, b), jnp.where(dn, fc, fb), jnp.where(dn, cnt, cb)
        return done, stuck, thr, need, a, b, fa, fb, cb, jnp.where(up, 1, jnp.where(dn, 0, last))

    st = (as_int(done0), jnp.zeros((1, TQ), I32), thr0, need0,
          a0, b0, fa0, excess(cb0), cb0, jnp.full((1, TQ), -1, I32))
    blind = jnp.where((qb + 1) * TQ <= ksel, 0, FAST_SELECT_BLIND)
    st = lax.fori_loop(0, blind, lambda i, s: fast_step(s), st)
    st = lax.while_loop(fast_cond, fast_body, (jnp.int32(0),) + st)
    done, thr, need, a, b, cb = st[1], st[3], st[4], st[5], st[6], st[9]
    open_lane = done == 0

    def close_cond(st):
        return any_lane((st[1] - st[0]) > 1)

    def close_body(st):
        ak, bk, cb = st
        mid = ak + ((bk - ak) >> 1)
        cnt = count(lambda kk: kk >= mid)
        ge = cnt >= ksel
        return jnp.where(ge, mid, ak), jnp.where(ge, bk, mid), jnp.where(ge, cb, cnt)

    ak, _, cb = lax.while_loop(close_cond, close_body, (
        jnp.where(open_lane, _sort_key(a), thr), jnp.where(open_lane, _sort_key(b), thr), cb))
    thr = jnp.where(open_lane, ak, thr)
    need = jnp.where(open_lane, ksel - cb, need)
    any_tie = any_lane(open_lane | (jnp.logical_not(trivial) & (n_gt0 < ksel) & (n_ge0 > ksel)))

    @pl.when(jnp.logical_not(any_tie))
    def _():
        write_mask(lambda kk: kk >= thr)

    @pl.when(any_tie)
    def _():
        kr = lax.broadcasted_iota(I32, (TK, TK), 0)
        kc = lax.broadcasted_iota(I32, (TK, TK), 1)
        lower = jnp.where(kc <= kr, 1.0, 0.0).astype(BF16)
        need_f = need.astype(F32)

        def body(j, run):
            kk = key_s[j]
            eq = kk == thr
            eqf = jnp.where(eq, 1.0, 0.0)
            rank = run + jnp.dot(lower, eqf.astype(BF16), preferred_element_type=F32)
            tied_in = jnp.where(eq, rank, float(TOPK_MAX + 1)) <= need_f
            madd_s[j] = jnp.where(kk > thr, 0.0, jnp.where(tied_in, 0.0, NEG_BIG))
            return run + jnp.sum(eqf, axis=0, keepdims=True)
        lax.fori_loop(0, nch, body, jnp.zeros((1, TQ), F32))

    m_s[...] = jnp.full(m_s.shape, NEG_BIG, F32)
    acc_s[...] = jnp.zeros(acc_s.shape, F32)

    def attend(j, bias_of):
        ks = pl.ds(pl.multiple_of(j * TK, TK), TK)
        ma = madd_s[j]
        for h in range(H):
            kp = k_ref[ks, LANES * (h // 2):LANES * (h // 2 + 1)]
            s = jnp.dot(kp, qpad_s[h], preferred_element_type=F32) + ma
            bias = bias_of(h)
            if bias is not None:
                s = s + bias
            s_s[h] = s
            cmax_s[h] = jnp.max(s.reshape(TK // 8, 8, TQ), axis=0)
        for h in range(H):
            m_old = m_s[h]
            m_new = jnp.maximum(m_old, jnp.max(cmax_s[h], axis=0, keepdims=True))
            alpha_s[h] = jnp.exp2(m_old - m_new)
            m_s[h] = m_new
            p_s[h] = jnp.exp2(s_s[h] - m_new).astype(BF16)
        for h in range(H):
            hs = slice(h * VT_ROWS, (h + 1) * VT_ROWS)
            pv = jnp.dot(vt_ref[j, hs, :], p_s[h], preferred_element_type=F32)
            acc_s[hs, :] = alpha_s[h] * acc_s[hs, :] + pv

    def far_body(j, c):
        attend(j, lambda h: None)
        return c

    lax.fori_loop(0, jnp.maximum(qb - 1, 0), far_body, 0)

    @pl.when(qb >= 1)
    def _():
        attend(qb - 1, lambda h: bias_s[h, 1])

    attend(qb, lambda h: bias_s[h, 0])

    outs = []
    for h in range(H):
        o = acc_s[h * VT_ROWS:h * VT_ROWS + HEAD_DIM, :] / acc_s[h * VT_ROWS + HEAD_DIM:h * VT_ROWS + HEAD_DIM + 1, :]
        ms = jnp.mean(o * o, axis=0, keepdims=True)
        outs.append(o * lax.rsqrt(ms + RMS_EPS))
    out_ref[...] = (jnp.concatenate(outs, axis=0).T * g_ref[...]).astype(out_ref.dtype)


def _dsa(qk, vt, qiw, kix, rel_bias, g, B, T):
    N = qk.shape[0]
    TQ = DSA_BLOCK
    nq = T // TQ
    A = ATTN_HEADS * HEAD_DIM
    ksel = min(TOPK_MAX, T // 4)
    assert MAX_DISTANCE <= TQ + 1
    return pl.pallas_call(
        functools.partial(_dsa_kernel, T, ksel),
        grid=(B, nq),
        in_specs=[pl.BlockSpec((TQ, A), lambda b, i: (b * nq + i, 0)),
                  pl.BlockSpec((T, A), lambda b, i: (b, 1)),
                  pl.BlockSpec((nq, ATTN_HEADS * VT_ROWS, TQ), lambda b, i: (b, 0, 0)),
                  pl.BlockSpec((TQ, IDX_COLS), lambda b, i: (b * nq + i, 0)),
                  pl.BlockSpec((T, KIX_COLS), lambda b, i: (b, 0)),
                  pl.BlockSpec(memory_space=pltpu.SMEM),
                  pl.BlockSpec((1, A), lambda b, i: (0, 0))],
        out_specs=pl.BlockSpec((TQ, A), lambda b, i: (b * nq + i, 0)),
        out_shape=jax.ShapeDtypeStruct((N, A), BF16),
        scratch_shapes=[pltpu.VMEM((IDX_HEADS, KIX_COLS, TQ), BF16),
                        pltpu.VMEM((ATTN_HEADS, LANES, TQ), BF16),
                        pltpu.VMEM((nq, TQ, TQ), I32),
                        pltpu.VMEM((nq, TQ, TQ), F32),
                        pltpu.VMEM((ATTN_HEADS, 2, TQ, TQ), F32),
                        pltpu.VMEM((ATTN_HEADS, 1, TQ), F32),
                        pltpu.VMEM((ATTN_HEADS * VT_ROWS, TQ), F32),
                        pltpu.VMEM((ATTN_HEADS, TQ, TQ), F32),
                        pltpu.VMEM((ATTN_HEADS, TQ, TQ), BF16),
                        pltpu.VMEM((ATTN_HEADS, 8, TQ), F32),
                        pltpu.VMEM((ATTN_HEADS, 1, TQ), F32)],
        compiler_params=_params("arbitrary", "arbitrary"),
        name="dsa",
    )(qk, qk, vt, qiw, kix, rel_bias, g.reshape(1, A))


def _mix_mlp_kernel(final, x_ref, rw_ref, att_ref, wo_ref, gt1_ref, g_ref, sc_ref, sh_ref, gt2_ref,
                    w1_ref, w2_ref, fg_ref, o_ref, x_s, h_s, acc_s):
    j = pl.program_id(1)

    @pl.when(j == 0)
    def _():
        R = rw_ref.shape[1]
        mixed = (jnp.dot(rw_ref[...], wo_ref[:R, :], preferred_element_type=F32)
                 + jnp.dot(att_ref[...], wo_ref[R:, :], preferred_element_type=F32))
        x1 = x_ref[...] + gt1_ref[...] * mixed
        x_s[...] = x1
        h_s[...] = _norm_mod(x1, g_ref[...], sc_ref[...], sh_ref[...]).astype(BF16)
        acc_s[...] = jnp.zeros_like(acc_s)

    u = jnp.dot(h_s[...], w1_ref[...], preferred_element_type=F32)
    u = jnp.square(jnp.maximum(u, 0.0))
    acc_s[...] += jnp.dot(u.astype(BF16), w2_ref[...], preferred_element_type=F32)

    @pl.when(j == pl.num_programs(1) - 1)
    def _():
        y = x_s[...] + gt2_ref[...] * acc_s[...]
        if final:
            ms = jnp.mean(y * y, axis=-1, keepdims=True)
            y = y * lax.rsqrt(ms + RMS_EPS) * fg_ref[...]
        o_ref[...] = y


def _mix_mlp(x2d, rw, att, wo, gt1, g, sc, sh, gt2, w1, w2, final_g, final, T):
    N, D = x2d.shape
    F = w1.shape[1]
    fc = min(MLP_FF_CHUNK, F)
    tm = min(MLP_TILE, T)
    nb = T // tm
    row = lambda i, j: (i, 0)
    per_b = lambda i, j: (i // nb, 0, 0)
    const = lambda i, j: (0, 0)
    return pl.pallas_call(
        functools.partial(_mix_mlp_kernel, final),
        grid=(N // tm, F // fc),
        in_specs=[pl.BlockSpec((tm, D), row),
                  pl.BlockSpec((tm, rw.shape[1]), row),
                  pl.BlockSpec((tm, att.shape[1]), row),
                  pl.BlockSpec(wo.shape, const, pipeline_mode=pl.Buffered(1)),
                  pl.BlockSpec((None, 1, D), per_b),
                  pl.BlockSpec((1, D), const),
                  pl.BlockSpec((None, 1, D), per_b),
                  pl.BlockSpec((None, 1, D), per_b),
                  pl.BlockSpec((None, 1, D), per_b),
                  pl.BlockSpec((D, fc), lambda i, j: (0, j)),
                  pl.BlockSpec((fc, D), lambda i, j: (j, 0)),
                  pl.BlockSpec((1, D), const)],
        out_specs=pl.BlockSpec((tm, D), row),
        out_shape=jax.ShapeDtypeStruct((N, D), F32),
        scratch_shapes=[pltpu.VMEM((tm, D), F32), pltpu.VMEM((tm, D), BF16), pltpu.VMEM((tm, D), F32)],
        compiler_params=_params("arbitrary", "arbitrary"),
        name="mix_mlp",
    )(x2d, rw, att, wo, gt1, g.reshape(1, D), sc, sh, gt2, w1, w2, final_g.reshape(1, D))


def _pad_cols(w, n):
    return jnp.pad(w, ((0, 0), (0, n - w.shape[1])))


def _split_bf16(w):
    hi = w.astype(BF16)
    return hi, (w - hi.astype(F32)).astype(BF16)


def _in_weights(l, w_in, mu_lora, decay_a, iclr_a, gate_a, vres_mu, vres_a):
    D = w_in.shape[1]
    R = RWKV_HEADS * HEAD_DIM
    w = w_in[l]
    mats = [(decay_a[l], mu_lora[l, 0]), (iclr_a[l], mu_lora[l, 1]), (gate_a[l], mu_lora[l, 2])]
    if l > 0:
        mats.append((vres_a[l - 1], vres_mu[l - 1]))
    now = _pad_cols(jnp.concatenate([a * (1.0 - mu)[:, None] for a, mu in mats], axis=1), LORA_PAD)
    prev = _pad_cols(jnp.concatenate([a * mu[:, None] for a, mu in mats], axis=1), LORA_PAD)
    wq = w[:, 3 * R:4 * R] * (HEAD_DIM ** -0.5 * LOG2E)
    wa = jnp.concatenate([w[:, :3 * R], now, prev, wq, w[:, 4 * R:5 * R]], axis=1).astype(BF16)
    wv = w[:, 5 * R:6 * R].reshape(D, ATTN_HEADS, HEAD_DIM)
    wv = jnp.pad(wv, ((0, 0), (0, 0), (0, VT_ROWS - HEAD_DIM))).reshape(D, ATTN_HEADS * VT_ROWS)
    wvt = wv.T.astype(BF16)
    wbh, wbl = _split_bf16(_pad_cols(w[:, 6 * R:], IDX_COLS))
    return wa, wvt, wbh, wbl


def kernel(x, c, w_ada, b_ada, norm1_g, norm2_g, w_in, mu_rkv, mu_lora, decay_w0, decay_a, decay_b, iclr_a0, iclr_a, iclr_b, gate_a, gate_b, k_k, k_a, r_k, lnx_g, lnx_b, vres_mu, vres_v0, vres_a, vres_b, attn_out_g, rel_bias, w_out, w_mlp1, w_mlp2, final_g):
    B, T, D = x.shape
    depth = w_in.shape[0]
    mod = _adaln(c, w_ada, b_ada)
    mod = mod.reshape(depth, B, 6, 1, D).transpose(0, 2, 1, 3, 4)
    x2d = x.reshape(B * T, D)
    v_first = None
    for l in range(depth):
        sh1, sc1, gt1, sh2, sc2, gt2 = (mod[l, i] for i in range(6))
        wa, wvt, wbh, wbl = _in_weights(l, w_in, mu_lora, decay_a, iclr_a, gate_a, vres_mu, vres_a)
        rkvl, qk, vt, qiw, kix = _inproj(x2d, norm1_g[l], sc1, sh1, wa, wvt, wbh, wbl, T)
        p = dict(mu_rkv=mu_rkv[l], decay_w0=decay_w0[l], decay_b=decay_b[l], iclr_a0=iclr_a0[l],
                 iclr_b=iclr_b[l], gate_b=gate_b[l], k_k=k_k[l], k_a=k_a[l], r_k=r_k[l],
                 lnx_g=lnx_g[l], lnx_b=lnx_b[l])
        if l > 0:
            p.update(vres_v0=vres_v0[l - 1], vres_b=vres_b[l - 1])
        rw, v_first = _rwkv(rkvl, v_first, p, B, T)
        att = _dsa(qk, vt, qiw, kix, rel_bias, attn_out_g[l], B, T)
        x2d = _mix_mlp(x2d, rw, att, w_out[l].astype(BF16), gt1, norm2_g[l], sc2, sh2, gt2,
                       w_mlp1[l].astype(BF16), w_mlp2[l].astype(BF16), final_g, l == depth - 1, T)
    return x2d.reshape(B, T, D)
```

```python
import functools
import math

import numpy as np
import jax
import jax.numpy as jnp
from jax import lax
from jax.experimental import pallas as pl
from jax.experimental.pallas import tpu as pltpu

F32 = jnp.float32
BF16 = jnp.bfloat16
I32 = jnp.int32

HEAD_DIM = 64
RWKV_HEADS = 8
ATTN_HEADS = 8
IDX_HEADS = 8
IDX_DIM = 64
TOPK_MAX = 256
N_BUCKETS = 32
MAX_DISTANCE = 128
RMS_EPS = 1e-6
LNX_EPS = 64e-5

LANES = 128
VMEM_LIMIT = 56 * 1024 * 1024
RWKV_CHUNK = 64
RWKV_GROUP = 256
RWKV_TILE = 512
RWKV_UNROLL = 8
DSA_BLOCK = 256
VT_ROWS = 80
LOG2E = math.log2(math.e)
INPROJ_TILE = 512
MLP_TILE = 1024
MLP_FF_CHUNK = 1024
LORA_PAD = 384
K_IDX_OFF = IDX_HEADS * IDX_DIM
W_IDX_OFF = K_IDX_OFF + IDX_DIM
IDX_COLS = 640
KIX_COLS = 4 * IDX_DIM
NEG_BIG = -1e30
HI = lax.Precision.HIGHEST
NT = (((1,), (1,)), ((), ()))


def _bdot(a, b):
    return jnp.dot(a.astype(BF16), b.astype(BF16), preferred_element_type=F32)


def _hdot(a, b):
    return jnp.dot(a, b, precision=HI, preferred_element_type=F32)


def _params(*sem):
    return pltpu.CompilerParams(dimension_semantics=sem, vmem_limit_bytes=VMEM_LIMIT)


def _adaln_kernel(c_ref, w_ref, b_ref, o_ref):
    c = c_ref[...]
    c_act = c * jax.nn.sigmoid(c)
    o_ref[...] = _hdot(c_act, w_ref[...]) + b_ref[...]


def _adaln(c, w_ada, b_ada):
    L, D, D6 = w_ada.shape
    B = c.shape[0]
    cb = 1024
    return pl.pallas_call(
        _adaln_kernel,
        grid=(L, D6 // cb),
        in_specs=[pl.BlockSpec((B, D), lambda l, j: (0, 0)),
                  pl.BlockSpec((None, D, cb), lambda l, j: (l, 0, j)),
                  pl.BlockSpec((None, 1, cb), lambda l, j: (l, 0, j))],
        out_specs=pl.BlockSpec((None, B, cb), lambda l, j: (l, 0, j)),
        out_shape=jax.ShapeDtypeStruct((L, B, D6), F32),
        compiler_params=_params("arbitrary", "arbitrary"),
        name="adaln",
    )(c, w_ada, b_ada.reshape(L, 1, D6))


def _norm_mod(x, g, sc, sh):
    ms = jnp.mean(x * x, axis=-1, keepdims=True)
    return (x * lax.rsqrt(ms + RMS_EPS) * g) * (1.0 + sc) + sh


def _inproj_kernel(x_ref, g_ref, sc_ref, sh_ref, wa_ref, wvt_ref, wbh_ref, wbl_ref,
                   rkvl_ref, qk_ref, vt_ref, qiw_ref, kix_ref):
    h = _norm_mod(x_ref[...], g_ref[...], sc_ref[...], sh_ref[...])
    hb = h.astype(BF16)
    hl = (h - hb.astype(F32)).astype(BF16)
    pa = jnp.dot(hb, wa_ref[...], preferred_element_type=F32)
    nr = rkvl_ref.shape[1]
    rkvl_ref[...] = pa[:, :nr]
    qk_ref[...] = pa[:, nr:].astype(BF16)
    vt = lax.dot_general(wvt_ref[...], hb, NT, preferred_element_type=F32)
    ones_row = lax.broadcasted_iota(I32, vt.shape, 0) % VT_ROWS == HEAD_DIM
    vt = jnp.where(ones_row, 1.0, vt).astype(BF16)
    for j in range(vt_ref.shape[0]):
        vt_ref[j] = vt[:, j * DSA_BLOCK:(j + 1) * DSA_BLOCK]
    pb = (jnp.dot(hb, wbh_ref[...], preferred_element_type=F32)
          + jnp.dot(hb, wbl_ref[...], preferred_element_type=F32)
          + jnp.dot(hl, wbh_ref[...], preferred_element_type=F32))
    qiw_ref[...] = pb
    ki = pb[:, K_IDX_OFF:W_IDX_OFF]
    kh = ki.astype(BF16)
    kl = (ki - kh.astype(F32)).astype(BF16)
    kix_ref[...] = jnp.concatenate([kh, kl, kh, jnp.zeros_like(kh)], axis=-1)


def _inproj(x2d, g, sc, sh, wa, wvt, wbh, wbl, T):
    N, D = x2d.shape
    nblk = INPROJ_TILE // DSA_BLOCK
    tm = INPROJ_TILE
    na = wa.shape[1]
    A = wvt.shape[0]
    nr = 3 * RWKV_HEADS * HEAD_DIM + 2 * LORA_PAD
    assert T % tm == 0 and (wa.shape[1] - nr) % LANES == 0
    nb = T // tm
    row = lambda i: (i, 0)
    per_b = lambda i: (i // nb, 0, 0)
    const = lambda i: (0, 0)
    once = pl.Buffered(1)
    return pl.pallas_call(
        _inproj_kernel,
        grid=(N // tm,),
        in_specs=[pl.BlockSpec((tm, D), row),
                  pl.BlockSpec((1, D), const),
                  pl.BlockSpec((None, 1, D), per_b),
                  pl.BlockSpec((None, 1, D), per_b),
                  pl.BlockSpec((D, na), const, pipeline_mode=once),
                  pl.BlockSpec((A, D), const, pipeline_mode=once),
                  pl.BlockSpec((D, IDX_COLS), const, pipeline_mode=once),
                  pl.BlockSpec((D, IDX_COLS), const, pipeline_mode=once)],
        out_specs=[pl.BlockSpec((tm, nr), row),
                   pl.BlockSpec((tm, na - nr), row),
                   pl.BlockSpec((nblk, A, DSA_BLOCK), lambda i: (i, 0, 0)),
                   pl.BlockSpec((tm, IDX_COLS), row),
                   pl.BlockSpec((tm, KIX_COLS), row)],
        out_shape=[jax.ShapeDtypeStruct((N, nr), F32),
                   jax.ShapeDtypeStruct((N, na - nr), BF16),
                   jax.ShapeDtypeStruct((N // DSA_BLOCK, A, DSA_BLOCK), BF16),
                   jax.ShapeDtypeStruct((N, IDX_COLS), F32),
                   jax.ShapeDtypeStruct((N, KIX_COLS), BF16)],
        compiler_params=_params("arbitrary"),
        name="inproj",
    )(x2d, g.reshape(1, D), sc, sh, wa, wvt, wbh, wbl)


def _head_ones(n):
    r = lax.broadcasted_iota(I32, (n, n), 0) // HEAD_DIM
    c = lax.broadcasted_iota(I32, (n, n), 1) // HEAD_DIM
    return jnp.where(r == c, 1.0, 0.0).astype(F32)


def _split(x):
    hi = x.astype(BF16)
    return hi, (x - hi.astype(F32)).astype(BF16)


def _split3(x):
    hi = x.astype(BF16)
    r = x - hi.astype(F32)
    mid = r.astype(BF16)
    return hi, mid, (r - mid.astype(F32)).astype(BF16)


def _block_diag(w, blocks):
    w = w.astype(BF16)
    return jnp.concatenate([jnp.where(m, w, jnp.zeros_like(w)) for m in blocks], axis=0)


def _block_diag_t(w, blocks):
    return jnp.concatenate([jnp.where(m, w, 0.0) for m in blocks], axis=0).T


def _fold_blocks(w):
    n = w.shape[1] // HEAD_DIM
    out = w[:HEAD_DIM]
    for h in range(1, n):
        out = out + w[h * HEAD_DIM:(h + 1) * HEAD_DIM]
    return out


def _mm(a, w):
    return jnp.dot(a.astype(BF16), w, preferred_element_type=F32)


def _rwkv_kernel(has_vres, TT, *refs):
    C = RWKV_CHUNK
    R = RWKV_HEADS * HEAD_DIM
    G = RWKV_GROUP
    NG = R // G
    NC = TT // C
    if has_vres:
        (rkvl_ref, vf_ref, mu_ref, w0_ref, db_ref, a0_ref, ib_ref, gb_ref, kk_ref, ka_ref,
         rk_ref, lg_ref, lb_ref, v0_ref, vb_ref, out_ref,
         prev_ref, S_ref, r_s, k_s, v_s, kk_s, b_s, lw_s, cum_s, y_s, q_s, y0_s, m_s, g_s) = refs
    else:
        (rkvl_ref, mu_ref, w0_ref, db_ref, a0_ref, ib_ref, gb_ref, kk_ref, ka_ref,
         rk_ref, lg_ref, lb_ref, out_ref, vfo_ref,
         prev_ref, S_ref, r_s, k_s, v_s, kk_s, b_s, lw_s, cum_s, y_s, q_s, y0_s, m_s, g_s) = refs

    @pl.when(pl.program_id(1) == 0)
    def _():
        prev_ref[...] = jnp.zeros_like(prev_ref)
        S_ref[...] = jnp.zeros_like(S_ref)

    row0 = lax.broadcasted_iota(I32, (TT, 1), 0) == 0
    prev = prev_ref[...]

    def shift(z, p):
        return jnp.where(row0, p, pltpu.roll(z, 1, 0))

    rkv = rkvl_ref[:, :3 * R]
    p1 = rkvl_ref[:, 3 * R:3 * R + LORA_PAD]
    p2 = rkvl_ref[:, 3 * R + LORA_PAD:]
    rkv_new = rkv + (shift(rkv, prev[:, :3 * R]) - rkv) * mu_ref[...]
    lora = p1 + shift(p2, prev[:, 3 * R:])
    prev_ref[...] = jnp.concatenate([rkv[TT - 1:TT, :], p2[TT - 1:TT, :]], axis=-1)

    r = rkv_new[:, :R]
    k = rkv_new[:, R:2 * R]
    v = rkv_new[:, 2 * R:]
    ones_h = _head_ones(G).astype(BF16)

    def head_sum(x):
        hi, lo = _split(x)
        return jnp.concatenate(
            [jnp.dot(hi[:, gi * G:(gi + 1) * G], ones_h, preferred_element_type=F32)
             + jnp.dot(lo[:, gi * G:(gi + 1) * G], ones_h, preferred_element_type=F32) for gi in range(NG)], axis=1)

    o_a = db_ref.shape[0]
    o_g = o_a + ib_ref.shape[0]
    o_v = o_g + gb_ref.shape[0]
    wlog = w0_ref[...] + _bdot(jnp.tanh(lora[:, :o_a]), db_ref[...])
    z = -wlog
    wlog = -(jnp.maximum(z, 0.0) + jnp.log(1.0 + jnp.exp(-jnp.abs(z)))) - 0.5
    lw = -jnp.exp(wlog)
    lw_s[...] = lw
    slab = min(TT, 4 * C)
    tr = lax.broadcasted_iota(I32, (slab, slab), 0)
    tc = lax.broadcasted_iota(I32, (slab, slab), 1)
    tri = jnp.where((tc <= tr) & (tr // C == tc // C), 1.0, 0.0).astype(BF16)
    parts = _split3(lw)
    for s0 in range(0, TT, slab):
        cum_s[s0:s0 + slab, :] = sum(jnp.dot(tri, part[s0:s0 + slab], preferred_element_type=F32) for part in parts)
    a = jax.nn.sigmoid(a0_ref[...] + _bdot(lora[:, o_a:o_g], ib_ref[...]))
    g = _bdot(jax.nn.sigmoid(lora[:, o_g:o_v]), gb_ref[...])
    kkr = k * kk_ref[...]
    kk = kkr * lax.rsqrt(jnp.maximum(head_sum(kkr * kkr), 1e-24))
    k = k * (1.0 + (a - 1.0) * ka_ref[...])
    if has_vres:
        v = v + (vf_ref[...] - v) * jax.nn.sigmoid(
            v0_ref[...] + _bdot(lora[:, o_v:o_v + vb_ref.shape[0]], vb_ref[...]))
    else:
        vfo_ref[...] = v
    r_s[...] = r
    k_s[...] = k
    v_s[...] = v
    kk_s[...] = kk
    b_s[...] = kk * a

    lane_g = lax.broadcasted_iota(I32, (C, RWKV_GROUP), 1)
    row_g = lax.broadcasted_iota(I32, (C, RWKV_GROUP), 0)
    blocks = [lane_g // HEAD_DIM == h for h in range(RWKV_GROUP // HEAD_DIM)]
    local = lane_g % HEAD_DIM
    strict = local < row_g
    incl = local <= row_g
    diag = local == row_g
    eye_cat = jnp.where(diag, 1.0, 0.0).astype(F32)
    levels = int(math.log2(C)) - 1

    def local_chunk(it, carry):
        chains = [(u, slice(gi * G, (gi + 1) * G)) for u in range(RWKV_UNROLL) for gi in range(NG)]
        each = lambda f: [f(i) for i in range(len(chains))]
        rd, kkd, kt, bt, kc, bc, vv, plast = ([] for _ in range(8))
        for u in range(RWKV_UNROLL):
            sl = pl.ds(pl.multiple_of((it * RWKV_UNROLL + u) * C, C), C)
            lw = lw_s[sl, :]
            cum = cum_s[sl, :]
            cl = cum[C - 1:C, :]
            pinv = jnp.exp(-cum)
            pc = jnp.exp(cl - cum)
            k_a = k_s[sl, :]
            b_a = b_s[sl, :]
            full = (r_s[sl, :] * jnp.exp(cum), kk_s[sl, :] * jnp.exp(cum - lw), k_a * pinv, b_a * pinv,
                    k_a * pc, b_a * pc, v_s[sl, :], jnp.exp(cl))
            for dst, x in zip((rd, kkd, kt, bt, kc, bc, vv, plast), full):
                dst.extend(x[:, gs] for (uu, gs) in chains if uu == u)
        kt_w = each(lambda i: _block_diag_t(kt[i], blocks).astype(BF16))
        bt_w = each(lambda i: _block_diag_t(bt[i], blocks).astype(BF16))
        bc_t = each(lambda i: _fold_blocks(_block_diag_t(bc[i], blocks)))
        kc_t = each(lambda i: _fold_blocks(_block_diag_t(kc[i], blocks)))
        rr = each(lambda i: _mm(jnp.concatenate([kkd[i], rd[i]], axis=0),
                                jnp.concatenate([kt_w[i], bt_w[i]], axis=1)))
        akk = each(lambda i: jnp.where(strict, rr[i][:C, :G], 0.0))
        ark = each(lambda i: jnp.where(incl, rr[i][C:, :G], 0.0))
        arb = each(lambda i: jnp.where(incl, rr[i][C:, G:], 0.0))
        npow = each(lambda i: jnp.where(strict, -rr[i][:C, G:], 0.0))
        x = each(lambda i: eye_cat + npow[i])
        npow = each(lambda i: _mm(npow[i], _block_diag(npow[i], blocks)))
        for _ in range(levels - 1):
            rr = each(lambda i: _mm(jnp.concatenate([npow[i], x[i]], axis=0), _block_diag(npow[i], blocks)))
            npow = each(lambda i: rr[i][:C])
            x = each(lambda i: x[i] + rr[i][C:])
        rr = each(lambda i: _mm(x[i], _block_diag(npow[i], blocks)))
        x = each(lambda i: x[i] + rr[i])
        rv = each(lambda i: _mm(jnp.concatenate([akk[i], ark[i], kc_t[i]], axis=0), _block_diag(vv[i], blocks)))
        ru = each(lambda i: _mm(x[i], jnp.concatenate([_block_diag(kkd[i], blocks),
                                                       _block_diag(rv[i][:C], blocks)], axis=1)))
        rr = each(lambda i: _mm(jnp.concatenate([arb[i], bc_t[i]], axis=0),
                                jnp.concatenate([_block_diag(ru[i][:, :G], blocks),
                                                 _block_diag(ru[i][:, G:], blocks)], axis=1)))
        for i, (u, gs) in enumerate(chains):
            c = it * RWKV_UNROLL + u
            q_s[c, :, gs] = rd[i] - rr[i][:C, :G]
            y0_s[c, :, gs] = rv[i][C:2 * C] - rr[i][:C, G:]
            dterm = jnp.where(diag, jnp.broadcast_to(plast[i], (C, G)), 0.0)
            m_s[c, :, gs] = dterm - rr[i][C:, :G]
            g_s[c, :, gs] = rv[i][2 * C:] - rr[i][C:, G:]
        return carry

    lax.fori_loop(0, NC // RWKV_UNROLL, local_chunk, 0)

    def scan_chunk(c, carry):
        sl = pl.ds(pl.multiple_of(c * C, C), C)
        for gi in range(NG):
            gs = slice(gi * RWKV_GROUP, (gi + 1) * RWKV_GROUP)
            rr = _mm(jnp.concatenate([q_s[c, :, gs], m_s[c, :, gs]], axis=0), _block_diag(S_ref[:, gs], blocks))
            y_s[sl, gs] = rr[:C] + y0_s[c, :, gs]
            S_ref[:, gs] = rr[C:] + g_s[c, :, gs]
        return carry

    lax.fori_loop(0, NC, scan_chunk, 0)

    y = y_s[...]
    inv_n = 1.0 / HEAD_DIM
    mean = head_sum(y) * inv_n
    yc = y - mean
    var = head_sum(yc * yc) * inv_n
    yn = yc * lax.rsqrt(var + LNX_EPS) * lg_ref[...] + lb_ref[...]
    bonus = head_sum(r_s[...] * k_s[...] * rk_ref[...]) * v_s[...]
    out_ref[...] = ((yn + bonus) * g).astype(out_ref.dtype)


def _rwkv(rkvl, v_first, p, B, T):
    N, nc = rkvl.shape
    R = RWKV_HEADS * HEAD_DIM
    TT = min(RWKV_TILE, T)
    nt = T // TT
    NC = TT // RWKV_CHUNK
    has_vres = v_first is not None
    row = lambda b, i: (b * nt + i, 0)
    const = lambda b, i: (0, 0)
    vec = lambda a: a.reshape(1, -1)
    ins = [rkvl]
    specs = [pl.BlockSpec((TT, nc), row)]
    if has_vres:
        ins.append(v_first)
        specs.append(pl.BlockSpec((TT, R), row))
    small = [vec(p["mu_rkv"]), vec(p["decay_w0"]), p["decay_b"].astype(BF16), vec(p["iclr_a0"]),
             p["iclr_b"].astype(BF16), p["gate_b"].astype(BF16), vec(p["k_k"]), vec(p["k_a"]),
             vec(p["r_k"]), vec(p["lnx_g"]), vec(p["lnx_b"])]
    if has_vres:
        small += [vec(p["vres_v0"]), p["vres_b"].astype(BF16)]
    ins += small
    specs += [pl.BlockSpec(a.shape, const) for a in small]
    out_shape = [jax.ShapeDtypeStruct((N, R), BF16)]
    out_specs = [pl.BlockSpec((TT, R), row)]
    if not has_vres:
        out_shape.append(jax.ShapeDtypeStruct((N, R), F32))
        out_specs.append(pl.BlockSpec((TT, R), row))
    scratch = [pltpu.VMEM((1, 3 * R + LORA_PAD), F32),
               pltpu.VMEM((HEAD_DIM, R), F32)]
    scratch += [pltpu.VMEM((TT, R), F32) for _ in range(8)]
    scratch += [pltpu.VMEM((NC, RWKV_CHUNK, R), F32) for _ in range(4)]
    res = pl.pallas_call(
        functools.partial(_rwkv_kernel, has_vres, TT),
        grid=(B, nt),
        in_specs=specs,
        out_specs=out_specs,
        out_shape=out_shape,
        scratch_shapes=scratch,
        compiler_params=_params("arbitrary", "arbitrary"),
        name="rwkv",
    )(*ins)
    if has_vres:
        return res[0], v_first
    return res[0], res[1]


def _bucket_boundaries():
    max_exact = N_BUCKETS // 2
    d = np.arange(0, 4 * MAX_DISTANCE, dtype=np.int64)
    nf = np.maximum(d, 1).astype(np.float32)
    large = max_exact + (np.log(nf / np.float32(max_exact)) / np.float32(math.log(MAX_DISTANCE / max_exact))
                         * np.float32(N_BUCKETS - max_exact)).astype(np.int32)
    large = np.minimum(large, N_BUCKETS - 1)
    bucket = np.where(d < max_exact, d, large)
    return [int(np.argmax(bucket >= j)) for j in range(max_exact + 1, N_BUCKETS)]


_BUCKET_STARTS = _bucket_boundaries()
KEY_NEG_INF = int(np.int32(np.array(-np.inf, np.float32).view(np.int32)) ^ np.int32(0x7FFFFFFF))
INT_MIN = -2 ** 31
INT_MAX = 2 ** 31 - 1
FAST_SELECT_BLIND = 14
FAST_SELECT_STEPS = 2
FAST_SELECT_TRIPS = 10
ACC_ROWS = 32


def _sort_key(s):
    bits = pltpu.bitcast(s, I32)
    bits = jnp.where(bits == INT_MIN, 0, bits)
    return bits ^ ((bits >> 31) & jnp.int32(0x7FFFFFFF))


def _key_to_float(k):
    return pltpu.bitcast(k ^ ((k >> 31) & jnp.int32(0x7FFFFFFF)), F32)


def _dsa_kernel(T, ksel, q_ref, k_ref, vt_ref, qiw_ref, kix_ref, relb_ref, g_ref, out_ref,
                lhs_s, qpad_s, key_s, madd_s, bias_s, m_s, acc_s, s_s, p_s, cmax_s, alpha_s):
    TQ = TK = DSA_BLOCK
    H = ATTN_HEADS
    qb = pl.program_id(1)
    nch = qb + 1
    rows = lax.broadcasted_iota(I32, (TK, TQ), 0)
    cols = lax.broadcasted_iota(I32, (TK, TQ), 1)

    @pl.when((pl.program_id(0) == 0) & (qb == 0))
    def _():
        max_exact = N_BUCKETS // 2
        for off in range(2):
            d = cols - rows + off * TK
            bucket = jnp.where(d < max_exact, jnp.maximum(d, 0), max_exact)
            for start in _BUCKET_STARTS:
                bucket = bucket + jnp.where(d >= start, 1, 0)
            for h in range(H):
                far = relb_ref[N_BUCKETS - 1, h]
                tile = jnp.zeros((TK, TQ), F32)
                for bk in range(N_BUCKETS - 1):
                    tile = jnp.where(bucket == bk, (relb_ref[bk, h] - far) * LOG2E, tile)
                bias_s[h, off] = tile

    qi_t = (qiw_ref[:, :IDX_HEADS * IDX_DIM] * (IDX_DIM ** -0.5)).T
    for h in range(IDX_HEADS):
        qh = qi_t[h * IDX_DIM:(h + 1) * IDX_DIM, :]
        hi = qh.astype(BF16)
        lo = (qh - hi.astype(F32)).astype(BF16)
        lhs_s[h] = jnp.concatenate([hi, hi, lo, jnp.zeros_like(hi)], axis=0)
    w_t = qiw_ref[:, K_IDX_OFF:IDX_COLS].T[IDX_DIM:IDX_DIM + IDX_HEADS, :] * (IDX_HEADS ** -0.5)

    q_t = q_ref[...].astype(F32).T
    zeros_h = jnp.zeros((HEAD_DIM, TQ), BF16)
    for h in range(H):
        qh = q_t[h * HEAD_DIM:(h + 1) * HEAD_DIM, :].astype(BF16)
        qpad_s[h] = jnp.concatenate([qh, zeros_h] if h % 2 == 0 else [zeros_h, qh], axis=0)

    def scores(j):
        kx = kix_ref[pl.ds(pl.multiple_of(j * TK, TK), TK), :]
        acc = jnp.zeros((TK, TQ), F32)
        for h in range(IDX_HEADS):
            d = jnp.dot(kx, lhs_s[h], preferred_element_type=F32)
            acc = acc + w_t[h:h + 1, :] * jnp.maximum(d, 0.0)
        return acc

    def stats(st, kk, mn_of):
        fold = lambda z: z.reshape(TK // ACC_ROWS, ACC_ROWS, TQ)
        neg, pos, mn, mx = st
        return (neg + jnp.sum(fold(kk >> 31), axis=0),
                pos + jnp.sum(fold((-kk) >> 31), axis=0),
                jnp.minimum(mn, jnp.min(fold(mn_of(kk)), axis=0)),
                jnp.maximum(mx, jnp.max(fold(kk), axis=0)))

    def score_body(j, st):
        kk = _sort_key(scores(j))
        key_s[j] = kk
        return stats(st, kk, lambda z: z)

    st = lax.fori_loop(0, qb, score_body,
                       (jnp.zeros((ACC_ROWS, TQ), I32), jnp.zeros((ACC_ROWS, TQ), I32),
                        jnp.full((ACC_ROWS, TQ), INT_MAX, I32), jnp.full((ACC_ROWS, TQ), INT_MIN, I32)))
    kd = _sort_key(jnp.where(rows <= cols, scores(qb), -jnp.inf))
    key_s[qb] = kd
    st = stats(st, kd, lambda z: jnp.where(z > KEY_NEG_INF, z, INT_MAX))
    n_ge0 = nch * TK + jnp.sum(st[0], axis=0, keepdims=True)
    n_gt0 = -jnp.sum(st[1], axis=0, keepdims=True)
    lo0 = _key_to_float(jnp.min(st[2], axis=0, keepdims=True))
    hi0 = _key_to_float(jnp.max(st[3], axis=0, keepdims=True) + 1)

    def count(pred):
        def body(j, acc):
            hit = jnp.where(pred(key_s[j]), 1, 0)
            return acc + jnp.sum(hit.reshape(TK // ACC_ROWS, ACC_ROWS, TQ), axis=0)
        acc = lax.fori_loop(0, nch, body, jnp.zeros((ACC_ROWS, TQ), I32))
        return jnp.sum(acc, axis=0, keepdims=True)

    def write_mask(sel_of):
        def body(j, c):
            madd_s[j] = jnp.where(sel_of(key_s[j]), 0.0, NEG_BIG)
            return c
        lax.fori_loop(0, nch, body, 0)

    kf = float(ksel)
    n_adm = qb * TQ + lax.broadcasted_iota(I32, (1, TQ), 1) + 1
    trivial = n_adm <= ksel
    positive = n_gt0 > ksel
    done0 = trivial | ((n_gt0 <= ksel) & (n_ge0 >= ksel))
    thr0 = jnp.where(trivial, KEY_NEG_INF + 1, jnp.where(n_gt0 == ksel, 1, 0))
    need0 = jnp.where(n_gt0 < ksel, ksel - n_gt0, ksel)
    a0 = jnp.where(positive, 0.0, lo0)
    b0 = jnp.where(positive, hi0, 0.0)
    cb0 = jnp.where(positive, 0, n_ge0)

    def excess(cnt):
        return jnp.log(cnt.astype(F32) + 0.5) - math.log(kf + 0.5)

    fa0 = excess(jnp.where(positive, n_gt0, n_adm))
    as_int = lambda m: jnp.where(m, 1, 0)

    def any_lane(m):
        return jnp.max(as_int(m)) > 0

    def fast_cond(st):
        it, done, stuck = st[0], st[1], st[2]
        return (it < FAST_SELECT_TRIPS) & any_lane((done + stuck) == 0)

    def fast_body(st):
        it, st = st[0], st[1:]
        for _ in range(FAST_SELECT_STEPS):
            st = fast_step(st)
        return (it + 1,) + st

    def fast_step(st):
        done, stuck, thr, need, a, b, fa, fb, cb, last = st
        c = a + (b - a) * (fa / (fa - fb))
        c = jnp.where((c > a) & (c < b), c, 0.5 * a + 0.5 * b)
        inside = (c > a) & (c < b)
        ckey = _sort_key(c)
        cnt = count(lambda kk: kk >= ckey)
        fc = excess(cnt)
        hit = inside & (cnt == ksel) & (done == 0)
        thr = jnp.where(hit, ckey, thr)
        need = jnp.where(hit, ksel, need)
        done = jnp.where(hit, 1, done)
        stuck = jnp.where(inside, stuck, 1)
        up = inside & (cnt > ksel)
        dn = inside & (cnt < ksel)
        fb = jnp.where(up & (last == 1), 0.5 * fb, fb)
        fa = jnp.where(dn & (last == 0), 0.5 * fa, fa)
        a, fa = jnp.where(up, c, a), jnp.where(up, fc, fa)
        b, fb, cb = jnp.where(dn, c, b), jnp.where(dn, fc, fb), jnp.where(dn, cnt, cb)
        return done, stuck, thr, need, a, b, fa, fb, cb, jnp.where(up, 1, jnp.where(dn, 0, last))

    st = (as_int(done0), jnp.zeros((1, TQ), I32), thr0, need0,
          a0, b0, fa0, excess(cb0), cb0, jnp.full((1, TQ), -1, I32))
    blind = jnp.where((qb + 1) * TQ <= ksel, 0, FAST_SELECT_BLIND)
    st = lax.fori_loop(0, blind, lambda i, s: fast_step(s), st)
    st = lax.while_loop(fast_cond, fast_body, (jnp.int32(0),) + st)
    done, thr, need, a, b, cb = st[1], st[3], st[4], st[5], st[6], st[9]
    open_lane = done == 0

    def close_cond(st):
        return any_lane((st[1] - st[0]) > 1)

    def close_body(st):
        ak, bk, cb = st
        mid = ak + ((bk - ak) >> 1)
        cnt = count(lambda kk: kk >= mid)
        ge = cnt >= ksel
        return jnp.where(ge, mid, ak), jnp.where(ge, bk, mid), jnp.where(ge, cb, cnt)

    ak, _, cb = lax.while_loop(close_cond, close_body, (
        jnp.where(open_lane, _sort_key(a), thr), jnp.where(open_lane, _sort_key(b), thr), cb))
    thr = jnp.where(open_lane, ak, thr)
    need = jnp.where(open_lane, ksel - cb, need)
    any_tie = any_lane(open_lane | (jnp.logical_not(trivial) & (n_gt0 < ksel) & (n_ge0 > ksel)))

    @pl.when(jnp.logical_not(any_tie))
    def _():
        write_mask(lambda kk: kk >= thr)

    @pl.when(any_tie)
    def _():
        kr = lax.broadcasted_iota(I32, (TK, TK), 0)
        kc = lax.broadcasted_iota(I32, (TK, TK), 1)
        lower = jnp.where(kc <= kr, 1.0, 0.0).astype(BF16)
        need_f = need.astype(F32)

        def body(j, run):
            kk = key_s[j]
            eq = kk == thr
            eqf = jnp.where(eq, 1.0, 0.0)
            rank = run + jnp.dot(lower, eqf.astype(BF16), preferred_element_type=F32)
            tied_in = jnp.where(eq, rank, float(TOPK_MAX + 1)) <= need_f
            madd_s[j] = jnp.where(kk > thr, 0.0, jnp.where(tied_in, 0.0, NEG_BIG))
            return run + jnp.sum(eqf, axis=0, keepdims=True)
        lax.fori_loop(0, nch, body, jnp.zeros((1, TQ), F32))

    m_s[...] = jnp.full(m_s.shape, NEG_BIG, F32)
    acc_s[...] = jnp.zeros(acc_s.shape, F32)

    def attend(j, bias_of):
        ks = pl.ds(pl.multiple_of(j * TK, TK), TK)
        ma = madd_s[j]
        for h in range(H):
            kp = k_ref[ks, LANES * (h // 2):LANES * (h // 2 + 1)]
            s = jnp.dot(kp, qpad_s[h], preferred_element_type=F32) + ma
            bias = bias_of(h)
            if bias is not None:
                s = s + bias
            s_s[h] = s
            cmax_s[h] = jnp.max(s.reshape(TK // 8, 8, TQ), axis=0)
        for h in range(H):
            m_old = m_s[h]
            m_new = jnp.maximum(m_old, jnp.max(cmax_s[h], axis=0, keepdims=True))
            alpha_s[h] = jnp.exp2(m_old - m_new)
            m_s[h] = m_new
            p_s[h] = jnp.exp2(s_s[h] - m_new).astype(BF16)
        for h in range(H):
            hs = slice(h * VT_ROWS, (h + 1) * VT_ROWS)
            pv = jnp.dot(vt_ref[j, hs, :], p_s[h], preferred_element_type=F32)
            acc_s[hs, :] = alpha_s[h] * acc_s[hs, :] + pv

    def far_body(j, c):
        attend(j, lambda h: None)
        return c

    lax.fori_loop(0, jnp.maximum(qb - 1, 0), far_body, 0)

    @pl.when(qb >= 1)
    def _():
        attend(qb - 1, lambda h: bias_s[h, 1])

    attend(qb, lambda h: bias_s[h, 0])

    outs = []
    for h in range(H):
        o = acc_s[h * VT_ROWS:h * VT_ROWS + HEAD_DIM, :] / acc_s[h * VT_ROWS + HEAD_DIM:h * VT_ROWS + HEAD_DIM + 1, :]
        ms = jnp.mean(o * o, axis=0, keepdims=True)
        outs.append(o * lax.rsqrt(ms + RMS_EPS))
    out_ref[...] = (jnp.concatenate(outs, axis=0).T * g_ref[...]).astype(out_ref.dtype)


def _dsa(qk, vt, qiw, kix, rel_bias, g, B, T):
    N = qk.shape[0]
    TQ = DSA_BLOCK
    nq = T // TQ
    A = ATTN_HEADS * HEAD_DIM
    ksel = min(TOPK_MAX, T // 4)
    assert MAX_DISTANCE <= TQ + 1
    return pl.pallas_call(
        functools.partial(_dsa_kernel, T, ksel),
        grid=(B, nq),
        in_specs=[pl.BlockSpec((TQ, A), lambda b, i: (b * nq + i, 0)),
                  pl.BlockSpec((T, A), lambda b, i: (b, 1)),
                  pl.BlockSpec((nq, ATTN_HEADS * VT_ROWS, TQ), lambda b, i: (b, 0, 0)),
                  pl.BlockSpec((TQ, IDX_COLS), lambda b, i: (b * nq + i, 0)),
                  pl.BlockSpec((T, KIX_COLS), lambda b, i: (b, 0)),
                  pl.BlockSpec(memory_space=pltpu.SMEM),
                  pl.BlockSpec((1, A), lambda b, i: (0, 0))],
        out_specs=pl.BlockSpec((TQ, A), lambda b, i: (b * nq + i, 0)),
        out_shape=jax.ShapeDtypeStruct((N, A), BF16),
        scratch_shapes=[pltpu.VMEM((IDX_HEADS, KIX_COLS, TQ), BF16),
                        pltpu.VMEM((ATTN_HEADS, LANES, TQ), BF16),
                        pltpu.VMEM((nq, TQ, TQ), I32),
                        pltpu.VMEM((nq, TQ, TQ), F32),
                        pltpu.VMEM((ATTN_HEADS, 2, TQ, TQ), F32),
                        pltpu.VMEM((ATTN_HEADS, 1, TQ), F32),
                        pltpu.VMEM((ATTN_HEADS * VT_ROWS, TQ), F32),
                        pltpu.VMEM((ATTN_HEADS, TQ, TQ), F32),
                        pltpu.VMEM((ATTN_HEADS, TQ, TQ), BF16),
                        pltpu.VMEM((ATTN_HEADS, 8, TQ), F32),
                        pltpu.VMEM((ATTN_HEADS, 1, TQ), F32)],
        compiler_params=_params("arbitrary", "arbitrary"),
        name="dsa",
    )(qk, qk, vt, qiw, kix, rel_bias, g.reshape(1, A))


def _mix_mlp_kernel(final, x_ref, rw_ref, att_ref, wo_ref, gt1_ref, g_ref, sc_ref, sh_ref, gt2_ref,
                    w1_ref, w2_ref, fg_ref, o_ref, x_s, h_s, acc_s):
    j = pl.program_id(1)

    @pl.when(j == 0)
    def _():
        R = rw_ref.shape[1]
        mixed = (jnp.dot(rw_ref[...], wo_ref[:R, :], preferred_element_type=F32)
                 + jnp.dot(att_ref[...], wo_ref[R:, :], preferred_element_type=F32))
        x1 = x_ref[...] + gt1_ref[...] * mixed
        x_s[...] = x1
        h_s[...] = _norm_mod(x1, g_ref[...], sc_ref[...], sh_ref[...]).astype(BF16)
        acc_s[...] = jnp.zeros_like(acc_s)

    u = jnp.dot(h_s[...], w1_ref[...], preferred_element_type=F32)
    u = jnp.square(jnp.maximum(u, 0.0))
    acc_s[...] += jnp.dot(u.astype(BF16), w2_ref[...], preferred_element_type=F32)

    @pl.when(j == pl.num_programs(1) - 1)
    def _():
        y = x_s[...] + gt2_ref[...] * acc_s[...]
        if final:
            ms = jnp.mean(y * y, axis=-1, keepdims=True)
            y = y * lax.rsqrt(ms + RMS_EPS) * fg_ref[...]
        o_ref[...] = y


def _mix_mlp(x2d, rw, att, wo, gt1, g, sc, sh, gt2, w1, w2, final_g, final, T):
    N, D = x2d.shape
    F = w1.shape[1]
    fc = min(MLP_FF_CHUNK, F)
    tm = min(MLP_TILE, T)
    nb = T // tm
    row = lambda i, j: (i, 0)
    per_b = lambda i, j: (i // nb, 0, 0)
    const = lambda i, j: (0, 0)
    return pl.pallas_call(
        functools.partial(_mix_mlp_kernel, final),
        grid=(N // tm, F // fc),
        in_specs=[pl.BlockSpec((tm, D), row),
                  pl.BlockSpec((tm, rw.shape[1]), row),
                  pl.BlockSpec((tm, att.shape[1]), row),
                  pl.BlockSpec(wo.shape, const, pipeline_mode=pl.Buffered(1)),
                  pl.BlockSpec((None, 1, D), per_b),
                  pl.BlockSpec((1, D), const),
                  pl.BlockSpec((None, 1, D), per_b),
                  pl.BlockSpec((None, 1, D), per_b),
                  pl.BlockSpec((None, 1, D), per_b),
                  pl.BlockSpec((D, fc), lambda i, j: (0, j)),
                  pl.BlockSpec((fc, D), lambda i, j: (j, 0)),
                  pl.BlockSpec((1, D), const)],
        out_specs=pl.BlockSpec((tm, D), row),
        out_shape=jax.ShapeDtypeStruct((N, D), F32),
        scratch_shapes=[pltpu.VMEM((tm, D), F32), pltpu.VMEM((tm, D), BF16), pltpu.VMEM((tm, D), F32)],
        compiler_params=_params("arbitrary", "arbitrary"),
        name="mix_mlp",
    )(x2d, rw, att, wo, gt1, g.reshape(1, D), sc, sh, gt2, w1, w2, final_g.reshape(1, D))


def _pad_cols(w, n):
    return jnp.pad(w, ((0, 0), (0, n - w.shape[1])))


def _split_bf16(w):
    hi = w.astype(BF16)
    return hi, (w - hi.astype(F32)).astype(BF16)


def _in_weights(l, w_in, mu_lora, decay_a, iclr_a, gate_a, vres_mu, vres_a):
    D = w_in.shape[1]
    R = RWKV_HEADS * HEAD_DIM
    w = w_in[l]
    mats = [(decay_a[l], mu_lora[l, 0]), (iclr_a[l], mu_lora[l, 1]), (gate_a[l], mu_lora[l, 2])]
    if l > 0:
        mats.append((vres_a[l - 1], vres_mu[l - 1]))
    now = _pad_cols(jnp.concatenate([a * (1.0 - mu)[:, None] for a, mu in mats], axis=1), LORA_PAD)
    prev = _pad_cols(jnp.concatenate([a * mu[:, None] for a, mu in mats], axis=1), LORA_PAD)
    wq = w[:, 3 * R:4 * R] * (HEAD_DIM ** -0.5 * LOG2E)
    wa = jnp.concatenate([w[:, :3 * R], now, prev, wq, w[:, 4 * R:5 * R]], axis=1).astype(BF16)
    wv = w[:, 5 * R:6 * R].reshape(D, ATTN_HEADS, HEAD_DIM)
    wv = jnp.pad(wv, ((0, 0), (0, 0), (0, VT_ROWS - HEAD_DIM))).reshape(D, ATTN_HEADS * VT_ROWS)
    wvt = wv.T.astype(BF16)
    wbh, wbl = _split_bf16(_pad_cols(w[:, 6 * R:], IDX_COLS))
    return wa, wvt, wbh, wbl


def kernel(x, c, w_ada, b_ada, norm1_g, norm2_g, w_in, mu_rkv, mu_lora, decay_w0, decay_a, decay_b, iclr_a0, iclr_a, iclr_b, gate_a, gate_b, k_k, k_a, r_k, lnx_g, lnx_b, vres_mu, vres_v0, vres_a, vres_b, attn_out_g, rel_bias, w_out, w_mlp1, w_mlp2, final_g):
    B, T, D = x.shape
    depth = w_in.shape[0]
    mod = _adaln(c, w_ada, b_ada)
    mod = mod.reshape(depth, B, 6, 1, D).transpose(0, 2, 1, 3, 4)
    x2d = x.reshape(B * T, D)
    v_first = None
    for l in range(depth):
        sh1, sc1, gt1, sh2, sc2, gt2 = (mod[l, i] for i in range(6))
        wa, wvt, wbh, wbl = _in_weights(l, w_in, mu_lora, decay_a, iclr_a, gate_a, vres_mu, vres_a)
        rkvl, qk, vt, qiw, kix = _inproj(x2d, norm1_g[l], sc1, sh1, wa, wvt, wbh, wbl, T)
        p = dict(mu_rkv=mu_rkv[l], decay_w0=decay_w0[l], decay_b=decay_b[l], iclr_a0=iclr_a0[l],
                 iclr_b=iclr_b[l], gate_b=gate_b[l], k_k=k_k[l], k_a=k_a[l], r_k=r_k[l],
                 lnx_g=lnx_g[l], lnx_b=lnx_b[l])
        if l > 0:
            p.update(vres_v0=vres_v0[l - 1], vres_b=vres_b[l - 1])
        rw, v_first = _rwkv(rkvl, v_first, p, B, T)
        att = _dsa(qk, vt, qiw, kix, rel_bias, attn_out_g[l], B, T)
        x2d = _mix_mlp(x2d, rw, att, w_out[l].astype(BF16), gt1, norm2_g[l], sc2, sh2, gt2,
                       w_mlp1[l].astype(BF16), w_mlp2[l].astype(BF16), final_g, l == depth - 1, T)
    return x2d.reshape(B, T, D)
```

```python
import functools
import math

import numpy as np
import jax
import jax.numpy as jnp
from jax import lax
from jax.experimental import pallas as pl
from jax.experimental.pallas import tpu as pltpu

F32 = jnp.float32
BF16 = jnp.bfloat16
I32 = jnp.int32

HEAD_DIM = 64
RWKV_HEADS = 8
ATTN_HEADS = 8
IDX_HEADS = 8
IDX_DIM = 64
TOPK_MAX = 256
N_BUCKETS = 32
MAX_DISTANCE = 128
RMS_EPS = 1e-6
LNX_EPS = 64e-5

LANES = 128
VMEM_LIMIT = 56 * 1024 * 1024
RWKV_CHUNK = 64
RWKV_GROUP = 256
INV_BASE = 4
RWKV_TILE = 512
RWKV_UNROLL = 8
DSA_BLOCK = 256
VT_ROWS = 80
LOG2E = math.log2(math.e)
INPROJ_TILE = 512
MLP_TILE = 1024
MLP_FF_CHUNK = 1024
LORA_PAD = 384
K_IDX_OFF = IDX_HEADS * IDX_DIM
W_IDX_OFF = K_IDX_OFF + IDX_DIM
IDX_COLS = 640
KIX_COLS = 4 * IDX_DIM
NEG_BIG = -1e30
HI = lax.Precision.HIGHEST
NT = (((1,), (1,)), ((), ()))


def _bdot(a, b):
    return jnp.dot(a.astype(BF16), b.astype(BF16), preferred_element_type=F32)


def _hdot(a, b):
    return jnp.dot(a, b, precision=HI, preferred_element_type=F32)


def _params(*sem):
    return pltpu.CompilerParams(dimension_semantics=sem, vmem_limit_bytes=VMEM_LIMIT)


def _adaln_kernel(c_ref, w_ref, b_ref, o_ref):
    c = c_ref[...]
    c_act = c * jax.nn.sigmoid(c)
    o_ref[...] = _hdot(c_act, w_ref[...]) + b_ref[...]


def _adaln(c, w_ada, b_ada):
    L, D, D6 = w_ada.shape
    B = c.shape[0]
    cb = 1024
    return pl.pallas_call(
        _adaln_kernel,
        grid=(L, D6 // cb),
        in_specs=[pl.BlockSpec((B, D), lambda l, j: (0, 0)),
                  pl.BlockSpec((None, D, cb), lambda l, j: (l, 0, j)),
                  pl.BlockSpec((None, 1, cb), lambda l, j: (l, 0, j))],
        out_specs=pl.BlockSpec((None, B, cb), lambda l, j: (l, 0, j)),
        out_shape=jax.ShapeDtypeStruct((L, B, D6), F32),
        compiler_params=_params("arbitrary", "arbitrary"),
        name="adaln",
    )(c, w_ada, b_ada.reshape(L, 1, D6))


def _norm_mod(x, g, sc, sh):
    ms = jnp.mean(x * x, axis=-1, keepdims=True)
    return (x * lax.rsqrt(ms + RMS_EPS) * g) * (1.0 + sc) + sh


def _inproj_kernel(x_ref, g_ref, sc_ref, sh_ref, wa_ref, wvt_ref, wbh_ref, wbl_ref,
                   rkvl_ref, qk_ref, vt_ref, qiw_ref, kix_ref):
    h = _norm_mod(x_ref[...], g_ref[...], sc_ref[...], sh_ref[...])
    hb = h.astype(BF16)
    hl = (h - hb.astype(F32)).astype(BF16)
    pa = jnp.dot(hb, wa_ref[...], preferred_element_type=F32)
    nr = rkvl_ref.shape[1]
    rkvl_ref[...] = pa[:, :nr]
    qk_ref[...] = pa[:, nr:].astype(BF16)
    vt = lax.dot_general(wvt_ref[...], hb, NT, preferred_element_type=F32)
    ones_row = lax.broadcasted_iota(I32, vt.shape, 0) % VT_ROWS == HEAD_DIM
    vt = jnp.where(ones_row, 1.0, vt).astype(BF16)
    for j in range(vt_ref.shape[0]):
        vt_ref[j] = vt[:, j * DSA_BLOCK:(j + 1) * DSA_BLOCK]
    pb = (jnp.dot(hb, wbh_ref[...], preferred_element_type=F32)
          + jnp.dot(hb, wbl_ref[...], preferred_element_type=F32)
          + jnp.dot(hl, wbh_ref[...], preferred_element_type=F32))
    qiw_ref[...] = pb
    ki = pb[:, K_IDX_OFF:W_IDX_OFF]
    kh = ki.astype(BF16)
    kl = (ki - kh.astype(F32)).astype(BF16)
    kix_ref[...] = jnp.concatenate([kh, kl, kh, jnp.zeros_like(kh)], axis=-1)


def _inproj(x2d, g, sc, sh, wa, wvt, wbh, wbl, T):
    N, D = x2d.shape
    nblk = INPROJ_TILE // DSA_BLOCK
    tm = INPROJ_TILE
    na = wa.shape[1]
    A = wvt.shape[0]
    nr = 3 * RWKV_HEADS * HEAD_DIM + 2 * LORA_PAD
    assert T % tm == 0 and (wa.shape[1] - nr) % LANES == 0
    nb = T // tm
    row = lambda i: (i, 0)
    per_b = lambda i: (i // nb, 0, 0)
    const = lambda i: (0, 0)
    once = pl.Buffered(1)
    return pl.pallas_call(
        _inproj_kernel,
        grid=(N // tm,),
        in_specs=[pl.BlockSpec((tm, D), row),
                  pl.BlockSpec((1, D), const),
                  pl.BlockSpec((None, 1, D), per_b),
                  pl.BlockSpec((None, 1, D), per_b),
                  pl.BlockSpec((D, na), const, pipeline_mode=once),
                  pl.BlockSpec((A, D), const, pipeline_mode=once),
                  pl.BlockSpec((D, IDX_COLS), const, pipeline_mode=once),
                  pl.BlockSpec((D, IDX_COLS), const, pipeline_mode=once)],
        out_specs=[pl.BlockSpec((tm, nr), row),
                   pl.BlockSpec((tm, na - nr), row),
                   pl.BlockSpec((nblk, A, DSA_BLOCK), lambda i: (i, 0, 0)),
                   pl.BlockSpec((tm, IDX_COLS), row),
                   pl.BlockSpec((tm, KIX_COLS), row)],
        out_shape=[jax.ShapeDtypeStruct((N, nr), F32),
                   jax.ShapeDtypeStruct((N, na - nr), BF16),
                   jax.ShapeDtypeStruct((N // DSA_BLOCK, A, DSA_BLOCK), BF16),
                   jax.ShapeDtypeStruct((N, IDX_COLS), F32),
                   jax.ShapeDtypeStruct((N, KIX_COLS), BF16)],
        compiler_params=_params("arbitrary"),
        name="inproj",
    )(x2d, g.reshape(1, D), sc, sh, wa, wvt, wbh, wbl)


def _head_ones(n):
    r = lax.broadcasted_iota(I32, (n, n), 0) // HEAD_DIM
    c = lax.broadcasted_iota(I32, (n, n), 1) // HEAD_DIM
    return jnp.where(r == c, 1.0, 0.0).astype(F32)


def _split(x):
    hi = x.astype(BF16)
    return hi, (x - hi.astype(F32)).astype(BF16)


def _split3(x):
    hi = x.astype(BF16)
    r = x - hi.astype(F32)
    mid = r.astype(BF16)
    return hi, mid, (r - mid.astype(F32)).astype(BF16)


def _block_diag(w, blocks):
    w = w.astype(BF16)
    return jnp.concatenate([jnp.where(m, w, jnp.zeros_like(w)) for m in blocks], axis=0)


def _block_diag_t(w, blocks):
    return jnp.concatenate([jnp.where(m, w, 0.0) for m in blocks], axis=0).T


def _fold_blocks(w):
    n = w.shape[1] // HEAD_DIM
    out = w[:HEAD_DIM]
    for h in range(1, n):
        out = out + w[h * HEAD_DIM:(h + 1) * HEAD_DIM]
    return out


def _mm(a, w):
    return jnp.dot(a.astype(BF16), w, preferred_element_type=F32)


def _rwkv_kernel(has_vres, TT, *refs):
    C = RWKV_CHUNK
    R = RWKV_HEADS * HEAD_DIM
    G = RWKV_GROUP
    NG = R // G
    NC = TT // C
    if has_vres:
        (rkvl_ref, vf_ref, mu_ref, w0_ref, db_ref, a0_ref, ib_ref, gb_ref, kk_ref, ka_ref,
         rk_ref, lg_ref, lb_ref, v0_ref, vb_ref, out_ref,
         prev_ref, S_ref, r_s, k_s, v_s, kk_s, b_s, lw_s, cum_s, y_s, q_s, y0_s, m_s, g_s) = refs
    else:
        (rkvl_ref, mu_ref, w0_ref, db_ref, a0_ref, ib_ref, gb_ref, kk_ref, ka_ref,
         rk_ref, lg_ref, lb_ref, out_ref, vfo_ref,
         prev_ref, S_ref, r_s, k_s, v_s, kk_s, b_s, lw_s, cum_s, y_s, q_s, y0_s, m_s, g_s) = refs

    @pl.when(pl.program_id(1) == 0)
    def _():
        prev_ref[...] = jnp.zeros_like(prev_ref)
        S_ref[...] = jnp.zeros_like(S_ref)

    row0 = lax.broadcasted_iota(I32, (TT, 1), 0) == 0
    prev = prev_ref[...]

    def shift(z, p):
        return jnp.where(row0, p, pltpu.roll(z, 1, 0))

    rkv = rkvl_ref[:, :3 * R]
    p1 = rkvl_ref[:, 3 * R:3 * R + LORA_PAD]
    p2 = rkvl_ref[:, 3 * R + LORA_PAD:]
    rkv_new = rkv + (shift(rkv, prev[:, :3 * R]) - rkv) * mu_ref[...]
    lora = p1 + shift(p2, prev[:, 3 * R:])
    prev_ref[...] = jnp.concatenate([rkv[TT - 1:TT, :], p2[TT - 1:TT, :]], axis=-1)

    r = rkv_new[:, :R]
    k = rkv_new[:, R:2 * R]
    v = rkv_new[:, 2 * R:]
    ones_h = _head_ones(G).astype(BF16)

    def head_sum(x):
        hi, lo = _split(x)
        return jnp.concatenate(
            [jnp.dot(hi[:, gi * G:(gi + 1) * G], ones_h, preferred_element_type=F32)
             + jnp.dot(lo[:, gi * G:(gi + 1) * G], ones_h, preferred_element_type=F32) for gi in range(NG)], axis=1)

    o_a = db_ref.shape[0]
    o_g = o_a + ib_ref.shape[0]
    o_v = o_g + gb_ref.shape[0]
    wlog = w0_ref[...] + _bdot(jnp.tanh(lora[:, :o_a]), db_ref[...])
    z = -wlog
    wlog = -(jnp.maximum(z, 0.0) + jnp.log(1.0 + jnp.exp(-jnp.abs(z)))) - 0.5
    lw = -jnp.exp(wlog)
    lw_s[...] = lw
    slab = min(TT, 4 * C)
    tr = lax.broadcasted_iota(I32, (slab, slab), 0)
    tc = lax.broadcasted_iota(I32, (slab, slab), 1)
    tri = jnp.where((tc <= tr) & (tr // C == tc // C), 1.0, 0.0).astype(BF16)
    parts = _split3(lw)
    for s0 in range(0, TT, slab):
        cum_s[s0:s0 + slab, :] = sum(jnp.dot(tri, part[s0:s0 + slab], preferred_element_type=F32) for part in parts)
    a = jax.nn.sigmoid(a0_ref[...] + _bdot(lora[:, o_a:o_g], ib_ref[...]))
    g = _bdot(jax.nn.sigmoid(lora[:, o_g:o_v]), gb_ref[...])
    kkr = k * kk_ref[...]
    kk = kkr * lax.rsqrt(jnp.maximum(head_sum(kkr * kkr), 1e-24))
    k = k * (1.0 + (a - 1.0) * ka_ref[...])
    if has_vres:
        v = v + (vf_ref[...] - v) * jax.nn.sigmoid(
            v0_ref[...] + _bdot(lora[:, o_v:o_v + vb_ref.shape[0]], vb_ref[...]))
    else:
        vfo_ref[...] = v
    r_s[...] = r
    k_s[...] = k
    v_s[...] = v
    kk_s[...] = kk
    b_s[...] = kk * a

    lane_g = lax.broadcasted_iota(I32, (C, RWKV_GROUP), 1)
    row_g = lax.broadcasted_iota(I32, (C, RWKV_GROUP), 0)
    blocks = [lane_g // HEAD_DIM == h for h in range(RWKV_GROUP // HEAD_DIM)]
    local = lane_g % HEAD_DIM
    strict = local < row_g
    incl = local <= row_g
    diag = local == row_g
    eye_cat = jnp.where(diag, 1.0, 0.0).astype(F32)
    base_mask = strict & (local // INV_BASE == row_g // INV_BASE)
    grow_masks = []
    s = INV_BASE
    while s < C:
        grow_masks.append((local // (2 * s) == row_g // (2 * s)) & (row_g % (2 * s) >= s) & (local % (2 * s) < s))
        s *= 2

    def local_chunk(it, carry):
        chains = [(u, slice(gi * G, (gi + 1) * G)) for u in range(RWKV_UNROLL) for gi in range(NG)]
        each = lambda f: [f(i) for i in range(len(chains))]
        rd, kkd, kt, bt, kc, bc, vv, plast = ([] for _ in range(8))
        for u in range(RWKV_UNROLL):
            sl = pl.ds(pl.multiple_of((it * RWKV_UNROLL + u) * C, C), C)
            lw = lw_s[sl, :]
            cum = cum_s[sl, :]
            cl = cum[C - 1:C, :]
            pinv = jnp.exp(-cum)
            pc = jnp.exp(cl - cum)
            k_a = k_s[sl, :]
            b_a = b_s[sl, :]
            full = (r_s[sl, :] * jnp.exp(cum), kk_s[sl, :] * jnp.exp(cum - lw), k_a * pinv, b_a * pinv,
                    k_a * pc, b_a * pc, v_s[sl, :], jnp.exp(cl))
            for dst, x in zip((rd, kkd, kt, bt, kc, bc, vv, plast), full):
                dst.extend(x[:, gs] for (uu, gs) in chains if uu == u)
        kt_w = each(lambda i: _block_diag_t(kt[i], blocks).astype(BF16))
        bt_w = each(lambda i: _block_diag_t(bt[i], blocks).astype(BF16))
        bc_t = each(lambda i: _fold_blocks(_block_diag_t(bc[i], blocks)))
        kc_t = each(lambda i: _fold_blocks(_block_diag_t(kc[i], blocks)))
        rr = each(lambda i: _mm(jnp.concatenate([kkd[i], rd[i]], axis=0),
                                jnp.concatenate([kt_w[i], bt_w[i]], axis=1)))
        akk = each(lambda i: jnp.where(strict, rr[i][:C, :G], 0.0))
        ark = each(lambda i: jnp.where(incl, rr[i][C:, :G], 0.0))
        arb = each(lambda i: jnp.where(incl, rr[i][C:, G:], 0.0))
        nmat = each(lambda i: jnp.where(strict, -rr[i][:C, G:], 0.0))
        nb = each(lambda i: jnp.where(base_mask, nmat[i], 0.0))
        sq = each(lambda i: _mm(nb[i], _block_diag(nb[i], blocks)))
        x = each(lambda i: eye_cat + nb[i])
        rr = each(lambda i: _mm(x[i], _block_diag(sq[i], blocks)))
        x = each(lambda i: x[i] + rr[i])
        for lower_left in grow_masks:
            u = each(lambda i: _mm(jnp.where(lower_left, nmat[i], 0.0), _block_diag(x[i], blocks)))
            rr = each(lambda i: _mm(x[i], _block_diag(u[i], blocks)))
            x = each(lambda i: x[i] + rr[i])
        rv = each(lambda i: _mm(jnp.concatenate([akk[i], ark[i], kc_t[i]], axis=0), _block_diag(vv[i], blocks)))
        ru = each(lambda i: _mm(x[i], jnp.concatenate([_block_diag(kkd[i], blocks),
                                                       _block_diag(rv[i][:C], blocks)], axis=1)))
        rr = each(lambda i: _mm(jnp.concatenate([arb[i], bc_t[i]], axis=0),
                                jnp.concatenate([_block_diag(ru[i][:, :G], blocks),
                                                 _block_diag(ru[i][:, G:], blocks)], axis=1)))
        for i, (u, gs) in enumerate(chains):
            c = it * RWKV_UNROLL + u
            q_s[c, :, gs] = rd[i] - rr[i][:C, :G]
            y0_s[c, :, gs] = rv[i][C:2 * C] - rr[i][:C, G:]
            dterm = jnp.where(diag, jnp.broadcast_to(plast[i], (C, G)), 0.0)
            m_s[c, :, gs] = dterm - rr[i][C:, :G]
            g_s[c, :, gs] = rv[i][2 * C:] - rr[i][C:, G:]
        return carry

    lax.fori_loop(0, NC // RWKV_UNROLL, local_chunk, 0)

    def scan_chunk(c, carry):
        sl = pl.ds(pl.multiple_of(c * C, C), C)
        for gi in range(NG):
            gs = slice(gi * RWKV_GROUP, (gi + 1) * RWKV_GROUP)
            rr = _mm(jnp.concatenate([q_s[c, :, gs], m_s[c, :, gs]], axis=0), _block_diag(S_ref[:, gs], blocks))
            y_s[sl, gs] = rr[:C] + y0_s[c, :, gs]
            S_ref[:, gs] = rr[C:] + g_s[c, :, gs]
        return carry

    lax.fori_loop(0, NC, scan_chunk, 0)

    y = y_s[...]
    inv_n = 1.0 / HEAD_DIM
    mean = head_sum(y) * inv_n
    yc = y - mean
    var = head_sum(yc * yc) * inv_n
    yn = yc * lax.rsqrt(var + LNX_EPS) * lg_ref[...] + lb_ref[...]
    bonus = head_sum(r_s[...] * k_s[...] * rk_ref[...]) * v_s[...]
    out_ref[...] = ((yn + bonus) * g).astype(out_ref.dtype)


def _rwkv(rkvl, v_first, p, B, T):
    N, nc = rkvl.shape
    R = RWKV_HEADS * HEAD_DIM
    TT = min(RWKV_TILE, T)
    nt = T // TT
    NC = TT // RWKV_CHUNK
    has_vres = v_first is not None
    row = lambda b, i: (b * nt + i, 0)
    const = lambda b, i: (0, 0)
    vec = lambda a: a.reshape(1, -1)
    ins = [rkvl]
    specs = [pl.BlockSpec((TT, nc), row)]
    if has_vres:
        ins.append(v_first)
        specs.append(pl.BlockSpec((TT, R), row))
    small = [vec(p["mu_rkv"]), vec(p["decay_w0"]), p["decay_b"].astype(BF16), vec(p["iclr_a0"]),
             p["iclr_b"].astype(BF16), p["gate_b"].astype(BF16), vec(p["k_k"]), vec(p["k_a"]),
             vec(p["r_k"]), vec(p["lnx_g"]), vec(p["lnx_b"])]
    if has_vres:
        small += [vec(p["vres_v0"]), p["vres_b"].astype(BF16)]
    ins += small
    specs += [pl.BlockSpec(a.shape, const) for a in small]
    out_shape = [jax.ShapeDtypeStruct((N, R), BF16)]
    out_specs = [pl.BlockSpec((TT, R), row)]
    if not has_vres:
        out_shape.append(jax.ShapeDtypeStruct((N, R), F32))
        out_specs.append(pl.BlockSpec((TT, R), row))
    scratch = [pltpu.VMEM((1, 3 * R + LORA_PAD), F32),
               pltpu.VMEM((HEAD_DIM, R), F32)]
    scratch += [pltpu.VMEM((TT, R), F32) for _ in range(8)]
    scratch += [pltpu.VMEM((NC, RWKV_CHUNK, R), F32) for _ in range(4)]
    res = pl.pallas_call(
        functools.partial(_rwkv_kernel, has_vres, TT),
        grid=(B, nt),
        in_specs=specs,
        out_specs=out_specs,
        out_shape=out_shape,
        scratch_shapes=scratch,
        compiler_params=_params("arbitrary", "arbitrary"),
        name="rwkv",
    )(*ins)
    if has_vres:
        return res[0], v_first
    return res[0], res[1]


def _bucket_boundaries():
    max_exact = N_BUCKETS // 2
    d = np.arange(0, 4 * MAX_DISTANCE, dtype=np.int64)
    nf = np.maximum(d, 1).astype(np.float32)
    large = max_exact + (np.log(nf / np.float32(max_exact)) / np.float32(math.log(MAX_DISTANCE / max_exact))
                         * np.float32(N_BUCKETS - max_exact)).astype(np.int32)
    large = np.minimum(large, N_BUCKETS - 1)
    bucket = np.where(d < max_exact, d, large)
    return [int(np.argmax(bucket >= j)) for j in range(max_exact + 1, N_BUCKETS)]


_BUCKET_STARTS = _bucket_boundaries()
KEY_NEG_INF = int(np.int32(np.array(-np.inf, np.float32).view(np.int32)) ^ np.int32(0x7FFFFFFF))
INT_MIN = -2 ** 31
INT_MAX = 2 ** 31 - 1
FAST_SELECT_BLIND = 14
FAST_SELECT_STEPS = 2
FAST_SELECT_TRIPS = 10
ACC_ROWS = 32


def _sort_key(s):
    bits = pltpu.bitcast(s, I32)
    bits = jnp.where(bits == INT_MIN, 0, bits)
    return bits ^ ((bits >> 31) & jnp.int32(0x7FFFFFFF))


def _key_to_float(k):
    return pltpu.bitcast(k ^ ((k >> 31) & jnp.int32(0x7FFFFFFF)), F32)


def _dsa_kernel(T, ksel, q_ref, k_ref, vt_ref, qiw_ref, kix_ref, relb_ref, g_ref, out_ref,
                lhs_s, qpad_s, key_s, madd_s, bias_s, m_s, acc_s, s_s, p_s, cmax_s, alpha_s):
    TQ = TK = DSA_BLOCK
    H = ATTN_HEADS
    qb = pl.program_id(1)
    nch = qb + 1
    rows = lax.broadcasted_iota(I32, (TK, TQ), 0)
    cols = lax.broadcasted_iota(I32, (TK, TQ), 1)

    @pl.when((pl.program_id(0) == 0) & (qb == 0))
    def _():
        max_exact = N_BUCKETS // 2
        for off in range(2):
            d = cols - rows + off * TK
            bucket = jnp.where(d < max_exact, jnp.maximum(d, 0), max_exact)
            for start in _BUCKET_STARTS:
                bucket = bucket + jnp.where(d >= start, 1, 0)
            for h in range(H):
                far = relb_ref[N_BUCKETS - 1, h]
                tile = jnp.zeros((TK, TQ), F32)
                for bk in range(N_BUCKETS - 1):
                    tile = jnp.where(bucket == bk, (relb_ref[bk, h] - far) * LOG2E, tile)
                bias_s[h, off] = tile

    qi_t = (qiw_ref[:, :IDX_HEADS * IDX_DIM] * (IDX_DIM ** -0.5)).T
    for h in range(IDX_HEADS):
        qh = qi_t[h * IDX_DIM:(h + 1) * IDX_DIM, :]
        hi = qh.astype(BF16)
        lo = (qh - hi.astype(F32)).astype(BF16)
        lhs_s[h] = jnp.concatenate([hi, hi, lo, jnp.zeros_like(hi)], axis=0)
    w_t = qiw_ref[:, K_IDX_OFF:IDX_COLS].T[IDX_DIM:IDX_DIM + IDX_HEADS, :] * (IDX_HEADS ** -0.5)

    q_t = q_ref[...].astype(F32).T
    zeros_h = jnp.zeros((HEAD_DIM, TQ), BF16)
    for h in range(H):
        qh = q_t[h * HEAD_DIM:(h + 1) * HEAD_DIM, :].astype(BF16)
        qpad_s[h] = jnp.concatenate([qh, zeros_h] if h % 2 == 0 else [zeros_h, qh], axis=0)

    def scores(j):
        kx = kix_ref[pl.ds(pl.multiple_of(j * TK, TK), TK), :]
        acc = jnp.zeros((TK, TQ), F32)
        for h in range(IDX_HEADS):
            d = jnp.dot(kx, lhs_s[h], preferred_element_type=F32)
            acc = acc + w_t[h:h + 1, :] * jnp.maximum(d, 0.0)
        return acc

    def stats(st, kk, mn_of):
        fold = lambda z: z.reshape(TK // ACC_ROWS, ACC_ROWS, TQ)
        neg, pos, mn, mx = st
        return (neg + jnp.sum(fold(kk >> 31), axis=0),
                pos + jnp.sum(fold((-kk) >> 31), axis=0),
                jnp.minimum(mn, jnp.min(fold(mn_of(kk)), axis=0)),
                jnp.maximum(mx, jnp.max(fold(kk), axis=0)))

    def score_body(j, st):
        kk = _sort_key(scores(j))
        key_s[j] = kk
        return stats(st, kk, lambda z: z)

    st = lax.fori_loop(0, qb, score_body,
                       (jnp.zeros((ACC_ROWS, TQ), I32), jnp.zeros((ACC_ROWS, TQ), I32),
                        jnp.full((ACC_ROWS, TQ), INT_MAX, I32), jnp.full((ACC_ROWS, TQ), INT_MIN, I32)))
    kd = _sort_key(jnp.where(rows <= cols, scores(qb), -jnp.inf))
    key_s[qb] = kd
    st = stats(st, kd, lambda z: jnp.where(z > KEY_NEG_INF, z, INT_MAX))
    n_ge0 = nch * TK + jnp.sum(st[0], axis=0, keepdims=True)
    n_gt0 = -jnp.sum(st[1], axis=0, keepdims=True)
    lo0 = _key_to_float(jnp.min(st[2], axis=0, keepdims=True))
    hi0 = _key_to_float(jnp.max(st[3], axis=0, keepdims=True) + 1)

    def count(pred):
        def body(j, acc):
            hit = jnp.where(pred(key_s[j]), 1, 0)
            return acc + jnp.sum(hit.reshape(TK // ACC_ROWS, ACC_ROWS, TQ), axis=0)
        acc = lax.fori_loop(0, nch, body, jnp.zeros((ACC_ROWS, TQ), I32))
        return jnp.sum(acc, axis=0, keepdims=True)

    def write_mask(sel_of):
        def body(j, c):
            madd_s[j] = jnp.where(sel_of(key_s[j]), 0.0, NEG_BIG)
            return c
        lax.fori_loop(0, nch, body, 0)

    kf = float(ksel)
    n_adm = qb * TQ + lax.broadcasted_iota(I32, (1, TQ), 1) + 1
    trivial = n_adm <= ksel
    positive = n_gt0 > ksel
    done0 = trivial | ((n_gt0 <= ksel) & (n_ge0 >= ksel))
    thr0 = jnp.where(trivial, KEY_NEG_INF + 1, jnp.where(n_gt0 == ksel, 1, 0))
    need0 = jnp.where(n_gt0 < ksel, ksel - n_gt0, ksel)
    a0 = jnp.where(positive, 0.0, lo0)
    b0 = jnp.where(positive, hi0, 0.0)
    cb0 = jnp.where(positive, 0, n_ge0)

    def excess(cnt):
        return jnp.log(cnt.astype(F32) + 0.5) - math.log(kf + 0.5)

    fa0 = excess(jnp.where(positive, n_gt0, n_adm))
    as_int = lambda m: jnp.where(m, 1, 0)

    def any_lane(m):
        return jnp.max(as_int(m)) > 0

    def fast_cond(st):
        it, done, stuck = st[0], st[1], st[2]
        return (it < FAST_SELECT_TRIPS) & any_lane((done + stuck) == 0)

    def fast_body(st):
        it, st = st[0], st[1:]
        for _ in range(FAST_SELECT_STEPS):
            st = fast_step(st)
        return (it + 1,) + st

    def fast_step(st):
        done, stuck, thr, need, a, b, fa, fb, cb, last = st
        c = a + (b - a) * (fa / (fa - fb))
        c = jnp.where((c > a) & (c < b), c, 0.5 * a + 0.5 * b)
        inside = (c > a) & (c < b)
        ckey = _sort_key(c)
        cnt = count(lambda kk: kk >= ckey)
        fc = excess(cnt)
        hit = inside & (cnt == ksel) & (done == 0)
        thr = jnp.where(hit, ckey, thr)
        need = jnp.where(hit, ksel, need)
        done = jnp.where(hit, 1, done)
        stuck = jnp.where(inside, stuck, 1)
        up = inside & (cnt > ksel)
        dn = inside & (cnt < ksel)
        fb = jnp.where(up & (last == 1), 0.5 * fb, fb)
        fa = jnp.where(dn & (last == 0), 0.5 * fa, fa)
        a, fa = jnp.where(up, c, a), jnp.where(up, fc, fa)
        b, fb, cb = jnp.where(dn, c, b), jnp.where(dn, fc, fb), jnp.where(dn, cnt, cb)
        return done, stuck, thr, need, a, b, fa, fb, cb, jnp.where(up, 1, jnp.where(dn, 0, last))

    st = (as_int(done0), jnp.zeros((1, TQ), I32), thr0, need0,
          a0, b0, fa0, excess(cb0), cb0, jnp.full((1, TQ), -1, I32))
    blind = jnp.where((qb + 1) * TQ <= ksel, 0, FAST_SELECT_BLIND)
    st = lax.fori_loop(0, blind, lambda i, s: fast_step(s), st)
    st = lax.while_loop(fast_cond, fast_body, (jnp.int32(0),) + st)
    done, thr, need, a, b, cb = st[1], st[3], st[4], st[5], st[6], st[9]
    open_lane = done == 0

    def close_cond(st):
        return any_lane((st[1] - st[0]) > 1)

    def close_body(st):
        ak, bk, cb = st
        mid = ak + ((bk - ak) >> 1)
        cnt = count(lambda kk: kk >= mid)
        ge = cnt >= ksel
        return jnp.where(ge, mid, ak), jnp.where(ge, bk, mid), jnp.where(ge, cb, cnt)

    ak, _, cb = lax.while_loop(close_cond, close_body, (
        jnp.where(open_lane, _sort_key(a), thr), jnp.where(open_lane, _sort_key(b), thr), cb))
    thr = jnp.where(open_lane, ak, thr)
    need = jnp.where(open_lane, ksel - cb, need)
    any_tie = any_lane(open_lane | (jnp.logical_not(trivial) & (n_gt0 < ksel) & (n_ge0 > ksel)))

    @pl.when(jnp.logical_not(any_tie))
    def _():
        write_mask(lambda kk: kk >= thr)

    @pl.when(any_tie)
    def _():
        kr = lax.broadcasted_iota(I32, (TK, TK), 0)
        kc = lax.broadcasted_iota(I32, (TK, TK), 1)
        lower = jnp.where(kc <= kr, 1.0, 0.0).astype(BF16)
        need_f = need.astype(F32)

        def body(j, run):
            kk = key_s[j]
            eq = kk == thr
            eqf = jnp.where(eq, 1.0, 0.0)
            rank = run + jnp.dot(lower, eqf.astype(BF16), preferred_element_type=F32)
            tied_in = jnp.where(eq, rank, float(TOPK_MAX + 1)) <= need_f
            madd_s[j] = jnp.where(kk > thr, 0.0, jnp.where(tied_in, 0.0, NEG_BIG))
            return run + jnp.sum(eqf, axis=0, keepdims=True)
        lax.fori_loop(0, nch, body, jnp.zeros((1, TQ), F32))

    m_s[...] = jnp.full(m_s.shape, NEG_BIG, F32)
    acc_s[...] = jnp.zeros(acc_s.shape, F32)

    def attend(j, bias_of):
        ks = pl.ds(pl.multiple_of(j * TK, TK), TK)
        ma = madd_s[j]
        for h in range(H):
            kp = k_ref[ks, LANES * (h // 2):LANES * (h // 2 + 1)]
            s = jnp.dot(kp, qpad_s[h], preferred_element_type=F32) + ma
            bias = bias_of(h)
            if bias is not None:
                s = s + bias
            s_s[h] = s
            cmax_s[h] = jnp.max(s.reshape(TK // 8, 8, TQ), axis=0)
        for h in range(H):
            m_old = m_s[h]
            m_new = jnp.maximum(m_old, jnp.max(cmax_s[h], axis=0, keepdims=True))
            alpha_s[h] = jnp.exp2(m_old - m_new)
            m_s[h] = m_new
            p_s[h] = jnp.exp2(s_s[h] - m_new).astype(BF16)
        for h in range(H):
            hs = slice(h * VT_ROWS, (h + 1) * VT_ROWS)
            pv = jnp.dot(vt_ref[j, hs, :], p_s[h], preferred_element_type=F32)
            acc_s[hs, :] = alpha_s[h] * acc_s[hs, :] + pv

    def far_body(j, c):
        attend(j, lambda h: None)
        return c

    lax.fori_loop(0, jnp.maximum(qb - 1, 0), far_body, 0)

    @pl.when(qb >= 1)
    def _():
        attend(qb - 1, lambda h: bias_s[h, 1])

    attend(qb, lambda h: bias_s[h, 0])

    outs = []
    for h in range(H):
        o = acc_s[h * VT_ROWS:h * VT_ROWS + HEAD_DIM, :] / acc_s[h * VT_ROWS + HEAD_DIM:h * VT_ROWS + HEAD_DIM + 1, :]
        ms = jnp.mean(o * o, axis=0, keepdims=True)
        outs.append(o * lax.rsqrt(ms + RMS_EPS))
    out_ref[...] = (jnp.concatenate(outs, axis=0).T * g_ref[...]).astype(out_ref.dtype)


def _dsa(qk, vt, qiw, kix, rel_bias, g, B, T):
    N = qk.shape[0]
    TQ = DSA_BLOCK
    nq = T // TQ
    A = ATTN_HEADS * HEAD_DIM
    ksel = min(TOPK_MAX, T // 4)
    assert MAX_DISTANCE <= TQ + 1
    return pl.pallas_call(
        functools.partial(_dsa_kernel, T, ksel),
        grid=(B, nq),
        in_specs=[pl.BlockSpec((TQ, A), lambda b, i: (b * nq + i, 0)),
                  pl.BlockSpec((T, A), lambda b, i: (b, 1)),
                  pl.BlockSpec((nq, ATTN_HEADS * VT_ROWS, TQ), lambda b, i: (b, 0, 0)),
                  pl.BlockSpec((TQ, IDX_COLS), lambda b, i: (b * nq + i, 0)),
                  pl.BlockSpec((T, KIX_COLS), lambda b, i: (b, 0)),
                  pl.BlockSpec(memory_space=pltpu.SMEM),
                  pl.BlockSpec((1, A), lambda b, i: (0, 0))],
        out_specs=pl.BlockSpec((TQ, A), lambda b, i: (b * nq + i, 0)),
        out_shape=jax.ShapeDtypeStruct((N, A), BF16),
        scratch_shapes=[pltpu.VMEM((IDX_HEADS, KIX_COLS, TQ), BF16),
                        pltpu.VMEM((ATTN_HEADS, LANES, TQ), BF16),
                        pltpu.VMEM((nq, TQ, TQ), I32),
                        pltpu.VMEM((nq, TQ, TQ), F32),
                        pltpu.VMEM((ATTN_HEADS, 2, TQ, TQ), F32),
                        pltpu.VMEM((ATTN_HEADS, 1, TQ), F32),
                        pltpu.VMEM((ATTN_HEADS * VT_ROWS, TQ), F32),
                        pltpu.VMEM((ATTN_HEADS, TQ, TQ), F32),
                        pltpu.VMEM((ATTN_HEADS, TQ, TQ), BF16),
                        pltpu.VMEM((ATTN_HEADS, 8, TQ), F32),
                        pltpu.VMEM((ATTN_HEADS, 1, TQ), F32)],
        compiler_params=_params("arbitrary", "arbitrary"),
        name="dsa",
    )(qk, qk, vt, qiw, kix, rel_bias, g.reshape(1, A))


def _mix_mlp_kernel(final, x_ref, rw_ref, att_ref, wo_ref, gt1_ref, g_ref, sc_ref, sh_ref, gt2_ref,
                    w1_ref, w2_ref, fg_ref, o_ref, x_s, h_s, acc_s):
    j = pl.program_id(1)

    @pl.when(j == 0)
    def _():
        R = rw_ref.shape[1]
        mixed = (jnp.dot(rw_ref[...], wo_ref[:R, :], preferred_element_type=F32)
                 + jnp.dot(att_ref[...], wo_ref[R:, :], preferred_element_type=F32))
        x1 = x_ref[...] + gt1_ref[...] * mixed
        x_s[...] = x1
        h_s[...] = _norm_mod(x1, g_ref[...], sc_ref[...], sh_ref[...]).astype(BF16)
        acc_s[...] = jnp.zeros_like(acc_s)

    u = jnp.dot(h_s[...], w1_ref[...], preferred_element_type=F32)
    u = jnp.square(jnp.maximum(u, 0.0))
    acc_s[...] += jnp.dot(u.astype(BF16), w2_ref[...], preferred_element_type=F32)

    @pl.when(j == pl.num_programs(1) - 1)
    def _():
        y = x_s[...] + gt2_ref[...] * acc_s[...]
        if final:
            ms = jnp.mean(y * y, axis=-1, keepdims=True)
            y = y * lax.rsqrt(ms + RMS_EPS) * fg_ref[...]
        o_ref[...] = y


def _mix_mlp(x2d, rw, att, wo, gt1, g, sc, sh, gt2, w1, w2, final_g, final, T):
    N, D = x2d.shape
    F = w1.shape[1]
    fc = min(MLP_FF_CHUNK, F)
    tm = min(MLP_TILE, T)
    nb = T // tm
    row = lambda i, j: (i, 0)
    per_b = lambda i, j: (i // nb, 0, 0)
    const = lambda i, j: (0, 0)
    return pl.pallas_call(
        functools.partial(_mix_mlp_kernel, final),
        grid=(N // tm, F // fc),
        in_specs=[pl.BlockSpec((tm, D), row),
                  pl.BlockSpec((tm, rw.shape[1]), row),
                  pl.BlockSpec((tm, att.shape[1]), row),
                  pl.BlockSpec(wo.shape, const, pipeline_mode=pl.Buffered(1)),
                  pl.BlockSpec((None, 1, D), per_b),
                  pl.BlockSpec((1, D), const),
                  pl.BlockSpec((None, 1, D), per_b),
                  pl.BlockSpec((None, 1, D), per_b),
                  pl.BlockSpec((None, 1, D), per_b),
                  pl.BlockSpec((D, fc), lambda i, j: (0, j)),
                  pl.BlockSpec((fc, D), lambda i, j: (j, 0)),
                  pl.BlockSpec((1, D), const)],
        out_specs=pl.BlockSpec((tm, D), row),
        out_shape=jax.ShapeDtypeStruct((N, D), F32),
        scratch_shapes=[pltpu.VMEM((tm, D), F32), pltpu.VMEM((tm, D), BF16), pltpu.VMEM((tm, D), F32)],
        compiler_params=_params("arbitrary", "arbitrary"),
        name="mix_mlp",
    )(x2d, rw, att, wo, gt1, g.reshape(1, D), sc, sh, gt2, w1, w2, final_g.reshape(1, D))


def _pad_cols(w, n):
    return jnp.pad(w, ((0, 0), (0, n - w.shape[1])))


def _split_bf16(w):
    hi = w.astype(BF16)
    return hi, (w - hi.astype(F32)).astype(BF16)


def _in_weights(l, w_in, mu_lora, decay_a, iclr_a, gate_a, vres_mu, vres_a):
    D = w_in.shape[1]
    R = RWKV_HEADS * HEAD_DIM
    w = w_in[l]
    mats = [(decay_a[l], mu_lora[l, 0]), (iclr_a[l], mu_lora[l, 1]), (gate_a[l], mu_lora[l, 2])]
    if l > 0:
        mats.append((vres_a[l - 1], vres_mu[l - 1]))
    now = _pad_cols(jnp.concatenate([a * (1.0 - mu)[:, None] for a, mu in mats], axis=1), LORA_PAD)
    prev = _pad_cols(jnp.concatenate([a * mu[:, None] for a, mu in mats], axis=1), LORA_PAD)
    wq = w[:, 3 * R:4 * R] * (HEAD_DIM ** -0.5 * LOG2E)
    wa = jnp.concatenate([w[:, :3 * R], now, prev, wq, w[:, 4 * R:5 * R]], axis=1).astype(BF16)
    wv = w[:, 5 * R:6 * R].reshape(D, ATTN_HEADS, HEAD_DIM)
    wv = jnp.pad(wv, ((0, 0), (0, 0), (0, VT_ROWS - HEAD_DIM))).reshape(D, ATTN_HEADS * VT_ROWS)
    wvt = wv.T.astype(BF16)
    wbh, wbl = _split_bf16(_pad_cols(w[:, 6 * R:], IDX_COLS))
    return wa, wvt, wbh, wbl


def kernel(x, c, w_ada, b_ada, norm1_g, norm2_g, w_in, mu_rkv, mu_lora, decay_w0, decay_a, decay_b, iclr_a0, iclr_a, iclr_b, gate_a, gate_b, k_k, k_a, r_k, lnx_g, lnx_b, vres_mu, vres_v0, vres_a, vres_b, attn_out_g, rel_bias, w_out, w_mlp1, w_mlp2, final_g):
    B, T, D = x.shape
    depth = w_in.shape[0]
    mod = _adaln(c, w_ada, b_ada)
    mod = mod.reshape(depth, B, 6, 1, D).transpose(0, 2, 1, 3, 4)
    x2d = x.reshape(B * T, D)
    v_first = None
    for l in range(depth):
        sh1, sc1, gt1, sh2, sc2, gt2 = (mod[l, i] for i in range(6))
        wa, wvt, wbh, wbl = _in_weights(l, w_in, mu_lora, decay_a, iclr_a, gate_a, vres_mu, vres_a)
        rkvl, qk, vt, qiw, kix = _inproj(x2d, norm1_g[l], sc1, sh1, wa, wvt, wbh, wbl, T)
        p = dict(mu_rkv=mu_rkv[l], decay_w0=decay_w0[l], decay_b=decay_b[l], iclr_a0=iclr_a0[l],
                 iclr_b=iclr_b[l], gate_b=gate_b[l], k_k=k_k[l], k_a=k_a[l], r_k=r_k[l],
                 lnx_g=lnx_g[l], lnx_b=lnx_b[l])
        if l > 0:
            p.update(vres_v0=vres_v0[l - 1], vres_b=vres_b[l - 1])
        rw, v_first = _rwkv(rkvl, v_first, p, B, T)
        att = _dsa(qk, vt, qiw, kix, rel_bias, attn_out_g[l], B, T)
        x2d = _mix_mlp(x2d, rw, att, w_out[l].astype(BF16), gt1, norm2_g[l], sc2, sh2, gt2,
                       w_mlp1[l].astype(BF16), w_mlp2[l].astype(BF16), final_g, l == depth - 1, T)
    return x2d.reshape(B, T, D)
```

```python
import functools
import math

import numpy as np
import jax
import jax.numpy as jnp
from jax import lax
from jax.experimental import pallas as pl
from jax.experimental.pallas import tpu as pltpu

F32 = jnp.float32
BF16 = jnp.bfloat16
I32 = jnp.int32

HEAD_DIM = 64
RWKV_HEADS = 8
ATTN_HEADS = 8
IDX_HEADS = 8
IDX_DIM = 64
TOPK_MAX = 256
N_BUCKETS = 32
MAX_DISTANCE = 128
RMS_EPS = 1e-6
LNX_EPS = 64e-5

LANES = 128
VMEM_LIMIT = 56 * 1024 * 1024
RWKV_CHUNK = 64
RWKV_GROUP = 256
INV_BASE = 4
RWKV_TILE = 512
RWKV_UNROLL = 8
DSA_BLOCK = 256
VT_ROWS = 80
LOG2E = math.log2(math.e)
INPROJ_TILE = 512
MLP_TILE = 1024
MLP_FF_CHUNK = 1024
LORA_PAD = 384
K_IDX_OFF = IDX_HEADS * IDX_DIM
W_IDX_OFF = K_IDX_OFF + IDX_DIM
IDX_COLS = 640
KIX_COLS = 4 * IDX_DIM
NEG_BIG = -1e30
HI = lax.Precision.HIGHEST
NT = (((1,), (1,)), ((), ()))


def _bdot(a, b):
    return jnp.dot(a.astype(BF16), b.astype(BF16), preferred_element_type=F32)


def _hdot(a, b):
    return jnp.dot(a, b, precision=HI, preferred_element_type=F32)


def _params(*sem):
    return pltpu.CompilerParams(dimension_semantics=sem, vmem_limit_bytes=VMEM_LIMIT)


def _adaln_kernel(c_ref, w_ref, b_ref, o_ref):
    c = c_ref[...]
    c_act = c * jax.nn.sigmoid(c)
    o_ref[...] = _hdot(c_act, w_ref[...]) + b_ref[...]


def _adaln(c, w_ada, b_ada):
    L, D, D6 = w_ada.shape
    B = c.shape[0]
    cb = 1024
    return pl.pallas_call(
        _adaln_kernel,
        grid=(L, D6 // cb),
        in_specs=[pl.BlockSpec((B, D), lambda l, j: (0, 0)),
                  pl.BlockSpec((None, D, cb), lambda l, j: (l, 0, j)),
                  pl.BlockSpec((None, 1, cb), lambda l, j: (l, 0, j))],
        out_specs=pl.BlockSpec((None, B, cb), lambda l, j: (l, 0, j)),
        out_shape=jax.ShapeDtypeStruct((L, B, D6), F32),
        compiler_params=_params("arbitrary", "arbitrary"),
        name="adaln",
    )(c, w_ada, b_ada.reshape(L, 1, D6))


def _norm_mod(x, g, sc, sh):
    ms = jnp.mean(x * x, axis=-1, keepdims=True)
    return (x * lax.rsqrt(ms + RMS_EPS) * g) * (1.0 + sc) + sh


def _inproj_kernel(x_ref, g_ref, sc_ref, sh_ref, wa_ref, wvt_ref, wbh_ref, wbl_ref,
                   rkvl_ref, qk_ref, vt_ref, qiw_ref, kix_ref):
    h = _norm_mod(x_ref[...], g_ref[...], sc_ref[...], sh_ref[...])
    hb = h.astype(BF16)
    hl = (h - hb.astype(F32)).astype(BF16)
    pa = jnp.dot(hb, wa_ref[...], preferred_element_type=F32)
    nr = rkvl_ref.shape[1]
    rkvl_ref[...] = pa[:, :nr]
    qk_ref[...] = pa[:, nr:].astype(BF16)
    vt = lax.dot_general(wvt_ref[...], hb, NT, preferred_element_type=F32)
    ones_row = lax.broadcasted_iota(I32, vt.shape, 0) % VT_ROWS == HEAD_DIM
    vt = jnp.where(ones_row, 1.0, vt).astype(BF16)
    for j in range(vt_ref.shape[0]):
        vt_ref[j] = vt[:, j * DSA_BLOCK:(j + 1) * DSA_BLOCK]
    pb = (jnp.dot(hb, wbh_ref[...], preferred_element_type=F32)
          + jnp.dot(hb, wbl_ref[...], preferred_element_type=F32)
          + jnp.dot(hl, wbh_ref[...], preferred_element_type=F32))
    qiw_ref[...] = pb
    ki = pb[:, K_IDX_OFF:W_IDX_OFF]
    kh = ki.astype(BF16)
    kl = (ki - kh.astype(F32)).astype(BF16)
    kix_ref[...] = jnp.concatenate([kh, kl, kh, jnp.zeros_like(kh)], axis=-1)


def _inproj(x2d, g, sc, sh, wa, wvt, wbh, wbl, T):
    N, D = x2d.shape
    nblk = INPROJ_TILE // DSA_BLOCK
    tm = INPROJ_TILE
    na = wa.shape[1]
    A = wvt.shape[0]
    nr = 3 * RWKV_HEADS * HEAD_DIM + 2 * LORA_PAD
    assert T % tm == 0 and (wa.shape[1] - nr) % LANES == 0
    nb = T // tm
    row = lambda i: (i, 0)
    per_b = lambda i: (i // nb, 0, 0)
    const = lambda i: (0, 0)
    once = pl.Buffered(1)
    return pl.pallas_call(
        _inproj_kernel,
        grid=(N // tm,),
        in_specs=[pl.BlockSpec((tm, D), row),
                  pl.BlockSpec((1, D), const),
                  pl.BlockSpec((None, 1, D), per_b),
                  pl.BlockSpec((None, 1, D), per_b),
                  pl.BlockSpec((D, na), const, pipeline_mode=once),
                  pl.BlockSpec((A, D), const, pipeline_mode=once),
                  pl.BlockSpec((D, IDX_COLS), const, pipeline_mode=once),
                  pl.BlockSpec((D, IDX_COLS), const, pipeline_mode=once)],
        out_specs=[pl.BlockSpec((tm, nr), row),
                   pl.BlockSpec((tm, na - nr), row),
                   pl.BlockSpec((nblk, A, DSA_BLOCK), lambda i: (i, 0, 0)),
                   pl.BlockSpec((tm, IDX_COLS), row),
                   pl.BlockSpec((tm, KIX_COLS), row)],
        out_shape=[jax.ShapeDtypeStruct((N, nr), F32),
                   jax.ShapeDtypeStruct((N, na - nr), BF16),
                   jax.ShapeDtypeStruct((N // DSA_BLOCK, A, DSA_BLOCK), BF16),
                   jax.ShapeDtypeStruct((N, IDX_COLS), F32),
                   jax.ShapeDtypeStruct((N, KIX_COLS), BF16)],
        compiler_params=_params("arbitrary"),
        name="inproj",
    )(x2d, g.reshape(1, D), sc, sh, wa, wvt, wbh, wbl)


def _head_ones(n):
    r = lax.broadcasted_iota(I32, (n, n), 0) // HEAD_DIM
    c = lax.broadcasted_iota(I32, (n, n), 1) // HEAD_DIM
    return jnp.where(r == c, 1.0, 0.0).astype(F32)


def _split(x):
    hi = x.astype(BF16)
    return hi, (x - hi.astype(F32)).astype(BF16)


def _split3(x):
    hi = x.astype(BF16)
    r = x - hi.astype(F32)
    mid = r.astype(BF16)
    return hi, mid, (r - mid.astype(F32)).astype(BF16)


def _block_diag(w, blocks):
    w = w.astype(BF16)
    return jnp.concatenate([jnp.where(m, w, jnp.zeros_like(w)) for m in blocks], axis=0)


def _block_diag_t(w, blocks):
    return jnp.concatenate([jnp.where(m, w, 0.0) for m in blocks], axis=0).T


def _fold_blocks(w):
    n = w.shape[1] // HEAD_DIM
    out = w[:HEAD_DIM]
    for h in range(1, n):
        out = out + w[h * HEAD_DIM:(h + 1) * HEAD_DIM]
    return out


def _mm(a, w):
    return jnp.dot(a.astype(BF16), w, preferred_element_type=F32)


def _rwkv_kernel(has_vres, TT, *refs):
    C = RWKV_CHUNK
    R = RWKV_HEADS * HEAD_DIM
    G = RWKV_GROUP
    NG = R // G
    NC = TT // C
    if has_vres:
        (rkvl_ref, vf_ref, mu_ref, w0_ref, db_ref, a0_ref, ib_ref, gb_ref, kk_ref, ka_ref,
         rk_ref, lg_ref, lb_ref, v0_ref, vb_ref, out_ref,
         prev_ref, S_ref, r_s, k_s, v_s, kk_s, b_s, lw_s, cum_s, y_s, q_s, y0_s, m_s, g_s) = refs
    else:
        (rkvl_ref, mu_ref, w0_ref, db_ref, a0_ref, ib_ref, gb_ref, kk_ref, ka_ref,
         rk_ref, lg_ref, lb_ref, out_ref, vfo_ref,
         prev_ref, S_ref, r_s, k_s, v_s, kk_s, b_s, lw_s, cum_s, y_s, q_s, y0_s, m_s, g_s) = refs

    @pl.when(pl.program_id(1) == 0)
    def _():
        prev_ref[...] = jnp.zeros_like(prev_ref)
        S_ref[...] = jnp.zeros_like(S_ref)

    row0 = lax.broadcasted_iota(I32, (TT, 1), 0) == 0
    prev = prev_ref[...]

    def shift(z, p):
        return jnp.where(row0, p, pltpu.roll(z, 1, 0))

    rkv = rkvl_ref[:, :3 * R]
    p1 = rkvl_ref[:, 3 * R:3 * R + LORA_PAD]
    p2 = rkvl_ref[:, 3 * R + LORA_PAD:]
    rkv_new = rkv + (shift(rkv, prev[:, :3 * R]) - rkv) * mu_ref[...]
    lora = p1 + shift(p2, prev[:, 3 * R:])
    prev_ref[...] = jnp.concatenate([rkv[TT - 1:TT, :], p2[TT - 1:TT, :]], axis=-1)

    r = rkv_new[:, :R]
    k = rkv_new[:, R:2 * R]
    v = rkv_new[:, 2 * R:]
    ones_h = _head_ones(G).astype(BF16)

    def head_sum(x):
        hi, lo = _split(x)
        return jnp.concatenate(
            [jnp.dot(hi[:, gi * G:(gi + 1) * G], ones_h, preferred_element_type=F32)
             + jnp.dot(lo[:, gi * G:(gi + 1) * G], ones_h, preferred_element_type=F32) for gi in range(NG)], axis=1)

    o_a = db_ref.shape[0]
    o_g = o_a + ib_ref.shape[0]
    o_v = o_g + gb_ref.shape[0]
    wlog = w0_ref[...] + _bdot(jnp.tanh(lora[:, :o_a]), db_ref[...])
    z = -wlog
    wlog = -(jnp.maximum(z, 0.0) + jnp.log(1.0 + jnp.exp(-jnp.abs(z)))) - 0.5
    lw = -jnp.exp(wlog)
    lw_s[...] = lw
    slab = min(TT, 4 * C)
    tr = lax.broadcasted_iota(I32, (slab, slab), 0)
    tc = lax.broadcasted_iota(I32, (slab, slab), 1)
    tri = jnp.where((tc <= tr) & (tr // C == tc // C), 1.0, 0.0).astype(BF16)
    parts = _split3(lw)
    for s0 in range(0, TT, slab):
        cum_s[s0:s0 + slab, :] = sum(jnp.dot(tri, part[s0:s0 + slab], preferred_element_type=F32) for part in parts)
    a = jax.nn.sigmoid(a0_ref[...] + _bdot(lora[:, o_a:o_g], ib_ref[...]))
    g = _bdot(jax.nn.sigmoid(lora[:, o_g:o_v]), gb_ref[...])
    kkr = k * kk_ref[...]
    kk = kkr * lax.rsqrt(jnp.maximum(head_sum(kkr * kkr), 1e-24))
    k = k * (1.0 + (a - 1.0) * ka_ref[...])
    if has_vres:
        v = v + (vf_ref[...] - v) * jax.nn.sigmoid(
            v0_ref[...] + _bdot(lora[:, o_v:o_v + vb_ref.shape[0]], vb_ref[...]))
    else:
        vfo_ref[...] = v
    r_s[...] = r
    k_s[...] = k
    v_s[...] = v
    kk_s[...] = kk
    b_s[...] = kk * a

    lane_g = lax.broadcasted_iota(I32, (C, RWKV_GROUP), 1)
    row_g = lax.broadcasted_iota(I32, (C, RWKV_GROUP), 0)
    blocks = [lane_g // HEAD_DIM == h for h in range(RWKV_GROUP // HEAD_DIM)]
    local = lane_g % HEAD_DIM
    strict = local < row_g
    incl = local <= row_g
    diag = local == row_g
    eye_cat = jnp.where(diag, 1.0, 0.0).astype(F32)
    base_mask = strict & (local // INV_BASE == row_g // INV_BASE)
    grow_masks = []
    s = INV_BASE
    while s < C:
        grow_masks.append((local // (2 * s) == row_g // (2 * s)) & (row_g % (2 * s) >= s) & (local % (2 * s) < s))
        s *= 2

    def local_chunk(it, carry):
        chains = [(u, slice(gi * G, (gi + 1) * G)) for u in range(RWKV_UNROLL) for gi in range(NG)]
        each = lambda f: [f(i) for i in range(len(chains))]
        rd, kkd, kt, bt, kc, bc, vv, plast = ([] for _ in range(8))
        for u in range(RWKV_UNROLL):
            sl = pl.ds(pl.multiple_of((it * RWKV_UNROLL + u) * C, C), C)
            lw = lw_s[sl, :]
            cum = cum_s[sl, :]
            cl = cum[C - 1:C, :]
            pinv = jnp.exp(-cum)
            pc = jnp.exp(cl - cum)
            k_a = k_s[sl, :]
            b_a = b_s[sl, :]
            full = (r_s[sl, :] * jnp.exp(cum), kk_s[sl, :] * jnp.exp(cum - lw), k_a * pinv, b_a * pinv,
                    k_a * pc, b_a * pc, v_s[sl, :], jnp.exp(cl))
            for dst, x in zip((rd, kkd, kt, bt, kc, bc, vv, plast), full):
                dst.extend(x[:, gs] for (uu, gs) in chains if uu == u)
        kt_w = each(lambda i: _block_diag_t(kt[i], blocks).astype(BF16))
        bt_w = each(lambda i: _block_diag_t(bt[i], blocks).astype(BF16))
        bc_t = each(lambda i: _fold_blocks(_block_diag_t(bc[i], blocks)))
        kc_t = each(lambda i: _fold_blocks(_block_diag_t(kc[i], blocks)))
        rr = each(lambda i: _mm(jnp.concatenate([kkd[i], rd[i]], axis=0),
                                jnp.concatenate([kt_w[i], bt_w[i]], axis=1)))
        akk = each(lambda i: jnp.where(strict, rr[i][:C, :G], 0.0))
        ark = each(lambda i: jnp.where(incl, rr[i][C:, :G], 0.0))
        arb = each(lambda i: jnp.where(incl, rr[i][C:, G:], 0.0))
        nmat = each(lambda i: jnp.where(strict, -rr[i][:C, G:], 0.0))
        nb = each(lambda i: jnp.where(base_mask, nmat[i], 0.0))
        sq = each(lambda i: _mm(nb[i], _block_diag(nb[i], blocks)))
        x = each(lambda i: eye_cat + nb[i])
        rr = each(lambda i: _mm(x[i], _block_diag(sq[i], blocks)))
        x = each(lambda i: x[i] + rr[i])
        for lower_left in grow_masks:
            u = each(lambda i: _mm(jnp.where(lower_left, nmat[i], 0.0), _block_diag(x[i], blocks)))
            rr = each(lambda i: _mm(x[i], _block_diag(u[i], blocks)))
            x = each(lambda i: x[i] + rr[i])
        rv = each(lambda i: _mm(jnp.concatenate([akk[i], ark[i], kc_t[i]], axis=0), _block_diag(vv[i], blocks)))
        ru = each(lambda i: _mm(x[i], jnp.concatenate([_block_diag(kkd[i], blocks),
                                                       _block_diag(rv[i][:C], blocks)], axis=1)))
        rr = each(lambda i: _mm(jnp.concatenate([arb[i], bc_t[i]], axis=0),
                                jnp.concatenate([_block_diag(ru[i][:, :G], blocks),
                                                 _block_diag(ru[i][:, G:], blocks)], axis=1)))
        for i, (u, gs) in enumerate(chains):
            c = it * RWKV_UNROLL + u
            q_s[c, :, gs] = rd[i] - rr[i][:C, :G]
            y0_s[c, :, gs] = rv[i][C:2 * C] - rr[i][:C, G:]
            dterm = jnp.where(diag, jnp.broadcast_to(plast[i], (C, G)), 0.0)
            m_s[c, :, gs] = dterm - rr[i][C:, :G]
            g_s[c, :, gs] = rv[i][2 * C:] - rr[i][C:, G:]
        return carry

    lax.fori_loop(0, NC // RWKV_UNROLL, local_chunk, 0)

    def scan_chunk(c, carry):
        sl = pl.ds(pl.multiple_of(c * C, C), C)
        for gi in range(NG):
            gs = slice(gi * RWKV_GROUP, (gi + 1) * RWKV_GROUP)
            rr = _mm(jnp.concatenate([q_s[c, :, gs], m_s[c, :, gs]], axis=0), _block_diag(S_ref[:, gs], blocks))
            y_s[sl, gs] = rr[:C] + y0_s[c, :, gs]
            S_ref[:, gs] = rr[C:] + g_s[c, :, gs]
        return carry

    lax.fori_loop(0, NC, scan_chunk, 0)

    y = y_s[...]
    inv_n = 1.0 / HEAD_DIM
    mean = head_sum(y) * inv_n
    yc = y - mean
    var = head_sum(yc * yc) * inv_n
    yn = yc * lax.rsqrt(var + LNX_EPS) * lg_ref[...] + lb_ref[...]
    bonus = head_sum(r_s[...] * k_s[...] * rk_ref[...]) * v_s[...]
    out_ref[...] = ((yn + bonus) * g).astype(out_ref.dtype)


def _rwkv(rkvl, v_first, p, B, T):
    N, nc = rkvl.shape
    R = RWKV_HEADS * HEAD_DIM
    TT = min(RWKV_TILE, T)
    nt = T // TT
    NC = TT // RWKV_CHUNK
    has_vres = v_first is not None
    row = lambda b, i: (b * nt + i, 0)
    const = lambda b, i: (0, 0)
    vec = lambda a: a.reshape(1, -1)
    ins = [rkvl]
    specs = [pl.BlockSpec((TT, nc), row)]
    if has_vres:
        ins.append(v_first)
        specs.append(pl.BlockSpec((TT, R), row))
    small = [vec(p["mu_rkv"]), vec(p["decay_w0"]), p["decay_b"].astype(BF16), vec(p["iclr_a0"]),
             p["iclr_b"].astype(BF16), p["gate_b"].astype(BF16), vec(p["k_k"]), vec(p["k_a"]),
             vec(p["r_k"]), vec(p["lnx_g"]), vec(p["lnx_b"])]
    if has_vres:
        small += [vec(p["vres_v0"]), p["vres_b"].astype(BF16)]
    ins += small
    specs += [pl.BlockSpec(a.shape, const) for a in small]
    out_shape = [jax.ShapeDtypeStruct((N, R), BF16)]
    out_specs = [pl.BlockSpec((TT, R), row)]
    if not has_vres:
        out_shape.append(jax.ShapeDtypeStruct((N, R), F32))
        out_specs.append(pl.BlockSpec((TT, R), row))
    scratch = [pltpu.VMEM((1, 3 * R + LORA_PAD), F32),
               pltpu.VMEM((HEAD_DIM, R), F32)]
    scratch += [pltpu.VMEM((TT, R), F32) for _ in range(8)]
    scratch += [pltpu.VMEM((NC, RWKV_CHUNK, R), F32) for _ in range(4)]
    res = pl.pallas_call(
        functools.partial(_rwkv_kernel, has_vres, TT),
        grid=(B, nt),
        in_specs=specs,
        out_specs=out_specs,
        out_shape=out_shape,
        scratch_shapes=scratch,
        compiler_params=_params("arbitrary", "arbitrary"),
        name="rwkv",
    )(*ins)
    if has_vres:
        return res[0], v_first
    return res[0], res[1]


def _bucket_boundaries():
    max_exact = N_BUCKETS // 2
    d = np.arange(0, 4 * MAX_DISTANCE, dtype=np.int64)
    nf = np.maximum(d, 1).astype(np.float32)
    large = max_exact + (np.log(nf / np.float32(max_exact)) / np.float32(math.log(MAX_DISTANCE / max_exact))
                         * np.float32(N_BUCKETS - max_exact)).astype(np.int32)
    large = np.minimum(large, N_BUCKETS - 1)
    bucket = np.where(d < max_exact, d, large)
    return [int(np.argmax(bucket >= j)) for j in range(max_exact + 1, N_BUCKETS)]


_BUCKET_STARTS = _bucket_boundaries()
KEY_NEG_INF = int(np.int32(np.array(-np.inf, np.float32).view(np.int32)) ^ np.int32(0x7FFFFFFF))
INT_MIN = -2 ** 31
INT_MAX = 2 ** 31 - 1
FAST_SELECT_BLIND = 12
FAST_SELECT_STEPS = 2
FAST_SELECT_TRIPS = 10
ACC_ROWS = 32


def _sort_key(s):
    bits = pltpu.bitcast(s, I32)
    bits = jnp.where(bits == INT_MIN, 0, bits)
    return bits ^ ((bits >> 31) & jnp.int32(0x7FFFFFFF))


def _key_to_float(k):
    return pltpu.bitcast(k ^ ((k >> 31) & jnp.int32(0x7FFFFFFF)), F32)


def _dsa_kernel(T, ksel, q_ref, k_ref, vt_ref, qiw_ref, kix_ref, relb_ref, g_ref, out_ref,
                lhs_s, qpad_s, key_s, madd_s, bias_s, m_s, acc_s, s_s, p_s, cmax_s, alpha_s):
    TQ = TK = DSA_BLOCK
    H = ATTN_HEADS
    qb = pl.program_id(1)
    nch = qb + 1
    rows = lax.broadcasted_iota(I32, (TK, TQ), 0)
    cols = lax.broadcasted_iota(I32, (TK, TQ), 1)

    @pl.when((pl.program_id(0) == 0) & (qb == 0))
    def _():
        max_exact = N_BUCKETS // 2
        for off in range(2):
            d = cols - rows + off * TK
            bucket = jnp.where(d < max_exact, jnp.maximum(d, 0), max_exact)
            for start in _BUCKET_STARTS:
                bucket = bucket + jnp.where(d >= start, 1, 0)
            for h in range(H):
                far = relb_ref[N_BUCKETS - 1, h]
                tile = jnp.zeros((TK, TQ), F32)
                for bk in range(N_BUCKETS - 1):
                    tile = jnp.where(bucket == bk, (relb_ref[bk, h] - far) * LOG2E, tile)
                bias_s[h, off] = tile

    qi_t = (qiw_ref[:, :IDX_HEADS * IDX_DIM] * (IDX_DIM ** -0.5)).T
    for h in range(IDX_HEADS):
        qh = qi_t[h * IDX_DIM:(h + 1) * IDX_DIM, :]
        hi = qh.astype(BF16)
        lo = (qh - hi.astype(F32)).astype(BF16)
        lhs_s[h] = jnp.concatenate([hi, hi, lo, jnp.zeros_like(hi)], axis=0)
    w_t = qiw_ref[:, K_IDX_OFF:IDX_COLS].T[IDX_DIM:IDX_DIM + IDX_HEADS, :] * (IDX_HEADS ** -0.5)

    q_t = q_ref[...].astype(F32).T
    zeros_h = jnp.zeros((HEAD_DIM, TQ), BF16)
    for h in range(H):
        qh = q_t[h * HEAD_DIM:(h + 1) * HEAD_DIM, :].astype(BF16)
        qpad_s[h] = jnp.concatenate([qh, zeros_h] if h % 2 == 0 else [zeros_h, qh], axis=0)

    def scores(j):
        kx = kix_ref[pl.ds(pl.multiple_of(j * TK, TK), TK), :]
        acc = jnp.zeros((TK, TQ), F32)
        for h in range(IDX_HEADS):
            d = jnp.dot(kx, lhs_s[h], preferred_element_type=F32)
            acc = acc + w_t[h:h + 1, :] * jnp.maximum(d, 0.0)
        return acc

    def stats(st, kk, mn_of):
        fold = lambda z: z.reshape(TK // ACC_ROWS, ACC_ROWS, TQ)
        neg, pos, mn, mx = st
        return (neg + jnp.sum(fold(kk >> 31), axis=0),
                pos + jnp.sum(fold((-kk) >> 31), axis=0),
                jnp.minimum(mn, jnp.min(fold(mn_of(kk)), axis=0)),
                jnp.maximum(mx, jnp.max(fold(kk), axis=0)))

    def score_body(j, st):
        kk = _sort_key(scores(j))
        key_s[j] = kk
        return stats(st, kk, lambda z: z)

    st = lax.fori_loop(0, qb, score_body,
                       (jnp.zeros((ACC_ROWS, TQ), I32), jnp.zeros((ACC_ROWS, TQ), I32),
                        jnp.full((ACC_ROWS, TQ), INT_MAX, I32), jnp.full((ACC_ROWS, TQ), INT_MIN, I32)))
    kd = _sort_key(jnp.where(rows <= cols, scores(qb), -jnp.inf))
    key_s[qb] = kd
    st = stats(st, kd, lambda z: jnp.where(z > KEY_NEG_INF, z, INT_MAX))
    n_ge0 = nch * TK + jnp.sum(st[0], axis=0, keepdims=True)
    n_gt0 = -jnp.sum(st[1], axis=0, keepdims=True)
    lo0 = _key_to_float(jnp.min(st[2], axis=0, keepdims=True))
    hi0 = _key_to_float(jnp.max(st[3], axis=0, keepdims=True) + 1)

    def count(pred):
        def body(j, acc):
            hit = jnp.where(pred(key_s[j]), 1, 0)
            return acc + jnp.sum(hit.reshape(TK // ACC_ROWS, ACC_ROWS, TQ), axis=0)
        acc = lax.fori_loop(0, nch, body, jnp.zeros((ACC_ROWS, TQ), I32))
        return jnp.sum(acc, axis=0, keepdims=True)

    def write_mask(sel_of):
        def body(j, c):
            madd_s[j] = jnp.where(sel_of(key_s[j]), 0.0, NEG_BIG)
            return c
        lax.fori_loop(0, nch, body, 0)

    kf = float(ksel)
    n_adm = qb * TQ + lax.broadcasted_iota(I32, (1, TQ), 1) + 1
    trivial = n_adm <= ksel
    positive = n_gt0 > ksel
    done0 = trivial | ((n_gt0 <= ksel) & (n_ge0 >= ksel))
    thr0 = jnp.where(trivial, KEY_NEG_INF + 1, jnp.where(n_gt0 == ksel, 1, 0))
    need0 = jnp.where(n_gt0 < ksel, ksel - n_gt0, ksel)
    a0 = jnp.where(positive, 0.0, lo0)
    b0 = jnp.where(positive, hi0, 0.0)
    cb0 = jnp.where(positive, 0, n_ge0)

    def excess(cnt):
        return jnp.log(cnt.astype(F32) + 0.5) - math.log(kf + 0.5)

    fa0 = excess(jnp.where(positive, n_gt0, n_adm))
    as_int = lambda m: jnp.where(m, 1, 0)

    def any_lane(m):
        return jnp.max(as_int(m)) > 0

    def fast_cond(st):
        it, done, stuck = st[0], st[1], st[2]
        return (it < FAST_SELECT_TRIPS) & any_lane((done + stuck) == 0)

    def fast_body(st):
        it, st = st[0], st[1:]
        for _ in range(FAST_SELECT_STEPS):
            st = fast_step(st)
        return (it + 1,) + st

    def fast_step(st):
        done, stuck, thr, need, a, b, fa, fb, cb, last = st
        c = a + (b - a) * (fa / (fa - fb))
        c = jnp.where((c > a) & (c < b), c, 0.5 * a + 0.5 * b)
        inside = (c > a) & (c < b)
        ckey = _sort_key(c)
        cnt = count(lambda kk: kk >= ckey)
        fc = excess(cnt)
        hit = inside & (cnt == ksel) & (done == 0)
        thr = jnp.where(hit, ckey, thr)
        need = jnp.where(hit, ksel, need)
        done = jnp.where(hit, 1, done)
        stuck = jnp.where(inside, stuck, 1)
        up = inside & (cnt > ksel)
        dn = inside & (cnt < ksel)
        fb = jnp.where(up & (last == 1), 0.5 * fb, fb)
        fa = jnp.where(dn & (last == 0), 0.5 * fa, fa)
        a, fa = jnp.where(up, c, a), jnp.where(up, fc, fa)
        b, fb, cb = jnp.where(dn, c, b), jnp.where(dn, fc, fb), jnp.where(dn, cnt, cb)
        return done, stuck, thr, need, a, b, fa, fb, cb, jnp.where(up, 1, jnp.where(dn, 0, last))

    st = (as_int(done0), jnp.zeros((1, TQ), I32), thr0, need0,
          a0, b0, fa0, excess(cb0), cb0, jnp.full((1, TQ), -1, I32))
    blind = jnp.where((qb + 1) * TQ <= ksel, 0, FAST_SELECT_BLIND)
    st = lax.fori_loop(0, blind, lambda i, s: fast_step(s), st)
    st = lax.while_loop(fast_cond, fast_body, (jnp.int32(0),) + st)
    done, thr, need, a, b, cb = st[1], st[3], st[4], st[5], st[6], st[9]
    open_lane = done == 0

    def close_cond(st):
        return any_lane((st[1] - st[0]) > 1)

    def close_body(st):
        ak, bk, cb = st
        mid = ak + ((bk - ak) >> 1)
        cnt = count(lambda kk: kk >= mid)
        ge = cnt >= ksel
        return jnp.where(ge, mid, ak), jnp.where(ge, bk, mid), jnp.where(ge, cb, cnt)

    ak, _, cb = lax.while_loop(close_cond, close_body, (
        jnp.where(open_lane, _sort_key(a), thr), jnp.where(open_lane, _sort_key(b), thr), cb))
    thr = jnp.where(open_lane, ak, thr)
    need = jnp.where(open_lane, ksel - cb, need)
    any_tie = any_lane(open_lane | (jnp.logical_not(trivial) & (n_gt0 < ksel) & (n_ge0 > ksel)))

    @pl.when(jnp.logical_not(any_tie))
    def _():
        write_mask(lambda kk: kk >= thr)

    @pl.when(any_tie)
    def _():
        kr = lax.broadcasted_iota(I32, (TK, TK), 0)
        kc = lax.broadcasted_iota(I32, (TK, TK), 1)
        lower = jnp.where(kc <= kr, 1.0, 0.0).astype(BF16)
        need_f = need.astype(F32)

        def body(j, run):
            kk = key_s[j]
            eq = kk == thr
            eqf = jnp.where(eq, 1.0, 0.0)
            rank = run + jnp.dot(lower, eqf.astype(BF16), preferred_element_type=F32)
            tied_in = jnp.where(eq, rank, float(TOPK_MAX + 1)) <= need_f
            madd_s[j] = jnp.where(kk > thr, 0.0, jnp.where(tied_in, 0.0, NEG_BIG))
            return run + jnp.sum(eqf, axis=0, keepdims=True)
        lax.fori_loop(0, nch, body, jnp.zeros((1, TQ), F32))

    m_s[...] = jnp.full(m_s.shape, NEG_BIG, F32)
    acc_s[...] = jnp.zeros(acc_s.shape, F32)

    def attend(j, bias_of):
        ks = pl.ds(pl.multiple_of(j * TK, TK), TK)
        ma = madd_s[j]
        for h in range(H):
            kp = k_ref[ks, LANES * (h // 2):LANES * (h // 2 + 1)]
            s = jnp.dot(kp, qpad_s[h], preferred_element_type=F32) + ma
            bias = bias_of(h)
            if bias is not None:
                s = s + bias
            s_s[h] = s
            cmax_s[h] = jnp.max(s.reshape(TK // 8, 8, TQ), axis=0)
        for h in range(H):
            m_old = m_s[h]
            m_new = jnp.maximum(m_old, jnp.max(cmax_s[h], axis=0, keepdims=True))
            alpha_s[h] = jnp.exp2(m_old - m_new)
            m_s[h] = m_new
            p_s[h] = jnp.exp2(s_s[h] - m_new).astype(BF16)
        for h in range(H):
            hs = slice(h * VT_ROWS, (h + 1) * VT_ROWS)
            pv = jnp.dot(vt_ref[j, hs, :], p_s[h], preferred_element_type=F32)
            acc_s[hs, :] = alpha_s[h] * acc_s[hs, :] + pv

    def far_body(j, c):
        attend(j, lambda h: None)
        return c

    lax.fori_loop(0, jnp.maximum(qb - 1, 0), far_body, 0)

    @pl.when(qb >= 1)
    def _():
        attend(qb - 1, lambda h: bias_s[h, 1])

    attend(qb, lambda h: bias_s[h, 0])

    outs = []
    for h in range(H):
        o = acc_s[h * VT_ROWS:h * VT_ROWS + HEAD_DIM, :] / acc_s[h * VT_ROWS + HEAD_DIM:h * VT_ROWS + HEAD_DIM + 1, :]
        ms = jnp.mean(o * o, axis=0, keepdims=True)
        outs.append(o * lax.rsqrt(ms + RMS_EPS))
    out_ref[...] = (jnp.concatenate(outs, axis=0).T * g_ref[...]).astype(out_ref.dtype)


def _dsa(qk, vt, qiw, kix, rel_bias, g, B, T):
    N = qk.shape[0]
    TQ = DSA_BLOCK
    nq = T // TQ
    A = ATTN_HEADS * HEAD_DIM
    ksel = min(TOPK_MAX, T // 4)
    assert MAX_DISTANCE <= TQ + 1
    return pl.pallas_call(
        functools.partial(_dsa_kernel, T, ksel),
        grid=(B, nq),
        in_specs=[pl.BlockSpec((TQ, A), lambda b, i: (b * nq + i, 0)),
                  pl.BlockSpec((T, A), lambda b, i: (b, 1)),
                  pl.BlockSpec((nq, ATTN_HEADS * VT_ROWS, TQ), lambda b, i: (b, 0, 0)),
                  pl.BlockSpec((TQ, IDX_COLS), lambda b, i: (b * nq + i, 0)),
                  pl.BlockSpec((T, KIX_COLS), lambda b, i: (b, 0)),
                  pl.BlockSpec(memory_space=pltpu.SMEM),
                  pl.BlockSpec((1, A), lambda b, i: (0, 0))],
        out_specs=pl.BlockSpec((TQ, A), lambda b, i: (b * nq + i, 0)),
        out_shape=jax.ShapeDtypeStruct((N, A), BF16),
        scratch_shapes=[pltpu.VMEM((IDX_HEADS, KIX_COLS, TQ), BF16),
                        pltpu.VMEM((ATTN_HEADS, LANES, TQ), BF16),
                        pltpu.VMEM((nq, TQ, TQ), I32),
                        pltpu.VMEM((nq, TQ, TQ), F32),
                        pltpu.VMEM((ATTN_HEADS, 2, TQ, TQ), F32),
                        pltpu.VMEM((ATTN_HEADS, 1, TQ), F32),
                        pltpu.VMEM((ATTN_HEADS * VT_ROWS, TQ), F32),
                        pltpu.VMEM((ATTN_HEADS, TQ, TQ), F32),
                        pltpu.VMEM((ATTN_HEADS, TQ, TQ), BF16),
                        pltpu.VMEM((ATTN_HEADS, 8, TQ), F32),
                        pltpu.VMEM((ATTN_HEADS, 1, TQ), F32)],
        compiler_params=_params("arbitrary", "arbitrary"),
        name="dsa",
    )(qk, qk, vt, qiw, kix, rel_bias, g.reshape(1, A))


def _mix_mlp_kernel(final, x_ref, rw_ref, att_ref, wo_ref, gt1_ref, g_ref, sc_ref, sh_ref, gt2_ref,
                    w1_ref, w2_ref, fg_ref, o_ref, x_s, h_s, acc_s):
    j = pl.program_id(1)

    @pl.when(j == 0)
    def _():
        R = rw_ref.shape[1]
        mixed = (jnp.dot(rw_ref[...], wo_ref[:R, :], preferred_element_type=F32)
                 + jnp.dot(att_ref[...], wo_ref[R:, :], preferred_element_type=F32))
        x1 = x_ref[...] + gt1_ref[...] * mixed
        x_s[...] = x1
        h_s[...] = _norm_mod(x1, g_ref[...], sc_ref[...], sh_ref[...]).astype(BF16)
        acc_s[...] = jnp.zeros_like(acc_s)

    u = jnp.dot(h_s[...], w1_ref[...], preferred_element_type=F32)
    u = jnp.square(jnp.maximum(u, 0.0))
    acc_s[...] += jnp.dot(u.astype(BF16), w2_ref[...], preferred_element_type=F32)

    @pl.when(j == pl.num_programs(1) - 1)
    def _():
        y = x_s[...] + gt2_ref[...] * acc_s[...]
        if final:
            ms = jnp.mean(y * y, axis=-1, keepdims=True)
            y = y * lax.rsqrt(ms + RMS_EPS) * fg_ref[...]
        o_ref[...] = y


def _mix_mlp(x2d, rw, att, wo, gt1, g, sc, sh, gt2, w1, w2, final_g, final, T):
    N, D = x2d.shape
    F = w1.shape[1]
    fc = min(MLP_FF_CHUNK, F)
    tm = min(MLP_TILE, T)
    nb = T // tm
    row = lambda i, j: (i, 0)
    per_b = lambda i, j: (i // nb, 0, 0)
    const = lambda i, j: (0, 0)
    return pl.pallas_call(
        functools.partial(_mix_mlp_kernel, final),
        grid=(N // tm, F // fc),
        in_specs=[pl.BlockSpec((tm, D), row),
                  pl.BlockSpec((tm, rw.shape[1]), row),
                  pl.BlockSpec((tm, att.shape[1]), row),
                  pl.BlockSpec(wo.shape, const, pipeline_mode=pl.Buffered(1)),
                  pl.BlockSpec((None, 1, D), per_b),
                  pl.BlockSpec((1, D), const),
                  pl.BlockSpec((None, 1, D), per_b),
                  pl.BlockSpec((None, 1, D), per_b),
                  pl.BlockSpec((None, 1, D), per_b),
                  pl.BlockSpec((D, fc), lambda i, j: (0, j)),
                  pl.BlockSpec((fc, D), lambda i, j: (j, 0)),
                  pl.BlockSpec((1, D), const)],
        out_specs=pl.BlockSpec((tm, D), row),
        out_shape=jax.ShapeDtypeStruct((N, D), F32),
        scratch_shapes=[pltpu.VMEM((tm, D), F32), pltpu.VMEM((tm, D), BF16), pltpu.VMEM((tm, D), F32)],
        compiler_params=_params("arbitrary", "arbitrary"),
        name="mix_mlp",
    )(x2d, rw, att, wo, gt1, g.reshape(1, D), sc, sh, gt2, w1, w2, final_g.reshape(1, D))


def _pad_cols(w, n):
    return jnp.pad(w, ((0, 0), (0, n - w.shape[1])))


def _split_bf16(w):
    hi = w.astype(BF16)
    return hi, (w - hi.astype(F32)).astype(BF16)


def _in_weights(l, w_in, mu_lora, decay_a, iclr_a, gate_a, vres_mu, vres_a):
    D = w_in.shape[1]
    R = RWKV_HEADS * HEAD_DIM
    w = w_in[l]
    mats = [(decay_a[l], mu_lora[l, 0]), (iclr_a[l], mu_lora[l, 1]), (gate_a[l], mu_lora[l, 2])]
    if l > 0:
        mats.append((vres_a[l - 1], vres_mu[l - 1]))
    now = _pad_cols(jnp.concatenate([a * (1.0 - mu)[:, None] for a, mu in mats], axis=1), LORA_PAD)
    prev = _pad_cols(jnp.concatenate([a * mu[:, None] for a, mu in mats], axis=1), LORA_PAD)
    wq = w[:, 3 * R:4 * R] * (HEAD_DIM ** -0.5 * LOG2E)
    wa = jnp.concatenate([w[:, :3 * R], now, prev, wq, w[:, 4 * R:5 * R]], axis=1).astype(BF16)
    wv = w[:, 5 * R:6 * R].reshape(D, ATTN_HEADS, HEAD_DIM)
    wv = jnp.pad(wv, ((0, 0), (0, 0), (0, VT_ROWS - HEAD_DIM))).reshape(D, ATTN_HEADS * VT_ROWS)
    wvt = wv.T.astype(BF16)
    wbh, wbl = _split_bf16(_pad_cols(w[:, 6 * R:], IDX_COLS))
    return wa, wvt, wbh, wbl


def kernel(x, c, w_ada, b_ada, norm1_g, norm2_g, w_in, mu_rkv, mu_lora, decay_w0, decay_a, decay_b, iclr_a0, iclr_a, iclr_b, gate_a, gate_b, k_k, k_a, r_k, lnx_g, lnx_b, vres_mu, vres_v0, vres_a, vres_b, attn_out_g, rel_bias, w_out, w_mlp1, w_mlp2, final_g):
    B, T, D = x.shape
    depth = w_in.shape[0]
    mod = _adaln(c, w_ada, b_ada)
    mod = mod.reshape(depth, B, 6, 1, D).transpose(0, 2, 1, 3, 4)
    x2d = x.reshape(B * T, D)
    v_first = None
    for l in range(depth):
        sh1, sc1, gt1, sh2, sc2, gt2 = (mod[l, i] for i in range(6))
        wa, wvt, wbh, wbl = _in_weights(l, w_in, mu_lora, decay_a, iclr_a, gate_a, vres_mu, vres_a)
        rkvl, qk, vt, qiw, kix = _inproj(x2d, norm1_g[l], sc1, sh1, wa, wvt, wbh, wbl, T)
        p = dict(mu_rkv=mu_rkv[l], decay_w0=decay_w0[l], decay_b=decay_b[l], iclr_a0=iclr_a0[l],
                 iclr_b=iclr_b[l], gate_b=gate_b[l], k_k=k_k[l], k_a=k_a[l], r_k=r_k[l],
                 lnx_g=lnx_g[l], lnx_b=lnx_b[l])
        if l > 0:
            p.update(vres_v0=vres_v0[l - 1], vres_b=vres_b[l - 1])
        rw, v_first = _rwkv(rkvl, v_first, p, B, T)
        att = _dsa(qk, vt, qiw, kix, rel_bias, attn_out_g[l], B, T)
        x2d = _mix_mlp(x2d, rw, att, w_out[l].astype(BF16), gt1, norm2_g[l], sc2, sh2, gt2,
                       w_mlp1[l].astype(BF16), w_mlp2[l].astype(BF16), final_g, l == depth - 1, T)
    return x2d.reshape(B, T, D)
```

```python
import functools
import math

import numpy as np
import jax
import jax.numpy as jnp
from jax import lax
from jax.experimental import pallas as pl
from jax.experimental.pallas import tpu as pltpu

F32 = jnp.float32
BF16 = jnp.bfloat16
I32 = jnp.int32

HEAD_DIM = 64
RWKV_HEADS = 8
ATTN_HEADS = 8
IDX_HEADS = 8
IDX_DIM = 64
TOPK_MAX = 256
N_BUCKETS = 32
MAX_DISTANCE = 128
RMS_EPS = 1e-6
LNX_EPS = 64e-5

LANES = 128
VMEM_LIMIT = 56 * 1024 * 1024
RWKV_CHUNK = 64
RWKV_GROUP = 256
INV_BASE = 4
RWKV_TILE = 512
RWKV_UNROLL = 8
DSA_BLOCK = 256
VT_ROWS = 80
LOG2E = math.log2(math.e)
INPROJ_TILE = 512
MLP_TILE = 1024
MLP_FF_CHUNK = 1024
LORA_PAD = 384
K_IDX_OFF = IDX_HEADS * IDX_DIM
W_IDX_OFF = K_IDX_OFF + IDX_DIM
IDX_COLS = 640
KIX_COLS = 4 * IDX_DIM
NEG_BIG = -1e30
HI = lax.Precision.HIGHEST
NT = (((1,), (1,)), ((), ()))


def _bdot(a, b):
    return jnp.dot(a.astype(BF16), b.astype(BF16), preferred_element_type=F32)


def _hdot(a, b):
    return jnp.dot(a, b, precision=HI, preferred_element_type=F32)


def _params(*sem):
    return pltpu.CompilerParams(dimension_semantics=sem, vmem_limit_bytes=VMEM_LIMIT)


def _adaln_kernel(c_ref, w_ref, b_ref, o_ref):
    c = c_ref[...]
    c_act = c * jax.nn.sigmoid(c)
    o_ref[...] = _hdot(c_act, w_ref[...]) + b_ref[...]


def _adaln(c, w_ada, b_ada):
    L, D, D6 = w_ada.shape
    B = c.shape[0]
    cb = 1024
    return pl.pallas_call(
        _adaln_kernel,
        grid=(L, D6 // cb),
        in_specs=[pl.BlockSpec((B, D), lambda l, j: (0, 0)),
                  pl.BlockSpec((None, D, cb), lambda l, j: (l, 0, j)),
                  pl.BlockSpec((None, 1, cb), lambda l, j: (l, 0, j))],
        out_specs=pl.BlockSpec((None, B, cb), lambda l, j: (l, 0, j)),
        out_shape=jax.ShapeDtypeStruct((L, B, D6), F32),
        compiler_params=_params("arbitrary", "arbitrary"),
        name="adaln",
    )(c, w_ada, b_ada.reshape(L, 1, D6))


def _norm_mod(x, g, sc, sh):
    ms = jnp.mean(x * x, axis=-1, keepdims=True)
    return (x * lax.rsqrt(ms + RMS_EPS) * g) * (1.0 + sc) + sh


def _inproj_kernel(x_ref, g_ref, sc_ref, sh_ref, wa_ref, wvt_ref, wbh_ref, wbl_ref,
                   rkvl_ref, qk_ref, vt_ref, qiw_ref, kix_ref):
    h = _norm_mod(x_ref[...], g_ref[...], sc_ref[...], sh_ref[...])
    hb = h.astype(BF16)
    hl = (h - hb.astype(F32)).astype(BF16)
    pa = jnp.dot(hb, wa_ref[...], preferred_element_type=F32)
    nr = rkvl_ref.shape[1]
    rkvl_ref[...] = pa[:, :nr]
    qk_ref[...] = pa[:, nr:].astype(BF16)
    vt = lax.dot_general(wvt_ref[...], hb, NT, preferred_element_type=F32)
    ones_row = lax.broadcasted_iota(I32, vt.shape, 0) % VT_ROWS == HEAD_DIM
    vt = jnp.where(ones_row, 1.0, vt).astype(BF16)
    for j in range(vt_ref.shape[0]):
        vt_ref[j] = vt[:, j * DSA_BLOCK:(j + 1) * DSA_BLOCK]
    pb = (jnp.dot(hb, wbh_ref[...], preferred_element_type=F32)
          + jnp.dot(hb, wbl_ref[...], preferred_element_type=F32)
          + jnp.dot(hl, wbh_ref[...], preferred_element_type=F32))
    qiw_ref[...] = pb
    ki = pb[:, K_IDX_OFF:W_IDX_OFF]
    kh = ki.astype(BF16)
    kl = (ki - kh.astype(F32)).astype(BF16)
    kix_ref[...] = jnp.concatenate([kh, kl, kh, jnp.zeros_like(kh)], axis=-1)


def _inproj(x2d, g, sc, sh, wa, wvt, wbh, wbl, T):
    N, D = x2d.shape
    nblk = INPROJ_TILE // DSA_BLOCK
    tm = INPROJ_TILE
    na = wa.shape[1]
    A = wvt.shape[0]
    nr = 3 * RWKV_HEADS * HEAD_DIM + 2 * LORA_PAD
    assert T % tm == 0 and (wa.shape[1] - nr) % LANES == 0
    nb = T // tm
    row = lambda i: (i, 0)
    per_b = lambda i: (i // nb, 0, 0)
    const = lambda i: (0, 0)
    once = pl.Buffered(1)
    return pl.pallas_call(
        _inproj_kernel,
        grid=(N // tm,),
        in_specs=[pl.BlockSpec((tm, D), row),
                  pl.BlockSpec((1, D), const),
                  pl.BlockSpec((None, 1, D), per_b),
                  pl.BlockSpec((None, 1, D), per_b),
                  pl.BlockSpec((D, na), const, pipeline_mode=once),
                  pl.BlockSpec((A, D), const, pipeline_mode=once),
                  pl.BlockSpec((D, IDX_COLS), const, pipeline_mode=once),
                  pl.BlockSpec((D, IDX_COLS), const, pipeline_mode=once)],
        out_specs=[pl.BlockSpec((tm, nr), row),
                   pl.BlockSpec((tm, na - nr), row),
                   pl.BlockSpec((nblk, A, DSA_BLOCK), lambda i: (i, 0, 0)),
                   pl.BlockSpec((tm, IDX_COLS), row),
                   pl.BlockSpec((tm, KIX_COLS), row)],
        out_shape=[jax.ShapeDtypeStruct((N, nr), F32),
                   jax.ShapeDtypeStruct((N, na - nr), BF16),
                   jax.ShapeDtypeStruct((N // DSA_BLOCK, A, DSA_BLOCK), BF16),
                   jax.ShapeDtypeStruct((N, IDX_COLS), F32),
                   jax.ShapeDtypeStruct((N, KIX_COLS), BF16)],
        compiler_params=_params("arbitrary"),
        name="inproj",
    )(x2d, g.reshape(1, D), sc, sh, wa, wvt, wbh, wbl)


def _head_ones(n):
    r = lax.broadcasted_iota(I32, (n, n), 0) // HEAD_DIM
    c = lax.broadcasted_iota(I32, (n, n), 1) // HEAD_DIM
    return jnp.where(r == c, 1.0, 0.0).astype(F32)


def _split(x):
    hi = x.astype(BF16)
    return hi, (x - hi.astype(F32)).astype(BF16)


def _split3(x):
    hi = x.astype(BF16)
    r = x - hi.astype(F32)
    mid = r.astype(BF16)
    return hi, mid, (r - mid.astype(F32)).astype(BF16)


def _block_diag(w, blocks):
    w = w.astype(BF16)
    return jnp.concatenate([jnp.where(m, w, jnp.zeros_like(w)) for m in blocks], axis=0)


def _block_diag_t(w, blocks):
    return jnp.concatenate([jnp.where(m, w, 0.0) for m in blocks], axis=0).T


def _fold_blocks(w):
    n = w.shape[1] // HEAD_DIM
    out = w[:HEAD_DIM]
    for h in range(1, n):
        out = out + w[h * HEAD_DIM:(h + 1) * HEAD_DIM]
    return out


def _mm(a, w):
    return jnp.dot(a.astype(BF16), w, preferred_element_type=F32)


def _rwkv_kernel(has_vres, TT, *refs):
    C = RWKV_CHUNK
    R = RWKV_HEADS * HEAD_DIM
    G = RWKV_GROUP
    NG = R // G
    NC = TT // C
    if has_vres:
        (rkvl_ref, vf_ref, mu_ref, w0_ref, db_ref, a0_ref, ib_ref, gb_ref, kk_ref, ka_ref,
         rk_ref, lg_ref, lb_ref, v0_ref, vb_ref, out_ref,
         prev_ref, S_ref, r_s, k_s, v_s, kk_s, b_s, lw_s, cum_s, y_s, q_s, y0_s, m_s, g_s) = refs
    else:
        (rkvl_ref, mu_ref, w0_ref, db_ref, a0_ref, ib_ref, gb_ref, kk_ref, ka_ref,
         rk_ref, lg_ref, lb_ref, out_ref, vfo_ref,
         prev_ref, S_ref, r_s, k_s, v_s, kk_s, b_s, lw_s, cum_s, y_s, q_s, y0_s, m_s, g_s) = refs

    @pl.when(pl.program_id(1) == 0)
    def _():
        prev_ref[...] = jnp.zeros_like(prev_ref)
        S_ref[...] = jnp.zeros_like(S_ref)

    row0 = lax.broadcasted_iota(I32, (TT, 1), 0) == 0
    prev = prev_ref[...]

    def shift(z, p):
        return jnp.where(row0, p, pltpu.roll(z, 1, 0))

    rkv = rkvl_ref[:, :3 * R]
    p1 = rkvl_ref[:, 3 * R:3 * R + LORA_PAD]
    p2 = rkvl_ref[:, 3 * R + LORA_PAD:]
    rkv_new = rkv + (shift(rkv, prev[:, :3 * R]) - rkv) * mu_ref[...]
    lora = p1 + shift(p2, prev[:, 3 * R:])
    prev_ref[...] = jnp.concatenate([rkv[TT - 1:TT, :], p2[TT - 1:TT, :]], axis=-1)

    r = rkv_new[:, :R]
    k = rkv_new[:, R:2 * R]
    v = rkv_new[:, 2 * R:]
    ones_h = _head_ones(G).astype(BF16)

    def head_sum(x):
        hi, lo = _split(x)
        return jnp.concatenate(
            [jnp.dot(hi[:, gi * G:(gi + 1) * G], ones_h, preferred_element_type=F32)
             + jnp.dot(lo[:, gi * G:(gi + 1) * G], ones_h, preferred_element_type=F32) for gi in range(NG)], axis=1)

    o_a = db_ref.shape[0]
    o_g = o_a + ib_ref.shape[0]
    o_v = o_g + gb_ref.shape[0]
    wlog = w0_ref[...] + _bdot(jnp.tanh(lora[:, :o_a]), db_ref[...])
    z = -wlog
    wlog = -(jnp.maximum(z, 0.0) + jnp.log(1.0 + jnp.exp(-jnp.abs(z)))) - 0.5
    lw = -jnp.exp(wlog)
    lw_s[...] = lw
    slab = min(TT, 4 * C)
    tr = lax.broadcasted_iota(I32, (slab, slab), 0)
    tc = lax.broadcasted_iota(I32, (slab, slab), 1)
    tri = jnp.where((tc <= tr) & (tr // C == tc // C), 1.0, 0.0).astype(BF16)
    parts = _split3(lw)
    for s0 in range(0, TT, slab):
        cum_s[s0:s0 + slab, :] = sum(jnp.dot(tri, part[s0:s0 + slab], preferred_element_type=F32) for part in parts)
    a = jax.nn.sigmoid(a0_ref[...] + _bdot(lora[:, o_a:o_g], ib_ref[...]))
    g = _bdot(jax.nn.sigmoid(lora[:, o_g:o_v]), gb_ref[...])
    kkr = k * kk_ref[...]
    kk = kkr * lax.rsqrt(jnp.maximum(head_sum(kkr * kkr), 1e-24))
    k = k * (1.0 + (a - 1.0) * ka_ref[...])
    if has_vres:
        v = v + (vf_ref[...] - v) * jax.nn.sigmoid(
            v0_ref[...] + _bdot(lora[:, o_v:o_v + vb_ref.shape[0]], vb_ref[...]))
    else:
        vfo_ref[...] = v
    r_s[...] = r
    k_s[...] = k
    v_s[...] = v
    kk_s[...] = kk
    b_s[...] = kk * a

    lane_g = lax.broadcasted_iota(I32, (C, RWKV_GROUP), 1)
    row_g = lax.broadcasted_iota(I32, (C, RWKV_GROUP), 0)
    blocks = [lane_g // HEAD_DIM == h for h in range(RWKV_GROUP // HEAD_DIM)]
    local = lane_g % HEAD_DIM
    strict = local < row_g
    incl = local <= row_g
    diag = local == row_g
    eye_cat = jnp.where(diag, 1.0, 0.0).astype(F32)
    base_mask = strict & (local // INV_BASE == row_g // INV_BASE)
    grow_masks = []
    s = INV_BASE
    while s < C:
        grow_masks.append((local // (2 * s) == row_g // (2 * s)) & (row_g % (2 * s) >= s) & (local % (2 * s) < s))
        s *= 2

    def local_chunk(it, carry):
        chains = [(u, slice(gi * G, (gi + 1) * G)) for u in range(RWKV_UNROLL) for gi in range(NG)]
        each = lambda f: [f(i) for i in range(len(chains))]
        rd, kkd, kt, bt, kc, bc, vv, plast = ([] for _ in range(8))
        for u in range(RWKV_UNROLL):
            sl = pl.ds(pl.multiple_of((it * RWKV_UNROLL + u) * C, C), C)
            lw = lw_s[sl, :]
            cum = cum_s[sl, :]
            cl = cum[C - 1:C, :]
            pinv = jnp.exp(-cum)
            pc = jnp.exp(cl - cum)
            k_a = k_s[sl, :]
            b_a = b_s[sl, :]
            full = (r_s[sl, :] * jnp.exp(cum), kk_s[sl, :] * jnp.exp(cum - lw), k_a * pinv, b_a * pinv,
                    k_a * pc, b_a * pc, v_s[sl, :], jnp.exp(cl))
            for dst, x in zip((rd, kkd, kt, bt, kc, bc, vv, plast), full):
                dst.extend(x[:, gs] for (uu, gs) in chains if uu == u)
        kt_w = each(lambda i: _block_diag_t(kt[i], blocks).astype(BF16))
        bt_w = each(lambda i: _block_diag_t(bt[i], blocks).astype(BF16))
        bc_t = each(lambda i: _fold_blocks(_block_diag_t(bc[i], blocks)))
        kc_t = each(lambda i: _fold_blocks(_block_diag_t(kc[i], blocks)))
        rr = each(lambda i: _mm(jnp.concatenate([kkd[i], rd[i]], axis=0),
                                jnp.concatenate([kt_w[i], bt_w[i]], axis=1)))
        akk = each(lambda i: jnp.where(strict, rr[i][:C, :G], 0.0))
        ark = each(lambda i: jnp.where(incl, rr[i][C:, :G], 0.0))
        arb = each(lambda i: jnp.where(incl, rr[i][C:, G:], 0.0))
        nmat = each(lambda i: jnp.where(strict, -rr[i][:C, G:], 0.0))
        nb = each(lambda i: jnp.where(base_mask, nmat[i], 0.0))
        sq = each(lambda i: _mm(nb[i], _block_diag(nb[i], blocks)))
        x = each(lambda i: eye_cat + nb[i])
        rr = each(lambda i: _mm(x[i], _block_diag(sq[i], blocks)))
        x = each(lambda i: x[i] + rr[i])
        for lower_left in grow_masks:
            u = each(lambda i: _mm(jnp.where(lower_left, nmat[i], 0.0), _block_diag(x[i], blocks)))
            rr = each(lambda i: _mm(x[i], _block_diag(u[i], blocks)))
            x = each(lambda i: x[i] + rr[i])
        rv = each(lambda i: _mm(jnp.concatenate([akk[i], ark[i], kc_t[i]], axis=0), _block_diag(vv[i], blocks)))
        ru = each(lambda i: _mm(x[i], jnp.concatenate([_block_diag(kkd[i], blocks),
                                                       _block_diag(rv[i][:C], blocks)], axis=1)))
        rr = each(lambda i: _mm(jnp.concatenate([arb[i], bc_t[i]], axis=0),
                                jnp.concatenate([_block_diag(ru[i][:, :G], blocks),
                                                 _block_diag(ru[i][:, G:], blocks)], axis=1)))
        for i, (u, gs) in enumerate(chains):
            c = it * RWKV_UNROLL + u
            q_s[c, :, gs] = rd[i] - rr[i][:C, :G]
            y0_s[c, :, gs] = rv[i][C:2 * C] - rr[i][:C, G:]
            dterm = jnp.where(diag, jnp.broadcast_to(plast[i], (C, G)), 0.0)
            m_s[c, :, gs] = dterm - rr[i][C:, :G]
            g_s[c, :, gs] = rv[i][2 * C:] - rr[i][C:, G:]
        return carry

    lax.fori_loop(0, NC // RWKV_UNROLL, local_chunk, 0)

    def scan_chunk(c, carry):
        sl = pl.ds(pl.multiple_of(c * C, C), C)
        for gi in range(NG):
            gs = slice(gi * RWKV_GROUP, (gi + 1) * RWKV_GROUP)
            rr = _mm(jnp.concatenate([q_s[c, :, gs], m_s[c, :, gs]], axis=0), _block_diag(S_ref[:, gs], blocks))
            y_s[sl, gs] = rr[:C] + y0_s[c, :, gs]
            S_ref[:, gs] = rr[C:] + g_s[c, :, gs]
        return carry

    lax.fori_loop(0, NC, scan_chunk, 0)

    y = y_s[...]
    inv_n = 1.0 / HEAD_DIM
    mean = head_sum(y) * inv_n
    yc = y - mean
    var = head_sum(yc * yc) * inv_n
    yn = yc * lax.rsqrt(var + LNX_EPS) * lg_ref[...] + lb_ref[...]
    bonus = head_sum(r_s[...] * k_s[...] * rk_ref[...]) * v_s[...]
    out_ref[...] = ((yn + bonus) * g).astype(out_ref.dtype)


def _rwkv(rkvl, v_first, p, B, T):
    N, nc = rkvl.shape
    R = RWKV_HEADS * HEAD_DIM
    TT = min(RWKV_TILE, T)
    nt = T // TT
    NC = TT // RWKV_CHUNK
    has_vres = v_first is not None
    row = lambda b, i: (b * nt + i, 0)
    const = lambda b, i: (0, 0)
    vec = lambda a: a.reshape(1, -1)
    ins = [rkvl]
    specs = [pl.BlockSpec((TT, nc), row)]
    if has_vres:
        ins.append(v_first)
        specs.append(pl.BlockSpec((TT, R), row))
    small = [vec(p["mu_rkv"]), vec(p["decay_w0"]), p["decay_b"].astype(BF16), vec(p["iclr_a0"]),
             p["iclr_b"].astype(BF16), p["gate_b"].astype(BF16), vec(p["k_k"]), vec(p["k_a"]),
             vec(p["r_k"]), vec(p["lnx_g"]), vec(p["lnx_b"])]
    if has_vres:
        small += [vec(p["vres_v0"]), p["vres_b"].astype(BF16)]
    ins += small
    specs += [pl.BlockSpec(a.shape, const) for a in small]
    out_shape = [jax.ShapeDtypeStruct((N, R), BF16)]
    out_specs = [pl.BlockSpec((TT, R), row)]
    if not has_vres:
        out_shape.append(jax.ShapeDtypeStruct((N, R), F32))
        out_specs.append(pl.BlockSpec((TT, R), row))
    scratch = [pltpu.VMEM((1, 3 * R + LORA_PAD), F32),
               pltpu.VMEM((HEAD_DIM, R), F32)]
    scratch += [pltpu.VMEM((TT, R), F32) for _ in range(8)]
    scratch += [pltpu.VMEM((NC, RWKV_CHUNK, R), F32) for _ in range(4)]
    res = pl.pallas_call(
        functools.partial(_rwkv_kernel, has_vres, TT),
        grid=(B, nt),
        in_specs=specs,
        out_specs=out_specs,
        out_shape=out_shape,
        scratch_shapes=scratch,
        compiler_params=_params("arbitrary", "arbitrary"),
        name="rwkv",
    )(*ins)
    if has_vres:
        return res[0], v_first
    return res[0], res[1]


def _bucket_boundaries():
    max_exact = N_BUCKETS // 2
    d = np.arange(0, 4 * MAX_DISTANCE, dtype=np.int64)
    nf = np.maximum(d, 1).astype(np.float32)
    large = max_exact + (np.log(nf / np.float32(max_exact)) / np.float32(math.log(MAX_DISTANCE / max_exact))
                         * np.float32(N_BUCKETS - max_exact)).astype(np.int32)
    large = np.minimum(large, N_BUCKETS - 1)
    bucket = np.where(d < max_exact, d, large)
    return [int(np.argmax(bucket >= j)) for j in range(max_exact + 1, N_BUCKETS)]


_BUCKET_STARTS = _bucket_boundaries()
KEY_NEG_INF = int(np.int32(np.array(-np.inf, np.float32).view(np.int32)) ^ np.int32(0x7FFFFFFF))
INT_MIN = -2 ** 31
INT_MAX = 2 ** 31 - 1
FAST_SELECT_BLIND = 14
FAST_SELECT_STEPS = 2
FAST_SELECT_TRIPS = 10
ACC_ROWS = 32


def _sort_key(s):
    bits = pltpu.bitcast(s, I32)
    bits = jnp.where(bits == INT_MIN, 0, bits)
    return bits ^ ((bits >> 31) & jnp.int32(0x7FFFFFFF))


def _key_to_float(k):
    return pltpu.bitcast(k ^ ((k >> 31) & jnp.int32(0x7FFFFFFF)), F32)


def _dsa_kernel(T, ksel, q_ref, k_ref, vt_ref, qiw_ref, kix_ref, relb_ref, g_ref, out_ref,
                lhs_s, qpad_s, key_s, madd_s, bias_s, m_s, acc_s, s_s, p_s, cmax_s, alpha_s):
    TQ = TK = DSA_BLOCK
    H = ATTN_HEADS
    qb = pl.program_id(1)
    nch = qb + 1
    rows = lax.broadcasted_iota(I32, (TK, TQ), 0)
    cols = lax.broadcasted_iota(I32, (TK, TQ), 1)

    @pl.when((pl.program_id(0) == 0) & (qb == 0))
    def _():
        max_exact = N_BUCKETS // 2
        for off in range(2):
            d = cols - rows + off * TK
            bucket = jnp.where(d < max_exact, jnp.maximum(d, 0), max_exact)
            for start in _BUCKET_STARTS:
                bucket = bucket + jnp.where(d >= start, 1, 0)
            for h in range(H):
                far = relb_ref[N_BUCKETS - 1, h]
                tile = jnp.zeros((TK, TQ), F32)
                for bk in range(N_BUCKETS - 1):
                    tile = jnp.where(bucket == bk, (relb_ref[bk, h] - far) * LOG2E, tile)
                bias_s[h, off] = tile

    qi_t = (qiw_ref[:, :IDX_HEADS * IDX_DIM] * (IDX_DIM ** -0.5)).T
    for h in range(IDX_HEADS):
        qh = qi_t[h * IDX_DIM:(h + 1) * IDX_DIM, :]
        hi = qh.astype(BF16)
        lo = (qh - hi.astype(F32)).astype(BF16)
        lhs_s[h] = jnp.concatenate([hi, hi, lo, jnp.zeros_like(hi)], axis=0)
    w_t = qiw_ref[:, K_IDX_OFF:IDX_COLS].T[IDX_DIM:IDX_DIM + IDX_HEADS, :] * (IDX_HEADS ** -0.5)

    q_t = q_ref[...].astype(F32).T
    zeros_h = jnp.zeros((HEAD_DIM, TQ), BF16)
    for h in range(H):
        qh = q_t[h * HEAD_DIM:(h + 1) * HEAD_DIM, :].astype(BF16)
        qpad_s[h] = jnp.concatenate([qh, zeros_h] if h % 2 == 0 else [zeros_h, qh], axis=0)

    def scores(j, n=1):
        kx = kix_ref[pl.ds(pl.multiple_of(j * TK, TK), n * TK), :]
        acc = jnp.zeros((n * TK, TQ), F32)
        for h in range(IDX_HEADS):
            d = jnp.dot(kx, lhs_s[h], preferred_element_type=F32)
            acc = acc + w_t[h:h + 1, :] * jnp.maximum(d, 0.0)
        return acc

    def stats(st, kk, mn_of):
        fold = lambda z: z.reshape(z.shape[0] // ACC_ROWS, ACC_ROWS, TQ)
        neg, pos, mn, mx = st
        return (neg + jnp.sum(fold(kk >> 31), axis=0),
                pos + jnp.sum(fold((-kk) >> 31), axis=0),
                jnp.minimum(mn, jnp.min(fold(mn_of(kk)), axis=0)),
                jnp.maximum(mx, jnp.max(fold(kk), axis=0)))

    def pair_body(jj, st):
        kk = _sort_key(scores(2 * jj, 2))
        key_s[pl.ds(2 * jj, 2)] = kk.reshape(2, TK, TQ)
        return stats(st, kk, lambda z: z)

    def odd_chunk(st):
        kk = _sort_key(scores(qb - 1))
        key_s[qb - 1] = kk
        return stats(st, kk, lambda z: z)

    st = lax.fori_loop(0, qb // 2, pair_body,
                       (jnp.zeros((ACC_ROWS, TQ), I32), jnp.zeros((ACC_ROWS, TQ), I32),
                        jnp.full((ACC_ROWS, TQ), INT_MAX, I32), jnp.full((ACC_ROWS, TQ), INT_MIN, I32)))
    st = lax.cond(qb % 2 == 1, odd_chunk, lambda s: s, st)
    kd = _sort_key(jnp.where(rows <= cols, scores(qb), -jnp.inf))
    key_s[qb] = kd
    st = stats(st, kd, lambda z: jnp.where(z > KEY_NEG_INF, z, INT_MAX))
    n_ge0 = nch * TK + jnp.sum(st[0], axis=0, keepdims=True)
    n_gt0 = -jnp.sum(st[1], axis=0, keepdims=True)
    lo0 = _key_to_float(jnp.min(st[2], axis=0, keepdims=True))
    hi0 = _key_to_float(jnp.max(st[3], axis=0, keepdims=True) + 1)

    def count(pred):
        def body(j, acc):
            hit = jnp.where(pred(key_s[j]), 1, 0)
            return acc + jnp.sum(hit.reshape(TK // ACC_ROWS, ACC_ROWS, TQ), axis=0)
        acc = lax.fori_loop(0, nch, body, jnp.zeros((ACC_ROWS, TQ), I32))
        return jnp.sum(acc, axis=0, keepdims=True)

    def write_mask(sel_of):
        def body(j, c):
            madd_s[j] = jnp.where(sel_of(key_s[j]), 0.0, NEG_BIG)
            return c
        lax.fori_loop(0, nch, body, 0)

    kf = float(ksel)
    n_adm = qb * TQ + lax.broadcasted_iota(I32, (1, TQ), 1) + 1
    trivial = n_adm <= ksel
    positive = n_gt0 > ksel
    done0 = trivial | ((n_gt0 <= ksel) & (n_ge0 >= ksel))
    thr0 = jnp.where(trivial, KEY_NEG_INF + 1, jnp.where(n_gt0 == ksel, 1, 0))
    need0 = jnp.where(n_gt0 < ksel, ksel - n_gt0, ksel)
    a0 = jnp.where(positive, 0.0, lo0)
    b0 = jnp.where(positive, hi0, 0.0)
    cb0 = jnp.where(positive, 0, n_ge0)

    def excess(cnt):
        return jnp.log(cnt.astype(F32) + 0.5) - math.log(kf + 0.5)

    fa0 = excess(jnp.where(positive, n_gt0, n_adm))
    as_int = lambda m: jnp.where(m, 1, 0)

    def any_lane(m):
        return jnp.max(as_int(m)) > 0

    def fast_cond(st):
        it, done, stuck = st[0], st[1], st[2]
        return (it < FAST_SELECT_TRIPS) & any_lane((done + stuck) == 0)

    def fast_body(st):
        it, st = st[0], st[1:]
        for _ in range(FAST_SELECT_STEPS):
            st = fast_step(st)
        return (it + 1,) + st

    def fast_step(st):
        done, stuck, thr, need, a, b, fa, fb, cb, last = st
        c = a + (b - a) * (fa / (fa - fb))
        c = jnp.where((c > a) & (c < b), c, 0.5 * a + 0.5 * b)
        inside = (c > a) & (c < b)
        ckey = _sort_key(c)
        cnt = count(lambda kk: kk >= ckey)
        fc = excess(cnt)
        hit = inside & (cnt == ksel) & (done == 0)
        thr = jnp.where(hit, ckey, thr)
        need = jnp.where(hit, ksel, need)
        done = jnp.where(hit, 1, done)
        stuck = jnp.where(inside, stuck, 1)
        up = inside & (cnt > ksel)
        dn = inside & (cnt < ksel)
        fb = jnp.where(up & (last == 1), 0.5 * fb, fb)
        fa = jnp.where(dn & (last == 0), 0.5 * fa, fa)
        a, fa = jnp.where(up, c, a), jnp.where(up, fc, fa)
        b, fb, cb = jnp.where(dn, c, b), jnp.where(dn, fc, fb), jnp.where(dn, cnt, cb)
        return done, stuck, thr, need, a, b, fa, fb, cb, jnp.where(up, 1, jnp.where(dn, 0, last))

    st = (as_int(done0), jnp.zeros((1, TQ), I32), thr0, need0,
          a0, b0, fa0, excess(cb0), cb0, jnp.full((1, TQ), -1, I32))
    blind = jnp.where((qb + 1) * TQ <= ksel, 0, FAST_SELECT_BLIND)
    st = lax.fori_loop(0, blind, lambda i, s: fast_step(s), st)
    st = lax.while_loop(fast_cond, fast_body, (jnp.int32(0),) + st)
    done, thr, need, a, b, cb = st[1], st[3], st[4], st[5], st[6], st[9]
    open_lane = done == 0

    def close_cond(st):
        return any_lane((st[1] - st[0]) > 1)

    def close_body(st):
        ak, bk, cb = st
        mid = ak + ((bk - ak) >> 1)
        cnt = count(lambda kk: kk >= mid)
        ge = cnt >= ksel
        return jnp.where(ge, mid, ak), jnp.where(ge, bk, mid), jnp.where(ge, cb, cnt)

    ak, _, cb = lax.while_loop(close_cond, close_body, (
        jnp.where(open_lane, _sort_key(a), thr), jnp.where(open_lane, _sort_key(b), thr), cb))
    thr = jnp.where(open_lane, ak, thr)
    need = jnp.where(open_lane, ksel - cb, need)
    any_tie = any_lane(open_lane | (jnp.logical_not(trivial) & (n_gt0 < ksel) & (n_ge0 > ksel)))

    @pl.when(jnp.logical_not(any_tie))
    def _():
        write_mask(lambda kk: kk >= thr)

    @pl.when(any_tie)
    def _():
        kr = lax.broadcasted_iota(I32, (TK, TK), 0)
        kc = lax.broadcasted_iota(I32, (TK, TK), 1)
        lower = jnp.where(kc <= kr, 1.0, 0.0).astype(BF16)
        need_f = need.astype(F32)

        def body(j, run):
            kk = key_s[j]
            eq = kk == thr
            eqf = jnp.where(eq, 1.0, 0.0)
            rank = run + jnp.dot(lower, eqf.astype(BF16), preferred_element_type=F32)
            tied_in = jnp.where(eq, rank, float(TOPK_MAX + 1)) <= need_f
            madd_s[j] = jnp.where(kk > thr, 0.0, jnp.where(tied_in, 0.0, NEG_BIG))
            return run + jnp.sum(eqf, axis=0, keepdims=True)
        lax.fori_loop(0, nch, body, jnp.zeros((1, TQ), F32))

    m_s[...] = jnp.full(m_s.shape, NEG_BIG, F32)
    acc_s[...] = jnp.zeros(acc_s.shape, F32)

    def attend(j, bias_of):
        ks = pl.ds(pl.multiple_of(j * TK, TK), TK)
        ma = madd_s[j]
        for h in range(H):
            kp = k_ref[ks, LANES * (h // 2):LANES * (h // 2 + 1)]
            s = jnp.dot(kp, qpad_s[h], preferred_element_type=F32) + ma
            bias = bias_of(h)
            if bias is not None:
                s = s + bias
            s_s[h] = s
            cmax_s[h] = jnp.max(s.reshape(TK // 8, 8, TQ), axis=0)
        for h in range(H):
            m_old = m_s[h]
            m_new = jnp.maximum(m_old, jnp.max(cmax_s[h], axis=0, keepdims=True))
            alpha_s[h] = jnp.exp2(m_old - m_new)
            m_s[h] = m_new
            p_s[h] = jnp.exp2(s_s[h] - m_new).astype(BF16)
        for h in range(H):
            hs = slice(h * VT_ROWS, (h + 1) * VT_ROWS)
            pv = jnp.dot(vt_ref[j, hs, :], p_s[h], preferred_element_type=F32)
            acc_s[hs, :] = alpha_s[h] * acc_s[hs, :] + pv

    def far_body(j, c):
        attend(j, lambda h: None)
        return c

    lax.fori_loop(0, jnp.maximum(qb - 1, 0), far_body, 0)

    @pl.when(qb >= 1)
    def _():
        attend(qb - 1, lambda h: bias_s[h, 1])

    attend(qb, lambda h: bias_s[h, 0])

    outs = []
    for h in range(H):
        o = acc_s[h * VT_ROWS:h * VT_ROWS + HEAD_DIM, :] / acc_s[h * VT_ROWS + HEAD_DIM:h * VT_ROWS + HEAD_DIM + 1, :]
        ms = jnp.mean(o * o, axis=0, keepdims=True)
        outs.append(o * lax.rsqrt(ms + RMS_EPS))
    out_ref[...] = (jnp.concatenate(outs, axis=0).T * g_ref[...]).astype(out_ref.dtype)


def _dsa(qk, vt, qiw, kix, rel_bias, g, B, T):
    N = qk.shape[0]
    TQ = DSA_BLOCK
    nq = T // TQ
    A = ATTN_HEADS * HEAD_DIM
    ksel = min(TOPK_MAX, T // 4)
    assert MAX_DISTANCE <= TQ + 1
    return pl.pallas_call(
        functools.partial(_dsa_kernel, T, ksel),
        grid=(B, nq),
        in_specs=[pl.BlockSpec((TQ, A), lambda b, i: (b * nq + i, 0)),
                  pl.BlockSpec((T, A), lambda b, i: (b, 1)),
                  pl.BlockSpec((nq, ATTN_HEADS * VT_ROWS, TQ), lambda b, i: (b, 0, 0)),
                  pl.BlockSpec((TQ, IDX_COLS), lambda b, i: (b * nq + i, 0)),
                  pl.BlockSpec((T, KIX_COLS), lambda b, i: (b, 0)),
                  pl.BlockSpec(memory_space=pltpu.SMEM),
                  pl.BlockSpec((1, A), lambda b, i: (0, 0))],
        out_specs=pl.BlockSpec((TQ, A), lambda b, i: (b * nq + i, 0)),
        out_shape=jax.ShapeDtypeStruct((N, A), BF16),
        scratch_shapes=[pltpu.VMEM((IDX_HEADS, KIX_COLS, TQ), BF16),
                        pltpu.VMEM((ATTN_HEADS, LANES, TQ), BF16),
                        pltpu.VMEM((nq, TQ, TQ), I32),
                        pltpu.VMEM((nq, TQ, TQ), F32),
                        pltpu.VMEM((ATTN_HEADS, 2, TQ, TQ), F32),
                        pltpu.VMEM((ATTN_HEADS, 1, TQ), F32),
                        pltpu.VMEM((ATTN_HEADS * VT_ROWS, TQ), F32),
                        pltpu.VMEM((ATTN_HEADS, TQ, TQ), F32),
                        pltpu.VMEM((ATTN_HEADS, TQ, TQ), BF16),
                        pltpu.VMEM((ATTN_HEADS, 8, TQ), F32),
                        pltpu.VMEM((ATTN_HEADS, 1, TQ), F32)],
        compiler_params=_params("arbitrary", "arbitrary"),
        name="dsa",
    )(qk, qk, vt, qiw, kix, rel_bias, g.reshape(1, A))


def _mix_mlp_kernel(final, x_ref, rw_ref, att_ref, wo_ref, gt1_ref, g_ref, sc_ref, sh_ref, gt2_ref,
                    w1_ref, w2_ref, fg_ref, o_ref, x_s, h_s, acc_s):
    j = pl.program_id(1)

    @pl.when(j == 0)
    def _():
        R = rw_ref.shape[1]
        mixed = (jnp.dot(rw_ref[...], wo_ref[:R, :], preferred_element_type=F32)
                 + jnp.dot(att_ref[...], wo_ref[R:, :], preferred_element_type=F32))
        x1 = x_ref[...] + gt1_ref[...] * mixed
        x_s[...] = x1
        h_s[...] = _norm_mod(x1, g_ref[...], sc_ref[...], sh_ref[...]).astype(BF16)
        acc_s[...] = jnp.zeros_like(acc_s)

    u = jnp.dot(h_s[...], w1_ref[...], preferred_element_type=F32)
    u = jnp.square(jnp.maximum(u, 0.0))
    acc_s[...] += jnp.dot(u.astype(BF16), w2_ref[...], preferred_element_type=F32)

    @pl.when(j == pl.num_programs(1) - 1)
    def _():
        y = x_s[...] + gt2_ref[...] * acc_s[...]
        if final:
            ms = jnp.mean(y * y, axis=-1, keepdims=True)
            y = y * lax.rsqrt(ms + RMS_EPS) * fg_ref[...]
        o_ref[...] = y


def _mix_mlp(x2d, rw, att, wo, gt1, g, sc, sh, gt2, w1, w2, final_g, final, T):
    N, D = x2d.shape
    F = w1.shape[1]
    fc = min(MLP_FF_CHUNK, F)
    tm = min(MLP_TILE, T)
    nb = T // tm
    row = lambda i, j: (i, 0)
    per_b = lambda i, j: (i // nb, 0, 0)
    const = lambda i, j: (0, 0)
    return pl.pallas_call(
        functools.partial(_mix_mlp_kernel, final),
        grid=(N // tm, F // fc),
        in_specs=[pl.BlockSpec((tm, D), row),
                  pl.BlockSpec((tm, rw.shape[1]), row),
                  pl.BlockSpec((tm, att.shape[1]), row),
                  pl.BlockSpec(wo.shape, const, pipeline_mode=pl.Buffered(1)),
                  pl.BlockSpec((None, 1, D), per_b),
                  pl.BlockSpec((1, D), const),
                  pl.BlockSpec((None, 1, D), per_b),
                  pl.BlockSpec((None, 1, D), per_b),
                  pl.BlockSpec((None, 1, D), per_b),
                  pl.BlockSpec((D, fc), lambda i, j: (0, j)),
                  pl.BlockSpec((fc, D), lambda i, j: (j, 0)),
                  pl.BlockSpec((1, D), const)],
        out_specs=pl.BlockSpec((tm, D), row),
        out_shape=jax.ShapeDtypeStruct((N, D), F32),
        scratch_shapes=[pltpu.VMEM((tm, D), F32), pltpu.VMEM((tm, D), BF16), pltpu.VMEM((tm, D), F32)],
        compiler_params=_params("arbitrary", "arbitrary"),
        name="mix_mlp",
    )(x2d, rw, att, wo, gt1, g.reshape(1, D), sc, sh, gt2, w1, w2, final_g.reshape(1, D))


def _pad_cols(w, n):
    return jnp.pad(w, ((0, 0), (0, n - w.shape[1])))


def _split_bf16(w):
    hi = w.astype(BF16)
    return hi, (w - hi.astype(F32)).astype(BF16)


def _in_weights(l, w_in, mu_lora, decay_a, iclr_a, gate_a, vres_mu, vres_a):
    D = w_in.shape[1]
    R = RWKV_HEADS * HEAD_DIM
    w = w_in[l]
    mats = [(decay_a[l], mu_lora[l, 0]), (iclr_a[l], mu_lora[l, 1]), (gate_a[l], mu_lora[l, 2])]
    if l > 0:
        mats.append((vres_a[l - 1], vres_mu[l - 1]))
    now = _pad_cols(jnp.concatenate([a * (1.0 - mu)[:, None] for a, mu in mats], axis=1), LORA_PAD)
    prev = _pad_cols(jnp.concatenate([a * mu[:, None] for a, mu in mats], axis=1), LORA_PAD)
    wq = w[:, 3 * R:4 * R] * (HEAD_DIM ** -0.5 * LOG2E)
    wa = jnp.concatenate([w[:, :3 * R], now, prev, wq, w[:, 4 * R:5 * R]], axis=1).astype(BF16)
    wv = w[:, 5 * R:6 * R].reshape(D, ATTN_HEADS, HEAD_DIM)
    wv = jnp.pad(wv, ((0, 0), (0, 0), (0, VT_ROWS - HEAD_DIM))).reshape(D, ATTN_HEADS * VT_ROWS)
    wvt = wv.T.astype(BF16)
    wbh, wbl = _split_bf16(_pad_cols(w[:, 6 * R:], IDX_COLS))
    return wa, wvt, wbh, wbl


def kernel(x, c, w_ada, b_ada, norm1_g, norm2_g, w_in, mu_rkv, mu_lora, decay_w0, decay_a, decay_b, iclr_a0, iclr_a, iclr_b, gate_a, gate_b, k_k, k_a, r_k, lnx_g, lnx_b, vres_mu, vres_v0, vres_a, vres_b, attn_out_g, rel_bias, w_out, w_mlp1, w_mlp2, final_g):
    B, T, D = x.shape
    depth = w_in.shape[0]
    mod = _adaln(c, w_ada, b_ada)
    mod = mod.reshape(depth, B, 6, 1, D).transpose(0, 2, 1, 3, 4)
    x2d = x.reshape(B * T, D)
    v_first = None
    for l in range(depth):
        sh1, sc1, gt1, sh2, sc2, gt2 = (mod[l, i] for i in range(6))
        wa, wvt, wbh, wbl = _in_weights(l, w_in, mu_lora, decay_a, iclr_a, gate_a, vres_mu, vres_a)
        rkvl, qk, vt, qiw, kix = _inproj(x2d, norm1_g[l], sc1, sh1, wa, wvt, wbh, wbl, T)
        p = dict(mu_rkv=mu_rkv[l], decay_w0=decay_w0[l], decay_b=decay_b[l], iclr_a0=iclr_a0[l],
                 iclr_b=iclr_b[l], gate_b=gate_b[l], k_k=k_k[l], k_a=k_a[l], r_k=r_k[l],
                 lnx_g=lnx_g[l], lnx_b=lnx_b[l])
        if l > 0:
            p.update(vres_v0=vres_v0[l - 1], vres_b=vres_b[l - 1])
        rw, v_first = _rwkv(rkvl, v_first, p, B, T)
        att = _dsa(qk, vt, qiw, kix, rel_bias, attn_out_g[l], B, T)
        x2d = _mix_mlp(x2d, rw, att, w_out[l].astype(BF16), gt1, norm2_g[l], sc2, sh2, gt2,
                       w_mlp1[l].astype(BF16), w_mlp2[l].astype(BF16), final_g, l == depth - 1, T)
    return x2d.reshape(B, T, D)
```

```python
import functools
import math

import numpy as np
import jax
import jax.numpy as jnp
from jax import lax
from jax.experimental import pallas as pl
from jax.experimental.pallas import tpu as pltpu

F32 = jnp.float32
BF16 = jnp.bfloat16
I32 = jnp.int32

HEAD_DIM = 64
RWKV_HEADS = 8
ATTN_HEADS = 8
IDX_HEADS = 8
IDX_DIM = 64
TOPK_MAX = 256
N_BUCKETS = 32
MAX_DISTANCE = 128
RMS_EPS = 1e-6
LNX_EPS = 64e-5

LANES = 128
VMEM_LIMIT = 56 * 1024 * 1024
RWKV_CHUNK = 64
RWKV_GROUP = 256
INV_BASE = 4
RWKV_TILE = 512
RWKV_UNROLL = 8
DSA_BLOCK = 256
VT_ROWS = 80
LOG2E = math.log2(math.e)
INPROJ_TILE = 512
MLP_TILE = 1024
MLP_FF_CHUNK = 1024
LORA_PAD = 384
K_IDX_OFF = IDX_HEADS * IDX_DIM
W_IDX_OFF = K_IDX_OFF + IDX_DIM
IDX_COLS = 640
KIX_COLS = 4 * IDX_DIM
NEG_BIG = -1e30
HI = lax.Precision.HIGHEST
NT = (((1,), (1,)), ((), ()))


def _bdot(a, b):
    return jnp.dot(a.astype(BF16), b.astype(BF16), preferred_element_type=F32)


def _hdot(a, b):
    return jnp.dot(a, b, precision=HI, preferred_element_type=F32)


def _params(*sem):
    return pltpu.CompilerParams(dimension_semantics=sem, vmem_limit_bytes=VMEM_LIMIT)


def _adaln_kernel(c_ref, w_ref, b_ref, o_ref):
    c = c_ref[...]
    c_act = c * jax.nn.sigmoid(c)
    o_ref[...] = _hdot(c_act, w_ref[...]) + b_ref[...]


def _adaln(c, w_ada, b_ada):
    L, D, D6 = w_ada.shape
    B = c.shape[0]
    cb = 1024
    return pl.pallas_call(
        _adaln_kernel,
        grid=(L, D6 // cb),
        in_specs=[pl.BlockSpec((B, D), lambda l, j: (0, 0)),
                  pl.BlockSpec((None, D, cb), lambda l, j: (l, 0, j)),
                  pl.BlockSpec((None, 1, cb), lambda l, j: (l, 0, j))],
        out_specs=pl.BlockSpec((None, B, cb), lambda l, j: (l, 0, j)),
        out_shape=jax.ShapeDtypeStruct((L, B, D6), F32),
        compiler_params=_params("arbitrary", "arbitrary"),
        name="adaln",
    )(c, w_ada, b_ada.reshape(L, 1, D6))


def _norm_mod(x, g, sc, sh):
    ms = jnp.mean(x * x, axis=-1, keepdims=True)
    return (x * lax.rsqrt(ms + RMS_EPS) * g) * (1.0 + sc) + sh


def _inproj_kernel(x_ref, g_ref, sc_ref, sh_ref, wa_ref, wvt_ref, wbh_ref, wbl_ref,
                   rkvl_ref, qk_ref, vt_ref, qiw_ref, kix_ref):
    h = _norm_mod(x_ref[...], g_ref[...], sc_ref[...], sh_ref[...])
    hb = h.astype(BF16)
    hl = (h - hb.astype(F32)).astype(BF16)
    pa = jnp.dot(hb, wa_ref[...], preferred_element_type=F32)
    nr = rkvl_ref.shape[1]
    rkvl_ref[...] = pa[:, :nr]
    qk_ref[...] = pa[:, nr:].astype(BF16)
    vt = lax.dot_general(wvt_ref[...], hb, NT, preferred_element_type=F32)
    ones_row = lax.broadcasted_iota(I32, vt.shape, 0) % VT_ROWS == HEAD_DIM
    vt = jnp.where(ones_row, 1.0, vt).astype(BF16)
    for j in range(vt_ref.shape[0]):
        vt_ref[j] = vt[:, j * DSA_BLOCK:(j + 1) * DSA_BLOCK]
    pb = (jnp.dot(hb, wbh_ref[...], preferred_element_type=F32)
          + jnp.dot(hb, wbl_ref[...], preferred_element_type=F32)
          + jnp.dot(hl, wbh_ref[...], preferred_element_type=F32))
    qiw_ref[...] = pb
    ki = pb[:, K_IDX_OFF:W_IDX_OFF]
    kh = ki.astype(BF16)
    kl = (ki - kh.astype(F32)).astype(BF16)
    kix_ref[...] = jnp.concatenate([kh, kl, kh, jnp.zeros_like(kh)], axis=-1)


def _inproj(x2d, g, sc, sh, wa, wvt, wbh, wbl, T):
    N, D = x2d.shape
    nblk = INPROJ_TILE // DSA_BLOCK
    tm = INPROJ_TILE
    na = wa.shape[1]
    A = wvt.shape[0]
    nr = 3 * RWKV_HEADS * HEAD_DIM + 2 * LORA_PAD
    assert T % tm == 0 and (wa.shape[1] - nr) % LANES == 0
    nb = T // tm
    row = lambda i: (i, 0)
    per_b = lambda i: (i // nb, 0, 0)
    const = lambda i: (0, 0)
    once = pl.Buffered(1)
    return pl.pallas_call(
        _inproj_kernel,
        grid=(N // tm,),
        in_specs=[pl.BlockSpec((tm, D), row),
                  pl.BlockSpec((1, D), const),
                  pl.BlockSpec((None, 1, D), per_b),
                  pl.BlockSpec((None, 1, D), per_b),
                  pl.BlockSpec((D, na), const, pipeline_mode=once),
                  pl.BlockSpec((A, D), const, pipeline_mode=once),
                  pl.BlockSpec((D, IDX_COLS), const, pipeline_mode=once),
                  pl.BlockSpec((D, IDX_COLS), const, pipeline_mode=once)],
        out_specs=[pl.BlockSpec((tm, nr), row),
                   pl.BlockSpec((tm, na - nr), row),
                   pl.BlockSpec((nblk, A, DSA_BLOCK), lambda i: (i, 0, 0)),
                   pl.BlockSpec((tm, IDX_COLS), row),
                   pl.BlockSpec((tm, KIX_COLS), row)],
        out_shape=[jax.ShapeDtypeStruct((N, nr), F32),
                   jax.ShapeDtypeStruct((N, na - nr), BF16),
                   jax.ShapeDtypeStruct((N // DSA_BLOCK, A, DSA_BLOCK), BF16),
                   jax.ShapeDtypeStruct((N, IDX_COLS), F32),
                   jax.ShapeDtypeStruct((N, KIX_COLS), BF16)],
        compiler_params=_params("arbitrary"),
        name="inproj",
    )(x2d, g.reshape(1, D), sc, sh, wa, wvt, wbh, wbl)


def _head_ones(n):
    r = lax.broadcasted_iota(I32, (n, n), 0) // HEAD_DIM
    c = lax.broadcasted_iota(I32, (n, n), 1) // HEAD_DIM
    return jnp.where(r == c, 1.0, 0.0).astype(F32)


def _split(x):
    hi = x.astype(BF16)
    return hi, (x - hi.astype(F32)).astype(BF16)


def _split3(x):
    hi = x.astype(BF16)
    r = x - hi.astype(F32)
    mid = r.astype(BF16)
    return hi, mid, (r - mid.astype(F32)).astype(BF16)


def _block_diag(w, blocks):
    w = w.astype(BF16)
    return jnp.concatenate([jnp.where(m, w, jnp.zeros_like(w)) for m in blocks], axis=0)


def _block_diag_t(w, blocks):
    return jnp.concatenate([jnp.where(m, w, 0.0) for m in blocks], axis=0).T


def _fold_blocks(w):
    n = w.shape[1] // HEAD_DIM
    out = w[:HEAD_DIM]
    for h in range(1, n):
        out = out + w[h * HEAD_DIM:(h + 1) * HEAD_DIM]
    return out


def _mm(a, w):
    return jnp.dot(a.astype(BF16), w, preferred_element_type=F32)


def _rwkv_kernel(has_vres, TT, *refs):
    C = RWKV_CHUNK
    R = RWKV_HEADS * HEAD_DIM
    G = RWKV_GROUP
    NG = R // G
    NC = TT // C
    if has_vres:
        (rkvl_ref, vf_ref, mu_ref, w0_ref, db_ref, a0_ref, ib_ref, gb_ref, kk_ref, ka_ref,
         rk_ref, lg_ref, lb_ref, v0_ref, vb_ref, out_ref,
         prev_ref, S_ref, r_s, k_s, v_s, kk_s, b_s, lw_s, cum_s, y_s, q_s, y0_s, m_s, g_s) = refs
    else:
        (rkvl_ref, mu_ref, w0_ref, db_ref, a0_ref, ib_ref, gb_ref, kk_ref, ka_ref,
         rk_ref, lg_ref, lb_ref, out_ref, vfo_ref,
         prev_ref, S_ref, r_s, k_s, v_s, kk_s, b_s, lw_s, cum_s, y_s, q_s, y0_s, m_s, g_s) = refs

    @pl.when(pl.program_id(1) == 0)
    def _():
        prev_ref[...] = jnp.zeros_like(prev_ref)
        S_ref[...] = jnp.zeros_like(S_ref)

    row0 = lax.broadcasted_iota(I32, (TT, 1), 0) == 0
    prev = prev_ref[...]

    def shift(z, p):
        return jnp.where(row0, p, pltpu.roll(z, 1, 0))

    rkv = rkvl_ref[:, :3 * R]
    p1 = rkvl_ref[:, 3 * R:3 * R + LORA_PAD]
    p2 = rkvl_ref[:, 3 * R + LORA_PAD:]
    rkv_new = rkv + (shift(rkv, prev[:, :3 * R]) - rkv) * mu_ref[...]
    lora = p1 + shift(p2, prev[:, 3 * R:])
    prev_ref[...] = jnp.concatenate([rkv[TT - 1:TT, :], p2[TT - 1:TT, :]], axis=-1)

    r = rkv_new[:, :R]
    k = rkv_new[:, R:2 * R]
    v = rkv_new[:, 2 * R:]
    ones_h = _head_ones(G).astype(BF16)

    def head_sum(x):
        hi, lo = _split(x)
        return jnp.concatenate(
            [jnp.dot(hi[:, gi * G:(gi + 1) * G], ones_h, preferred_element_type=F32)
             + jnp.dot(lo[:, gi * G:(gi + 1) * G], ones_h, preferred_element_type=F32) for gi in range(NG)], axis=1)

    o_a = db_ref.shape[0]
    o_g = o_a + ib_ref.shape[0]
    o_v = o_g + gb_ref.shape[0]
    wlog = w0_ref[...] + _bdot(jnp.tanh(lora[:, :o_a]), db_ref[...])
    z = -wlog
    wlog = -(jnp.maximum(z, 0.0) + jnp.log(1.0 + jnp.exp(-jnp.abs(z)))) - 0.5
    lw = -jnp.exp(wlog)
    lw_s[...] = lw
    slab = min(TT, 4 * C)
    tr = lax.broadcasted_iota(I32, (slab, slab), 0)
    tc = lax.broadcasted_iota(I32, (slab, slab), 1)
    tri = jnp.where((tc <= tr) & (tr // C == tc // C), 1.0, 0.0).astype(BF16)
    parts = _split3(lw)
    for s0 in range(0, TT, slab):
        cum_s[s0:s0 + slab, :] = sum(jnp.dot(tri, part[s0:s0 + slab], preferred_element_type=F32) for part in parts)
    a = jax.nn.sigmoid(a0_ref[...] + _bdot(lora[:, o_a:o_g], ib_ref[...]))
    g = _bdot(jax.nn.sigmoid(lora[:, o_g:o_v]), gb_ref[...])
    kkr = k * kk_ref[...]
    kk = kkr * lax.rsqrt(jnp.maximum(head_sum(kkr * kkr), 1e-24))
    k = k * (1.0 + (a - 1.0) * ka_ref[...])
    if has_vres:
        v = v + (vf_ref[...] - v) * jax.nn.sigmoid(
            v0_ref[...] + _bdot(lora[:, o_v:o_v + vb_ref.shape[0]], vb_ref[...]))
    else:
        vfo_ref[...] = v
    r_s[...] = r
    k_s[...] = k
    v_s[...] = v
    kk_s[...] = kk
    b_s[...] = kk * a

    lane_g = lax.broadcasted_iota(I32, (C, RWKV_GROUP), 1)
    row_g = lax.broadcasted_iota(I32, (C, RWKV_GROUP), 0)
    blocks = [lane_g // HEAD_DIM == h for h in range(RWKV_GROUP // HEAD_DIM)]
    local = lane_g % HEAD_DIM
    strict = local < row_g
    incl = local <= row_g
    diag = local == row_g
    eye_cat = jnp.where(diag, 1.0, 0.0).astype(F32)
    base_mask = strict & (local // INV_BASE == row_g // INV_BASE)
    grow_masks = []
    s = INV_BASE
    while s < C:
        grow_masks.append((local // (2 * s) == row_g // (2 * s)) & (row_g % (2 * s) >= s) & (local % (2 * s) < s))
        s *= 2

    def local_chunk(it, carry):
        chains = [(u, slice(gi * G, (gi + 1) * G)) for u in range(RWKV_UNROLL) for gi in range(NG)]
        each = lambda f: [f(i) for i in range(len(chains))]
        rd, kkd, kt, bt, kc, bc, vv, plast = ([] for _ in range(8))
        for u in range(RWKV_UNROLL):
            sl = pl.ds(pl.multiple_of((it * RWKV_UNROLL + u) * C, C), C)
            lw = lw_s[sl, :]
            cum = cum_s[sl, :]
            cl = cum[C - 1:C, :]
            pinv = jnp.exp(-cum)
            pc = jnp.exp(cl - cum)
            k_a = k_s[sl, :]
            b_a = b_s[sl, :]
            full = (r_s[sl, :] * jnp.exp(cum), kk_s[sl, :] * jnp.exp(cum - lw), k_a * pinv, b_a * pinv,
                    k_a * pc, b_a * pc, v_s[sl, :], jnp.exp(cl))
            for dst, x in zip((rd, kkd, kt, bt, kc, bc, vv, plast), full):
                dst.extend(x[:, gs] for (uu, gs) in chains if uu == u)
        kt_w = each(lambda i: _block_diag_t(kt[i], blocks).astype(BF16))
        bt_w = each(lambda i: _block_diag_t(bt[i], blocks).astype(BF16))
        bc_t = each(lambda i: _fold_blocks(_block_diag_t(bc[i], blocks)))
        kc_t = each(lambda i: _fold_blocks(_block_diag_t(kc[i], blocks)))
        rr = each(lambda i: _mm(jnp.concatenate([kkd[i], rd[i]], axis=0),
                                jnp.concatenate([kt_w[i], bt_w[i]], axis=1)))
        akk = each(lambda i: jnp.where(strict, rr[i][:C, :G], 0.0))
        ark = each(lambda i: jnp.where(incl, rr[i][C:, :G], 0.0))
        arb = each(lambda i: jnp.where(incl, rr[i][C:, G:], 0.0))
        nmat = each(lambda i: jnp.where(strict, -rr[i][:C, G:], 0.0))
        nb = each(lambda i: jnp.where(base_mask, nmat[i], 0.0))
        sq = each(lambda i: _mm(nb[i], _block_diag(nb[i], blocks)))
        x = each(lambda i: eye_cat + nb[i])
        rr = each(lambda i: _mm(x[i], _block_diag(sq[i], blocks)))
        x = each(lambda i: x[i] + rr[i])
        for lower_left in grow_masks:
            u = each(lambda i: _mm(jnp.where(lower_left, nmat[i], 0.0), _block_diag(x[i], blocks)))
            rr = each(lambda i: _mm(x[i], _block_diag(u[i], blocks)))
            x = each(lambda i: x[i] + rr[i])
        rv = each(lambda i: _mm(jnp.concatenate([akk[i], ark[i], kc_t[i]], axis=0), _block_diag(vv[i], blocks)))
        ru = each(lambda i: _mm(x[i], jnp.concatenate([_block_diag(kkd[i], blocks),
                                                       _block_diag(rv[i][:C], blocks)], axis=1)))
        rr = each(lambda i: _mm(jnp.concatenate([arb[i], bc_t[i]], axis=0),
                                jnp.concatenate([_block_diag(ru[i][:, :G], blocks),
                                                 _block_diag(ru[i][:, G:], blocks)], axis=1)))
        for i, (u, gs) in enumerate(chains):
            c = it * RWKV_UNROLL + u
            q_s[c, :, gs] = rd[i] - rr[i][:C, :G]
            y0_s[c, :, gs] = rv[i][C:2 * C] - rr[i][:C, G:]
            dterm = jnp.where(diag, jnp.broadcast_to(plast[i], (C, G)), 0.0)
            m_s[c, :, gs] = dterm - rr[i][C:, :G]
            g_s[c, :, gs] = rv[i][2 * C:] - rr[i][C:, G:]
        return carry

    lax.fori_loop(0, NC // RWKV_UNROLL, local_chunk, 0)

    def scan_chunk(c, carry):
        sl = pl.ds(pl.multiple_of(c * C, C), C)
        for gi in range(NG):
            gs = slice(gi * RWKV_GROUP, (gi + 1) * RWKV_GROUP)
            rr = _mm(jnp.concatenate([q_s[c, :, gs], m_s[c, :, gs]], axis=0), _block_diag(S_ref[:, gs], blocks))
            y_s[sl, gs] = rr[:C] + y0_s[c, :, gs]
            S_ref[:, gs] = rr[C:] + g_s[c, :, gs]
        return carry

    lax.fori_loop(0, NC, scan_chunk, 0)

    y = y_s[...]
    inv_n = 1.0 / HEAD_DIM
    mean = head_sum(y) * inv_n
    yc = y - mean
    var = head_sum(yc * yc) * inv_n
    yn = yc * lax.rsqrt(var + LNX_EPS) * lg_ref[...] + lb_ref[...]
    bonus = head_sum(r_s[...] * k_s[...] * rk_ref[...]) * v_s[...]
    out_ref[...] = ((yn + bonus) * g).astype(out_ref.dtype)


def _rwkv(rkvl, v_first, p, B, T):
    N, nc = rkvl.shape
    R = RWKV_HEADS * HEAD_DIM
    TT = min(RWKV_TILE, T)
    nt = T // TT
    NC = TT // RWKV_CHUNK
    has_vres = v_first is not None
    row = lambda b, i: (b * nt + i, 0)
    const = lambda b, i: (0, 0)
    vec = lambda a: a.reshape(1, -1)
    ins = [rkvl]
    specs = [pl.BlockSpec((TT, nc), row)]
    if has_vres:
        ins.append(v_first)
        specs.append(pl.BlockSpec((TT, R), row))
    small = [vec(p["mu_rkv"]), vec(p["decay_w0"]), p["decay_b"].astype(BF16), vec(p["iclr_a0"]),
             p["iclr_b"].astype(BF16), p["gate_b"].astype(BF16), vec(p["k_k"]), vec(p["k_a"]),
             vec(p["r_k"]), vec(p["lnx_g"]), vec(p["lnx_b"])]
    if has_vres:
        small += [vec(p["vres_v0"]), p["vres_b"].astype(BF16)]
    ins += small
    specs += [pl.BlockSpec(a.shape, const) for a in small]
    out_shape = [jax.ShapeDtypeStruct((N, R), BF16)]
    out_specs = [pl.BlockSpec((TT, R), row)]
    if not has_vres:
        out_shape.append(jax.ShapeDtypeStruct((N, R), F32))
        out_specs.append(pl.BlockSpec((TT, R), row))
    scratch = [pltpu.VMEM((1, 3 * R + LORA_PAD), F32),
               pltpu.VMEM((HEAD_DIM, R), F32)]
    scratch += [pltpu.VMEM((TT, R), F32) for _ in range(8)]
    scratch += [pltpu.VMEM((NC, RWKV_CHUNK, R), F32) for _ in range(4)]
    res = pl.pallas_call(
        functools.partial(_rwkv_kernel, has_vres, TT),
        grid=(B, nt),
        in_specs=specs,
        out_specs=out_specs,
        out_shape=out_shape,
        scratch_shapes=scratch,
        compiler_params=_params("arbitrary", "arbitrary"),
        name="rwkv",
    )(*ins)
    if has_vres:
        return res[0], v_first
    return res[0], res[1]


def _bucket_boundaries():
    max_exact = N_BUCKETS // 2
    d = np.arange(0, 4 * MAX_DISTANCE, dtype=np.int64)
    nf = np.maximum(d, 1).astype(np.float32)
    large = max_exact + (np.log(nf / np.float32(max_exact)) / np.float32(math.log(MAX_DISTANCE / max_exact))
                         * np.float32(N_BUCKETS - max_exact)).astype(np.int32)
    large = np.minimum(large, N_BUCKETS - 1)
    bucket = np.where(d < max_exact, d, large)
    return [int(np.argmax(bucket >= j)) for j in range(max_exact + 1, N_BUCKETS)]


_BUCKET_STARTS = _bucket_boundaries()
KEY_NEG_INF = int(np.int32(np.array(-np.inf, np.float32).view(np.int32)) ^ np.int32(0x7FFFFFFF))
INT_MIN = -2 ** 31
INT_MAX = 2 ** 31 - 1
FAST_SELECT_BLIND = 14
FAST_SELECT_STEPS = 2
FAST_SELECT_TRIPS = 10
ACC_ROWS = 32


def _sort_key(s):
    bits = pltpu.bitcast(s, I32)
    bits = jnp.where(bits == INT_MIN, 0, bits)
    return bits ^ ((bits >> 31) & jnp.int32(0x7FFFFFFF))


def _key_to_float(k):
    return pltpu.bitcast(k ^ ((k >> 31) & jnp.int32(0x7FFFFFFF)), F32)


def _dsa_kernel(T, ksel, q_ref, k_ref, vt_ref, qiw_ref, kix_ref, relb_ref, g_ref, out_ref,
                lhs_s, qpad_s, key_s, madd_s, bias_s, m_s, acc_s, s_s, p_s, cmax_s, alpha_s):
    TQ = TK = DSA_BLOCK
    H = ATTN_HEADS
    qb = pl.program_id(1)
    nch = qb + 1
    rows = lax.broadcasted_iota(I32, (TK, TQ), 0)
    cols = lax.broadcasted_iota(I32, (TK, TQ), 1)

    @pl.when((pl.program_id(0) == 0) & (qb == 0))
    def _():
        max_exact = N_BUCKETS // 2
        for off in range(2):
            d = cols - rows + off * TK
            bucket = jnp.where(d < max_exact, jnp.maximum(d, 0), max_exact)
            for start in _BUCKET_STARTS:
                bucket = bucket + jnp.where(d >= start, 1, 0)
            for h in range(H):
                far = relb_ref[N_BUCKETS - 1, h]
                tile = jnp.zeros((TK, TQ), F32)
                for bk in range(N_BUCKETS - 1):
                    tile = jnp.where(bucket == bk, (relb_ref[bk, h] - far) * LOG2E, tile)
                bias_s[h, off] = tile

    qi_t = (qiw_ref[:, :IDX_HEADS * IDX_DIM] * (IDX_DIM ** -0.5)).T
    for h in range(IDX_HEADS):
        qh = qi_t[h * IDX_DIM:(h + 1) * IDX_DIM, :]
        hi = qh.astype(BF16)
        lo = (qh - hi.astype(F32)).astype(BF16)
        lhs_s[h] = jnp.concatenate([hi, hi, lo, jnp.zeros_like(hi)], axis=0)
    w_t = qiw_ref[:, K_IDX_OFF:IDX_COLS].T[IDX_DIM:IDX_DIM + IDX_HEADS, :] * (IDX_HEADS ** -0.5)

    q_t = q_ref[...].astype(F32).T
    zeros_h = jnp.zeros((HEAD_DIM, TQ), BF16)
    for h in range(H):
        qh = q_t[h * HEAD_DIM:(h + 1) * HEAD_DIM, :].astype(BF16)
        qpad_s[h] = jnp.concatenate([qh, zeros_h] if h % 2 == 0 else [zeros_h, qh], axis=0)

    def scores(j, n=1):
        kx = kix_ref[pl.ds(pl.multiple_of(j * TK, TK), n * TK), :]
        acc = jnp.zeros((n * TK, TQ), F32)
        for h in range(IDX_HEADS):
            d = jnp.dot(kx, lhs_s[h], preferred_element_type=F32)
            acc = acc + w_t[h:h + 1, :] * jnp.maximum(d, 0.0)
        return acc

    def stats(st, kk, mn_of):
        fold = lambda z: z.reshape(z.shape[0] // ACC_ROWS, ACC_ROWS, TQ)
        neg, pos, mn, mx = st
        return (neg + jnp.sum(fold(kk >> 31), axis=0),
                pos + jnp.sum(fold((-kk) >> 31), axis=0),
                jnp.minimum(mn, jnp.min(fold(mn_of(kk)), axis=0)),
                jnp.maximum(mx, jnp.max(fold(kk), axis=0)))

    def pair_body(jj, st):
        kk = _sort_key(scores(2 * jj, 2))
        key_s[pl.ds(2 * jj, 2)] = kk.reshape(2, TK, TQ)
        return stats(st, kk, lambda z: z)

    def odd_chunk(st):
        kk = _sort_key(scores(qb - 1))
        key_s[qb - 1] = kk
        return stats(st, kk, lambda z: z)

    st = lax.fori_loop(0, qb // 2, pair_body,
                       (jnp.zeros((ACC_ROWS, TQ), I32), jnp.zeros((ACC_ROWS, TQ), I32),
                        jnp.full((ACC_ROWS, TQ), INT_MAX, I32), jnp.full((ACC_ROWS, TQ), INT_MIN, I32)))
    st = lax.cond(qb % 2 == 1, odd_chunk, lambda s: s, st)
    kd = _sort_key(jnp.where(rows <= cols, scores(qb), -jnp.inf))
    key_s[qb] = kd
    st = stats(st, kd, lambda z: jnp.where(z > KEY_NEG_INF, z, INT_MAX))
    n_ge0 = nch * TK + jnp.sum(st[0], axis=0, keepdims=True)
    n_gt0 = -jnp.sum(st[1], axis=0, keepdims=True)
    lo0 = _key_to_float(jnp.min(st[2], axis=0, keepdims=True))
    hi0 = _key_to_float(jnp.max(st[3], axis=0, keepdims=True) + 1)

    def count(pred):
        def body(j, acc):
            hit = jnp.where(pred(key_s[j]), 1, 0)
            return acc + jnp.sum(hit.reshape(TK // ACC_ROWS, ACC_ROWS, TQ), axis=0)
        acc = lax.fori_loop(0, nch, body, jnp.zeros((ACC_ROWS, TQ), I32))
        return jnp.sum(acc, axis=0, keepdims=True)

    def write_mask(sel_of):
        def body(j, c):
            madd_s[j] = jnp.where(sel_of(key_s[j]), 0.0, NEG_BIG)
            return c
        lax.fori_loop(0, nch, body, 0)

    kf = float(ksel)
    n_adm = qb * TQ + lax.broadcasted_iota(I32, (1, TQ), 1) + 1
    trivial = n_adm <= ksel
    positive = n_gt0 > ksel
    done0 = trivial | ((n_gt0 <= ksel) & (n_ge0 >= ksel))
    thr0 = jnp.where(trivial, KEY_NEG_INF + 1, jnp.where(n_gt0 == ksel, 1, 0))
    need0 = jnp.where(n_gt0 < ksel, ksel - n_gt0, ksel)
    a0 = jnp.where(positive, 0.0, lo0)
    b0 = jnp.where(positive, hi0, 0.0)
    cb0 = jnp.where(positive, 0, n_ge0)

    def excess(cnt):
        return jnp.log(cnt.astype(F32) + 0.5) - math.log(kf + 0.5)

    fa0 = excess(jnp.where(positive, n_gt0, n_adm))
    as_int = lambda m: jnp.where(m, 1, 0)

    def any_lane(m):
        return jnp.max(as_int(m)) > 0

    def fast_cond(st):
        it, done, stuck = st[0], st[1], st[2]
        return (it < FAST_SELECT_TRIPS) & any_lane((done + stuck) == 0)

    def fast_body(st):
        it, st = st[0], st[1:]
        for _ in range(FAST_SELECT_STEPS):
            st = fast_step(st)
        return (it + 1,) + st

    def fast_step(st):
        done, stuck, thr, need, a, b, fa, fb, cb, last = st
        c = a + (b - a) * (fa / (fa - fb))
        c = jnp.where((c > a) & (c < b), c, 0.5 * a + 0.5 * b)
        inside = (c > a) & (c < b)
        ckey = _sort_key(c)
        cnt = count(lambda kk: kk >= ckey)
        fc = excess(cnt)
        hit = inside & (cnt == ksel) & (done == 0)
        thr = jnp.where(hit, ckey, thr)
        need = jnp.where(hit, ksel, need)
        done = jnp.where(hit, 1, done)
        stuck = jnp.where(inside, stuck, 1)
        up = inside & (cnt > ksel)
        dn = inside & (cnt < ksel)
        fb = jnp.where(up & (last == 1), 0.5 * fb, fb)
        fa = jnp.where(dn & (last == 0), 0.5 * fa, fa)
        a, fa = jnp.where(up, c, a), jnp.where(up, fc, fa)
        b, fb, cb = jnp.where(dn, c, b), jnp.where(dn, fc, fb), jnp.where(dn, cnt, cb)
        return done, stuck, thr, need, a, b, fa, fb, cb, jnp.where(up, 1, jnp.where(dn, 0, last))

    st = (as_int(done0), jnp.zeros((1, TQ), I32), thr0, need0,
          a0, b0, fa0, excess(cb0), cb0, jnp.full((1, TQ), -1, I32))
    blind = jnp.where((qb + 1) * TQ <= ksel, 0, FAST_SELECT_BLIND)
    st = lax.fori_loop(0, blind, lambda i, s: fast_step(s), st)
    st = lax.while_loop(fast_cond, fast_body, (jnp.int32(0),) + st)
    done, thr, need, a, b, cb = st[1], st[3], st[4], st[5], st[6], st[9]
    open_lane = done == 0

    def close_cond(st):
        return any_lane((st[1] - st[0]) > 1)

    def close_body(st):
        ak, bk, cb = st
        mid = ak + ((bk - ak) >> 1)
        cnt = count(lambda kk: kk >= mid)
        ge = cnt >= ksel
        return jnp.where(ge, mid, ak), jnp.where(ge, bk, mid), jnp.where(ge, cb, cnt)

    ak, _, cb = lax.while_loop(close_cond, close_body, (
        jnp.where(open_lane, _sort_key(a), thr), jnp.where(open_lane, _sort_key(b), thr), cb))
    thr = jnp.where(open_lane, ak, thr)
    need = jnp.where(open_lane, ksel - cb, need)
    any_tie = any_lane(open_lane | (jnp.logical_not(trivial) & (n_gt0 < ksel) & (n_ge0 > ksel)))

    @pl.when(jnp.logical_not(any_tie))
    def _():
        write_mask(lambda kk: kk >= thr)

    @pl.when(any_tie)
    def _():
        kr = lax.broadcasted_iota(I32, (TK, TK), 0)
        kc = lax.broadcasted_iota(I32, (TK, TK), 1)
        lower = jnp.where(kc <= kr, 1.0, 0.0).astype(BF16)
        need_f = need.astype(F32)

        def body(j, run):
            kk = key_s[j]
            eq = kk == thr
            eqf = jnp.where(eq, 1.0, 0.0)
            rank = run + jnp.dot(lower, eqf.astype(BF16), preferred_element_type=F32)
            tied_in = jnp.where(eq, rank, float(TOPK_MAX + 1)) <= need_f
            madd_s[j] = jnp.where(kk > thr, 0.0, jnp.where(tied_in, 0.0, NEG_BIG))
            return run + jnp.sum(eqf, axis=0, keepdims=True)
        lax.fori_loop(0, nch, body, jnp.zeros((1, TQ), F32))

    m_s[...] = jnp.full(m_s.shape, NEG_BIG, F32)
    acc_s[...] = jnp.zeros(acc_s.shape, F32)

    def attend(j, bias_of, n=1):
        ks = pl.ds(pl.multiple_of(j * TK, TK), n * TK)
        ma = madd_s[pl.ds(j, n)].reshape(n * TK, TQ)
        for h in range(H):
            kp = k_ref[ks, LANES * (h // 2):LANES * (h // 2 + 1)]
            s = jnp.dot(kp, qpad_s[h], preferred_element_type=F32) + ma
            bias = bias_of(h)
            if bias is not None:
                s = s + bias
            s_s[h, :n * TK] = s
            cmax_s[h] = jnp.max(s.reshape(n * TK // 8, 8, TQ), axis=0)
        for h in range(H):
            m_old = m_s[h]
            m_new = jnp.maximum(m_old, jnp.max(cmax_s[h], axis=0, keepdims=True))
            alpha_s[h] = jnp.exp2(m_old - m_new)
            m_s[h] = m_new
            p_s[h, :n * TK] = jnp.exp2(s_s[h, :n * TK] - m_new).astype(BF16)
        for h in range(H):
            hs = slice(h * VT_ROWS, (h + 1) * VT_ROWS)
            pv = sum(jnp.dot(vt_ref[j + c, hs, :], p_s[h, c * TK:(c + 1) * TK], preferred_element_type=F32)
                     for c in range(n))
            acc_s[hs, :] = alpha_s[h] * acc_s[hs, :] + pv

    n_far = jnp.maximum(qb - 1, 0)

    def far_body(jj, c):
        attend(2 * jj, lambda h: None, n=2)
        return c

    lax.fori_loop(0, n_far // 2, far_body, 0)

    @pl.when(n_far % 2 == 1)
    def _():
        attend(n_far - 1, lambda h: None)

    @pl.when(qb >= 1)
    def _():
        attend(qb - 1, lambda h: bias_s[h, 1])

    attend(qb, lambda h: bias_s[h, 0])

    outs = []
    for h in range(H):
        o = acc_s[h * VT_ROWS:h * VT_ROWS + HEAD_DIM, :] / acc_s[h * VT_ROWS + HEAD_DIM:h * VT_ROWS + HEAD_DIM + 1, :]
        ms = jnp.mean(o * o, axis=0, keepdims=True)
        outs.append(o * lax.rsqrt(ms + RMS_EPS))
    out_ref[...] = (jnp.concatenate(outs, axis=0).T * g_ref[...]).astype(out_ref.dtype)


def _dsa(qk, vt, qiw, kix, rel_bias, g, B, T):
    N = qk.shape[0]
    TQ = DSA_BLOCK
    nq = T // TQ
    A = ATTN_HEADS * HEAD_DIM
    ksel = min(TOPK_MAX, T // 4)
    assert MAX_DISTANCE <= TQ + 1
    return pl.pallas_call(
        functools.partial(_dsa_kernel, T, ksel),
        grid=(B, nq),
        in_specs=[pl.BlockSpec((TQ, A), lambda b, i: (b * nq + i, 0)),
                  pl.BlockSpec((T, A), lambda b, i: (b, 1)),
                  pl.BlockSpec((nq, ATTN_HEADS * VT_ROWS, TQ), lambda b, i: (b, 0, 0)),
                  pl.BlockSpec((TQ, IDX_COLS), lambda b, i: (b * nq + i, 0)),
                  pl.BlockSpec((T, KIX_COLS), lambda b, i: (b, 0)),
                  pl.BlockSpec(memory_space=pltpu.SMEM),
                  pl.BlockSpec((1, A), lambda b, i: (0, 0))],
        out_specs=pl.BlockSpec((TQ, A), lambda b, i: (b * nq + i, 0)),
        out_shape=jax.ShapeDtypeStruct((N, A), BF16),
        scratch_shapes=[pltpu.VMEM((IDX_HEADS, KIX_COLS, TQ), BF16),
                        pltpu.VMEM((ATTN_HEADS, LANES, TQ), BF16),
                        pltpu.VMEM((nq, TQ, TQ), I32),
                        pltpu.VMEM((nq, TQ, TQ), F32),
                        pltpu.VMEM((ATTN_HEADS, 2, TQ, TQ), F32),
                        pltpu.VMEM((ATTN_HEADS, 1, TQ), F32),
                        pltpu.VMEM((ATTN_HEADS * VT_ROWS, TQ), F32),
                        pltpu.VMEM((ATTN_HEADS, 2 * TQ, TQ), F32),
                        pltpu.VMEM((ATTN_HEADS, 2 * TQ, TQ), BF16),
                        pltpu.VMEM((ATTN_HEADS, 8, TQ), F32),
                        pltpu.VMEM((ATTN_HEADS, 1, TQ), F32)],
        compiler_params=_params("arbitrary", "arbitrary"),
        name="dsa",
    )(qk, qk, vt, qiw, kix, rel_bias, g.reshape(1, A))


def _mix_mlp_kernel(final, x_ref, rw_ref, att_ref, wo_ref, gt1_ref, g_ref, sc_ref, sh_ref, gt2_ref,
                    w1_ref, w2_ref, fg_ref, o_ref, x_s, h_s, acc_s):
    j = pl.program_id(1)

    @pl.when(j == 0)
    def _():
        R = rw_ref.shape[1]
        mixed = (jnp.dot(rw_ref[...], wo_ref[:R, :], preferred_element_type=F32)
                 + jnp.dot(att_ref[...], wo_ref[R:, :], preferred_element_type=F32))
        x1 = x_ref[...] + gt1_ref[...] * mixed
        x_s[...] = x1
        h_s[...] = _norm_mod(x1, g_ref[...], sc_ref[...], sh_ref[...]).astype(BF16)
        acc_s[...] = jnp.zeros_like(acc_s)

    u = jnp.dot(h_s[...], w1_ref[...], preferred_element_type=F32)
    u = jnp.square(jnp.maximum(u, 0.0))
    acc_s[...] += jnp.dot(u.astype(BF16), w2_ref[...], preferred_element_type=F32)

    @pl.when(j == pl.num_programs(1) - 1)
    def _():
        y = x_s[...] + gt2_ref[...] * acc_s[...]
        if final:
            ms = jnp.mean(y * y, axis=-1, keepdims=True)
            y = y * lax.rsqrt(ms + RMS_EPS) * fg_ref[...]
        o_ref[...] = y


def _mix_mlp(x2d, rw, att, wo, gt1, g, sc, sh, gt2, w1, w2, final_g, final, T):
    N, D = x2d.shape
    F = w1.shape[1]
    fc = min(MLP_FF_CHUNK, F)
    tm = min(MLP_TILE, T)
    nb = T // tm
    row = lambda i, j: (i, 0)
    per_b = lambda i, j: (i // nb, 0, 0)
    const = lambda i, j: (0, 0)
    return pl.pallas_call(
        functools.partial(_mix_mlp_kernel, final),
        grid=(N // tm, F // fc),
        in_specs=[pl.BlockSpec((tm, D), row),
                  pl.BlockSpec((tm, rw.shape[1]), row),
                  pl.BlockSpec((tm, att.shape[1]), row),
                  pl.BlockSpec(wo.shape, const, pipeline_mode=pl.Buffered(1)),
                  pl.BlockSpec((None, 1, D), per_b),
                  pl.BlockSpec((1, D), const),
                  pl.BlockSpec((None, 1, D), per_b),
                  pl.BlockSpec((None, 1, D), per_b),
                  pl.BlockSpec((None, 1, D), per_b),
                  pl.BlockSpec((D, fc), lambda i, j: (0, j)),
                  pl.BlockSpec((fc, D), lambda i, j: (j, 0)),
                  pl.BlockSpec((1, D), const)],
        out_specs=pl.BlockSpec((tm, D), row),
        out_shape=jax.ShapeDtypeStruct((N, D), F32),
        scratch_shapes=[pltpu.VMEM((tm, D), F32), pltpu.VMEM((tm, D), BF16), pltpu.VMEM((tm, D), F32)],
        compiler_params=_params("arbitrary", "arbitrary"),
        name="mix_mlp",
    )(x2d, rw, att, wo, gt1, g.reshape(1, D), sc, sh, gt2, w1, w2, final_g.reshape(1, D))


def _pad_cols(w, n):
    return jnp.pad(w, ((0, 0), (0, n - w.shape[1])))


def _split_bf16(w):
    hi = w.astype(BF16)
    return hi, (w - hi.astype(F32)).astype(BF16)


def _in_weights(l, w_in, mu_lora, decay_a, iclr_a, gate_a, vres_mu, vres_a):
    D = w_in.shape[1]
    R = RWKV_HEADS * HEAD_DIM
    w = w_in[l]
    mats = [(decay_a[l], mu_lora[l, 0]), (iclr_a[l], mu_lora[l, 1]), (gate_a[l], mu_lora[l, 2])]
    if l > 0:
        mats.append((vres_a[l - 1], vres_mu[l - 1]))
    now = _pad_cols(jnp.concatenate([a * (1.0 - mu)[:, None] for a, mu in mats], axis=1), LORA_PAD)
    prev = _pad_cols(jnp.concatenate([a * mu[:, None] for a, mu in mats], axis=1), LORA_PAD)
    wq = w[:, 3 * R:4 * R] * (HEAD_DIM ** -0.5 * LOG2E)
    wa = jnp.concatenate([w[:, :3 * R], now, prev, wq, w[:, 4 * R:5 * R]], axis=1).astype(BF16)
    wv = w[:, 5 * R:6 * R].reshape(D, ATTN_HEADS, HEAD_DIM)
    wv = jnp.pad(wv, ((0, 0), (0, 0), (0, VT_ROWS - HEAD_DIM))).reshape(D, ATTN_HEADS * VT_ROWS)
    wvt = wv.T.astype(BF16)
    wbh, wbl = _split_bf16(_pad_cols(w[:, 6 * R:], IDX_COLS))
    return wa, wvt, wbh, wbl


def kernel(x, c, w_ada, b_ada, norm1_g, norm2_g, w_in, mu_rkv, mu_lora, decay_w0, decay_a, decay_b, iclr_a0, iclr_a, iclr_b, gate_a, gate_b, k_k, k_a, r_k, lnx_g, lnx_b, vres_mu, vres_v0, vres_a, vres_b, attn_out_g, rel_bias, w_out, w_mlp1, w_mlp2, final_g):
    B, T, D = x.shape
    depth = w_in.shape[0]
    mod = _adaln(c, w_ada, b_ada)
    mod = mod.reshape(depth, B, 6, 1, D).transpose(0, 2, 1, 3, 4)
    x2d = x.reshape(B * T, D)
    v_first = None
    for l in range(depth):
        sh1, sc1, gt1, sh2, sc2, gt2 = (mod[l, i] for i in range(6))
        wa, wvt, wbh, wbl = _in_weights(l, w_in, mu_lora, decay_a, iclr_a, gate_a, vres_mu, vres_a)
        rkvl, qk, vt, qiw, kix = _inproj(x2d, norm1_g[l], sc1, sh1, wa, wvt, wbh, wbl, T)
        p = dict(mu_rkv=mu_rkv[l], decay_w0=decay_w0[l], decay_b=decay_b[l], iclr_a0=iclr_a0[l],
                 iclr_b=iclr_b[l], gate_b=gate_b[l], k_k=k_k[l], k_a=k_a[l], r_k=r_k[l],
                 lnx_g=lnx_g[l], lnx_b=lnx_b[l])
        if l > 0:
            p.update(vres_v0=vres_v0[l - 1], vres_b=vres_b[l - 1])
        rw, v_first = _rwkv(rkvl, v_first, p, B, T)
        att = _dsa(qk, vt, qiw, kix, rel_bias, attn_out_g[l], B, T)
        x2d = _mix_mlp(x2d, rw, att, w_out[l].astype(BF16), gt1, norm2_g[l], sc2, sh2, gt2,
                       w_mlp1[l].astype(BF16), w_mlp2[l].astype(BF16), final_g, l == depth - 1, T)
    return x2d.reshape(B, T, D)
```
